```python
import math
import jax, jax.numpy as jnp
from jax import lax
import numpy as np

D_MODEL = 1024
BATCH = 4
SEQ = 4096
DEPTH = 1

DN_HEADS = 4
DN_HEAD_DIM = 128
DN_WIDTH = DN_HEADS * DN_HEAD_DIM
DN_QKV_WIDTH = 3 * DN_WIDTH
DN_CONV = 4
DN_CHUNK = 64
SWA_HEADS = 8
SWA_KV_HEADS = 2
SWA_HEAD_DIM = 64
SWA_GROUP = SWA_HEADS // SWA_KV_HEADS
SWA_WIDTH = SWA_HEADS * SWA_HEAD_DIM
SWA_KV_WIDTH = SWA_KV_HEADS * SWA_HEAD_DIM
WINDOW = 128
SWA_BLOCK = 128
REL_BUCKETS = 32
REL_MAX_DIST = 128
N_BRANCHES = 2
D_FF = ((8 * D_MODEL // 3 + 255) // 256) * 256
D_IN = DN_QKV_WIDTH + DN_WIDTH + 2 * DN_HEADS + SWA_WIDTH + 2 * SWA_KV_WIDTH + N_BRANCHES * D_MODEL
EPS = 1e-6

kernel_name = "hybrid_gdn_swa_gated_merge"


def rms_norm(x, gain):
    xf = x.astype(jnp.float32)
    y = xf * lax.rsqrt(jnp.mean(xf * xf, axis=-1, keepdims=True) + EPS)
    return (y * gain.astype(jnp.float32)).astype(x.dtype)


def l2_norm(x):
    xf = x.astype(jnp.float32)
    return xf * lax.rsqrt(jnp.sum(xf * xf, axis=-1, keepdims=True) + EPS)


def causal_dwconv(x, w):
    k = w.shape[0]
    return lax.conv_general_dilated(
        x, w[:, None, :], window_strides=(1,), padding=((k - 1, 0),),
        dimension_numbers=("NWC", "WIO", "NWC"), feature_group_count=x.shape[-1])


def chunk_gated_delta_rule(q, k, v, g, beta):
    B, S, H, DK = q.shape
    DV = v.shape[-1]
    C = DN_CHUNK
    NC = S // C
    q = q * (DK ** -0.5)

    def to_chunks(t):
        return t.reshape(B, NC, C, H, t.shape[-1]).transpose(0, 3, 1, 2, 4)

    qc, kc, vc = to_chunks(q), to_chunks(k), to_chunks(v)
    gc = jnp.cumsum(g.reshape(B, NC, C, H).transpose(0, 3, 1, 2), axis=-1)
    bc = beta.reshape(B, NC, C, H).transpose(0, 3, 1, 2)[..., None]
    k_beta = kc * bc
    v_beta = vc * bc

    idx = jnp.arange(C)
    lower_incl = idx[:, None] >= idx[None, :]
    strict_lower = idx[:, None] > idx[None, :]
    decay = jnp.exp(jnp.where(lower_incl, gc[..., :, None] - gc[..., None, :], -jnp.inf))

    a_mat = jnp.where(strict_lower, jnp.einsum("bhncd,bhnsd->bhncs", k_beta, kc) * decay, 0.0)
    lhs = a_mat + jnp.eye(C, dtype=a_mat.dtype)
    rhs = jnp.concatenate([v_beta, k_beta * jnp.exp(gc)[..., None]], axis=-1)
    sol = lax.linalg.triangular_solve(lhs, rhs, left_side=True, lower=True, unit_diagonal=True)
    u, w = sol[..., :DV], sol[..., DV:]
    qk = jnp.einsum("bhncd,bhnsd->bhncs", qc, kc) * decay

    def step(state, inp):
        q_i, k_i, u_i, w_i, g_i, qk_i = inp
        v_new = u_i - jnp.einsum("bhck,bhkv->bhcv", w_i, state)
        o_i = (jnp.einsum("bhck,bhkv->bhcv", q_i * jnp.exp(g_i)[..., None], state)
               + jnp.einsum("bhcs,bhsv->bhcv", qk_i, v_new))
        g_last = g_i[..., -1]
        k_dec = k_i * jnp.exp(g_last[..., None] - g_i)[..., None]
        state = state * jnp.exp(g_last)[..., None, None] + jnp.einsum("bhck,bhcv->bhkv", k_dec, v_new)
        return state, o_i

    xs = tuple(jnp.moveaxis(t, 2, 0) for t in (qc, kc, u, w, gc, qk))
    state0 = jnp.zeros((B, H, DK, DV), jnp.float32)
    _, o = lax.scan(step, state0, xs)
    return o.transpose(1, 0, 3, 2, 4).reshape(B, S, H, DV)


def gated_deltanet_branch(qkv, z, b_raw, a_raw, conv_w, a_log, dt_bias, out_gain):
    B, S, _ = qkv.shape
    qkv = jax.nn.silu(causal_dwconv(qkv, conv_w))
    q, k, v = jnp.split(qkv, 3, axis=-1)
    q = l2_norm(q.reshape(B, S, DN_HEADS, DN_HEAD_DIM))
    k = l2_norm(k.reshape(B, S, DN_HEADS, DN_HEAD_DIM))
    v = v.reshape(B, S, DN_HEADS, DN_HEAD_DIM).astype(jnp.float32)
    beta = jax.nn.sigmoid(b_raw.astype(jnp.float32))
    g = -jnp.exp(a_log.astype(jnp.float32)) * jax.nn.softplus(a_raw.astype(jnp.float32) + dt_bias.astype(jnp.float32))
    o = chunk_gated_delta_rule(q, k, v, g, beta)
    o = rms_norm(o, out_gain) * jax.nn.silu(z.reshape(B, S, DN_HEADS, DN_HEAD_DIM).astype(jnp.float32))
    return o.reshape(B, S, DN_WIDTH).astype(qkv.dtype)


def t5_causal_bucket(dist):
    n = jnp.maximum(dist, 0)
    max_exact = REL_BUCKETS // 2
    nf = jnp.maximum(n, 1).astype(jnp.float32)
    large = max_exact + (jnp.log(nf / max_exact) / math.log(REL_MAX_DIST / max_exact)
                         * (REL_BUCKETS - max_exact)).astype(jnp.int32)
    large = jnp.minimum(large, REL_BUCKETS - 1)
    return jnp.where(n < max_exact, n, large)


def sliding_window_branch(q, k, v, q_gain, k_gain, sinks, rel_bias):
    B, S, _ = q.shape
    NB = S // SWA_BLOCK
    q = rms_norm(q.reshape(B, S, SWA_HEADS, SWA_HEAD_DIM), q_gain)
    k = rms_norm(k.reshape(B, S, SWA_KV_HEADS, SWA_HEAD_DIM), k_gain)
    v = v.reshape(B, S, SWA_KV_HEADS, SWA_HEAD_DIM)
    qb = q.reshape(B, NB, SWA_BLOCK, SWA_KV_HEADS, SWA_GROUP, SWA_HEAD_DIM).astype(jnp.float32)

    def band(t):
        tp = jnp.pad(t, ((0, 0), (SWA_BLOCK, 0), (0, 0), (0, 0)))
        tp = tp.reshape(B, NB + 1, SWA_BLOCK, SWA_KV_HEADS, SWA_HEAD_DIM)
        return jnp.concatenate([tp[:, :-1], tp[:, 1:]], axis=2).astype(jnp.float32)

    kb, vb = band(k), band(v)
    logits = jnp.einsum("bnqkgd,bnskd->bnkgqs", qb, kb) * (SWA_HEAD_DIM ** -0.5)

    qi = jnp.arange(SWA_BLOCK)[:, None]
    kj = jnp.arange(2 * SWA_BLOCK)[None, :]
    dist = SWA_BLOCK + qi - kj
    in_window = (dist >= 0) & (dist < WINDOW)
    key_pos = (jnp.arange(NB)[:, None] - 1) * SWA_BLOCK + jnp.arange(2 * SWA_BLOCK)[None, :]
    mask = in_window[None, :, :] & (key_pos >= 0)[:, None, :]

    bias = rel_bias.astype(jnp.float32)[t5_causal_bucket(dist)]
    bias = bias.transpose(2, 0, 1).reshape(SWA_KV_HEADS, SWA_GROUP, SWA_BLOCK, 2 * SWA_BLOCK)
    logits = jnp.where(mask[None, :, None, None], logits + bias, -jnp.inf)

    sink = sinks.astype(jnp.float32).reshape(SWA_KV_HEADS, SWA_GROUP)[None, None, :, :, None, None]
    m = jnp.maximum(jnp.max(logits, axis=-1, keepdims=True), sink)
    p = jnp.exp(logits - m)
    denom = jnp.sum(p, axis=-1, keepdims=True) + jnp.exp(sink - m)
    out = jnp.einsum("bnkgqs,bnskd->bnqkgd", p / denom, vb)
    return out.reshape(B, S, SWA_WIDTH).astype(q.dtype)


def split_combined(p):
    sizes = (DN_QKV_WIDTH, DN_WIDTH, DN_HEADS, DN_HEADS, SWA_WIDTH, SWA_KV_WIDTH, SWA_KV_WIDTH,
             N_BRANCHES * D_MODEL)
    return jnp.split(p, np.cumsum(sizes)[:-1].tolist(), axis=-1)


def setup_inputs(seed: int = 0) -> dict:
    key = jax.random.key(seed)
    ks = jax.random.split(key, 20)
    f32 = jnp.float32
    L = DEPTH

    def nrm(k, shape, scale):
        return jax.random.normal(k, shape, f32) * scale

    dt = jnp.exp(jax.random.uniform(ks[5], (L, DN_HEADS), f32, math.log(1e-3), math.log(1e-1)))
    return {
        "x": nrm(ks[0], (BATCH, SEQ, D_MODEL), 1.0),
        "attn_norm": 1.0 + nrm(ks[1], (L, D_MODEL), 0.02),
        "w_in": nrm(ks[2], (L, D_MODEL, D_IN), D_MODEL ** -0.5),
        "dn_conv": nrm(ks[3], (L, DN_CONV, DN_QKV_WIDTH), DN_CONV ** -0.5),
        "dn_a_log": jnp.log(jax.random.uniform(ks[4], (L, DN_HEADS), f32, 1.0, 16.0)),
        "dn_dt_bias": dt + jnp.log(-jnp.expm1(-dt)),
        "dn_out_norm": 1.0 + nrm(ks[6], (L, DN_HEAD_DIM), 0.02),
        "swa_q_norm": 1.0 + nrm(ks[7], (L, SWA_HEAD_DIM), 0.02),
        "swa_k_norm": 1.0 + nrm(ks[8], (L, SWA_HEAD_DIM), 0.02),
        "swa_sinks": nrm(ks[9], (L, SWA_HEADS), 0.5),
        "rel_bias": nrm(ks[10], (REL_BUCKETS, SWA_HEADS), 0.1),
        "w_branch_dn": nrm(ks[11], (L, DN_WIDTH, D_MODEL), DN_WIDTH ** -0.5),
        "w_branch_swa": nrm(ks[12], (L, SWA_WIDTH, D_MODEL), SWA_WIDTH ** -0.5),
        "w_out": nrm(ks[13], (L, D_MODEL, D_MODEL), D_MODEL ** -0.5),
        "ffn_norm": 1.0 + nrm(ks[14], (L, D_MODEL), 0.02),
        "w_gate": nrm(ks[15], (L, D_MODEL, D_FF), D_MODEL ** -0.5),
        "w_up": nrm(ks[16], (L, D_MODEL, D_FF), D_MODEL ** -0.5),
        "w_down": nrm(ks[17], (L, D_FF, D_MODEL), D_FF ** -0.5),
    }


def reference(x, attn_norm, w_in, dn_conv, dn_a_log, dn_dt_bias, dn_out_norm, swa_q_norm,
              swa_k_norm, swa_sinks, rel_bias, w_branch_dn, w_branch_swa, w_out, ffn_norm,
              w_gate, w_up, w_down):
    B, S, _ = x.shape
    for l in range(DEPTH):
        h = rms_norm(x, attn_norm[l])
        proj = h @ w_in[l]
        dn_qkv, dn_z, dn_b, dn_a, sq, sk, sv, gate_raw = split_combined(proj)
        y_dn = gated_deltanet_branch(dn_qkv, dn_z, dn_b, dn_a, dn_conv[l], dn_a_log[l],
                                     dn_dt_bias[l], dn_out_norm[l])
        y_swa = sliding_window_branch(sq, sk, sv, swa_q_norm[l], swa_k_norm[l], swa_sinks[l], rel_bias)
        gates = jax.nn.sigmoid(gate_raw.astype(jnp.float32)).astype(x.dtype)
        gates = gates.reshape(B, S, N_BRANCHES, D_MODEL)
        merged = gates[:, :, 0] * (y_dn @ w_branch_dn[l]) + gates[:, :, 1] * (y_swa @ w_branch_swa[l])
        x = x + merged @ w_out[l]
        h2 = rms_norm(x, ffn_norm[l])
        x = x + (jax.nn.silu(h2 @ w_gate[l]) * (h2 @ w_up[l])) @ w_down[l]
    return x
```

```python
import functools
import math

import numpy as np
import jax
import jax.numpy as jnp
from jax import lax
from jax.experimental import pallas as pl
from jax.experimental.pallas import tpu as pltpu

D_MODEL = 1024
DN_HEADS = 4
DN_HEAD_DIM = 128
DN_WIDTH = DN_HEADS * DN_HEAD_DIM
DN_QKV_WIDTH = 3 * DN_WIDTH
DN_CONV = 4
DN_CHUNK = 64
SWA_HEADS = 8
SWA_KV_HEADS = 2
SWA_HEAD_DIM = 64
SWA_WIDTH = SWA_HEADS * SWA_HEAD_DIM
SWA_KV_WIDTH = SWA_KV_HEADS * SWA_HEAD_DIM
WINDOW = 128
SWA_BLOCK = 128
REL_BUCKETS = 32
REL_MAX_DIST = 128
D_FF = 2816
EPS = 1e-6

LANES = 128
SUBLANES = 8
VMEM_LIMIT = 56 * 1024 * 1024

_SEG_QKV = (0, DN_QKV_WIDTH)
_SEG_Z = (_SEG_QKV[1], _SEG_QKV[1] + DN_WIDTH)
_SEG_SQ = (_SEG_Z[1], _SEG_Z[1] + SWA_WIDTH)
_SEG_SKV = (_SEG_SQ[1], _SEG_SQ[1] + 2 * SWA_KV_WIDTH)
_SEG_G = (_SEG_SKV[1], _SEG_SKV[1] + 2 * D_MODEL)
_SEG_BA = (_SEG_G[1], _SEG_G[1] + LANES)
_SEGS = (_SEG_QKV, _SEG_Z, _SEG_SQ, _SEG_SKV, _SEG_G, _SEG_BA)
D_IN_PAD = _SEG_BA[1]

F32 = jnp.float32
BF16 = jnp.bfloat16
HI = lax.Precision.HIGHEST


def _mm(a, b, dims=(((1,), (0,)), ((), ())), exact=False):
    if exact:
        return lax.dot_general(a.astype(F32), b.astype(F32), dims, precision=HI,
                               preferred_element_type=F32)
    return lax.dot_general(a.astype(BF16), b.astype(BF16), dims, preferred_element_type=F32)


_NT = (((1,), (1,)), ((), ()))
_TN = (((0,), (0,)), ((), ()))


def _sigmoid(x):
    return 1.0 / (1.0 + jnp.exp(-x))


def _silu(x):
    return x * _sigmoid(x)


def _in_proj_kernel(x_ref, gain_ref, w_ref, qkv_ref, z_ref, sq_ref, skv_ref, g_ref, ba_ref):
    x = x_ref[...]
    ms = jnp.mean(x * x, axis=-1, keepdims=True)
    h = (x * lax.rsqrt(ms + EPS) * gain_ref[...]).astype(BF16)
    for (lo, hi), o_ref in zip(_SEGS, (qkv_ref, z_ref, sq_ref, skv_ref, g_ref, ba_ref)):
        o_ref[...] = jnp.dot(h, w_ref[:, lo:hi], preferred_element_type=F32)


def _in_proj(x2, gain, w_all, tm):
    T = x2.shape[0]
    out_shape = tuple(jax.ShapeDtypeStruct((T, hi - lo), F32) for lo, hi in _SEGS)
    out_specs = tuple(pl.BlockSpec((tm, hi - lo), lambda i: (i, 0)) for lo, hi in _SEGS)
    return pl.pallas_call(
        _in_proj_kernel,
        grid=(T // tm,),
        in_specs=[
            pl.BlockSpec((tm, D_MODEL), lambda i: (i, 0)),
            pl.BlockSpec((1, D_MODEL), lambda i: (0, 0)),
            pl.BlockSpec((D_MODEL, D_IN_PAD), lambda i: (0, 0), pipeline_mode=pl.Buffered(1)),
        ],
        out_specs=out_specs,
        out_shape=out_shape,
        compiler_params=pltpu.CompilerParams(
            dimension_semantics=("arbitrary",), vmem_limit_bytes=VMEM_LIMIT),
        name="in_proj",
    )(x2, gain, w_all)


DN_TILE = 256
DN_NCH = DN_TILE // DN_CHUNK
DN_LOG_CHUNK = int(math.log2(DN_CHUNK))
DN_TAIL = SUBLANES


def _deltanet_kernel(qkv_ref, z_ref, ba_ref, convw_ref, alog_ref, dtb_ref, gain_ref, o_ref,
                     xbuf, state):
    t = pl.program_id(1)

    @pl.when(t == 0)
    def _():
        xbuf[0:DN_TAIL, :] = jnp.zeros((DN_TAIL, DN_QKV_WIDTH), F32)
        state[...] = jnp.zeros_like(state)

    @pl.when(t > 0)
    def _():
        xbuf[0:DN_TAIL, :] = xbuf[DN_TILE:DN_TILE + DN_TAIL, :]

    xbuf[DN_TAIL:DN_TAIL + DN_TILE, :] = qkv_ref[0]

    w = convw_ref[...]
    acc = w[DN_CONV - 1:DN_CONV, :] * xbuf[DN_TAIL:DN_TAIL + DN_TILE, :]
    for j in range(DN_CONV - 1):
        off = DN_TAIL - (DN_CONV - 1) + j
        acc = acc + w[j:j + 1, :] * xbuf[off:off + DN_TILE, :]
    act = _silu(acc)

    ba = ba_ref[0]
    beta = _sigmoid(ba)
    xs = ba + dtb_ref[...]
    softplus = jnp.maximum(xs, 0.0) + jnp.log1p(jnp.exp(-jnp.abs(xs)))
    g = -jnp.exp(alog_ref[...]) * softplus

    row = lax.broadcasted_iota(jnp.int32, (DN_TILE, DN_TILE), 0)
    col = lax.broadcasted_iota(jnp.int32, (DN_TILE, DN_TILE), 1)
    same_chunk = (row >> DN_LOG_CHUNK) == (col >> DN_LOG_CHUNK)
    lower_incl = same_chunk & (row >= col)
    strict_lower = same_chunk & (row > col)
    gc = _mm(jnp.where(lower_incl, 1.0, 0.0), g, exact=True)
    rev = _mm(jnp.where(same_chunk & (row < col), 1.0, 0.0), g, exact=True)
    gc_t = gc.T
    eye = jnp.where(row == col, 1.0, 0.0)

    scale = DN_HEAD_DIM ** -0.5
    for h in range(DN_HEADS):
        q = act[:, h * DN_HEAD_DIM:(h + 1) * DN_HEAD_DIM]
        k = act[:, DN_WIDTH + h * DN_HEAD_DIM:DN_WIDTH + (h + 1) * DN_HEAD_DIM]
        v = act[:, 2 * DN_WIDTH + h * DN_HEAD_DIM:2 * DN_WIDTH + (h + 1) * DN_HEAD_DIM]
        q = q * (lax.rsqrt(jnp.sum(q * q, axis=-1, keepdims=True) + EPS) * scale)
        k = k * lax.rsqrt(jnp.sum(k * k, axis=-1, keepdims=True) + EPS)
        beta_c = beta[:, h:h + 1]
        gc_c = gc[:, DN_HEADS + h:DN_HEADS + h + 1]
        gc_r = gc_t[DN_HEADS + h:DN_HEADS + h + 1, :]
        rev_c = rev[:, DN_HEADS + h:DN_HEADS + h + 1]

        kb = k * beta_c
        decay = jnp.exp(jnp.where(lower_incl, gc_c - gc_r, -jnp.inf))
        gram = _mm(jnp.concatenate([kb, q], axis=0), k, _NT)
        a_mat = jnp.where(strict_lower, gram[:DN_TILE] * decay, 0.0)
        qk = gram[DN_TILE:] * decay

        n_pow = -a_mat
        inv = eye + n_pow
        for _ in range(DN_LOG_CHUNK - 1):
            n_pow = _mm(n_pow, n_pow)
            inv = inv + _mm(inv, n_pow)

        eg = jnp.exp(gc_c)
        sol = _mm(inv, jnp.concatenate([v * beta_c, kb * eg], axis=1))
        u = sol[:, :DN_HEAD_DIM]
        wmat = sol[:, DN_HEAD_DIM:]
        qe = q * eg
        kdec = k * jnp.exp(rev_c)

        s = state[h]
        v_new = []
        o_inter = []
        for c in range(DN_NCH):
            r0, r1 = c * DN_CHUNK, (c + 1) * DN_CHUNK
            ws = _mm(jnp.concatenate([wmat[r0:r1], qe[r0:r1]], axis=0), s)
            vn = u[r0:r1] - ws[:DN_CHUNK]
            o_inter.append(ws[DN_CHUNK:])
            g_last = gc[r1 - 1:r1, DN_HEADS + h:DN_HEADS + h + 1]
            s = s * jnp.exp(g_last) + _mm(kdec[r0:r1], vn, _TN)
            v_new.append(vn)
        state[h] = s
        o = jnp.concatenate(o_inter, axis=0) + _mm(qk, jnp.concatenate(v_new, axis=0))

        o = o * lax.rsqrt(jnp.mean(o * o, axis=-1, keepdims=True) + EPS) * gain_ref[...]
        zh = z_ref[0, :, h * DN_HEAD_DIM:(h + 1) * DN_HEAD_DIM]
        o_ref[0, :, h * DN_HEAD_DIM:(h + 1) * DN_HEAD_DIM] = o * _silu(zh)


def _deltanet(qkv, z, ba, convw, alog_lane, dtb_lane, out_gain):
    B, S, _ = qkv.shape
    return pl.pallas_call(
        _deltanet_kernel,
        grid=(B, S // DN_TILE),
        in_specs=[
            pl.BlockSpec((1, DN_TILE, DN_QKV_WIDTH), lambda b, t: (b, t, 0)),
            pl.BlockSpec((1, DN_TILE, DN_WIDTH), lambda b, t: (b, t, 0)),
            pl.BlockSpec((1, DN_TILE, LANES), lambda b, t: (b, t, 0)),
            pl.BlockSpec((DN_CONV, DN_QKV_WIDTH), lambda b, t: (0, 0)),
            pl.BlockSpec((1, LANES), lambda b, t: (0, 0)),
            pl.BlockSpec((1, LANES), lambda b, t: (0, 0)),
            pl.BlockSpec((1, DN_HEAD_DIM), lambda b, t: (0, 0)),
        ],
        out_specs=pl.BlockSpec((1, DN_TILE, DN_WIDTH), lambda b, t: (b, t, 0)),
        out_shape=jax.ShapeDtypeStruct((B, S, DN_WIDTH), F32),
        scratch_shapes=[
            pltpu.VMEM((DN_TAIL + DN_TILE, DN_QKV_WIDTH), F32),
            pltpu.VMEM((DN_HEADS, DN_HEAD_DIM, DN_HEAD_DIM), F32),
        ],
        compiler_params=pltpu.CompilerParams(
            dimension_semantics=("arbitrary", "arbitrary"), vmem_limit_bytes=VMEM_LIMIT),
        name="deltanet",
    )(qkv, z, ba, convw, alog_lane, dtb_lane, out_gain)


SWA_BAND = 2 * SWA_BLOCK
SWA_SLOTS = SWA_WIDTH // LANES
SWA_HEADS_PER_SLOT = LANES // SWA_HEAD_DIM


def _swa_kernel(sinks_ref, sq_ref, kv_prev_ref, kv_cur_ref, bias_ref, qgain_ref, kgain_ref,
                seg_q_ref, seg_k_ref, o_ref):
    n = pl.program_id(1)
    q = sq_ref[0]
    q = q * lax.rsqrt(_mm(q * q, seg_q_ref[...], exact=True) + EPS) * qgain_ref[...]
    kv = jnp.concatenate([kv_prev_ref[0], kv_cur_ref[0]], axis=0)
    k = kv[:, :SWA_KV_WIDTH]
    v = kv[:, SWA_KV_WIDTH:]
    k = k * lax.rsqrt(_mm(k * k, seg_k_ref[...], exact=True) + EPS) * kgain_ref[...]

    lane = lax.broadcasted_iota(jnp.int32, (SWA_BAND, LANES), 1)
    low = lane < SWA_HEAD_DIM
    k_rot = pltpu.roll(k, SWA_HEAD_DIM, axis=1)
    v_rot = pltpu.roll(v, SWA_HEAD_DIM, axis=1)
    k_lo = (jnp.where(low, k, 0.0), jnp.where(low, k_rot, 0.0))
    k_hi = (jnp.where(low, 0.0, k_rot), jnp.where(low, 0.0, k))
    v_lo = (jnp.where(low, v, 0.0), jnp.where(low, v_rot, 0.0))
    v_hi = (jnp.where(low, 0.0, v_rot), jnp.where(low, 0.0, v))

    key_col = lax.broadcasted_iota(jnp.int32, (SWA_BLOCK, SWA_BAND), 1)
    key_valid = (key_col >= SWA_BLOCK) | (n > 0)
    out_low = lax.broadcasted_iota(jnp.int32, (SWA_BLOCK, LANES), 1) < SWA_HEAD_DIM

    slots_per_kv = SWA_SLOTS // SWA_KV_HEADS
    for j in range(SWA_SLOTS):
        kh = j // slots_per_kv
        k_bd = jnp.concatenate([k_lo[kh], k_hi[kh]], axis=0)
        v_bd = jnp.concatenate([v_lo[kh], v_hi[kh]], axis=0)
        logits = _mm(q[:, j * LANES:(j + 1) * LANES], k_bd, _NT)
        probs = []
        inv_den = []
        for r in range(SWA_HEADS_PER_SLOT):
            hd = j * SWA_HEADS_PER_SLOT + r
            lg = logits[:, r * SWA_BAND:(r + 1) * SWA_BAND] + bias_ref[hd]
            lg = jnp.where(key_valid, lg, -jnp.inf)
            sink = sinks_ref[hd]
            m = jnp.maximum(jnp.max(lg, axis=-1, keepdims=True), sink)
            p = jnp.exp(lg - m)
            den = jnp.sum(p, axis=-1, keepdims=True) + jnp.exp(sink - m)
            probs.append(p)
            inv_den.append(1.0 / den)
        out = _mm(jnp.concatenate(probs, axis=1), v_bd)
        o_ref[0, :, j * LANES:(j + 1) * LANES] = out * jnp.where(out_low, inv_den[0], inv_den[1])


def _swa(sq, skv, sinks, bias_tab, qgain, kgain, seg_q, seg_k):
    B, S, _ = sq.shape
    nb = S // SWA_BLOCK
    kvw = 2 * SWA_KV_WIDTH
    grid_spec = pltpu.PrefetchScalarGridSpec(
        num_scalar_prefetch=1,
        grid=(B, nb),
        in_specs=[
            pl.BlockSpec((1, SWA_BLOCK, SWA_WIDTH), lambda b, n, s: (b, n, 0)),
            pl.BlockSpec((1, SWA_BLOCK, kvw), lambda b, n, s: (b, jnp.maximum(n - 1, 0), 0)),
            pl.BlockSpec((1, SWA_BLOCK, kvw), lambda b, n, s: (b, n, 0)),
            pl.BlockSpec((SWA_HEADS, SWA_BLOCK, SWA_BAND), lambda b, n, s: (0, 0, 0)),
            pl.BlockSpec((1, SWA_WIDTH), lambda b, n, s: (0, 0)),
            pl.BlockSpec((1, SWA_KV_WIDTH), lambda b, n, s: (0, 0)),
            pl.BlockSpec((SWA_WIDTH, SWA_WIDTH), lambda b, n, s: (0, 0)),
            pl.BlockSpec((SWA_KV_WIDTH, SWA_KV_WIDTH), lambda b, n, s: (0, 0)),
        ],
        out_specs=pl.BlockSpec((1, SWA_BLOCK, SWA_WIDTH), lambda b, n, s: (b, n, 0)),
    )
    return pl.pallas_call(
        _swa_kernel,
        grid_spec=grid_spec,
        out_shape=jax.ShapeDtypeStruct((B, S, SWA_WIDTH), F32),
        compiler_params=pltpu.CompilerParams(
            dimension_semantics=("arbitrary", "arbitrary"), vmem_limit_bytes=VMEM_LIMIT),
        name="swa",
    )(sinks, sq, skv, skv, bias_tab, qgain, kgain, seg_q, seg_k)


def _merge_kernel(x_ref, ydn_ref, yswa_ref, g_ref, wa_ref, wb_ref, wo_ref, gain_ref,
                  x1_ref, h2_ref):
    ga = _sigmoid(g_ref[:, :D_MODEL])
    gb = _sigmoid(g_ref[:, D_MODEL:])
    merged = ga * _mm(ydn_ref[...], wa_ref[...]) + gb * _mm(yswa_ref[...], wb_ref[...])
    x1 = x_ref[...] + _mm(merged, wo_ref[...])
    x1_ref[...] = x1
    ms = jnp.mean(x1 * x1, axis=-1, keepdims=True)
    h2_ref[...] = (x1 * lax.rsqrt(ms + EPS) * gain_ref[...]).astype(BF16)


def _merge(x2, ydn, yswa, graw, wa, wb, wo, gain, tm):
    T = x2.shape[0]
    row = lambda w: pl.BlockSpec((tm, w), lambda i: (i, 0))
    full = lambda a, b: pl.BlockSpec((a, b), lambda i: (0, 0))
    return pl.pallas_call(
        _merge_kernel,
        grid=(T // tm,),
        in_specs=[row(D_MODEL), row(DN_WIDTH), row(SWA_WIDTH), row(2 * D_MODEL),
                  full(DN_WIDTH, D_MODEL), full(SWA_WIDTH, D_MODEL), full(D_MODEL, D_MODEL),
                  full(1, D_MODEL)],
        out_specs=(row(D_MODEL), row(D_MODEL)),
        out_shape=(jax.ShapeDtypeStruct((T, D_MODEL), F32),
                   jax.ShapeDtypeStruct((T, D_MODEL), BF16)),
        compiler_params=pltpu.CompilerParams(
            dimension_semantics=("arbitrary",), vmem_limit_bytes=VMEM_LIMIT),
        name="merge",
    )(x2, ydn, yswa, graw, wa, wb, wo, gain)


def _ffn_kernel(x1_ref, h2_ref, wg_ref, wu_ref, wd_ref, o_ref, acc_ref):
    j = pl.program_id(1)

    @pl.when(j == 0)
    def _():
        acc_ref[...] = x1_ref[...]

    h2 = h2_ref[...]
    gate = jnp.dot(h2, wg_ref[...], preferred_element_type=F32)
    up = jnp.dot(h2, wu_ref[...], preferred_element_type=F32)
    acc_ref[...] += _mm(_silu(gate) * up, wd_ref[...])

    @pl.when(j == pl.num_programs(1) - 1)
    def _():
        o_ref[...] = acc_ref[...]


def _ffn(x1, h2, wg, wu, wd, tm, tf):
    T = x1.shape[0]
    return pl.pallas_call(
        _ffn_kernel,
        grid=(T // tm, D_FF // tf),
        in_specs=[
            pl.BlockSpec((tm, D_MODEL), lambda i, j: (i, 0)),
            pl.BlockSpec((tm, D_MODEL), lambda i, j: (i, 0)),
            pl.BlockSpec((D_MODEL, tf), lambda i, j: (0, j)),
            pl.BlockSpec((D_MODEL, tf), lambda i, j: (0, j)),
            pl.BlockSpec((tf, D_MODEL), lambda i, j: (j, 0)),
        ],
        out_specs=pl.BlockSpec((tm, D_MODEL), lambda i, j: (i, 0)),
        out_shape=jax.ShapeDtypeStruct((T, D_MODEL), F32),
        scratch_shapes=[pltpu.VMEM((tm, D_MODEL), F32)],
        compiler_params=pltpu.CompilerParams(
            dimension_semantics=("arbitrary", "arbitrary"), vmem_limit_bytes=VMEM_LIMIT),
        name="ffn",
    )(x1, h2, wg, wu, wd)


def _t5_bucket_table():
    qi = jnp.arange(SWA_BLOCK)[:, None]
    kj = jnp.arange(SWA_BAND)[None, :]
    dist = SWA_BLOCK + qi - kj
    in_window = (dist >= 0) & (dist < WINDOW)
    n = jnp.maximum(dist, 0)
    max_exact = REL_BUCKETS // 2
    nf = jnp.maximum(n, 1).astype(F32)
    large = max_exact + (jnp.log(nf / max_exact) / math.log(REL_MAX_DIST / max_exact)
                         * (REL_BUCKETS - max_exact)).astype(jnp.int32)
    large = jnp.minimum(large, REL_BUCKETS - 1)
    return jnp.where(n < max_exact, n, large), in_window


def _segment_mean_matrix(width, seg):
    idx = np.arange(width) // seg
    return jnp.asarray((idx[:, None] == idx[None, :]).astype(np.float32) / seg)


def kernel(x, attn_norm, w_in, dn_conv, dn_a_log, dn_dt_bias, dn_out_norm, swa_q_norm, swa_k_norm, swa_sinks, rel_bias, w_branch_dn, w_branch_swa, w_out, ffn_norm, w_gate, w_up, w_down):
    B, S, D = x.shape
    T = B * S
    depth = w_in.shape[0]
    bucket, in_window = _t5_bucket_table()
    seg_q = _segment_mean_matrix(SWA_WIDTH, SWA_HEAD_DIM)
    seg_k = _segment_mean_matrix(SWA_KV_WIDTH, SWA_HEAD_DIM)
    bias_tab = jnp.where(in_window[None], rel_bias.astype(F32)[bucket].transpose(2, 0, 1), -jnp.inf)

    x2 = x.reshape(T, D)
    for l in range(depth):
        wl = w_in[l]
        o_qkv = 0
        o_z = o_qkv + DN_QKV_WIDTH
        o_b = o_z + DN_WIDTH
        o_sq = o_b + 2 * DN_HEADS
        o_skv = o_sq + SWA_WIDTH
        o_g = o_skv + 2 * SWA_KV_WIDTH
        w_all = jnp.concatenate([
            wl[:, o_qkv:o_z], wl[:, o_z:o_b], wl[:, o_sq:o_skv], wl[:, o_skv:o_g], wl[:, o_g:],
            wl[:, o_b:o_sq], jnp.zeros((D, LANES - 2 * DN_HEADS), wl.dtype)], axis=1).astype(BF16)

        qkv, z, sq, skv, graw, ba = _in_proj(x2, attn_norm[l][None, :], w_all, tm=512)

        pad_lo = jnp.zeros((DN_HEADS,), F32)
        pad_hi = jnp.zeros((LANES - 2 * DN_HEADS,), F32)
        alog_lane = jnp.concatenate([pad_lo, dn_a_log[l].astype(F32), pad_hi])[None, :]
        dtb_lane = jnp.concatenate([pad_lo, dn_dt_bias[l].astype(F32), pad_hi])[None, :]
        y_dn = _deltanet(qkv.reshape(B, S, -1), z.reshape(B, S, -1), ba.reshape(B, S, -1),
                         dn_conv[l], alog_lane, dtb_lane, dn_out_norm[l][None, :])

        qgain = jnp.tile(swa_q_norm[l].astype(F32), SWA_HEADS)[None, :] * (SWA_HEAD_DIM ** -0.5)
        kgain = jnp.tile(swa_k_norm[l].astype(F32), SWA_KV_HEADS)[None, :]
        y_swa = _swa(sq.reshape(B, S, -1), skv.reshape(B, S, -1), swa_sinks[l].astype(F32),
                     bias_tab, qgain, kgain, seg_q, seg_k)

        x1, h2 = _merge(x2, y_dn.reshape(T, -1), y_swa.reshape(T, -1), graw,
                        w_branch_dn[l].astype(BF16), w_branch_swa[l].astype(BF16),
                        w_out[l].astype(BF16), ffn_norm[l][None, :], tm=512)
        x2 = _ffn(x1, h2, w_gate[l].astype(BF16), w_up[l].astype(BF16), w_down[l].astype(BF16),
                  tm=512, tf=1408)
    return x2.reshape(B, S, D)
```

```python
import functools
import math

import numpy as np
import jax
import jax.numpy as jnp
from jax import lax
from jax.experimental import pallas as pl
from jax.experimental.pallas import tpu as pltpu

D_MODEL = 1024
DN_HEADS = 4
DN_HEAD_DIM = 128
DN_WIDTH = DN_HEADS * DN_HEAD_DIM
DN_QKV_WIDTH = 3 * DN_WIDTH
DN_CONV = 4
DN_CHUNK = 64
SWA_HEADS = 8
SWA_KV_HEADS = 2
SWA_HEAD_DIM = 64
SWA_WIDTH = SWA_HEADS * SWA_HEAD_DIM
SWA_KV_WIDTH = SWA_KV_HEADS * SWA_HEAD_DIM
WINDOW = 128
SWA_BLOCK = 128
REL_BUCKETS = 32
REL_MAX_DIST = 128
D_FF = 2816
EPS = 1e-6

LANES = 128
SUBLANES = 8
VMEM_LIMIT = 56 * 1024 * 1024

_SEG_QKV = (0, DN_QKV_WIDTH)
_SEG_Z = (_SEG_QKV[1], _SEG_QKV[1] + DN_WIDTH)
_SEG_SQ = (_SEG_Z[1], _SEG_Z[1] + SWA_WIDTH)
_SEG_SKV = (_SEG_SQ[1], _SEG_SQ[1] + 2 * SWA_KV_WIDTH)
_SEG_G = (_SEG_SKV[1], _SEG_SKV[1] + 2 * D_MODEL)
_SEG_BA = (_SEG_G[1], _SEG_G[1] + LANES)
_SEGS = (_SEG_QKV, _SEG_Z, _SEG_SQ, _SEG_SKV, _SEG_G, _SEG_BA)
D_IN_PAD = _SEG_BA[1]

F32 = jnp.float32
BF16 = jnp.bfloat16
HI = lax.Precision.HIGHEST


def _mm(a, b, dims=(((1,), (0,)), ((), ())), exact=False):
    if exact:
        return lax.dot_general(a.astype(F32), b.astype(F32), dims, precision=HI,
                               preferred_element_type=F32)
    return lax.dot_general(a.astype(BF16), b.astype(BF16), dims, preferred_element_type=F32)


_NT = (((1,), (1,)), ((), ()))
_TN = (((0,), (0,)), ((), ()))


def _sigmoid(x):
    return 1.0 / (1.0 + jnp.exp(-x))


def _silu(x):
    return x * _sigmoid(x)


def _in_proj_kernel(x_ref, gain_ref, w_ref, qkv_ref, z_ref, sq_ref, skv_ref, g_ref, ba_ref):
    x = x_ref[...]
    ms = jnp.mean(x * x, axis=-1, keepdims=True)
    h = (x * lax.rsqrt(ms + EPS) * gain_ref[...]).astype(BF16)
    for (lo, hi), o_ref in zip(_SEGS, (qkv_ref, z_ref, sq_ref, skv_ref, g_ref, ba_ref)):
        o_ref[...] = jnp.dot(h, w_ref[:, lo:hi], preferred_element_type=F32)


def _in_proj(x2, gain, w_all, tm):
    T = x2.shape[0]
    out_shape = tuple(jax.ShapeDtypeStruct((T, hi - lo), F32) for lo, hi in _SEGS)
    out_specs = tuple(pl.BlockSpec((tm, hi - lo), lambda i: (i, 0)) for lo, hi in _SEGS)
    return pl.pallas_call(
        _in_proj_kernel,
        grid=(T // tm,),
        in_specs=[
            pl.BlockSpec((tm, D_MODEL), lambda i: (i, 0)),
            pl.BlockSpec((1, D_MODEL), lambda i: (0, 0)),
            pl.BlockSpec((D_MODEL, D_IN_PAD), lambda i: (0, 0), pipeline_mode=pl.Buffered(1)),
        ],
        out_specs=out_specs,
        out_shape=out_shape,
        compiler_params=pltpu.CompilerParams(
            dimension_semantics=("arbitrary",), vmem_limit_bytes=VMEM_LIMIT),
        name="in_proj",
    )(x2, gain, w_all)


DN_TILE = 256
DN_NCH = DN_TILE // DN_CHUNK
DN_LOG_CHUNK = int(math.log2(DN_CHUNK))
DN_TAIL = SUBLANES


def _deltanet_kernel(qkv_ref, z_ref, ba_ref, convw_ref, alog_ref, dtb_ref, gain_ref, o_ref,
                     xbuf, state):
    t = pl.program_id(1)

    @pl.when(t == 0)
    def _():
        xbuf[0:DN_TAIL, :] = jnp.zeros((DN_TAIL, DN_QKV_WIDTH), F32)
        state[...] = jnp.zeros_like(state)

    @pl.when(t > 0)
    def _():
        xbuf[0:DN_TAIL, :] = xbuf[DN_TILE:DN_TILE + DN_TAIL, :]

    xbuf[DN_TAIL:DN_TAIL + DN_TILE, :] = qkv_ref[0]

    w = convw_ref[...]
    acc = w[DN_CONV - 1:DN_CONV, :] * xbuf[DN_TAIL:DN_TAIL + DN_TILE, :]
    for j in range(DN_CONV - 1):
        off = DN_TAIL - (DN_CONV - 1) + j
        acc = acc + w[j:j + 1, :] * xbuf[off:off + DN_TILE, :]
    act = _silu(acc)

    ba = ba_ref[0]
    beta = _sigmoid(ba)
    xs = ba + dtb_ref[...]
    softplus = jnp.maximum(xs, 0.0) + jnp.log1p(jnp.exp(-jnp.abs(xs)))
    g = -jnp.exp(alog_ref[...]) * softplus

    row = lax.broadcasted_iota(jnp.int32, (DN_TILE, DN_TILE), 0)
    col = lax.broadcasted_iota(jnp.int32, (DN_TILE, DN_TILE), 1)
    same_chunk = (row >> DN_LOG_CHUNK) == (col >> DN_LOG_CHUNK)
    lower_incl = same_chunk & (row >= col)
    strict_lower = same_chunk & (row > col)
    gc = _mm(jnp.where(lower_incl, 1.0, 0.0), g, exact=True)
    rev = _mm(jnp.where(same_chunk & (row < col), 1.0, 0.0), g, exact=True)
    gc_t = gc.T
    eye = jnp.where(row == col, 1.0, 0.0)

    scale = DN_HEAD_DIM ** -0.5
    heads = range(DN_HEADS)
    gcol = lambda arr, h: arr[:, DN_HEADS + h:DN_HEADS + h + 1]
    q, k, kb, rhs = [], [], [], []
    for h in heads:
        qh = act[:, h * DN_HEAD_DIM:(h + 1) * DN_HEAD_DIM]
        kh = act[:, DN_WIDTH + h * DN_HEAD_DIM:DN_WIDTH + (h + 1) * DN_HEAD_DIM]
        vh = act[:, 2 * DN_WIDTH + h * DN_HEAD_DIM:2 * DN_WIDTH + (h + 1) * DN_HEAD_DIM]
        qh = qh * (lax.rsqrt(jnp.sum(qh * qh, axis=-1, keepdims=True) + EPS) * scale)
        kh = kh * lax.rsqrt(jnp.sum(kh * kh, axis=-1, keepdims=True) + EPS)
        beta_c = beta[:, h:h + 1]
        q.append(qh)
        k.append(kh)
        kb.append(kh * beta_c)
        rhs.append(jnp.concatenate([vh * beta_c, kb[h] * jnp.exp(gcol(gc, h))], axis=1))
    gram = [_mm(jnp.concatenate([kb[h], q[h]], axis=0), k[h], _NT) for h in heads]
    qk, n_pow, inv = [], [], []
    for h in heads:
        gc_r = gc_t[DN_HEADS + h:DN_HEADS + h + 1, :]
        decay = jnp.exp(jnp.where(lower_incl, gcol(gc, h) - gc_r, -jnp.inf))
        qk.append(gram[h][DN_TILE:] * decay)
        n_pow.append(-jnp.where(strict_lower, gram[h][:DN_TILE] * decay, 0.0))
        inv.append(eye + n_pow[h])
    for _ in range(DN_LOG_CHUNK - 1):
        n_pow = [_mm(n_pow[h], n_pow[h]) for h in heads]
        inv = [inv[h] + _mm(inv[h], n_pow[h]) for h in heads]
    sol = [_mm(inv[h], rhs[h]) for h in heads]
    u = [sol[h][:, :DN_HEAD_DIM] for h in heads]
    wmat = [sol[h][:, DN_HEAD_DIM:] for h in heads]
    qe = [q[h] * jnp.exp(gcol(gc, h)) for h in heads]
    kdec = [k[h] * jnp.exp(gcol(rev, h)) for h in heads]

    s = [state[h] for h in heads]
    v_new = [[] for _ in heads]
    o_inter = [[] for _ in heads]
    for c in range(DN_NCH):
        r0, r1 = c * DN_CHUNK, (c + 1) * DN_CHUNK
        ws = [_mm(jnp.concatenate([wmat[h][r0:r1], qe[h][r0:r1]], axis=0), s[h]) for h in heads]
        for h in heads:
            v_new[h].append(u[h][r0:r1] - ws[h][:DN_CHUNK])
            o_inter[h].append(ws[h][DN_CHUNK:])
        upd = [_mm(kdec[h][r0:r1], v_new[h][c], _TN) for h in heads]
        s = [s[h] * jnp.exp(gc[r1 - 1:r1, DN_HEADS + h:DN_HEADS + h + 1]) + upd[h]
             for h in heads]
    o_intra = [_mm(qk[h], jnp.concatenate(v_new[h], axis=0)) for h in heads]
    for h in heads:
        state[h] = s[h]
        o = jnp.concatenate(o_inter[h], axis=0) + o_intra[h]
        o = o * lax.rsqrt(jnp.mean(o * o, axis=-1, keepdims=True) + EPS) * gain_ref[...]
        zh = z_ref[0, :, h * DN_HEAD_DIM:(h + 1) * DN_HEAD_DIM]
        o_ref[0, :, h * DN_HEAD_DIM:(h + 1) * DN_HEAD_DIM] = o * _silu(zh)


def _deltanet(qkv, z, ba, convw, alog_lane, dtb_lane, out_gain):
    B, S, _ = qkv.shape
    return pl.pallas_call(
        _deltanet_kernel,
        grid=(B, S // DN_TILE),
        in_specs=[
            pl.BlockSpec((1, DN_TILE, DN_QKV_WIDTH), lambda b, t: (b, t, 0)),
            pl.BlockSpec((1, DN_TILE, DN_WIDTH), lambda b, t: (b, t, 0)),
            pl.BlockSpec((1, DN_TILE, LANES), lambda b, t: (b, t, 0)),
            pl.BlockSpec((DN_CONV, DN_QKV_WIDTH), lambda b, t: (0, 0)),
            pl.BlockSpec((1, LANES), lambda b, t: (0, 0)),
            pl.BlockSpec((1, LANES), lambda b, t: (0, 0)),
            pl.BlockSpec((1, DN_HEAD_DIM), lambda b, t: (0, 0)),
        ],
        out_specs=pl.BlockSpec((1, DN_TILE, DN_WIDTH), lambda b, t: (b, t, 0)),
        out_shape=jax.ShapeDtypeStruct((B, S, DN_WIDTH), F32),
        scratch_shapes=[
            pltpu.VMEM((DN_TAIL + DN_TILE, DN_QKV_WIDTH), F32),
            pltpu.VMEM((DN_HEADS, DN_HEAD_DIM, DN_HEAD_DIM), F32),
        ],
        compiler_params=pltpu.CompilerParams(
            dimension_semantics=("arbitrary", "arbitrary"), vmem_limit_bytes=VMEM_LIMIT),
        name="deltanet",
    )(qkv, z, ba, convw, alog_lane, dtb_lane, out_gain)


SWA_BAND = 2 * SWA_BLOCK
SWA_SLOTS = SWA_WIDTH // LANES
SWA_HEADS_PER_SLOT = LANES // SWA_HEAD_DIM


def _swa_kernel(sinks_ref, sq_ref, kv_prev_ref, kv_cur_ref, bias_ref, qgain_ref, kgain_ref,
                seg_q_ref, seg_k_ref, o_ref):
    n = pl.program_id(1)
    q = sq_ref[0]
    q = q * lax.rsqrt(_mm(q * q, seg_q_ref[...], exact=True) + EPS) * qgain_ref[...]
    kv = jnp.concatenate([kv_prev_ref[0], kv_cur_ref[0]], axis=0)
    k = kv[:, :SWA_KV_WIDTH]
    v = kv[:, SWA_KV_WIDTH:]
    k = k * lax.rsqrt(_mm(k * k, seg_k_ref[...], exact=True) + EPS) * kgain_ref[...]

    lane = lax.broadcasted_iota(jnp.int32, (SWA_BAND, LANES), 1)
    low = lane < SWA_HEAD_DIM
    k_rot = pltpu.roll(k, SWA_HEAD_DIM, axis=1)
    v_rot = pltpu.roll(v, SWA_HEAD_DIM, axis=1)
    k_lo = (jnp.where(low, k, 0.0), jnp.where(low, k_rot, 0.0))
    k_hi = (jnp.where(low, 0.0, k_rot), jnp.where(low, 0.0, k))
    v_lo = (jnp.where(low, v, 0.0), jnp.where(low, v_rot, 0.0))
    v_hi = (jnp.where(low, 0.0, v_rot), jnp.where(low, 0.0, v))

    key_col = lax.broadcasted_iota(jnp.int32, (SWA_BLOCK, SWA_BAND), 1)
    key_valid = (key_col >= SWA_BLOCK) | (n > 0)
    out_low = lax.broadcasted_iota(jnp.int32, (SWA_BLOCK, LANES), 1) < SWA_HEAD_DIM

    slots_per_kv = SWA_SLOTS // SWA_KV_HEADS
    for j in range(SWA_SLOTS):
        kh = j // slots_per_kv
        k_bd = jnp.concatenate([k_lo[kh], k_hi[kh]], axis=0)
        v_bd = jnp.concatenate([v_lo[kh], v_hi[kh]], axis=0)
        logits = _mm(q[:, j * LANES:(j + 1) * LANES], k_bd, _NT)
        probs = []
        inv_den = []
        for r in range(SWA_HEADS_PER_SLOT):
            hd = j * SWA_HEADS_PER_SLOT + r
            lg = logits[:, r * SWA_BAND:(r + 1) * SWA_BAND] + bias_ref[hd]
            lg = jnp.where(key_valid, lg, -jnp.inf)
            sink = sinks_ref[hd]
            m = jnp.maximum(jnp.max(lg, axis=-1, keepdims=True), sink)
            p = jnp.exp(lg - m)
            den = jnp.sum(p, axis=-1, keepdims=True) + jnp.exp(sink - m)
            probs.append(p)
            inv_den.append(1.0 / den)
        out = _mm(jnp.concatenate(probs, axis=1), v_bd)
        o_ref[0, :, j * LANES:(j + 1) * LANES] = out * jnp.where(out_low, inv_den[0], inv_den[1])


def _swa(sq, skv, sinks, bias_tab, qgain, kgain, seg_q, seg_k):
    B, S, _ = sq.shape
    nb = S // SWA_BLOCK
    kvw = 2 * SWA_KV_WIDTH
    grid_spec = pltpu.PrefetchScalarGridSpec(
        num_scalar_prefetch=1,
        grid=(B, nb),
        in_specs=[
            pl.BlockSpec((1, SWA_BLOCK, SWA_WIDTH), lambda b, n, s: (b, n, 0)),
            pl.BlockSpec((1, SWA_BLOCK, kvw), lambda b, n, s: (b, jnp.maximum(n - 1, 0), 0)),
            pl.BlockSpec((1, SWA_BLOCK, kvw), lambda b, n, s: (b, n, 0)),
            pl.BlockSpec((SWA_HEADS, SWA_BLOCK, SWA_BAND), lambda b, n, s: (0, 0, 0)),
            pl.BlockSpec((1, SWA_WIDTH), lambda b, n, s: (0, 0)),
            pl.BlockSpec((1, SWA_KV_WIDTH), lambda b, n, s: (0, 0)),
            pl.BlockSpec((SWA_WIDTH, SWA_WIDTH), lambda b, n, s: (0, 0)),
            pl.BlockSpec((SWA_KV_WIDTH, SWA_KV_WIDTH), lambda b, n, s: (0, 0)),
        ],
        out_specs=pl.BlockSpec((1, SWA_BLOCK, SWA_WIDTH), lambda b, n, s: (b, n, 0)),
    )
    return pl.pallas_call(
        _swa_kernel,
        grid_spec=grid_spec,
        out_shape=jax.ShapeDtypeStruct((B, S, SWA_WIDTH), F32),
        compiler_params=pltpu.CompilerParams(
            dimension_semantics=("arbitrary", "arbitrary"), vmem_limit_bytes=VMEM_LIMIT),
        name="swa",
    )(sinks, sq, skv, skv, bias_tab, qgain, kgain, seg_q, seg_k)


def _merge_kernel(x_ref, ydn_ref, yswa_ref, g_ref, wa_ref, wb_ref, wo_ref, gain_ref,
                  x1_ref, h2_ref):
    ga = _sigmoid(g_ref[:, :D_MODEL])
    gb = _sigmoid(g_ref[:, D_MODEL:])
    merged = ga * _mm(ydn_ref[...], wa_ref[...]) + gb * _mm(yswa_ref[...], wb_ref[...])
    x1 = x_ref[...] + _mm(merged, wo_ref[...])
    x1_ref[...] = x1
    ms = jnp.mean(x1 * x1, axis=-1, keepdims=True)
    h2_ref[...] = (x1 * lax.rsqrt(ms + EPS) * gain_ref[...]).astype(BF16)


def _merge(x2, ydn, yswa, graw, wa, wb, wo, gain, tm):
    T = x2.shape[0]
    row = lambda w: pl.BlockSpec((tm, w), lambda i: (i, 0))
    full = lambda a, b: pl.BlockSpec((a, b), lambda i: (0, 0))
    return pl.pallas_call(
        _merge_kernel,
        grid=(T // tm,),
        in_specs=[row(D_MODEL), row(DN_WIDTH), row(SWA_WIDTH), row(2 * D_MODEL),
                  full(DN_WIDTH, D_MODEL), full(SWA_WIDTH, D_MODEL), full(D_MODEL, D_MODEL),
                  full(1, D_MODEL)],
        out_specs=(row(D_MODEL), row(D_MODEL)),
        out_shape=(jax.ShapeDtypeStruct((T, D_MODEL), F32),
                   jax.ShapeDtypeStruct((T, D_MODEL), BF16)),
        compiler_params=pltpu.CompilerParams(
            dimension_semantics=("arbitrary",), vmem_limit_bytes=VMEM_LIMIT),
        name="merge",
    )(x2, ydn, yswa, graw, wa, wb, wo, gain)


def _ffn_kernel(x1_ref, h2_ref, wg_ref, wu_ref, wd_ref, o_ref, acc_ref):
    j = pl.program_id(1)

    @pl.when(j == 0)
    def _():
        acc_ref[...] = x1_ref[...]

    h2 = h2_ref[...]
    gate = jnp.dot(h2, wg_ref[...], preferred_element_type=F32)
    up = jnp.dot(h2, wu_ref[...], preferred_element_type=F32)
    acc_ref[...] += _mm(_silu(gate) * up, wd_ref[...])

    @pl.when(j == pl.num_programs(1) - 1)
    def _():
        o_ref[...] = acc_ref[...]


def _ffn(x1, h2, wg, wu, wd, tm, tf):
    T = x1.shape[0]
    return pl.pallas_call(
        _ffn_kernel,
        grid=(T // tm, D_FF // tf),
        in_specs=[
            pl.BlockSpec((tm, D_MODEL), lambda i, j: (i, 0)),
            pl.BlockSpec((tm, D_MODEL), lambda i, j: (i, 0)),
            pl.BlockSpec((D_MODEL, tf), lambda i, j: (0, j)),
            pl.BlockSpec((D_MODEL, tf), lambda i, j: (0, j)),
            pl.BlockSpec((tf, D_MODEL), lambda i, j: (j, 0)),
        ],
        out_specs=pl.BlockSpec((tm, D_MODEL), lambda i, j: (i, 0)),
        out_shape=jax.ShapeDtypeStruct((T, D_MODEL), F32),
        scratch_shapes=[pltpu.VMEM((tm, D_MODEL), F32)],
        compiler_params=pltpu.CompilerParams(
            dimension_semantics=("arbitrary", "arbitrary"), vmem_limit_bytes=VMEM_LIMIT),
        name="ffn",
    )(x1, h2, wg, wu, wd)


def _t5_bucket_table():
    qi = jnp.arange(SWA_BLOCK)[:, None]
    kj = jnp.arange(SWA_BAND)[None, :]
    dist = SWA_BLOCK + qi - kj
    in_window = (dist >= 0) & (dist < WINDOW)
    n = jnp.maximum(dist, 0)
    max_exact = REL_BUCKETS // 2
    nf = jnp.maximum(n, 1).astype(F32)
    large = max_exact + (jnp.log(nf / max_exact) / math.log(REL_MAX_DIST / max_exact)
                         * (REL_BUCKETS - max_exact)).astype(jnp.int32)
    large = jnp.minimum(large, REL_BUCKETS - 1)
    return jnp.where(n < max_exact, n, large), in_window


def _segment_mean_matrix(width, seg):
    idx = np.arange(width) // seg
    return jnp.asarray((idx[:, None] == idx[None, :]).astype(np.float32) / seg)


def kernel(x, attn_norm, w_in, dn_conv, dn_a_log, dn_dt_bias, dn_out_norm, swa_q_norm, swa_k_norm, swa_sinks, rel_bias, w_branch_dn, w_branch_swa, w_out, ffn_norm, w_gate, w_up, w_down):
    B, S, D = x.shape
    T = B * S
    depth = w_in.shape[0]
    bucket, in_window = _t5_bucket_table()
    seg_q = _segment_mean_matrix(SWA_WIDTH, SWA_HEAD_DIM)
    seg_k = _segment_mean_matrix(SWA_KV_WIDTH, SWA_HEAD_DIM)
    rel = rel_bias.astype(F32)
    bias_tab = sum(jnp.where(bucket[None] == b, rel[b][:, None, None], 0.0)
                   for b in range(REL_BUCKETS))
    bias_tab = jnp.where(in_window[None], bias_tab, -jnp.inf)

    x2 = x.reshape(T, D)
    for l in range(depth):
        wl = w_in[l]
        o_qkv = 0
        o_z = o_qkv + DN_QKV_WIDTH
        o_b = o_z + DN_WIDTH
        o_sq = o_b + 2 * DN_HEADS
        o_skv = o_sq + SWA_WIDTH
        o_g = o_skv + 2 * SWA_KV_WIDTH
        w_all = jnp.concatenate([
            wl[:, o_qkv:o_z], wl[:, o_z:o_b], wl[:, o_sq:o_skv], wl[:, o_skv:o_g], wl[:, o_g:],
            wl[:, o_b:o_sq], jnp.zeros((D, LANES - 2 * DN_HEADS), wl.dtype)], axis=1).astype(BF16)

        qkv, z, sq, skv, graw, ba = _in_proj(x2, attn_norm[l][None, :], w_all, tm=512)

        pad_lo = jnp.zeros((DN_HEADS,), F32)
        pad_hi = jnp.zeros((LANES - 2 * DN_HEADS,), F32)
        alog_lane = jnp.concatenate([pad_lo, dn_a_log[l].astype(F32), pad_hi])[None, :]
        dtb_lane = jnp.concatenate([pad_lo, dn_dt_bias[l].astype(F32), pad_hi])[None, :]
        y_dn = _deltanet(qkv.reshape(B, S, -1), z.reshape(B, S, -1), ba.reshape(B, S, -1),
                         dn_conv[l], alog_lane, dtb_lane, dn_out_norm[l][None, :])

        qgain = jnp.tile(swa_q_norm[l].astype(F32), SWA_HEADS)[None, :] * (SWA_HEAD_DIM ** -0.5)
        kgain = jnp.tile(swa_k_norm[l].astype(F32), SWA_KV_HEADS)[None, :]
        y_swa = _swa(sq.reshape(B, S, -1), skv.reshape(B, S, -1), swa_sinks[l].astype(F32),
                     bias_tab, qgain, kgain, seg_q, seg_k)

        x1, h2 = _merge(x2, y_dn.reshape(T, -1), y_swa.reshape(T, -1), graw,
                        w_branch_dn[l].astype(BF16), w_branch_swa[l].astype(BF16),
                        w_out[l].astype(BF16), ffn_norm[l][None, :], tm=512)
        x2 = _ffn(x1, h2, w_gate[l].astype(BF16), w_up[l].astype(BF16), w_down[l].astype(BF16),
                  tm=512, tf=1408)
    return x2.reshape(B, S, D)
```

```python
import functools
import math

import numpy as np
import jax
import jax.numpy as jnp
from jax import lax
from jax.experimental import pallas as pl
from jax.experimental.pallas import tpu as pltpu

D_MODEL = 1024
DN_HEADS = 4
DN_HEAD_DIM = 128
DN_WIDTH = DN_HEADS * DN_HEAD_DIM
DN_QKV_WIDTH = 3 * DN_WIDTH
DN_CONV = 4
DN_CHUNK = 64
SWA_HEADS = 8
SWA_KV_HEADS = 2
SWA_HEAD_DIM = 64
SWA_WIDTH = SWA_HEADS * SWA_HEAD_DIM
SWA_KV_WIDTH = SWA_KV_HEADS * SWA_HEAD_DIM
WINDOW = 128
SWA_BLOCK = 128
REL_BUCKETS = 32
REL_MAX_DIST = 128
D_FF = 2816
EPS = 1e-6

LANES = 128
SUBLANES = 8
VMEM_LIMIT = 56 * 1024 * 1024

_SEG_QKV = (0, DN_QKV_WIDTH)
_SEG_Z = (_SEG_QKV[1], _SEG_QKV[1] + DN_WIDTH)
_SEG_SQ = (_SEG_Z[1], _SEG_Z[1] + SWA_WIDTH)
_SEG_SKV = (_SEG_SQ[1], _SEG_SQ[1] + 2 * SWA_KV_WIDTH)
_SEG_G = (_SEG_SKV[1], _SEG_SKV[1] + 2 * D_MODEL)
_SEG_BA = (_SEG_G[1], _SEG_G[1] + LANES)
_SEGS = (_SEG_QKV, _SEG_Z, _SEG_SQ, _SEG_SKV, _SEG_G, _SEG_BA)
D_IN_PAD = _SEG_BA[1]

F32 = jnp.float32
BF16 = jnp.bfloat16
HI = lax.Precision.HIGHEST


def _mm(a, b, dims=(((1,), (0,)), ((), ())), exact=False):
    if exact:
        return lax.dot_general(a.astype(F32), b.astype(F32), dims, precision=HI,
                               preferred_element_type=F32)
    return lax.dot_general(a.astype(BF16), b.astype(BF16), dims, preferred_element_type=F32)


_NT = (((1,), (1,)), ((), ()))
_TN = (((0,), (0,)), ((), ()))


def _sigmoid(x):
    return 1.0 / (1.0 + jnp.exp(-x))


def _silu(x):
    return x * _sigmoid(x)


def _in_proj_kernel(x_ref, gain_ref, w_ref, qkv_ref, z_ref, sq_ref, skv_ref, g_ref, ba_ref):
    x = x_ref[...]
    ms = jnp.mean(x * x, axis=-1, keepdims=True)
    h = (x * lax.rsqrt(ms + EPS) * gain_ref[...]).astype(BF16)
    for (lo, hi), o_ref in zip(_SEGS, (qkv_ref, z_ref, sq_ref, skv_ref, g_ref, ba_ref)):
        o_ref[...] = jnp.dot(h, w_ref[:, lo:hi], preferred_element_type=F32)


def _in_proj(x2, gain, w_all, tm):
    T = x2.shape[0]
    out_shape = tuple(jax.ShapeDtypeStruct((T, hi - lo), F32) for lo, hi in _SEGS)
    out_specs = tuple(pl.BlockSpec((tm, hi - lo), lambda i: (i, 0)) for lo, hi in _SEGS)
    return pl.pallas_call(
        _in_proj_kernel,
        grid=(T // tm,),
        in_specs=[
            pl.BlockSpec((tm, D_MODEL), lambda i: (i, 0)),
            pl.BlockSpec((1, D_MODEL), lambda i: (0, 0)),
            pl.BlockSpec((D_MODEL, D_IN_PAD), lambda i: (0, 0), pipeline_mode=pl.Buffered(1)),
        ],
        out_specs=out_specs,
        out_shape=out_shape,
        compiler_params=pltpu.CompilerParams(
            dimension_semantics=("arbitrary",), vmem_limit_bytes=VMEM_LIMIT),
        name="in_proj",
    )(x2, gain, w_all)


DN_TILE = 256
DN_NCH = DN_TILE // DN_CHUNK
DN_LOG_CHUNK = int(math.log2(DN_CHUNK))
DN_TAIL = SUBLANES


def _deltanet_kernel(qkv_ref, z_ref, ba_ref, convw_ref, alog_ref, dtb_ref, gain_ref, o_ref,
                     xbuf, state):
    t = pl.program_id(1)

    @pl.when(t == 0)
    def _():
        xbuf[0:DN_TAIL, :] = jnp.zeros((DN_TAIL, DN_QKV_WIDTH), F32)
        state[...] = jnp.zeros_like(state)

    @pl.when(t > 0)
    def _():
        xbuf[0:DN_TAIL, :] = xbuf[DN_TILE:DN_TILE + DN_TAIL, :]

    xbuf[DN_TAIL:DN_TAIL + DN_TILE, :] = qkv_ref[0]

    w = convw_ref[...]
    acc = w[DN_CONV - 1:DN_CONV, :] * xbuf[DN_TAIL:DN_TAIL + DN_TILE, :]
    for j in range(DN_CONV - 1):
        off = DN_TAIL - (DN_CONV - 1) + j
        acc = acc + w[j:j + 1, :] * xbuf[off:off + DN_TILE, :]
    act = _silu(acc)

    ba = ba_ref[0]
    beta = _sigmoid(ba)
    xs = ba + dtb_ref[...]
    softplus = jnp.maximum(xs, 0.0) + jnp.log1p(jnp.exp(-jnp.abs(xs)))
    g = -jnp.exp(alog_ref[...]) * softplus

    row = lax.broadcasted_iota(jnp.int32, (DN_TILE, DN_TILE), 0)
    col = lax.broadcasted_iota(jnp.int32, (DN_TILE, DN_TILE), 1)
    same_chunk = (row >> DN_LOG_CHUNK) == (col >> DN_LOG_CHUNK)
    lower_incl = same_chunk & (row >= col)
    strict_lower = same_chunk & (row > col)
    gc = _mm(jnp.where(lower_incl, 1.0, 0.0), g, exact=True)
    rev = _mm(jnp.where(same_chunk & (row < col), 1.0, 0.0), g, exact=True)
    gc_t = gc.T
    eye = jnp.where(row == col, 1.0, 0.0)

    scale = DN_HEAD_DIM ** -0.5
    heads = range(DN_HEADS)
    gcol = lambda arr, h: arr[:, DN_HEADS + h:DN_HEADS + h + 1]
    q, k, kb, rhs = [], [], [], []
    for h in heads:
        qh = act[:, h * DN_HEAD_DIM:(h + 1) * DN_HEAD_DIM]
        kh = act[:, DN_WIDTH + h * DN_HEAD_DIM:DN_WIDTH + (h + 1) * DN_HEAD_DIM]
        vh = act[:, 2 * DN_WIDTH + h * DN_HEAD_DIM:2 * DN_WIDTH + (h + 1) * DN_HEAD_DIM]
        qh = qh * (lax.rsqrt(jnp.sum(qh * qh, axis=-1, keepdims=True) + EPS) * scale)
        kh = kh * lax.rsqrt(jnp.sum(kh * kh, axis=-1, keepdims=True) + EPS)
        beta_c = beta[:, h:h + 1]
        q.append(qh)
        k.append(kh)
        kb.append(kh * beta_c)
        rhs.append(jnp.concatenate([vh * beta_c, kb[h] * jnp.exp(gcol(gc, h))], axis=1))
    gram = [_mm(jnp.concatenate([kb[h], q[h]], axis=0), k[h], _NT) for h in heads]
    qk, n_pow, inv = [], [], []
    for h in heads:
        gc_r = gc_t[DN_HEADS + h:DN_HEADS + h + 1, :]
        decay = jnp.exp(jnp.where(lower_incl, gcol(gc, h) - gc_r, -jnp.inf))
        qk.append(gram[h][DN_TILE:] * decay)
        n_pow.append(-jnp.where(strict_lower, gram[h][:DN_TILE] * decay, 0.0))
        inv.append(eye + n_pow[h])
    for _ in range(DN_LOG_CHUNK - 1):
        n_pow = [_mm(n_pow[h], n_pow[h]) for h in heads]
        inv = [inv[h] + _mm(inv[h], n_pow[h]) for h in heads]
    sol = [_mm(inv[h], rhs[h]) for h in heads]
    u = [sol[h][:, :DN_HEAD_DIM] for h in heads]
    wmat = [sol[h][:, DN_HEAD_DIM:] for h in heads]
    qe = [q[h] * jnp.exp(gcol(gc, h)) for h in heads]
    kdec = [k[h] * jnp.exp(gcol(rev, h)) for h in heads]

    s = [state[h] for h in heads]
    v_new = [[] for _ in heads]
    o_inter = [[] for _ in heads]
    for c in range(DN_NCH):
        r0, r1 = c * DN_CHUNK, (c + 1) * DN_CHUNK
        ws = [_mm(jnp.concatenate([wmat[h][r0:r1], qe[h][r0:r1]], axis=0), s[h]) for h in heads]
        for h in heads:
            v_new[h].append(u[h][r0:r1] - ws[h][:DN_CHUNK])
            o_inter[h].append(ws[h][DN_CHUNK:])
        upd = [_mm(kdec[h][r0:r1], v_new[h][c], _TN) for h in heads]
        s = [s[h] * jnp.exp(gc[r1 - 1:r1, DN_HEADS + h:DN_HEADS + h + 1]) + upd[h]
             for h in heads]
    o_intra = [_mm(qk[h], jnp.concatenate(v_new[h], axis=0)) for h in heads]
    for h in heads:
        state[h] = s[h]
        o = jnp.concatenate(o_inter[h], axis=0) + o_intra[h]
        o = o * lax.rsqrt(jnp.mean(o * o, axis=-1, keepdims=True) + EPS) * gain_ref[...]
        zh = z_ref[0, :, h * DN_HEAD_DIM:(h + 1) * DN_HEAD_DIM]
        o_ref[0, :, h * DN_HEAD_DIM:(h + 1) * DN_HEAD_DIM] = o * _silu(zh)


def _deltanet(qkv, z, ba, convw, alog_lane, dtb_lane, out_gain):
    B, S, _ = qkv.shape
    return pl.pallas_call(
        _deltanet_kernel,
        grid=(B, S // DN_TILE),
        in_specs=[
            pl.BlockSpec((1, DN_TILE, DN_QKV_WIDTH), lambda b, t: (b, t, 0)),
            pl.BlockSpec((1, DN_TILE, DN_WIDTH), lambda b, t: (b, t, 0)),
            pl.BlockSpec((1, DN_TILE, LANES), lambda b, t: (b, t, 0)),
            pl.BlockSpec((DN_CONV, DN_QKV_WIDTH), lambda b, t: (0, 0)),
            pl.BlockSpec((1, LANES), lambda b, t: (0, 0)),
            pl.BlockSpec((1, LANES), lambda b, t: (0, 0)),
            pl.BlockSpec((1, DN_HEAD_DIM), lambda b, t: (0, 0)),
        ],
        out_specs=pl.BlockSpec((1, DN_TILE, DN_WIDTH), lambda b, t: (b, t, 0)),
        out_shape=jax.ShapeDtypeStruct((B, S, DN_WIDTH), F32),
        scratch_shapes=[
            pltpu.VMEM((DN_TAIL + DN_TILE, DN_QKV_WIDTH), F32),
            pltpu.VMEM((DN_HEADS, DN_HEAD_DIM, DN_HEAD_DIM), F32),
        ],
        compiler_params=pltpu.CompilerParams(
            dimension_semantics=("arbitrary", "arbitrary"), vmem_limit_bytes=VMEM_LIMIT),
        name="deltanet",
    )(qkv, z, ba, convw, alog_lane, dtb_lane, out_gain)


SWA_BAND = 2 * SWA_BLOCK
SWA_SLOTS = SWA_WIDTH // LANES
SWA_HEADS_PER_SLOT = LANES // SWA_HEAD_DIM


def _split_mm(x, w):
    rows = x.shape[0]
    hi = x.astype(BF16)
    lo = (x - hi.astype(F32)).astype(BF16)
    r = jnp.dot(jnp.concatenate([hi, lo], axis=0), w, preferred_element_type=F32)
    return r[:rows] + r[rows:]


def _swa_kernel(sinks_ref, sq_ref, kv_ref, bias_ref, qgain_ref, kgain_ref, seg_ref, o_ref,
                band_ref):
    n = pl.program_id(1)
    q = sq_ref[0]
    kv = kv_ref[0]
    k = kv[:, :SWA_KV_WIDTH]
    parts = [q[:, j * LANES:(j + 1) * LANES] for j in range(SWA_SLOTS)] + [k]
    ms = _split_mm(jnp.concatenate([p * p for p in parts], axis=0), seg_ref[...])
    qn = [parts[j] * lax.rsqrt(ms[j * SWA_BLOCK:(j + 1) * SWA_BLOCK] + EPS)
          * qgain_ref[:, j * LANES:(j + 1) * LANES] for j in range(SWA_SLOTS)]
    kn = k * lax.rsqrt(ms[SWA_SLOTS * SWA_BLOCK:] + EPS) * kgain_ref[...]

    @pl.when(n == 0)
    def _():
        band_ref[0:SWA_BLOCK, :] = jnp.zeros((SWA_BLOCK, 2 * SWA_KV_WIDTH), F32)

    @pl.when(n > 0)
    def _():
        band_ref[0:SWA_BLOCK, :] = band_ref[SWA_BLOCK:SWA_BAND, :]

    band_ref[SWA_BLOCK:SWA_BAND, :SWA_KV_WIDTH] = kn
    band_ref[SWA_BLOCK:SWA_BAND, SWA_KV_WIDTH:] = kv[:, SWA_KV_WIDTH:]
    k = band_ref[:, :SWA_KV_WIDTH]
    v = band_ref[:, SWA_KV_WIDTH:]

    lane = lax.broadcasted_iota(jnp.int32, (SWA_BAND, LANES), 1)
    low = lane < SWA_HEAD_DIM
    k_rot = pltpu.roll(k, SWA_HEAD_DIM, axis=1)
    v_rot = pltpu.roll(v, SWA_HEAD_DIM, axis=1)
    k_lo = (jnp.where(low, k, 0.0), jnp.where(low, k_rot, 0.0))
    k_hi = (jnp.where(low, 0.0, k_rot), jnp.where(low, 0.0, k))
    v_lo = (jnp.where(low, v, 0.0), jnp.where(low, v_rot, 0.0))
    v_hi = (jnp.where(low, 0.0, v_rot), jnp.where(low, 0.0, v))

    out_low = lax.broadcasted_iota(jnp.int32, (SWA_BLOCK, LANES), 1) < SWA_HEAD_DIM

    slots_per_kv = SWA_SLOTS // SWA_KV_HEADS
    k_bd = [jnp.concatenate([k_lo[kh], k_hi[kh]], axis=0).astype(BF16)
            for kh in range(SWA_KV_HEADS)]
    v_bd = [jnp.concatenate([v_lo[kh], v_hi[kh]], axis=0).astype(BF16)
            for kh in range(SWA_KV_HEADS)]
    logits = [_mm(qn[j], k_bd[j // slots_per_kv], _NT) for j in range(SWA_SLOTS)]
    probs = [[] for _ in range(SWA_SLOTS)]
    inv_den = [[] for _ in range(SWA_SLOTS)]
    for j in range(SWA_SLOTS):
        for r in range(SWA_HEADS_PER_SLOT):
            hd = j * SWA_HEADS_PER_SLOT + r
            lg = logits[j][:, r * SWA_BAND:(r + 1) * SWA_BAND] + bias_ref[0, hd]
            sink = sinks_ref[hd]
            m = jnp.maximum(jnp.max(lg, axis=-1, keepdims=True), sink)
            p = jnp.exp(lg - m)
            den = jnp.sum(p, axis=-1, keepdims=True) + jnp.exp(sink - m)
            probs[j].append(p.astype(BF16))
            inv_den[j].append(1.0 / den)
    outs = [_mm(jnp.concatenate(probs[j], axis=1), v_bd[j // slots_per_kv])
            for j in range(SWA_SLOTS)]
    for j in range(SWA_SLOTS):
        o_ref[0, :, j * LANES:(j + 1) * LANES] = (
            outs[j] * jnp.where(out_low, inv_den[j][0], inv_den[j][1]))


def _swa(sq, skv, sinks, bias_tabs, qgain, kgain, seg):
    B, S, _ = sq.shape
    nb = S // SWA_BLOCK
    kvw = 2 * SWA_KV_WIDTH
    grid_spec = pltpu.PrefetchScalarGridSpec(
        num_scalar_prefetch=1,
        grid=(B, nb),
        in_specs=[
            pl.BlockSpec((1, SWA_BLOCK, SWA_WIDTH), lambda b, n, s: (b, n, 0)),
            pl.BlockSpec((1, SWA_BLOCK, kvw), lambda b, n, s: (b, n, 0)),
            pl.BlockSpec((1, SWA_HEADS, SWA_BLOCK, SWA_BAND),
                         lambda b, n, s: (jnp.minimum(n, 1), 0, 0, 0)),
            pl.BlockSpec((1, SWA_WIDTH), lambda b, n, s: (0, 0)),
            pl.BlockSpec((1, SWA_KV_WIDTH), lambda b, n, s: (0, 0)),
            pl.BlockSpec((LANES, LANES), lambda b, n, s: (0, 0)),
        ],
        out_specs=pl.BlockSpec((1, SWA_BLOCK, SWA_WIDTH), lambda b, n, s: (b, n, 0)),
        scratch_shapes=[pltpu.VMEM((SWA_BAND, kvw), F32)],
    )
    return pl.pallas_call(
        _swa_kernel,
        grid_spec=grid_spec,
        out_shape=jax.ShapeDtypeStruct((B, S, SWA_WIDTH), F32),
        compiler_params=pltpu.CompilerParams(
            dimension_semantics=("arbitrary", "arbitrary"), vmem_limit_bytes=VMEM_LIMIT),
        name="swa",
    )(sinks, sq, skv, bias_tabs, qgain, kgain, seg)


def _merge_kernel(x_ref, ydn_ref, yswa_ref, g_ref, wa_ref, wb_ref, wo_ref, gain_ref,
                  x1_ref, h2_ref):
    ga = _sigmoid(g_ref[:, :D_MODEL])
    gb = _sigmoid(g_ref[:, D_MODEL:])
    merged = ga * _mm(ydn_ref[...], wa_ref[...]) + gb * _mm(yswa_ref[...], wb_ref[...])
    x1 = x_ref[...] + _mm(merged, wo_ref[...])
    x1_ref[...] = x1
    ms = jnp.mean(x1 * x1, axis=-1, keepdims=True)
    h2_ref[...] = (x1 * lax.rsqrt(ms + EPS) * gain_ref[...]).astype(BF16)


def _merge(x2, ydn, yswa, graw, wa, wb, wo, gain, tm):
    T = x2.shape[0]
    row = lambda w: pl.BlockSpec((tm, w), lambda i: (i, 0))
    full = lambda a, b: pl.BlockSpec((a, b), lambda i: (0, 0))
    return pl.pallas_call(
        _merge_kernel,
        grid=(T // tm,),
        in_specs=[row(D_MODEL), row(DN_WIDTH), row(SWA_WIDTH), row(2 * D_MODEL),
                  full(DN_WIDTH, D_MODEL), full(SWA_WIDTH, D_MODEL), full(D_MODEL, D_MODEL),
                  full(1, D_MODEL)],
        out_specs=(row(D_MODEL), row(D_MODEL)),
        out_shape=(jax.ShapeDtypeStruct((T, D_MODEL), F32),
                   jax.ShapeDtypeStruct((T, D_MODEL), BF16)),
        compiler_params=pltpu.CompilerParams(
            dimension_semantics=("arbitrary",), vmem_limit_bytes=VMEM_LIMIT),
        name="merge",
    )(x2, ydn, yswa, graw, wa, wb, wo, gain)


def _ffn_kernel(x1_ref, h2_ref, wg_ref, wu_ref, wd_ref, o_ref, acc_ref):
    j = pl.program_id(1)

    @pl.when(j == 0)
    def _():
        acc_ref[...] = x1_ref[...]

    h2 = h2_ref[...]
    gate = jnp.dot(h2, wg_ref[...], preferred_element_type=F32)
    up = jnp.dot(h2, wu_ref[...], preferred_element_type=F32)
    acc_ref[...] += _mm(_silu(gate) * up, wd_ref[...])

    @pl.when(j == pl.num_programs(1) - 1)
    def _():
        o_ref[...] = acc_ref[...]


def _ffn(x1, h2, wg, wu, wd, tm, tf):
    T = x1.shape[0]
    return pl.pallas_call(
        _ffn_kernel,
        grid=(T // tm, D_FF // tf),
        in_specs=[
            pl.BlockSpec((tm, D_MODEL), lambda i, j: (i, 0)),
            pl.BlockSpec((tm, D_MODEL), lambda i, j: (i, 0)),
            pl.BlockSpec((D_MODEL, tf), lambda i, j: (0, j)),
            pl.BlockSpec((D_MODEL, tf), lambda i, j: (0, j)),
            pl.BlockSpec((tf, D_MODEL), lambda i, j: (j, 0)),
        ],
        out_specs=pl.BlockSpec((tm, D_MODEL), lambda i, j: (i, 0)),
        out_shape=jax.ShapeDtypeStruct((T, D_MODEL), F32),
        scratch_shapes=[pltpu.VMEM((tm, D_MODEL), F32)],
        compiler_params=pltpu.CompilerParams(
            dimension_semantics=("arbitrary", "arbitrary"), vmem_limit_bytes=VMEM_LIMIT),
        name="ffn",
    )(x1, h2, wg, wu, wd)


def _t5_bucket_table():
    qi = jnp.arange(SWA_BLOCK)[:, None]
    kj = jnp.arange(SWA_BAND)[None, :]
    dist = SWA_BLOCK + qi - kj
    in_window = (dist >= 0) & (dist < WINDOW)
    n = jnp.maximum(dist, 0)
    max_exact = REL_BUCKETS // 2
    nf = jnp.maximum(n, 1).astype(F32)
    large = max_exact + (jnp.log(nf / max_exact) / math.log(REL_MAX_DIST / max_exact)
                         * (REL_BUCKETS - max_exact)).astype(jnp.int32)
    large = jnp.minimum(large, REL_BUCKETS - 1)
    return jnp.where(n < max_exact, n, large), in_window


def _segment_mean_matrix(width, seg):
    idx = np.arange(width) // seg
    return jnp.asarray((idx[:, None] == idx[None, :]).astype(np.float32) / seg, dtype=BF16)


def kernel(x, attn_norm, w_in, dn_conv, dn_a_log, dn_dt_bias, dn_out_norm, swa_q_norm, swa_k_norm, swa_sinks, rel_bias, w_branch_dn, w_branch_swa, w_out, ffn_norm, w_gate, w_up, w_down):
    B, S, D = x.shape
    T = B * S
    depth = w_in.shape[0]
    bucket, in_window = _t5_bucket_table()
    seg = _segment_mean_matrix(LANES, SWA_HEAD_DIM)
    rel = rel_bias.astype(F32)
    bias_tab = sum(jnp.where(bucket[None] == b, rel[b][:, None, None], 0.0)
                   for b in range(REL_BUCKETS))
    bias_tab = jnp.where(in_window[None], bias_tab, -jnp.inf)
    has_prev = jnp.arange(SWA_BAND)[None, None, :] >= SWA_BLOCK
    bias_tabs = jnp.stack([jnp.where(has_prev, bias_tab, -jnp.inf), bias_tab])

    x2 = x.reshape(T, D)
    for l in range(depth):
        wl = w_in[l]
        o_qkv = 0
        o_z = o_qkv + DN_QKV_WIDTH
        o_b = o_z + DN_WIDTH
        o_sq = o_b + 2 * DN_HEADS
        o_skv = o_sq + SWA_WIDTH
        o_g = o_skv + 2 * SWA_KV_WIDTH
        w_all = jnp.concatenate([
            wl[:, o_qkv:o_z], wl[:, o_z:o_b], wl[:, o_sq:o_skv], wl[:, o_skv:o_g], wl[:, o_g:],
            wl[:, o_b:o_sq], jnp.zeros((D, LANES - 2 * DN_HEADS), wl.dtype)], axis=1).astype(BF16)

        qkv, z, sq, skv, graw, ba = _in_proj(x2, attn_norm[l][None, :], w_all, tm=512)

        pad_lo = jnp.zeros((DN_HEADS,), F32)
        pad_hi = jnp.zeros((LANES - 2 * DN_HEADS,), F32)
        alog_lane = jnp.concatenate([pad_lo, dn_a_log[l].astype(F32), pad_hi])[None, :]
        dtb_lane = jnp.concatenate([pad_lo, dn_dt_bias[l].astype(F32), pad_hi])[None, :]
        y_dn = _deltanet(qkv.reshape(B, S, -1), z.reshape(B, S, -1), ba.reshape(B, S, -1),
                         dn_conv[l], alog_lane, dtb_lane, dn_out_norm[l][None, :])

        qgain = jnp.tile(swa_q_norm[l].astype(F32), SWA_HEADS)[None, :] * (SWA_HEAD_DIM ** -0.5)
        kgain = jnp.tile(swa_k_norm[l].astype(F32), SWA_KV_HEADS)[None, :]
        y_swa = _swa(sq.reshape(B, S, -1), skv.reshape(B, S, -1), swa_sinks[l].astype(F32),
                     bias_tabs, qgain, kgain, seg)

        x1, h2 = _merge(x2, y_dn.reshape(T, -1), y_swa.reshape(T, -1), graw,
                        w_branch_dn[l].astype(BF16), w_branch_swa[l].astype(BF16),
                        w_out[l].astype(BF16), ffn_norm[l][None, :], tm=512)
        x2 = _ffn(x1, h2, w_gate[l].astype(BF16), w_up[l].astype(BF16), w_down[l].astype(BF16),
                  tm=512, tf=1408)
    return x2.reshape(B, S, D)
```

```python
import functools
import math

import numpy as np
import jax
import jax.numpy as jnp
from jax import lax
from jax.experimental import pallas as pl
from jax.experimental.pallas import tpu as pltpu

D_MODEL = 1024
DN_HEADS = 4
DN_HEAD_DIM = 128
DN_WIDTH = DN_HEADS * DN_HEAD_DIM
DN_QKV_WIDTH = 3 * DN_WIDTH
DN_CONV = 4
DN_CHUNK = 64
SWA_HEADS = 8
SWA_KV_HEADS = 2
SWA_HEAD_DIM = 64
SWA_WIDTH = SWA_HEADS * SWA_HEAD_DIM
SWA_KV_WIDTH = SWA_KV_HEADS * SWA_HEAD_DIM
WINDOW = 128
SWA_BLOCK = 128
REL_BUCKETS = 32
REL_MAX_DIST = 128
D_FF = 2816
EPS = 1e-6

LANES = 128
SUBLANES = 8
MXU_WIDTH = 256
VMEM_LIMIT = 56 * 1024 * 1024

_SEG_QKV = (0, DN_QKV_WIDTH)
_SEG_Z = (_SEG_QKV[1], _SEG_QKV[1] + DN_WIDTH)
_SEG_SQ = (_SEG_Z[1], _SEG_Z[1] + SWA_WIDTH)
_SEG_SKV = (_SEG_SQ[1], _SEG_SQ[1] + 2 * SWA_KV_WIDTH)
_SEG_G = (_SEG_SKV[1], _SEG_SKV[1] + 2 * D_MODEL)
_SEG_BA = (_SEG_G[1], _SEG_G[1] + LANES)
_SEGS = (_SEG_QKV, _SEG_Z, _SEG_SQ, _SEG_SKV, _SEG_G, _SEG_BA)
D_IN_PAD = _SEG_BA[1]

F32 = jnp.float32
BF16 = jnp.bfloat16


def _mm(a, b, dims=(((1,), (0,)), ((), ()))):
    return lax.dot_general(a.astype(BF16), b.astype(BF16), dims, preferred_element_type=F32)


_NT = (((1,), (1,)), ((), ()))
_TN = (((0,), (0,)), ((), ()))


def _sigmoid(x):
    return 1.0 / (1.0 + jnp.exp(-x))


def _silu(x):
    return x * _sigmoid(x)


CONV_TAIL = SUBLANES
CONV_ROWS = 128


def _in_proj_kernel(x_ref, gain_ref, w_ref, convw_ref, qkv_ref, z_ref, sq_ref, skv_ref, g_ref,
                    ba_ref, xbuf, *, tm, tiles_per_seq):
    i = pl.program_id(0)

    @pl.when(i == 0)
    def _():
        xbuf[tm:tm + CONV_TAIL, :] = jnp.zeros((CONV_TAIL, DN_QKV_WIDTH), F32)

    x = x_ref[...]
    ms = jnp.mean(x * x, axis=-1, keepdims=True)
    h = (x * lax.rsqrt(ms + EPS) * gain_ref[...]).astype(BF16)
    tail = xbuf[tm:tm + CONV_TAIL, :]
    xbuf[0:CONV_TAIL, :] = jnp.where(i % tiles_per_seq == 0, 0.0, tail)
    lo, hi = _SEG_QKV
    xbuf[CONV_TAIL:CONV_TAIL + tm, :] = jnp.dot(h, w_ref[:, lo:hi], preferred_element_type=F32)

    scale = DN_HEAD_DIM ** -0.5

    def preprocess(c):
        c0, c1 = c * DN_HEAD_DIM, (c + 1) * DN_HEAD_DIM
        w = convw_ref[:, c0:c1]
        for r0 in range(0, tm, CONV_ROWS):
            acc = w[DN_CONV - 1:DN_CONV, :] * xbuf[CONV_TAIL + r0:CONV_TAIL + r0 + CONV_ROWS, c0:c1]
            for j in range(DN_CONV - 1):
                off = CONV_TAIL - (DN_CONV - 1) + j + r0
                acc = acc + w[j:j + 1, :] * xbuf[off:off + CONV_ROWS, c0:c1]
            act = _silu(acc)
            if c0 < 2 * DN_WIDTH:
                norm = lax.rsqrt(jnp.sum(act * act, axis=-1, keepdims=True) + EPS)
                act = act * (norm * scale if c0 < DN_WIDTH else norm)
            qkv_ref[r0:r0 + CONV_ROWS, c0:c1] = act.astype(BF16)

    pieces = []
    for (lo, hi), o_ref in zip(_SEGS[1:], (z_ref, sq_ref, skv_ref, g_ref, ba_ref)):
        for p0 in range(lo, hi, MXU_WIDTH):
            pieces.append((o_ref, p0 - lo, min(p0 + MXU_WIDTH, hi) - lo, lo))
    n_blocks = DN_QKV_WIDTH // DN_HEAD_DIM
    for idx, (o_ref, p0, p1, base) in enumerate(pieces):
        o_ref[:, p0:p1] = jnp.dot(h, w_ref[:, base + p0:base + p1], preferred_element_type=F32)
        if idx < n_blocks:
            preprocess(idx)
    assert len(pieces) >= n_blocks


def _in_proj(x2, gain, w_all, convw, tm, seq_len):
    T = x2.shape[0]
    dtypes = (BF16, F32, F32, F32, F32, F32)
    out_shape = tuple(jax.ShapeDtypeStruct((T, hi - lo), dt) for (lo, hi), dt in zip(_SEGS, dtypes))
    out_specs = tuple(pl.BlockSpec((tm, hi - lo), lambda i: (i, 0)) for lo, hi in _SEGS)
    return pl.pallas_call(
        functools.partial(_in_proj_kernel, tm=tm, tiles_per_seq=seq_len // tm),
        grid=(T // tm,),
        in_specs=[
            pl.BlockSpec((tm, D_MODEL), lambda i: (i, 0)),
            pl.BlockSpec((1, D_MODEL), lambda i: (0, 0)),
            pl.BlockSpec((D_MODEL, D_IN_PAD), lambda i: (0, 0), pipeline_mode=pl.Buffered(1)),
            pl.BlockSpec((DN_CONV, DN_QKV_WIDTH), lambda i: (0, 0)),
        ],
        out_specs=out_specs,
        out_shape=out_shape,
        scratch_shapes=[pltpu.VMEM((CONV_TAIL + tm, DN_QKV_WIDTH), F32)],
        compiler_params=pltpu.CompilerParams(
            dimension_semantics=("arbitrary",), vmem_limit_bytes=VMEM_LIMIT),
        name="in_proj",
    )(x2, gain, w_all, convw)


DN_TILE = 256
DN_NCH = DN_TILE // DN_CHUNK
DN_LOG_CHUNK = int(math.log2(DN_CHUNK))


def _deltanet_kernel(qkv_ref, z_ref, ba_ref, alog_ref, dtb_ref, gain_ref, o_ref, state):
    t = pl.program_id(1)

    @pl.when(t == 0)
    def _():
        state[...] = jnp.zeros_like(state)

    ba = ba_ref[0]
    beta = _sigmoid(ba)
    xs = ba + dtb_ref[...]
    softplus = jnp.maximum(xs, 0.0) + jnp.log1p(jnp.exp(-jnp.abs(xs)))
    g = -jnp.exp(alog_ref[...]) * softplus

    row = lax.broadcasted_iota(jnp.int32, (DN_TILE, DN_TILE), 0)
    col = lax.broadcasted_iota(jnp.int32, (DN_TILE, DN_TILE), 1)
    same_chunk = (row >> DN_LOG_CHUNK) == (col >> DN_LOG_CHUNK)
    lower_incl = same_chunk & (row >= col)
    strict_lower = same_chunk & (row > col)
    g_hi = g.astype(BF16)
    g_lo = (g - g_hi.astype(F32)).astype(BF16)
    gc2 = jnp.dot(jnp.where(lower_incl, 1.0, 0.0).astype(BF16),
                  jnp.concatenate([g_hi, g_lo], axis=1), preferred_element_type=F32)
    gc = gc2[:, :LANES] + gc2[:, LANES:]
    g_last = jnp.concatenate(
        [jnp.broadcast_to(gc[(c + 1) * DN_CHUNK - 1:(c + 1) * DN_CHUNK, :], (DN_CHUNK, LANES))
         for c in range(DN_NCH)], axis=0)
    rev = g_last - gc
    gc_t = gc.T
    eye = jnp.where(row == col, 1.0, 0.0)

    heads = range(DN_HEADS)
    gcol = lambda arr, h: arr[:, DN_HEADS + h:DN_HEADS + h + 1]
    q, k, kb, rhs = [], [], [], []
    for h in heads:
        qh = qkv_ref[0, :, h * DN_HEAD_DIM:(h + 1) * DN_HEAD_DIM].astype(F32)
        kh = qkv_ref[0, :, DN_WIDTH + h * DN_HEAD_DIM:DN_WIDTH + (h + 1) * DN_HEAD_DIM].astype(F32)
        vh = qkv_ref[0, :, 2 * DN_WIDTH + h * DN_HEAD_DIM:2 * DN_WIDTH + (h + 1) * DN_HEAD_DIM].astype(F32)
        beta_c = beta[:, h:h + 1]
        q.append(qh)
        k.append(kh)
        kb.append(kh * beta_c)
        rhs.append(jnp.concatenate([vh * beta_c, kb[h] * jnp.exp(gcol(gc, h))], axis=1))
    gram = [_mm(jnp.concatenate([kb[h], q[h]], axis=0), k[h], _NT) for h in heads]
    qk, n_pow, inv = [], [], []
    for h in heads:
        gc_r = gc_t[DN_HEADS + h:DN_HEADS + h + 1, :]
        decay = jnp.exp(jnp.where(lower_incl, gcol(gc, h) - gc_r, -jnp.inf))
        qk.append(gram[h][DN_TILE:] * decay)
        n_pow.append(-jnp.where(strict_lower, gram[h][:DN_TILE] * decay, 0.0))
        inv.append(eye + n_pow[h])
    for _ in range(DN_LOG_CHUNK - 1):
        n_pow = [_mm(n_pow[h], n_pow[h]) for h in heads]
        inv = [inv[h] + _mm(inv[h], n_pow[h]) for h in heads]
    sol = [_mm(inv[h], rhs[h]) for h in heads]
    u = [sol[h][:, :DN_HEAD_DIM] for h in heads]
    wmat = [sol[h][:, DN_HEAD_DIM:] for h in heads]
    qe = [q[h] * jnp.exp(gcol(gc, h)) for h in heads]
    kdec = [k[h] * jnp.exp(gcol(rev, h)) for h in heads]

    s = [state[h] for h in heads]
    v_new = [[] for _ in heads]
    o_inter = [[] for _ in heads]
    for c in range(DN_NCH):
        r0, r1 = c * DN_CHUNK, (c + 1) * DN_CHUNK
        ws = [_mm(jnp.concatenate([wmat[h][r0:r1], qe[h][r0:r1]], axis=0), s[h]) for h in heads]
        for h in heads:
            v_new[h].append(u[h][r0:r1] - ws[h][:DN_CHUNK])
            o_inter[h].append(ws[h][DN_CHUNK:])
        upd = [_mm(kdec[h][r0:r1], v_new[h][c], _TN) for h in heads]
        s = [s[h] * jnp.exp(gc[r1 - 1:r1, DN_HEADS + h:DN_HEADS + h + 1]) + upd[h]
             for h in heads]
    o_intra = [_mm(qk[h], jnp.concatenate(v_new[h], axis=0)) for h in heads]
    for h in heads:
        state[h] = s[h]
        o = jnp.concatenate(o_inter[h], axis=0) + o_intra[h]
        o = o * lax.rsqrt(jnp.mean(o * o, axis=-1, keepdims=True) + EPS) * gain_ref[...]
        zh = z_ref[0, :, h * DN_HEAD_DIM:(h + 1) * DN_HEAD_DIM]
        o_ref[0, :, h * DN_HEAD_DIM:(h + 1) * DN_HEAD_DIM] = o * _silu(zh)


def _deltanet(qkv, z, ba, alog_lane, dtb_lane, out_gain):
    B, S, _ = qkv.shape
    return pl.pallas_call(
        _deltanet_kernel,
        grid=(B, S // DN_TILE),
        in_specs=[
            pl.BlockSpec((1, DN_TILE, DN_QKV_WIDTH), lambda b, t: (b, t, 0)),
            pl.BlockSpec((1, DN_TILE, DN_WIDTH), lambda b, t: (b, t, 0)),
            pl.BlockSpec((1, DN_TILE, LANES), lambda b, t: (b, t, 0)),
            pl.BlockSpec((1, LANES), lambda b, t: (0, 0)),
            pl.BlockSpec((1, LANES), lambda b, t: (0, 0)),
            pl.BlockSpec((1, DN_HEAD_DIM), lambda b, t: (0, 0)),
        ],
        out_specs=pl.BlockSpec((1, DN_TILE, DN_WIDTH), lambda b, t: (b, t, 0)),
        out_shape=jax.ShapeDtypeStruct((B, S, DN_WIDTH), F32),
        scratch_shapes=[pltpu.VMEM((DN_HEADS, DN_HEAD_DIM, DN_HEAD_DIM), F32)],
        compiler_params=pltpu.CompilerParams(
            dimension_semantics=("arbitrary", "arbitrary"), vmem_limit_bytes=VMEM_LIMIT),
        name="deltanet",
    )(qkv, z, ba, alog_lane, dtb_lane, out_gain)


SWA_BAND = 2 * SWA_BLOCK
SWA_SLOTS = SWA_WIDTH // LANES
SWA_HEADS_PER_SLOT = LANES // SWA_HEAD_DIM


def _split_mm(x, w):
    rows = x.shape[0]
    hi = x.astype(BF16)
    lo = (x - hi.astype(F32)).astype(BF16)
    r = jnp.dot(jnp.concatenate([hi, lo], axis=0), w, preferred_element_type=F32)
    return r[:rows] + r[rows:]


def _swa_kernel(sinks_ref, sq_ref, kv_ref, bias_ref, qgain_ref, kgain_ref, seg_ref, o_ref,
                band_ref):
    n = pl.program_id(1)
    q = sq_ref[0]
    kv = kv_ref[0]
    k = kv[:, :SWA_KV_WIDTH]
    parts = [q[:, j * LANES:(j + 1) * LANES] for j in range(SWA_SLOTS)] + [k]
    ms = _split_mm(jnp.concatenate([p * p for p in parts], axis=0), seg_ref[...])
    qn = [parts[j] * lax.rsqrt(ms[j * SWA_BLOCK:(j + 1) * SWA_BLOCK] + EPS)
          * qgain_ref[:, j * LANES:(j + 1) * LANES] for j in range(SWA_SLOTS)]
    kn = k * lax.rsqrt(ms[SWA_SLOTS * SWA_BLOCK:] + EPS) * kgain_ref[...]

    @pl.when(n == 0)
    def _():
        band_ref[0:SWA_BLOCK, :] = jnp.zeros((SWA_BLOCK, 2 * SWA_KV_WIDTH), F32)

    @pl.when(n > 0)
    def _():
        band_ref[0:SWA_BLOCK, :] = band_ref[SWA_BLOCK:SWA_BAND, :]

    band_ref[SWA_BLOCK:SWA_BAND, :SWA_KV_WIDTH] = kn
    band_ref[SWA_BLOCK:SWA_BAND, SWA_KV_WIDTH:] = kv[:, SWA_KV_WIDTH:]
    k = band_ref[:, :SWA_KV_WIDTH]
    v = band_ref[:, SWA_KV_WIDTH:]

    lane = lax.broadcasted_iota(jnp.int32, (SWA_BAND, LANES), 1)
    low = lane < SWA_HEAD_DIM
    k_rot = pltpu.roll(k, SWA_HEAD_DIM, axis=1)
    v_rot = pltpu.roll(v, SWA_HEAD_DIM, axis=1)
    k_lo = (jnp.where(low, k, 0.0), jnp.where(low, k_rot, 0.0))
    k_hi = (jnp.where(low, 0.0, k_rot), jnp.where(low, 0.0, k))
    v_lo = (jnp.where(low, v, 0.0), jnp.where(low, v_rot, 0.0))
    v_hi = (jnp.where(low, 0.0, v_rot), jnp.where(low, 0.0, v))

    out_low = lax.broadcasted_iota(jnp.int32, (SWA_BLOCK, LANES), 1) < SWA_HEAD_DIM

    slots_per_kv = SWA_SLOTS // SWA_KV_HEADS
    k_bd = [jnp.concatenate([k_lo[kh], k_hi[kh]], axis=0).astype(BF16)
            for kh in range(SWA_KV_HEADS)]
    v_bd = [jnp.concatenate([v_lo[kh], v_hi[kh]], axis=0).astype(BF16)
            for kh in range(SWA_KV_HEADS)]
    logits = [_mm(qn[j], k_bd[j // slots_per_kv], _NT) for j in range(SWA_SLOTS)]
    probs = [[] for _ in range(SWA_SLOTS)]
    inv_den = [[] for _ in range(SWA_SLOTS)]
    for j in range(SWA_SLOTS):
        for r in range(SWA_HEADS_PER_SLOT):
            hd = j * SWA_HEADS_PER_SLOT + r
            lg = logits[j][:, r * SWA_BAND:(r + 1) * SWA_BAND] + bias_ref[0, hd]
            sink = sinks_ref[hd]
            m = jnp.maximum(jnp.max(lg, axis=-1, keepdims=True), sink)
            p = jnp.exp(lg - m)
            den = jnp.sum(p, axis=-1, keepdims=True) + jnp.exp(sink - m)
            probs[j].append(p.astype(BF16))
            inv_den[j].append(1.0 / den)
    outs = [_mm(jnp.concatenate(probs[j], axis=1), v_bd[j // slots_per_kv])
            for j in range(SWA_SLOTS)]
    for j in range(SWA_SLOTS):
        o_ref[0, :, j * LANES:(j + 1) * LANES] = (
            outs[j] * jnp.where(out_low, inv_den[j][0], inv_den[j][1]))


def _swa(sq, skv, sinks, bias_tabs, qgain, kgain, seg):
    B, S, _ = sq.shape
    nb = S // SWA_BLOCK
    kvw = 2 * SWA_KV_WIDTH
    grid_spec = pltpu.PrefetchScalarGridSpec(
        num_scalar_prefetch=1,
        grid=(B, nb),
        in_specs=[
            pl.BlockSpec((1, SWA_BLOCK, SWA_WIDTH), lambda b, n, s: (b, n, 0)),
            pl.BlockSpec((1, SWA_BLOCK, kvw), lambda b, n, s: (b, n, 0)),
            pl.BlockSpec((1, SWA_HEADS, SWA_BLOCK, SWA_BAND),
                         lambda b, n, s: (jnp.minimum(n, 1), 0, 0, 0)),
            pl.BlockSpec((1, SWA_WIDTH), lambda b, n, s: (0, 0)),
            pl.BlockSpec((1, SWA_KV_WIDTH), lambda b, n, s: (0, 0)),
            pl.BlockSpec((LANES, LANES), lambda b, n, s: (0, 0)),
        ],
        out_specs=pl.BlockSpec((1, SWA_BLOCK, SWA_WIDTH), lambda b, n, s: (b, n, 0)),
        scratch_shapes=[pltpu.VMEM((SWA_BAND, kvw), F32)],
    )
    return pl.pallas_call(
        _swa_kernel,
        grid_spec=grid_spec,
        out_shape=jax.ShapeDtypeStruct((B, S, SWA_WIDTH), F32),
        compiler_params=pltpu.CompilerParams(
            dimension_semantics=("arbitrary", "arbitrary"), vmem_limit_bytes=VMEM_LIMIT),
        name="swa",
    )(sinks, sq, skv, bias_tabs, qgain, kgain, seg)


def _merge_kernel(x_ref, ydn_ref, yswa_ref, g_ref, wa_ref, wb_ref, wo_ref, gain_ref,
                  x1_ref, h2_ref):
    ga = _sigmoid(g_ref[:, :D_MODEL])
    gb = _sigmoid(g_ref[:, D_MODEL:])
    merged = ga * _mm(ydn_ref[...], wa_ref[...]) + gb * _mm(yswa_ref[...], wb_ref[...])
    x1 = x_ref[...] + _mm(merged, wo_ref[...])
    x1_ref[...] = x1
    ms = jnp.mean(x1 * x1, axis=-1, keepdims=True)
    h2_ref[...] = (x1 * lax.rsqrt(ms + EPS) * gain_ref[...]).astype(BF16)


def _merge(x2, ydn, yswa, graw, wa, wb, wo, gain, tm):
    T = x2.shape[0]
    row = lambda w: pl.BlockSpec((tm, w), lambda i: (i, 0))
    full = lambda a, b: pl.BlockSpec((a, b), lambda i: (0, 0))
    return pl.pallas_call(
        _merge_kernel,
        grid=(T // tm,),
        in_specs=[row(D_MODEL), row(DN_WIDTH), row(SWA_WIDTH), row(2 * D_MODEL),
                  full(DN_WIDTH, D_MODEL), full(SWA_WIDTH, D_MODEL), full(D_MODEL, D_MODEL),
                  full(1, D_MODEL)],
        out_specs=(row(D_MODEL), row(D_MODEL)),
        out_shape=(jax.ShapeDtypeStruct((T, D_MODEL), F32),
                   jax.ShapeDtypeStruct((T, D_MODEL), BF16)),
        compiler_params=pltpu.CompilerParams(
            dimension_semantics=("arbitrary",), vmem_limit_bytes=VMEM_LIMIT),
        name="merge",
    )(x2, ydn, yswa, graw, wa, wb, wo, gain)


def _ffn_kernel(x1_ref, h2_ref, wg_ref, wu_ref, wd_ref, o_ref, acc_ref):
    j = pl.program_id(1)

    @pl.when(j == 0)
    def _():
        acc_ref[...] = x1_ref[...]

    h2 = h2_ref[...]
    gate = jnp.dot(h2, wg_ref[...], preferred_element_type=F32)
    up = jnp.dot(h2, wu_ref[...], preferred_element_type=F32)
    acc_ref[...] += _mm(_silu(gate) * up, wd_ref[...])

    @pl.when(j == pl.num_programs(1) - 1)
    def _():
        o_ref[...] = acc_ref[...]


def _ffn(x1, h2, wg, wu, wd, tm, tf):
    T = x1.shape[0]
    return pl.pallas_call(
        _ffn_kernel,
        grid=(T // tm, D_FF // tf),
        in_specs=[
            pl.BlockSpec((tm, D_MODEL), lambda i, j: (i, 0)),
            pl.BlockSpec((tm, D_MODEL), lambda i, j: (i, 0)),
            pl.BlockSpec((D_MODEL, tf), lambda i, j: (0, j)),
            pl.BlockSpec((D_MODEL, tf), lambda i, j: (0, j)),
            pl.BlockSpec((tf, D_MODEL), lambda i, j: (j, 0)),
        ],
        out_specs=pl.BlockSpec((tm, D_MODEL), lambda i, j: (i, 0)),
        out_shape=jax.ShapeDtypeStruct((T, D_MODEL), F32),
        scratch_shapes=[pltpu.VMEM((tm, D_MODEL), F32)],
        compiler_params=pltpu.CompilerParams(
            dimension_semantics=("arbitrary", "arbitrary"), vmem_limit_bytes=VMEM_LIMIT),
        name="ffn",
    )(x1, h2, wg, wu, wd)


def _t5_bucket_table():
    qi = jnp.arange(SWA_BLOCK)[:, None]
    kj = jnp.arange(SWA_BAND)[None, :]
    dist = SWA_BLOCK + qi - kj
    in_window = (dist >= 0) & (dist < WINDOW)
    n = jnp.maximum(dist, 0)
    max_exact = REL_BUCKETS // 2
    nf = jnp.maximum(n, 1).astype(F32)
    large = max_exact + (jnp.log(nf / max_exact) / math.log(REL_MAX_DIST / max_exact)
                         * (REL_BUCKETS - max_exact)).astype(jnp.int32)
    large = jnp.minimum(large, REL_BUCKETS - 1)
    return jnp.where(n < max_exact, n, large), in_window


def _segment_mean_matrix(width, seg):
    idx = np.arange(width) // seg
    return jnp.asarray((idx[:, None] == idx[None, :]).astype(np.float32) / seg, dtype=BF16)


def kernel(x, attn_norm, w_in, dn_conv, dn_a_log, dn_dt_bias, dn_out_norm, swa_q_norm, swa_k_norm, swa_sinks, rel_bias, w_branch_dn, w_branch_swa, w_out, ffn_norm, w_gate, w_up, w_down):
    B, S, D = x.shape
    T = B * S
    depth = w_in.shape[0]
    bucket, in_window = _t5_bucket_table()
    seg = _segment_mean_matrix(LANES, SWA_HEAD_DIM)
    rel = rel_bias.astype(F32)
    bias_tab = sum(jnp.where(bucket[None] == b, rel[b][:, None, None], 0.0)
                   for b in range(REL_BUCKETS))
    bias_tab = jnp.where(in_window[None], bias_tab, -jnp.inf)
    has_prev = jnp.arange(SWA_BAND)[None, None, :] >= SWA_BLOCK
    bias_tabs = jnp.stack([jnp.where(has_prev, bias_tab, -jnp.inf), bias_tab])

    x2 = x.reshape(T, D)
    for l in range(depth):
        wl = w_in[l]
        o_qkv = 0
        o_z = o_qkv + DN_QKV_WIDTH
        o_b = o_z + DN_WIDTH
        o_sq = o_b + 2 * DN_HEADS
        o_skv = o_sq + SWA_WIDTH
        o_g = o_skv + 2 * SWA_KV_WIDTH
        w_all = jnp.concatenate([
            wl[:, o_qkv:o_z], wl[:, o_z:o_b], wl[:, o_sq:o_skv], wl[:, o_skv:o_g], wl[:, o_g:],
            wl[:, o_b:o_sq], jnp.zeros((D, LANES - 2 * DN_HEADS), wl.dtype)], axis=1).astype(BF16)

        qkv, z, sq, skv, graw, ba = _in_proj(x2, attn_norm[l][None, :], w_all, dn_conv[l],
                                             tm=512, seq_len=S)

        pad_lo = jnp.zeros((DN_HEADS,), F32)
        pad_hi = jnp.zeros((LANES - 2 * DN_HEADS,), F32)
        alog_lane = jnp.concatenate([pad_lo, dn_a_log[l].astype(F32), pad_hi])[None, :]
        dtb_lane = jnp.concatenate([pad_lo, dn_dt_bias[l].astype(F32), pad_hi])[None, :]
        y_dn = _deltanet(qkv.reshape(B, S, -1), z.reshape(B, S, -1), ba.reshape(B, S, -1),
                         alog_lane, dtb_lane, dn_out_norm[l][None, :])

        qgain = jnp.tile(swa_q_norm[l].astype(F32), SWA_HEADS)[None, :] * (SWA_HEAD_DIM ** -0.5)
        kgain = jnp.tile(swa_k_norm[l].astype(F32), SWA_KV_HEADS)[None, :]
        y_swa = _swa(sq.reshape(B, S, -1), skv.reshape(B, S, -1), swa_sinks[l].astype(F32),
                     bias_tabs, qgain, kgain, seg)

        x1, h2 = _merge(x2, y_dn.reshape(T, -1), y_swa.reshape(T, -1), graw,
                        w_branch_dn[l].astype(BF16), w_branch_swa[l].astype(BF16),
                        w_out[l].astype(BF16), ffn_norm[l][None, :], tm=512)
        x2 = _ffn(x1, h2, w_gate[l].astype(BF16), w_up[l].astype(BF16), w_down[l].astype(BF16),
                  tm=512, tf=1408)
    return x2.reshape(B, S, D)
```

```python
import functools
import math

import numpy as np
import jax
import jax.numpy as jnp
from jax import lax
from jax.experimental import pallas as pl
from jax.experimental.pallas import tpu as pltpu

D_MODEL = 1024
DN_HEADS = 4
DN_HEAD_DIM = 128
DN_WIDTH = DN_HEADS * DN_HEAD_DIM
DN_QKV_WIDTH = 3 * DN_WIDTH
DN_CONV = 4
DN_CHUNK = 64
SWA_HEADS = 8
SWA_KV_HEADS = 2
SWA_HEAD_DIM = 64
SWA_WIDTH = SWA_HEADS * SWA_HEAD_DIM
SWA_KV_WIDTH = SWA_KV_HEADS * SWA_HEAD_DIM
WINDOW = 128
SWA_BLOCK = 128
REL_BUCKETS = 32
REL_MAX_DIST = 128
D_FF = 2816
EPS = 1e-6

LANES = 128
SUBLANES = 8
MXU_WIDTH = 256
VMEM_LIMIT = 56 * 1024 * 1024

_SEG_QKV = (0, DN_QKV_WIDTH)
_SEG_Z = (_SEG_QKV[1], _SEG_QKV[1] + DN_WIDTH)
_SEG_SQ = (_SEG_Z[1], _SEG_Z[1] + SWA_WIDTH)
_SEG_SKV = (_SEG_SQ[1], _SEG_SQ[1] + 2 * SWA_KV_WIDTH)
_SEG_G = (_SEG_SKV[1], _SEG_SKV[1] + 2 * D_MODEL)
_SEG_BA = (_SEG_G[1], _SEG_G[1] + LANES)
_SEGS = (_SEG_QKV, _SEG_Z, _SEG_SQ, _SEG_SKV, _SEG_G, _SEG_BA)
D_IN_PAD = _SEG_BA[1]

F32 = jnp.float32
BF16 = jnp.bfloat16


def _mm(a, b, dims=(((1,), (0,)), ((), ()))):
    return lax.dot_general(a.astype(BF16), b.astype(BF16), dims, preferred_element_type=F32)


_NT = (((1,), (1,)), ((), ()))
_TN = (((0,), (0,)), ((), ()))


def _sigmoid(x):
    return 1.0 / (1.0 + jnp.exp(-x))


def _silu(x):
    return x * _sigmoid(x)


CONV_TAIL = SUBLANES
CONV_ROWS = 128


def _in_proj_kernel(x_ref, gain_ref, w_ref, convw_ref, qkv_ref, z_ref, sq_ref, skv_ref, g_ref,
                    ba_ref, xbuf, *, tm, tiles_per_seq):
    i = pl.program_id(0)

    @pl.when(i == 0)
    def _():
        xbuf[tm:tm + CONV_TAIL, :] = jnp.zeros((CONV_TAIL, DN_QKV_WIDTH), F32)

    x = x_ref[...]
    ms = jnp.mean(x * x, axis=-1, keepdims=True)
    h = (x * lax.rsqrt(ms + EPS) * gain_ref[...]).astype(BF16)
    tail = xbuf[tm:tm + CONV_TAIL, :]
    xbuf[0:CONV_TAIL, :] = jnp.where(i % tiles_per_seq == 0, 0.0, tail)
    lo, hi = _SEG_QKV
    xbuf[CONV_TAIL:CONV_TAIL + tm, :] = jnp.dot(h, w_ref[:, lo:hi], preferred_element_type=F32)

    scale = DN_HEAD_DIM ** -0.5

    def preprocess(c):
        c0, c1 = c * DN_HEAD_DIM, (c + 1) * DN_HEAD_DIM
        w = convw_ref[:, c0:c1]
        for r0 in range(0, tm, CONV_ROWS):
            acc = w[DN_CONV - 1:DN_CONV, :] * xbuf[CONV_TAIL + r0:CONV_TAIL + r0 + CONV_ROWS, c0:c1]
            for j in range(DN_CONV - 1):
                off = CONV_TAIL - (DN_CONV - 1) + j + r0
                acc = acc + w[j:j + 1, :] * xbuf[off:off + CONV_ROWS, c0:c1]
            act = _silu(acc)
            if c0 < 2 * DN_WIDTH:
                norm = lax.rsqrt(jnp.sum(act * act, axis=-1, keepdims=True) + EPS)
                act = act * (norm * scale if c0 < DN_WIDTH else norm)
            qkv_ref[r0:r0 + CONV_ROWS, c0:c1] = act.astype(BF16)

    pieces = []
    for (lo, hi), o_ref in zip(_SEGS[1:], (z_ref, sq_ref, skv_ref, g_ref, ba_ref)):
        for p0 in range(lo, hi, MXU_WIDTH):
            pieces.append((o_ref, p0 - lo, min(p0 + MXU_WIDTH, hi) - lo, lo))
    n_blocks = DN_QKV_WIDTH // DN_HEAD_DIM
    for idx, (o_ref, p0, p1, base) in enumerate(pieces):
        o_ref[:, p0:p1] = jnp.dot(h, w_ref[:, base + p0:base + p1],
                                  preferred_element_type=F32).astype(o_ref.dtype)
        if idx < n_blocks:
            preprocess(idx)
    assert len(pieces) >= n_blocks


def _in_proj(x2, gain, w_all, convw, tm, seq_len):
    T = x2.shape[0]
    dtypes = (BF16, F32, F32, F32, BF16, F32)
    out_shape = tuple(jax.ShapeDtypeStruct((T, hi - lo), dt) for (lo, hi), dt in zip(_SEGS, dtypes))
    out_specs = tuple(pl.BlockSpec((tm, hi - lo), lambda i: (i, 0)) for lo, hi in _SEGS)
    return pl.pallas_call(
        functools.partial(_in_proj_kernel, tm=tm, tiles_per_seq=seq_len // tm),
        grid=(T // tm,),
        in_specs=[
            pl.BlockSpec((tm, D_MODEL), lambda i: (i, 0)),
            pl.BlockSpec((1, D_MODEL), lambda i: (0, 0)),
            pl.BlockSpec((D_MODEL, D_IN_PAD), lambda i: (0, 0), pipeline_mode=pl.Buffered(1)),
            pl.BlockSpec((DN_CONV, DN_QKV_WIDTH), lambda i: (0, 0)),
        ],
        out_specs=out_specs,
        out_shape=out_shape,
        scratch_shapes=[pltpu.VMEM((CONV_TAIL + tm, DN_QKV_WIDTH), F32)],
        compiler_params=pltpu.CompilerParams(
            dimension_semantics=("arbitrary",), vmem_limit_bytes=VMEM_LIMIT),
        name="in_proj",
    )(x2, gain, w_all, convw)


DN_TILE = 256
DN_NCH = DN_TILE // DN_CHUNK
DN_LOG_CHUNK = int(math.log2(DN_CHUNK))


def _deltanet_kernel(qkv_ref, z_ref, ba_ref, alog_ref, dtb_ref, gain_ref, o_ref, state):
    t = pl.program_id(1)

    @pl.when(t == 0)
    def _():
        state[...] = jnp.zeros_like(state)

    ba = ba_ref[0]
    beta = _sigmoid(ba)
    xs = ba + dtb_ref[...]
    softplus = jnp.maximum(xs, 0.0) + jnp.log1p(jnp.exp(-jnp.abs(xs)))
    g = -jnp.exp(alog_ref[...]) * softplus

    row = lax.broadcasted_iota(jnp.int32, (DN_TILE, DN_TILE), 0)
    col = lax.broadcasted_iota(jnp.int32, (DN_TILE, DN_TILE), 1)
    same_chunk = (row >> DN_LOG_CHUNK) == (col >> DN_LOG_CHUNK)
    lower_incl = same_chunk & (row >= col)
    strict_lower = same_chunk & (row > col)
    g_hi = g.astype(BF16)
    g_lo = (g - g_hi.astype(F32)).astype(BF16)
    gc2 = jnp.dot(jnp.where(lower_incl, 1.0, 0.0).astype(BF16),
                  jnp.concatenate([g_hi, g_lo], axis=1), preferred_element_type=F32)
    gc = gc2[:, :LANES] + gc2[:, LANES:]
    g_last = jnp.concatenate(
        [jnp.broadcast_to(gc[(c + 1) * DN_CHUNK - 1:(c + 1) * DN_CHUNK, :], (DN_CHUNK, LANES))
         for c in range(DN_NCH)], axis=0)
    rev = g_last - gc
    gc_t = gc.T
    eye = jnp.where(row == col, 1.0, 0.0)

    heads = range(DN_HEADS)
    gcol = lambda arr, h: arr[:, DN_HEADS + h:DN_HEADS + h + 1]
    q, k, kb, rhs = [], [], [], []
    for h in heads:
        qh = qkv_ref[0, :, h * DN_HEAD_DIM:(h + 1) * DN_HEAD_DIM].astype(F32)
        kh = qkv_ref[0, :, DN_WIDTH + h * DN_HEAD_DIM:DN_WIDTH + (h + 1) * DN_HEAD_DIM].astype(F32)
        vh = qkv_ref[0, :, 2 * DN_WIDTH + h * DN_HEAD_DIM:2 * DN_WIDTH + (h + 1) * DN_HEAD_DIM].astype(F32)
        beta_c = beta[:, h:h + 1]
        q.append(qh)
        k.append(kh)
        kb.append(kh * beta_c)
        rhs.append(jnp.concatenate([vh * beta_c, kb[h] * jnp.exp(gcol(gc, h))], axis=1))
    gram = [_mm(jnp.concatenate([kb[h], q[h]], axis=0), k[h], _NT) for h in heads]
    qk, n_pow, inv = [], [], []
    for h in heads:
        gc_r = gc_t[DN_HEADS + h:DN_HEADS + h + 1, :]
        decay = jnp.exp(jnp.where(lower_incl, gcol(gc, h) - gc_r, -jnp.inf))
        qk.append(gram[h][DN_TILE:] * decay)
        n_pow.append(-jnp.where(strict_lower, gram[h][:DN_TILE] * decay, 0.0))
        inv.append(eye + n_pow[h])
    for _ in range(DN_LOG_CHUNK - 1):
        n_pow = [_mm(n_pow[h], n_pow[h]) for h in heads]
        inv = [inv[h] + _mm(inv[h], n_pow[h]) for h in heads]
    sol = [_mm(inv[h], rhs[h]) for h in heads]
    u = [sol[h][:, :DN_HEAD_DIM] for h in heads]
    wmat = [sol[h][:, DN_HEAD_DIM:] for h in heads]
    qe = [q[h] * jnp.exp(gcol(gc, h)) for h in heads]
    kdec = [k[h] * jnp.exp(gcol(rev, h)) for h in heads]

    s = [state[h] for h in heads]
    v_new = [[] for _ in heads]
    o_inter = [[] for _ in heads]
    for c in range(DN_NCH):
        r0, r1 = c * DN_CHUNK, (c + 1) * DN_CHUNK
        ws = [_mm(jnp.concatenate([wmat[h][r0:r1], qe[h][r0:r1]], axis=0), s[h]) for h in heads]
        for h in heads:
            v_new[h].append(u[h][r0:r1] - ws[h][:DN_CHUNK])
            o_inter[h].append(ws[h][DN_CHUNK:])
        upd = [_mm(kdec[h][r0:r1], v_new[h][c], _TN) for h in heads]
        s = [s[h] * jnp.exp(gc[r1 - 1:r1, DN_HEADS + h:DN_HEADS + h + 1]) + upd[h]
             for h in heads]
    o_intra = [_mm(qk[h], jnp.concatenate(v_new[h], axis=0)) for h in heads]
    for h in heads:
        state[h] = s[h]
        o = jnp.concatenate(o_inter[h], axis=0) + o_intra[h]
        o = o * lax.rsqrt(jnp.mean(o * o, axis=-1, keepdims=True) + EPS) * gain_ref[...]
        zh = z_ref[0, :, h * DN_HEAD_DIM:(h + 1) * DN_HEAD_DIM]
        o_ref[0, :, h * DN_HEAD_DIM:(h + 1) * DN_HEAD_DIM] = (o * _silu(zh)).astype(o_ref.dtype)


def _deltanet(qkv, z, ba, alog_lane, dtb_lane, out_gain):
    B, S, _ = qkv.shape
    return pl.pallas_call(
        _deltanet_kernel,
        grid=(B, S // DN_TILE),
        in_specs=[
            pl.BlockSpec((1, DN_TILE, DN_QKV_WIDTH), lambda b, t: (b, t, 0)),
            pl.BlockSpec((1, DN_TILE, DN_WIDTH), lambda b, t: (b, t, 0)),
            pl.BlockSpec((1, DN_TILE, LANES), lambda b, t: (b, t, 0)),
            pl.BlockSpec((1, LANES), lambda b, t: (0, 0)),
            pl.BlockSpec((1, LANES), lambda b, t: (0, 0)),
            pl.BlockSpec((1, DN_HEAD_DIM), lambda b, t: (0, 0)),
        ],
        out_specs=pl.BlockSpec((1, DN_TILE, DN_WIDTH), lambda b, t: (b, t, 0)),
        out_shape=jax.ShapeDtypeStruct((B, S, DN_WIDTH), BF16),
        scratch_shapes=[pltpu.VMEM((DN_HEADS, DN_HEAD_DIM, DN_HEAD_DIM), F32)],
        compiler_params=pltpu.CompilerParams(
            dimension_semantics=("arbitrary", "arbitrary"), vmem_limit_bytes=VMEM_LIMIT),
        name="deltanet",
    )(qkv, z, ba, alog_lane, dtb_lane, out_gain)


SWA_BAND = 2 * SWA_BLOCK
SWA_SLOTS = SWA_WIDTH // LANES
SWA_HEADS_PER_SLOT = LANES // SWA_HEAD_DIM


def _split_mm(x, w):
    rows = x.shape[0]
    hi = x.astype(BF16)
    lo = (x - hi.astype(F32)).astype(BF16)
    r = jnp.dot(jnp.concatenate([hi, lo], axis=0), w, preferred_element_type=F32)
    return r[:rows] + r[rows:]


def _swa_kernel(sinks_ref, sq_ref, kv_ref, bias_ref, qgain_ref, kgain_ref, seg_ref, o_ref,
                band_ref):
    n = pl.program_id(1)
    q = sq_ref[0]
    kv = kv_ref[0]
    k = kv[:, :SWA_KV_WIDTH]
    parts = [q[:, j * LANES:(j + 1) * LANES] for j in range(SWA_SLOTS)] + [k]
    ms = _split_mm(jnp.concatenate([p * p for p in parts], axis=0), seg_ref[...])
    qn = [parts[j] * lax.rsqrt(ms[j * SWA_BLOCK:(j + 1) * SWA_BLOCK] + EPS)
          * qgain_ref[:, j * LANES:(j + 1) * LANES] for j in range(SWA_SLOTS)]
    kn = k * lax.rsqrt(ms[SWA_SLOTS * SWA_BLOCK:] + EPS) * kgain_ref[...]

    @pl.when(n == 0)
    def _():
        band_ref[0:SWA_BLOCK, :] = jnp.zeros((SWA_BLOCK, 2 * SWA_KV_WIDTH), F32)

    @pl.when(n > 0)
    def _():
        band_ref[0:SWA_BLOCK, :] = band_ref[SWA_BLOCK:SWA_BAND, :]

    band_ref[SWA_BLOCK:SWA_BAND, :SWA_KV_WIDTH] = kn
    band_ref[SWA_BLOCK:SWA_BAND, SWA_KV_WIDTH:] = kv[:, SWA_KV_WIDTH:]
    k = band_ref[:, :SWA_KV_WIDTH]
    v = band_ref[:, SWA_KV_WIDTH:]

    lane = lax.broadcasted_iota(jnp.int32, (SWA_BAND, LANES), 1)
    low = lane < SWA_HEAD_DIM
    k_rot = pltpu.roll(k, SWA_HEAD_DIM, axis=1)
    v_rot = pltpu.roll(v, SWA_HEAD_DIM, axis=1)
    k_lo = (jnp.where(low, k, 0.0), jnp.where(low, k_rot, 0.0))
    k_hi = (jnp.where(low, 0.0, k_rot), jnp.where(low, 0.0, k))
    v_lo = (jnp.where(low, v, 0.0), jnp.where(low, v_rot, 0.0))
    v_hi = (jnp.where(low, 0.0, v_rot), jnp.where(low, 0.0, v))

    out_low = lax.broadcasted_iota(jnp.int32, (SWA_BLOCK, LANES), 1) < SWA_HEAD_DIM

    slots_per_kv = SWA_SLOTS // SWA_KV_HEADS
    k_bd = [jnp.concatenate([k_lo[kh], k_hi[kh]], axis=0).astype(BF16)
            for kh in range(SWA_KV_HEADS)]
    v_bd = [jnp.concatenate([v_lo[kh], v_hi[kh]], axis=0).astype(BF16)
            for kh in range(SWA_KV_HEADS)]
    logits = [_mm(qn[j], k_bd[j // slots_per_kv], _NT) for j in range(SWA_SLOTS)]
    probs = [[] for _ in range(SWA_SLOTS)]
    inv_den = [[] for _ in range(SWA_SLOTS)]
    for j in range(SWA_SLOTS):
        for r in range(SWA_HEADS_PER_SLOT):
            hd = j * SWA_HEADS_PER_SLOT + r
            lg = logits[j][:, r * SWA_BAND:(r + 1) * SWA_BAND] + bias_ref[0, hd]
            sink = sinks_ref[hd]
            m = jnp.maximum(jnp.max(lg, axis=-1, keepdims=True), sink)
            p = jnp.exp(lg - m)
            den = jnp.sum(p, axis=-1, keepdims=True) + jnp.exp(sink - m)
            probs[j].append(p.astype(BF16))
            inv_den[j].append(1.0 / den)
    outs = [_mm(jnp.concatenate(probs[j], axis=1), v_bd[j // slots_per_kv])
            for j in range(SWA_SLOTS)]
    for j in range(SWA_SLOTS):
        o_ref[0, :, j * LANES:(j + 1) * LANES] = (
            outs[j] * jnp.where(out_low, inv_den[j][0], inv_den[j][1])).astype(o_ref.dtype)


def _swa(sq, skv, sinks, bias_tabs, qgain, kgain, seg):
    B, S, _ = sq.shape
    nb = S // SWA_BLOCK
    kvw = 2 * SWA_KV_WIDTH
    grid_spec = pltpu.PrefetchScalarGridSpec(
        num_scalar_prefetch=1,
        grid=(B, nb),
        in_specs=[
            pl.BlockSpec((1, SWA_BLOCK, SWA_WIDTH), lambda b, n, s: (b, n, 0)),
            pl.BlockSpec((1, SWA_BLOCK, kvw), lambda b, n, s: (b, n, 0)),
            pl.BlockSpec((1, SWA_HEADS, SWA_BLOCK, SWA_BAND),
                         lambda b, n, s: (jnp.minimum(n, 1), 0, 0, 0)),
            pl.BlockSpec((1, SWA_WIDTH), lambda b, n, s: (0, 0)),
            pl.BlockSpec((1, SWA_KV_WIDTH), lambda b, n, s: (0, 0)),
            pl.BlockSpec((LANES, LANES), lambda b, n, s: (0, 0)),
        ],
        out_specs=pl.BlockSpec((1, SWA_BLOCK, SWA_WIDTH), lambda b, n, s: (b, n, 0)),
        scratch_shapes=[pltpu.VMEM((SWA_BAND, kvw), F32)],
    )
    return pl.pallas_call(
        _swa_kernel,
        grid_spec=grid_spec,
        out_shape=jax.ShapeDtypeStruct((B, S, SWA_WIDTH), BF16),
        compiler_params=pltpu.CompilerParams(
            dimension_semantics=("arbitrary", "arbitrary"), vmem_limit_bytes=VMEM_LIMIT),
        name="swa",
    )(sinks, sq, skv, bias_tabs, qgain, kgain, seg)


def _merge_ffn_kernel(x_ref, ydn_ref, yswa_ref, g_ref, wa_ref, wb_ref, wo_ref, gain_ref,
                      wg_ref, wu_ref, wd_ref, o_ref, acc_ref, h2_ref):
    j = pl.program_id(1)

    @pl.when(j == 0)
    def _():
        ga = _sigmoid(g_ref[:, :D_MODEL].astype(F32))
        gb = _sigmoid(g_ref[:, D_MODEL:].astype(F32))
        merged = ga * _mm(ydn_ref[...], wa_ref[...]) + gb * _mm(yswa_ref[...], wb_ref[...])
        x1 = x_ref[...] + _mm(merged, wo_ref[...])
        acc_ref[...] = x1
        ms = jnp.mean(x1 * x1, axis=-1, keepdims=True)
        h2_ref[...] = (x1 * lax.rsqrt(ms + EPS) * gain_ref[...]).astype(BF16)

    h2 = h2_ref[...]
    gate = jnp.dot(h2, wg_ref[...], preferred_element_type=F32)
    up = jnp.dot(h2, wu_ref[...], preferred_element_type=F32)
    acc_ref[...] += _mm(_silu(gate) * up, wd_ref[...])

    @pl.when(j == pl.num_programs(1) - 1)
    def _():
        o_ref[...] = acc_ref[...]


def _merge_ffn(x2, ydn, yswa, graw, wa, wb, wo, gain, wg, wu, wd, tm, tf):
    T = x2.shape[0]
    row = lambda w: pl.BlockSpec((tm, w), lambda i, j: (i, 0))
    once = lambda a, b: pl.BlockSpec((a, b), lambda i, j: (0, 0), pipeline_mode=pl.Buffered(1))
    return pl.pallas_call(
        _merge_ffn_kernel,
        grid=(T // tm, D_FF // tf),
        in_specs=[row(D_MODEL), row(DN_WIDTH), row(SWA_WIDTH), row(2 * D_MODEL),
                  once(DN_WIDTH, D_MODEL), once(SWA_WIDTH, D_MODEL), once(D_MODEL, D_MODEL),
                  once(1, D_MODEL),
                  pl.BlockSpec((D_MODEL, tf), lambda i, j: (0, j)),
                  pl.BlockSpec((D_MODEL, tf), lambda i, j: (0, j)),
                  pl.BlockSpec((tf, D_MODEL), lambda i, j: (j, 0))],
        out_specs=row(D_MODEL),
        out_shape=jax.ShapeDtypeStruct((T, D_MODEL), F32),
        scratch_shapes=[pltpu.VMEM((tm, D_MODEL), F32), pltpu.VMEM((tm, D_MODEL), BF16)],
        compiler_params=pltpu.CompilerParams(
            dimension_semantics=("arbitrary", "arbitrary"), vmem_limit_bytes=VMEM_LIMIT),
        name="merge_ffn",
    )(x2, ydn, yswa, graw, wa, wb, wo, gain, wg, wu, wd)


def _t5_bucket_table():
    qi = jnp.arange(SWA_BLOCK)[:, None]
    kj = jnp.arange(SWA_BAND)[None, :]
    dist = SWA_BLOCK + qi - kj
    in_window = (dist >= 0) & (dist < WINDOW)
    n = jnp.maximum(dist, 0)
    max_exact = REL_BUCKETS // 2
    nf = jnp.maximum(n, 1).astype(F32)
    large = max_exact + (jnp.log(nf / max_exact) / math.log(REL_MAX_DIST / max_exact)
                         * (REL_BUCKETS - max_exact)).astype(jnp.int32)
    large = jnp.minimum(large, REL_BUCKETS - 1)
    return jnp.where(n < max_exact, n, large), in_window


def _segment_mean_matrix(width, seg):
    idx = np.arange(width) // seg
    return jnp.asarray((idx[:, None] == idx[None, :]).astype(np.float32) / seg, dtype=BF16)


def kernel(x, attn_norm, w_in, dn_conv, dn_a_log, dn_dt_bias, dn_out_norm, swa_q_norm, swa_k_norm, swa_sinks, rel_bias, w_branch_dn, w_branch_swa, w_out, ffn_norm, w_gate, w_up, w_down):
    B, S, D = x.shape
    T = B * S
    depth = w_in.shape[0]
    bucket, in_window = _t5_bucket_table()
    seg = _segment_mean_matrix(LANES, SWA_HEAD_DIM)
    rel = rel_bias.astype(F32)
    bias_tab = sum(jnp.where(bucket[None] == b, rel[b][:, None, None], 0.0)
                   for b in range(REL_BUCKETS))
    bias_tab = jnp.where(in_window[None], bias_tab, -jnp.inf)
    has_prev = jnp.arange(SWA_BAND)[None, None, :] >= SWA_BLOCK
    bias_tabs = jnp.stack([jnp.where(has_prev, bias_tab, -jnp.inf), bias_tab])

    x2 = x.reshape(T, D)
    for l in range(depth):
        wl = w_in[l].astype(BF16)
        o_qkv = 0
        o_z = o_qkv + DN_QKV_WIDTH
        o_b = o_z + DN_WIDTH
        o_sq = o_b + 2 * DN_HEADS
        o_skv = o_sq + SWA_WIDTH
        o_g = o_skv + 2 * SWA_KV_WIDTH
        w_all = jnp.concatenate([
            wl[:, o_qkv:o_z], wl[:, o_z:o_b], wl[:, o_sq:o_skv], wl[:, o_skv:o_g], wl[:, o_g:],
            wl[:, o_b:o_sq], jnp.zeros((D, LANES - 2 * DN_HEADS), wl.dtype)], axis=1)

        qkv, z, sq, skv, graw, ba = _in_proj(x2, attn_norm[l][None, :], w_all, dn_conv[l],
                                             tm=512, seq_len=S)

        pad_lo = jnp.zeros((DN_HEADS,), F32)
        pad_hi = jnp.zeros((LANES - 2 * DN_HEADS,), F32)
        alog_lane = jnp.concatenate([pad_lo, dn_a_log[l].astype(F32), pad_hi])[None, :]
        dtb_lane = jnp.concatenate([pad_lo, dn_dt_bias[l].astype(F32), pad_hi])[None, :]
        y_dn = _deltanet(qkv.reshape(B, S, -1), z.reshape(B, S, -1), ba.reshape(B, S, -1),
                         alog_lane, dtb_lane, dn_out_norm[l][None, :])

        qgain = jnp.tile(swa_q_norm[l].astype(F32), SWA_HEADS)[None, :] * (SWA_HEAD_DIM ** -0.5)
        kgain = jnp.tile(swa_k_norm[l].astype(F32), SWA_KV_HEADS)[None, :]
        y_swa = _swa(sq.reshape(B, S, -1), skv.reshape(B, S, -1), swa_sinks[l].astype(F32),
                     bias_tabs, qgain, kgain, seg)

        x2 = _merge_ffn(x2, y_dn.reshape(T, -1), y_swa.reshape(T, -1), graw,
                        w_branch_dn[l].astype(BF16), w_branch_swa[l].astype(BF16),
                        w_out[l].astype(BF16), ffn_norm[l][None, :],
                        w_gate[l].astype(BF16), w_up[l].astype(BF16), w_down[l].astype(BF16),
                        tm=512, tf=1408)
    return x2.reshape(B, S, D)
```

```python
import functools
import math

import numpy as np
import jax
import jax.numpy as jnp
from jax import lax
from jax.experimental import pallas as pl
from jax.experimental.pallas import tpu as pltpu

D_MODEL = 1024
DN_HEADS = 4
DN_HEAD_DIM = 128
DN_WIDTH = DN_HEADS * DN_HEAD_DIM
DN_QKV_WIDTH = 3 * DN_WIDTH
DN_CONV = 4
DN_CHUNK = 64
SWA_HEADS = 8
SWA_KV_HEADS = 2
SWA_HEAD_DIM = 64
SWA_WIDTH = SWA_HEADS * SWA_HEAD_DIM
SWA_KV_WIDTH = SWA_KV_HEADS * SWA_HEAD_DIM
WINDOW = 128
SWA_BLOCK = 128
REL_BUCKETS = 32
REL_MAX_DIST = 128
D_FF = 2816
EPS = 1e-6

LANES = 128
SUBLANES = 8
MXU_WIDTH = 256
VMEM_LIMIT = 56 * 1024 * 1024

W_HEAD = DN_QKV_WIDTH + DN_WIDTH
W_REST = 2 * D_MODEL + SWA_WIDTH + 2 * SWA_KV_WIDTH
REST_PIECES = W_REST // MXU_WIDTH
PIECE_GATES = 0
PIECE_SQ = 2 * D_MODEL // MXU_WIDTH
PIECE_SKV = PIECE_SQ + SWA_WIDTH // MXU_WIDTH
QKV_BLOCKS = DN_QKV_WIDTH // DN_HEAD_DIM

F32 = jnp.float32
BF16 = jnp.bfloat16


def _mm(a, b, dims=(((1,), (0,)), ((), ()))):
    return lax.dot_general(a.astype(BF16), b.astype(BF16), dims, preferred_element_type=F32)


_NT = (((1,), (1,)), ((), ()))
_TN = (((0,), (0,)), ((), ()))


def _sigmoid(x):
    return 1.0 / (1.0 + jnp.exp(-x))


def _silu(x):
    return x * _sigmoid(x)


CONV_TAIL = SUBLANES
CONV_ROWS = 128


def _in_proj_kernel(x_ref, gain_ref, w_head_ref, w_rest_ref, w_ba_ref, convw_ref,
                    qkv_ref, z_ref, rest_ref, ba_ref, xbuf, h_ref, *, tm, tiles_per_seq):
    i = pl.program_id(0)

    @pl.when(i == 0)
    def _():
        xbuf[:, tm:tm + CONV_TAIL, :] = jnp.zeros((QKV_BLOCKS, CONV_TAIL, DN_HEAD_DIM), F32)

    x = x_ref[...]
    ms = jnp.mean(x * x, axis=-1, keepdims=True)
    h_ref[...] = (x * lax.rsqrt(ms + EPS) * gain_ref[...]).astype(BF16)
    tail = xbuf[:, tm:tm + CONV_TAIL, :]
    xbuf[:, 0:CONV_TAIL, :] = jnp.where(i % tiles_per_seq == 0, 0.0, tail)
    raw = jnp.dot(h_ref[...], w_head_ref[:, :DN_QKV_WIDTH], preferred_element_type=F32)
    for c in range(QKV_BLOCKS):
        xbuf[c, CONV_TAIL:CONV_TAIL + tm, :] = raw[:, c * DN_HEAD_DIM:(c + 1) * DN_HEAD_DIM]

    scale = DN_HEAD_DIM ** -0.5

    def preprocess(c):
        w = convw_ref[c]
        for r0 in range(0, tm, CONV_ROWS):
            acc = w[DN_CONV - 1:DN_CONV, :] * xbuf[c, CONV_TAIL + r0:CONV_TAIL + r0 + CONV_ROWS, :]
            for j in range(DN_CONV - 1):
                off = CONV_TAIL - (DN_CONV - 1) + j + r0
                acc = acc + w[j:j + 1, :] * xbuf[c, off:off + CONV_ROWS, :]
            act = _silu(acc)
            norm = lax.rsqrt(jnp.sum(act * act, axis=-1, keepdims=True) + EPS)
            factor = jnp.where(c < DN_HEADS, norm * scale, jnp.where(c < 2 * DN_HEADS, norm, 1.0))
            qkv_ref[c, r0:r0 + CONV_ROWS, :] = (act * factor).astype(BF16)

    def step(c, carry):
        preprocess(c)
        rest_ref[c] = jnp.dot(h_ref[...], w_rest_ref[c],
                              preferred_element_type=F32).astype(rest_ref.dtype)
        return carry

    assert REST_PIECES == QKV_BLOCKS - 1
    lax.fori_loop(0, REST_PIECES, step, 0)
    z_ref[...] = jnp.dot(h_ref[...], w_head_ref[:, DN_QKV_WIDTH:], preferred_element_type=F32)
    ba_ref[...] = jnp.dot(h_ref[...], w_ba_ref[...], preferred_element_type=F32)
    preprocess(QKV_BLOCKS - 1)


def _in_proj(x2, gain, w_in_bf16, w_rest, w_ba, convw, tm, seq_len):
    T = x2.shape[0]
    once = lambda shape: pl.BlockSpec(shape, lambda i: (0,) * len(shape),
                                      pipeline_mode=pl.Buffered(1))
    return pl.pallas_call(
        functools.partial(_in_proj_kernel, tm=tm, tiles_per_seq=seq_len // tm),
        grid=(T // tm,),
        in_specs=[
            pl.BlockSpec((tm, D_MODEL), lambda i: (i, 0)),
            pl.BlockSpec((1, D_MODEL), lambda i: (0, 0)),
            once((D_MODEL, W_HEAD)),
            once((REST_PIECES, D_MODEL, MXU_WIDTH)),
            once((D_MODEL, LANES)),
            pl.BlockSpec((QKV_BLOCKS, DN_CONV, DN_HEAD_DIM), lambda i: (0, 0, 0)),
        ],
        out_specs=(
            pl.BlockSpec((QKV_BLOCKS, tm, DN_HEAD_DIM), lambda i: (0, i, 0)),
            pl.BlockSpec((tm, DN_WIDTH), lambda i: (i, 0)),
            pl.BlockSpec((REST_PIECES, tm, MXU_WIDTH), lambda i: (0, i, 0)),
            pl.BlockSpec((tm, LANES), lambda i: (i, 0)),
        ),
        out_shape=(
            jax.ShapeDtypeStruct((QKV_BLOCKS, T, DN_HEAD_DIM), BF16),
            jax.ShapeDtypeStruct((T, DN_WIDTH), F32),
            jax.ShapeDtypeStruct((REST_PIECES, T, MXU_WIDTH), BF16),
            jax.ShapeDtypeStruct((T, LANES), F32),
        ),
        scratch_shapes=[pltpu.VMEM((QKV_BLOCKS, CONV_TAIL + tm, DN_HEAD_DIM), F32),
                        pltpu.VMEM((tm, D_MODEL), BF16)],
        compiler_params=pltpu.CompilerParams(
            dimension_semantics=("arbitrary",), vmem_limit_bytes=VMEM_LIMIT),
        name="in_proj",
    )(x2, gain, w_in_bf16, w_rest, w_ba, convw)


DN_TILE = 256
DN_NCH = DN_TILE // DN_CHUNK
DN_LOG_CHUNK = int(math.log2(DN_CHUNK))


def _deltanet_kernel(qkv_ref, z_ref, ba_ref, alog_ref, dtb_ref, gain_ref, o_ref, state):
    t = pl.program_id(1)

    @pl.when(t == 0)
    def _():
        state[...] = jnp.zeros_like(state)

    ba = ba_ref[0]
    beta = _sigmoid(ba)
    xs = ba + dtb_ref[...]
    softplus = jnp.maximum(xs, 0.0) + jnp.log1p(jnp.exp(-jnp.abs(xs)))
    g = -jnp.exp(alog_ref[...]) * softplus

    row = lax.broadcasted_iota(jnp.int32, (DN_TILE, DN_TILE), 0)
    col = lax.broadcasted_iota(jnp.int32, (DN_TILE, DN_TILE), 1)
    same_chunk = (row >> DN_LOG_CHUNK) == (col >> DN_LOG_CHUNK)
    lower_incl = same_chunk & (row >= col)
    strict_lower = same_chunk & (row > col)
    g_hi = g.astype(BF16)
    g_lo = (g - g_hi.astype(F32)).astype(BF16)
    gc2 = jnp.dot(jnp.where(lower_incl, 1.0, 0.0).astype(BF16),
                  jnp.concatenate([g_hi, g_lo], axis=1), preferred_element_type=F32)
    gc = gc2[:, :LANES] + gc2[:, LANES:]
    g_last = jnp.concatenate(
        [jnp.broadcast_to(gc[(c + 1) * DN_CHUNK - 1:(c + 1) * DN_CHUNK, :], (DN_CHUNK, LANES))
         for c in range(DN_NCH)], axis=0)
    rev = g_last - gc
    gc_t = gc.T
    eye = jnp.where(row == col, 1.0, 0.0)

    heads = range(DN_HEADS)
    gcol = lambda arr, h: arr[:, DN_HEADS + h:DN_HEADS + h + 1]
    q, k, kb, rhs = [], [], [], []
    for h in heads:
        qh = qkv_ref[h].astype(F32)
        kh = qkv_ref[DN_HEADS + h].astype(F32)
        vh = qkv_ref[2 * DN_HEADS + h].astype(F32)
        beta_c = beta[:, h:h + 1]
        q.append(qh)
        k.append(kh)
        kb.append(kh * beta_c)
        rhs.append(jnp.concatenate([vh * beta_c, kb[h] * jnp.exp(gcol(gc, h))], axis=1))
    gram = [_mm(jnp.concatenate([kb[h], q[h]], axis=0), k[h], _NT) for h in heads]
    qk, n_pow, inv = [], [], []
    for h in heads:
        gc_r = gc_t[DN_HEADS + h:DN_HEADS + h + 1, :]
        decay = jnp.exp(jnp.where(lower_incl, gcol(gc, h) - gc_r, -jnp.inf))
        qk.append(gram[h][DN_TILE:] * decay)
        n_pow.append(-jnp.where(strict_lower, gram[h][:DN_TILE] * decay, 0.0))
        inv.append(eye + n_pow[h])
    for _ in range(DN_LOG_CHUNK - 1):
        n_pow = [_mm(n_pow[h], n_pow[h]) for h in heads]
        inv = [inv[h] + _mm(inv[h], n_pow[h]) for h in heads]
    sol = [_mm(inv[h], rhs[h]) for h in heads]
    u = [sol[h][:, :DN_HEAD_DIM] for h in heads]
    wmat = [sol[h][:, DN_HEAD_DIM:] for h in heads]
    qe = [q[h] * jnp.exp(gcol(gc, h)) for h in heads]
    kdec = [k[h] * jnp.exp(gcol(rev, h)) for h in heads]

    s = [state[h] for h in heads]
    v_new = [[] for _ in heads]
    o_inter = [[] for _ in heads]
    for c in range(DN_NCH):
        r0, r1 = c * DN_CHUNK, (c + 1) * DN_CHUNK
        ws = [_mm(jnp.concatenate([wmat[h][r0:r1], qe[h][r0:r1]], axis=0), s[h]) for h in heads]
        for h in heads:
            v_new[h].append(u[h][r0:r1] - ws[h][:DN_CHUNK])
            o_inter[h].append(ws[h][DN_CHUNK:])
        upd = [_mm(kdec[h][r0:r1], v_new[h][c], _TN) for h in heads]
        s = [s[h] * jnp.exp(gc[r1 - 1:r1, DN_HEADS + h:DN_HEADS + h + 1]) + upd[h]
             for h in heads]
    o_intra = [_mm(qk[h], jnp.concatenate(v_new[h], axis=0)) for h in heads]
    for h in heads:
        state[h] = s[h]
        o = jnp.concatenate(o_inter[h], axis=0) + o_intra[h]
        o = o * lax.rsqrt(jnp.mean(o * o, axis=-1, keepdims=True) + EPS) * gain_ref[...]
        zh = z_ref[0, :, h * DN_HEAD_DIM:(h + 1) * DN_HEAD_DIM]
        o_ref[0, :, h * DN_HEAD_DIM:(h + 1) * DN_HEAD_DIM] = (o * _silu(zh)).astype(o_ref.dtype)


def _deltanet(qkv, z, ba, alog_lane, dtb_lane, out_gain):
    B, S, _ = z.shape
    tiles = S // DN_TILE
    return pl.pallas_call(
        _deltanet_kernel,
        grid=(B, tiles),
        in_specs=[
            pl.BlockSpec((QKV_BLOCKS, DN_TILE, DN_HEAD_DIM), lambda b, t: (0, b * tiles + t, 0)),
            pl.BlockSpec((1, DN_TILE, DN_WIDTH), lambda b, t: (b, t, 0)),
            pl.BlockSpec((1, DN_TILE, LANES), lambda b, t: (b, t, 0)),
            pl.BlockSpec((1, LANES), lambda b, t: (0, 0)),
            pl.BlockSpec((1, LANES), lambda b, t: (0, 0)),
            pl.BlockSpec((1, DN_HEAD_DIM), lambda b, t: (0, 0)),
        ],
        out_specs=pl.BlockSpec((1, DN_TILE, DN_WIDTH), lambda b, t: (b, t, 0)),
        out_shape=jax.ShapeDtypeStruct((B, S, DN_WIDTH), BF16),
        scratch_shapes=[pltpu.VMEM((DN_HEADS, DN_HEAD_DIM, DN_HEAD_DIM), F32)],
        compiler_params=pltpu.CompilerParams(
            dimension_semantics=("arbitrary", "arbitrary"), vmem_limit_bytes=VMEM_LIMIT),
        name="deltanet",
    )(qkv, z, ba, alog_lane, dtb_lane, out_gain)


SWA_BAND = 2 * SWA_BLOCK
SWA_SLOTS = SWA_WIDTH // LANES
SWA_HEADS_PER_SLOT = LANES // SWA_HEAD_DIM


def _split_mm(x, w):
    rows = x.shape[0]
    hi = x.astype(BF16)
    lo = (x - hi.astype(F32)).astype(BF16)
    r = jnp.dot(jnp.concatenate([hi, lo], axis=0), w, preferred_element_type=F32)
    return r[:rows] + r[rows:]


def _swa_kernel(sinks_ref, sq_ref, kv_ref, bias_ref, qgain_ref, kgain_ref, seg_ref, o_ref,
                band_ref):
    n = pl.program_id(1)
    kv = kv_ref[0].astype(F32)
    k = kv[:, :SWA_KV_WIDTH]
    slots_per_piece = MXU_WIDTH // LANES
    q_slots = [sq_ref[j // slots_per_piece, :, (j % slots_per_piece) * LANES:
                      (j % slots_per_piece + 1) * LANES].astype(F32) for j in range(SWA_SLOTS)]
    parts = q_slots + [k]
    ms = _split_mm(jnp.concatenate([p * p for p in parts], axis=0), seg_ref[...])
    qn = [parts[j] * lax.rsqrt(ms[j * SWA_BLOCK:(j + 1) * SWA_BLOCK] + EPS)
          * qgain_ref[:, j * LANES:(j + 1) * LANES] for j in range(SWA_SLOTS)]
    kn = k * lax.rsqrt(ms[SWA_SLOTS * SWA_BLOCK:] + EPS) * kgain_ref[...]

    @pl.when(n == 0)
    def _():
        band_ref[0:SWA_BLOCK, :] = jnp.zeros((SWA_BLOCK, 2 * SWA_KV_WIDTH), F32)

    @pl.when(n > 0)
    def _():
        band_ref[0:SWA_BLOCK, :] = band_ref[SWA_BLOCK:SWA_BAND, :]

    band_ref[SWA_BLOCK:SWA_BAND, :SWA_KV_WIDTH] = kn
    band_ref[SWA_BLOCK:SWA_BAND, SWA_KV_WIDTH:] = kv[:, SWA_KV_WIDTH:]
    k = band_ref[:, :SWA_KV_WIDTH]
    v = band_ref[:, SWA_KV_WIDTH:]

    lane = lax.broadcasted_iota(jnp.int32, (SWA_BAND, LANES), 1)
    low = lane < SWA_HEAD_DIM
    k_rot = pltpu.roll(k, SWA_HEAD_DIM, axis=1)
    v_rot = pltpu.roll(v, SWA_HEAD_DIM, axis=1)
    k_lo = (jnp.where(low, k, 0.0), jnp.where(low, k_rot, 0.0))
    k_hi = (jnp.where(low, 0.0, k_rot), jnp.where(low, 0.0, k))
    v_lo = (jnp.where(low, v, 0.0), jnp.where(low, v_rot, 0.0))
    v_hi = (jnp.where(low, 0.0, v_rot), jnp.where(low, 0.0, v))

    out_low = lax.broadcasted_iota(jnp.int32, (SWA_BLOCK, LANES), 1) < SWA_HEAD_DIM

    slots_per_kv = SWA_SLOTS // SWA_KV_HEADS
    k_bd = [jnp.concatenate([k_lo[kh], k_hi[kh]], axis=0).astype(BF16)
            for kh in range(SWA_KV_HEADS)]
    v_bd = [jnp.concatenate([v_lo[kh], v_hi[kh]], axis=0).astype(BF16)
            for kh in range(SWA_KV_HEADS)]
    logits = [_mm(qn[j], k_bd[j // slots_per_kv], _NT) for j in range(SWA_SLOTS)]
    probs = [[] for _ in range(SWA_SLOTS)]
    inv_den = [[] for _ in range(SWA_SLOTS)]
    for j in range(SWA_SLOTS):
        for r in range(SWA_HEADS_PER_SLOT):
            hd = j * SWA_HEADS_PER_SLOT + r
            lg = logits[j][:, r * SWA_BAND:(r + 1) * SWA_BAND] + bias_ref[0, hd]
            sink = sinks_ref[hd]
            m = jnp.maximum(jnp.max(lg, axis=-1, keepdims=True), sink)
            p = jnp.exp(lg - m)
            den = jnp.sum(p, axis=-1, keepdims=True) + jnp.exp(sink - m)
            probs[j].append(p.astype(BF16))
            inv_den[j].append(1.0 / den)
    outs = [_mm(jnp.concatenate(probs[j], axis=1), v_bd[j // slots_per_kv])
            for j in range(SWA_SLOTS)]
    for j in range(SWA_SLOTS):
        o_ref[0, :, j * LANES:(j + 1) * LANES] = (
            outs[j] * jnp.where(out_low, inv_den[j][0], inv_den[j][1])).astype(o_ref.dtype)


def _swa(rest, batch, sinks, bias_tabs, qgain, kgain, seg):
    B = batch
    S = rest.shape[1] // B
    nb = S // SWA_BLOCK
    kvw = 2 * SWA_KV_WIDTH
    sq_pieces = SWA_WIDTH // MXU_WIDTH
    assert kvw == MXU_WIDTH
    grid_spec = pltpu.PrefetchScalarGridSpec(
        num_scalar_prefetch=1,
        grid=(B, nb),
        in_specs=[
            pl.BlockSpec((sq_pieces, SWA_BLOCK, MXU_WIDTH),
                         lambda b, n, s: (PIECE_SQ // sq_pieces, b * nb + n, 0)),
            pl.BlockSpec((1, SWA_BLOCK, kvw), lambda b, n, s: (PIECE_SKV, b * nb + n, 0)),
            pl.BlockSpec((1, SWA_HEADS, SWA_BLOCK, SWA_BAND),
                         lambda b, n, s: (jnp.minimum(n, 1), 0, 0, 0)),
            pl.BlockSpec((1, SWA_WIDTH), lambda b, n, s: (0, 0)),
            pl.BlockSpec((1, SWA_KV_WIDTH), lambda b, n, s: (0, 0)),
            pl.BlockSpec((LANES, LANES), lambda b, n, s: (0, 0)),
        ],
        out_specs=pl.BlockSpec((1, SWA_BLOCK, SWA_WIDTH), lambda b, n, s: (b, n, 0)),
        scratch_shapes=[pltpu.VMEM((SWA_BAND, kvw), F32)],
    )
    return pl.pallas_call(
        _swa_kernel,
        grid_spec=grid_spec,
        out_shape=jax.ShapeDtypeStruct((B, S, SWA_WIDTH), BF16),
        compiler_params=pltpu.CompilerParams(
            dimension_semantics=("arbitrary", "arbitrary"), vmem_limit_bytes=VMEM_LIMIT),
        name="swa",
    )(sinks, rest, rest, bias_tabs, qgain, kgain, seg)


def _merge_ffn_kernel(x_ref, ydn_ref, yswa_ref, g_ref, wa_ref, wb_ref, wo_ref, gain_ref,
                      wg_ref, wu_ref, wd_ref, o_ref, acc_ref, h2_ref):
    j = pl.program_id(1)

    @pl.when(j == 0)
    def _():
        per_branch = D_MODEL // MXU_WIDTH
        gates = [_sigmoid(jnp.concatenate(
            [g_ref[br * per_branch + p].astype(F32) for p in range(per_branch)], axis=1))
            for br in range(2)]
        ga, gb = gates
        merged = ga * _mm(ydn_ref[...], wa_ref[...]) + gb * _mm(yswa_ref[...], wb_ref[...])
        x1 = x_ref[...] + _mm(merged, wo_ref[...])
        acc_ref[...] = x1
        ms = jnp.mean(x1 * x1, axis=-1, keepdims=True)
        h2_ref[...] = (x1 * lax.rsqrt(ms + EPS) * gain_ref[...]).astype(BF16)

    h2 = h2_ref[...]
    gate = jnp.dot(h2, wg_ref[...], preferred_element_type=F32)
    up = jnp.dot(h2, wu_ref[...], preferred_element_type=F32)
    acc_ref[...] += _mm(_silu(gate) * up, wd_ref[...])

    @pl.when(j == pl.num_programs(1) - 1)
    def _():
        o_ref[...] = acc_ref[...]


def _merge_ffn(x2, ydn, yswa, graw, wa, wb, wo, gain, wg, wu, wd, tm, tf):
    T = x2.shape[0]
    row = lambda w: pl.BlockSpec((tm, w), lambda i, j: (i, 0))
    once = lambda a, b: pl.BlockSpec((a, b), lambda i, j: (0, 0), pipeline_mode=pl.Buffered(1))
    return pl.pallas_call(
        _merge_ffn_kernel,
        grid=(T // tm, D_FF // tf),
        in_specs=[row(D_MODEL), row(DN_WIDTH), row(SWA_WIDTH),
                  pl.BlockSpec((PIECE_SQ - PIECE_GATES, tm, MXU_WIDTH), lambda i, j: (0, i, 0)),
                  once(DN_WIDTH, D_MODEL), once(SWA_WIDTH, D_MODEL), once(D_MODEL, D_MODEL),
                  once(1, D_MODEL),
                  pl.BlockSpec((D_MODEL, tf), lambda i, j: (0, j)),
                  pl.BlockSpec((D_MODEL, tf), lambda i, j: (0, j)),
                  pl.BlockSpec((tf, D_MODEL), lambda i, j: (j, 0))],
        out_specs=row(D_MODEL),
        out_shape=jax.ShapeDtypeStruct((T, D_MODEL), F32),
        scratch_shapes=[pltpu.VMEM((tm, D_MODEL), F32), pltpu.VMEM((tm, D_MODEL), BF16)],
        compiler_params=pltpu.CompilerParams(
            dimension_semantics=("arbitrary", "arbitrary"), vmem_limit_bytes=VMEM_LIMIT),
        name="merge_ffn",
    )(x2, ydn, yswa, graw, wa, wb, wo, gain, wg, wu, wd)


def _t5_bucket_table():
    qi = jnp.arange(SWA_BLOCK)[:, None]
    kj = jnp.arange(SWA_BAND)[None, :]
    dist = SWA_BLOCK + qi - kj
    in_window = (dist >= 0) & (dist < WINDOW)
    n = jnp.maximum(dist, 0)
    max_exact = REL_BUCKETS // 2
    nf = jnp.maximum(n, 1).astype(F32)
    large = max_exact + (jnp.log(nf / max_exact) / math.log(REL_MAX_DIST / max_exact)
                         * (REL_BUCKETS - max_exact)).astype(jnp.int32)
    large = jnp.minimum(large, REL_BUCKETS - 1)
    return jnp.where(n < max_exact, n, large), in_window


def _segment_mean_matrix(width, seg):
    idx = np.arange(width) // seg
    return jnp.asarray((idx[:, None] == idx[None, :]).astype(np.float32) / seg, dtype=BF16)


def kernel(x, attn_norm, w_in, dn_conv, dn_a_log, dn_dt_bias, dn_out_norm, swa_q_norm, swa_k_norm, swa_sinks, rel_bias, w_branch_dn, w_branch_swa, w_out, ffn_norm, w_gate, w_up, w_down):
    B, S, D = x.shape
    T = B * S
    depth = w_in.shape[0]
    bucket, in_window = _t5_bucket_table()
    seg = _segment_mean_matrix(LANES, SWA_HEAD_DIM)
    rel = rel_bias.astype(F32)
    bias_tab = sum(jnp.where(bucket[None] == b, rel[b][:, None, None], 0.0)
                   for b in range(REL_BUCKETS))
    bias_tab = jnp.where(in_window[None], bias_tab, -jnp.inf)
    has_prev = jnp.arange(SWA_BAND)[None, None, :] >= SWA_BLOCK
    bias_tabs = jnp.stack([jnp.where(has_prev, bias_tab, -jnp.inf), bias_tab])

    x2 = x.reshape(T, D)
    for l in range(depth):
        wl = w_in[l].astype(BF16)
        o_ba = W_HEAD
        o_sq = o_ba + 2 * DN_HEADS
        o_g = o_sq + SWA_WIDTH + 2 * SWA_KV_WIDTH
        w_rest = jnp.concatenate([wl[:, o_g:], wl[:, o_sq:o_g]], axis=1)
        w_rest = w_rest.reshape(D, REST_PIECES, MXU_WIDTH).transpose(1, 0, 2)
        w_ba = jnp.pad(wl[:, o_ba:o_sq], ((0, 0), (0, LANES - 2 * DN_HEADS)))
        convw = dn_conv[l].astype(F32).reshape(DN_CONV, QKV_BLOCKS, DN_HEAD_DIM).transpose(1, 0, 2)

        qkv, z, rest, ba = _in_proj(x2, attn_norm[l][None, :], wl, w_rest, w_ba, convw,
                                    tm=512, seq_len=S)

        pad_lo = jnp.zeros((DN_HEADS,), F32)
        pad_hi = jnp.zeros((LANES - 2 * DN_HEADS,), F32)
        alog_lane = jnp.concatenate([pad_lo, dn_a_log[l].astype(F32), pad_hi])[None, :]
        dtb_lane = jnp.concatenate([pad_lo, dn_dt_bias[l].astype(F32), pad_hi])[None, :]
        y_dn = _deltanet(qkv, z.reshape(B, S, -1), ba.reshape(B, S, -1),
                         alog_lane, dtb_lane, dn_out_norm[l][None, :])

        qgain = jnp.tile(swa_q_norm[l].astype(F32), SWA_HEADS)[None, :] * (SWA_HEAD_DIM ** -0.5)
        kgain = jnp.tile(swa_k_norm[l].astype(F32), SWA_KV_HEADS)[None, :]
        y_swa = _swa(rest, B, swa_sinks[l].astype(F32), bias_tabs, qgain, kgain, seg)

        x2 = _merge_ffn(x2, y_dn.reshape(T, -1), y_swa.reshape(T, -1), rest,
                        w_branch_dn[l].astype(BF16), w_branch_swa[l].astype(BF16),
                        w_out[l].astype(BF16), ffn_norm[l][None, :],
                        w_gate[l].astype(BF16), w_up[l].astype(BF16), w_down[l].astype(BF16),
                        tm=512, tf=1408)
    return x2.reshape(B, S, D)
```

```python
import functools
import math

import numpy as np
import jax
import jax.numpy as jnp
from jax import lax
from jax.experimental import pallas as pl
from jax.experimental.pallas import tpu as pltpu

D_MODEL = 1024
DN_HEADS = 4
DN_HEAD_DIM = 128
DN_WIDTH = DN_HEADS * DN_HEAD_DIM
DN_QKV_WIDTH = 3 * DN_WIDTH
DN_CONV = 4
DN_CHUNK = 64
SWA_HEADS = 8
SWA_KV_HEADS = 2
SWA_HEAD_DIM = 64
SWA_WIDTH = SWA_HEADS * SWA_HEAD_DIM
SWA_KV_WIDTH = SWA_KV_HEADS * SWA_HEAD_DIM
WINDOW = 128
SWA_BLOCK = 128
REL_BUCKETS = 32
REL_MAX_DIST = 128
D_FF = 2816
EPS = 1e-6

LANES = 128
SUBLANES = 8
MXU_WIDTH = 256
VMEM_LIMIT = 56 * 1024 * 1024

W_HEAD = DN_QKV_WIDTH + DN_WIDTH
W_REST = 2 * D_MODEL + SWA_WIDTH + 2 * SWA_KV_WIDTH
REST_PIECES = W_REST // MXU_WIDTH
PIECE_GATES = 0
PIECE_SQ = 2 * D_MODEL // MXU_WIDTH
PIECE_SKV = PIECE_SQ + SWA_WIDTH // MXU_WIDTH
QKV_BLOCKS = DN_QKV_WIDTH // DN_HEAD_DIM

F32 = jnp.float32
BF16 = jnp.bfloat16


def _mm(a, b, dims=(((1,), (0,)), ((), ()))):
    return lax.dot_general(a.astype(BF16), b.astype(BF16), dims, preferred_element_type=F32)


_NT = (((1,), (1,)), ((), ()))
_TN = (((0,), (0,)), ((), ()))


def _sigmoid(x):
    return 1.0 / (1.0 + jnp.exp(-x))


def _silu(x):
    return x * _sigmoid(x)


CONV_TAIL = SUBLANES
CONV_ROWS = 128


def _in_proj_kernel(x_ref, gain_ref, w_head_ref, w_rest_ref, w_ba_ref,
                    qkv_ref, z_ref, rest_ref, ba_ref):
    x = x_ref[...]
    ms = jnp.mean(x * x, axis=-1, keepdims=True)
    h = (x * lax.rsqrt(ms + EPS) * gain_ref[...]).astype(BF16)
    raw = jnp.dot(h, w_head_ref[:, :DN_QKV_WIDTH], preferred_element_type=F32)
    for c in range(QKV_BLOCKS):
        qkv_ref[c] = raw[:, c * DN_HEAD_DIM:(c + 1) * DN_HEAD_DIM]
    z_ref[...] = jnp.dot(h, w_head_ref[:, DN_QKV_WIDTH:], preferred_element_type=F32)
    for p in range(REST_PIECES):
        rest_ref[p] = jnp.dot(h, w_rest_ref[p], preferred_element_type=F32).astype(rest_ref.dtype)
    ba_ref[...] = jnp.dot(h, w_ba_ref[...], preferred_element_type=F32)


def _in_proj(x2, gain, w_in_bf16, w_rest, w_ba, tm):
    T = x2.shape[0]
    once = lambda shape: pl.BlockSpec(shape, lambda i: (0,) * len(shape),
                                      pipeline_mode=pl.Buffered(1))
    return pl.pallas_call(
        _in_proj_kernel,
        grid=(T // tm,),
        in_specs=[
            pl.BlockSpec((tm, D_MODEL), lambda i: (i, 0)),
            pl.BlockSpec((1, D_MODEL), lambda i: (0, 0)),
            once((D_MODEL, W_HEAD)),
            once((REST_PIECES, D_MODEL, MXU_WIDTH)),
            once((D_MODEL, LANES)),
        ],
        out_specs=(
            pl.BlockSpec((QKV_BLOCKS, tm, DN_HEAD_DIM), lambda i: (0, i, 0)),
            pl.BlockSpec((tm, DN_WIDTH), lambda i: (i, 0)),
            pl.BlockSpec((REST_PIECES, tm, MXU_WIDTH), lambda i: (0, i, 0)),
            pl.BlockSpec((tm, LANES), lambda i: (i, 0)),
        ),
        out_shape=(
            jax.ShapeDtypeStruct((QKV_BLOCKS, T, DN_HEAD_DIM), F32),
            jax.ShapeDtypeStruct((T, DN_WIDTH), F32),
            jax.ShapeDtypeStruct((REST_PIECES, T, MXU_WIDTH), BF16),
            jax.ShapeDtypeStruct((T, LANES), F32),
        ),
        compiler_params=pltpu.CompilerParams(
            dimension_semantics=("arbitrary",), vmem_limit_bytes=VMEM_LIMIT),
        name="in_proj",
    )(x2, gain, w_in_bf16, w_rest, w_ba)


DN_TILE = 256
DN_NCH = DN_TILE // DN_CHUNK
DN_LOG_CHUNK = int(math.log2(DN_CHUNK))


def _deltanet_kernel(raw_ref, raw_next_ref, z_ref, ba_ref, convw_ref, alog_ref, dtb_ref, gain_ref,
                     o_ref, state, xbuf, qkv_ref, qkv_next_ref):
    t = pl.program_id(1)
    scale = DN_HEAD_DIM ** -0.5

    def preprocess(dst_ref):
        for c in range(QKV_BLOCKS):
            w = convw_ref[c]
            for r0 in range(0, DN_TILE, CONV_ROWS):
                acc = w[DN_CONV - 1:DN_CONV, :] * xbuf[c, CONV_TAIL + r0:CONV_TAIL + r0 + CONV_ROWS, :]
                for j in range(DN_CONV - 1):
                    off = CONV_TAIL - (DN_CONV - 1) + j + r0
                    acc = acc + w[j:j + 1, :] * xbuf[c, off:off + CONV_ROWS, :]
                act = _silu(acc)
                if c < 2 * DN_HEADS:
                    norm = lax.rsqrt(jnp.sum(act * act, axis=-1, keepdims=True) + EPS)
                    act = act * (norm * scale if c < DN_HEADS else norm)
                dst_ref[c, r0:r0 + CONV_ROWS, :] = act.astype(BF16)

    @pl.when(t == 0)
    def _():
        state[...] = jnp.zeros_like(state)
        xbuf[:, 0:CONV_TAIL, :] = jnp.zeros((QKV_BLOCKS, CONV_TAIL, DN_HEAD_DIM), F32)
        xbuf[:, CONV_TAIL:, :] = raw_ref[...]
        preprocess(qkv_ref)

    xbuf[:, 0:CONV_TAIL, :] = raw_ref[:, DN_TILE - CONV_TAIL:DN_TILE, :]
    xbuf[:, CONV_TAIL:, :] = raw_next_ref[...]
    preprocess(qkv_next_ref)

    ba = ba_ref[0]
    beta = _sigmoid(ba)
    xs = ba + dtb_ref[...]
    softplus = jnp.maximum(xs, 0.0) + jnp.log1p(jnp.exp(-jnp.abs(xs)))
    g = -jnp.exp(alog_ref[...]) * softplus

    row = lax.broadcasted_iota(jnp.int32, (DN_TILE, DN_TILE), 0)
    col = lax.broadcasted_iota(jnp.int32, (DN_TILE, DN_TILE), 1)
    same_chunk = (row >> DN_LOG_CHUNK) == (col >> DN_LOG_CHUNK)
    lower_incl = same_chunk & (row >= col)
    strict_lower = same_chunk & (row > col)
    g_hi = g.astype(BF16)
    g_lo = (g - g_hi.astype(F32)).astype(BF16)
    gc2 = jnp.dot(jnp.where(lower_incl, 1.0, 0.0).astype(BF16),
                  jnp.concatenate([g_hi, g_lo], axis=1), preferred_element_type=F32)
    gc = gc2[:, :LANES] + gc2[:, LANES:]
    g_last = jnp.concatenate(
        [jnp.broadcast_to(gc[(c + 1) * DN_CHUNK - 1:(c + 1) * DN_CHUNK, :], (DN_CHUNK, LANES))
         for c in range(DN_NCH)], axis=0)
    rev = g_last - gc
    gc_t = gc.T
    eye = jnp.where(row == col, 1.0, 0.0)

    heads = range(DN_HEADS)
    gcol = lambda arr, h: arr[:, DN_HEADS + h:DN_HEADS + h + 1]
    q, k, kb, rhs = [], [], [], []
    for h in heads:
        qh = qkv_ref[h].astype(F32)
        kh = qkv_ref[DN_HEADS + h].astype(F32)
        vh = qkv_ref[2 * DN_HEADS + h].astype(F32)
        beta_c = beta[:, h:h + 1]
        q.append(qh)
        k.append(kh)
        kb.append(kh * beta_c)
        rhs.append(jnp.concatenate([vh * beta_c, kb[h] * jnp.exp(gcol(gc, h))], axis=1))
    gram = [_mm(jnp.concatenate([kb[h], q[h]], axis=0), k[h], _NT) for h in heads]
    qk, n_pow, inv = [], [], []
    for h in heads:
        gc_r = gc_t[DN_HEADS + h:DN_HEADS + h + 1, :]
        decay = jnp.exp(jnp.where(lower_incl, gcol(gc, h) - gc_r, -jnp.inf))
        qk.append(gram[h][DN_TILE:] * decay)
        n_pow.append(-jnp.where(strict_lower, gram[h][:DN_TILE] * decay, 0.0))
        inv.append(eye + n_pow[h])
    for _ in range(DN_LOG_CHUNK - 1):
        n_pow = [_mm(n_pow[h], n_pow[h]) for h in heads]
        inv = [inv[h] + _mm(inv[h], n_pow[h]) for h in heads]
    sol = [_mm(inv[h], rhs[h]) for h in heads]
    u = [sol[h][:, :DN_HEAD_DIM] for h in heads]
    wmat = [sol[h][:, DN_HEAD_DIM:] for h in heads]
    qe = [q[h] * jnp.exp(gcol(gc, h)) for h in heads]
    kdec = [k[h] * jnp.exp(gcol(rev, h)) for h in heads]

    s = [state[h] for h in heads]
    v_new = [[] for _ in heads]
    o_inter = [[] for _ in heads]
    for c in range(DN_NCH):
        r0, r1 = c * DN_CHUNK, (c + 1) * DN_CHUNK
        ws = [_mm(jnp.concatenate([wmat[h][r0:r1], qe[h][r0:r1]], axis=0), s[h]) for h in heads]
        for h in heads:
            v_new[h].append(u[h][r0:r1] - ws[h][:DN_CHUNK])
            o_inter[h].append(ws[h][DN_CHUNK:])
        upd = [_mm(kdec[h][r0:r1], v_new[h][c], _TN) for h in heads]
        s = [s[h] * jnp.exp(gc[r1 - 1:r1, DN_HEADS + h:DN_HEADS + h + 1]) + upd[h]
             for h in heads]
    o_intra = [_mm(qk[h], jnp.concatenate(v_new[h], axis=0)) for h in heads]
    for h in heads:
        state[h] = s[h]
        o = jnp.concatenate(o_inter[h], axis=0) + o_intra[h]
        o = o * lax.rsqrt(jnp.mean(o * o, axis=-1, keepdims=True) + EPS) * gain_ref[...]
        zh = z_ref[0, :, h * DN_HEAD_DIM:(h + 1) * DN_HEAD_DIM]
        o_ref[0, :, h * DN_HEAD_DIM:(h + 1) * DN_HEAD_DIM] = (o * _silu(zh)).astype(o_ref.dtype)
    qkv_ref[...] = qkv_next_ref[...]


def _deltanet(raw, z, ba, convw, alog_lane, dtb_lane, out_gain):
    B, S, _ = z.shape
    tiles = S // DN_TILE
    raw_block = (QKV_BLOCKS, DN_TILE, DN_HEAD_DIM)
    return pl.pallas_call(
        _deltanet_kernel,
        grid=(B, tiles),
        in_specs=[
            pl.BlockSpec(raw_block, lambda b, t: (0, b * tiles + t, 0)),
            pl.BlockSpec(raw_block, lambda b, t: (0, b * tiles + jnp.minimum(t + 1, tiles - 1), 0)),
            pl.BlockSpec((1, DN_TILE, DN_WIDTH), lambda b, t: (b, t, 0)),
            pl.BlockSpec((1, DN_TILE, LANES), lambda b, t: (b, t, 0)),
            pl.BlockSpec((QKV_BLOCKS, DN_CONV, DN_HEAD_DIM), lambda b, t: (0, 0, 0)),
            pl.BlockSpec((1, LANES), lambda b, t: (0, 0)),
            pl.BlockSpec((1, LANES), lambda b, t: (0, 0)),
            pl.BlockSpec((1, DN_HEAD_DIM), lambda b, t: (0, 0)),
        ],
        out_specs=pl.BlockSpec((1, DN_TILE, DN_WIDTH), lambda b, t: (b, t, 0)),
        out_shape=jax.ShapeDtypeStruct((B, S, DN_WIDTH), BF16),
        scratch_shapes=[pltpu.VMEM((DN_HEADS, DN_HEAD_DIM, DN_HEAD_DIM), F32),
                        pltpu.VMEM((QKV_BLOCKS, CONV_TAIL + DN_TILE, DN_HEAD_DIM), F32),
                        pltpu.VMEM(raw_block, BF16), pltpu.VMEM(raw_block, BF16)],
        compiler_params=pltpu.CompilerParams(
            dimension_semantics=("arbitrary", "arbitrary"), vmem_limit_bytes=VMEM_LIMIT),
        name="deltanet",
    )(raw, raw, z, ba, convw, alog_lane, dtb_lane, out_gain)


SWA_BAND = 2 * SWA_BLOCK
SWA_SLOTS = SWA_WIDTH // LANES
SWA_HEADS_PER_SLOT = LANES // SWA_HEAD_DIM


def _split_mm(x, w):
    rows = x.shape[0]
    hi = x.astype(BF16)
    lo = (x - hi.astype(F32)).astype(BF16)
    r = jnp.dot(jnp.concatenate([hi, lo], axis=0), w, preferred_element_type=F32)
    return r[:rows] + r[rows:]


def _swa_kernel(sinks_ref, sq_ref, kv_ref, bias_ref, qgain_ref, kgain_ref, seg_ref, o_ref,
                band_ref):
    n = pl.program_id(1)
    kv = kv_ref[0].astype(F32)
    k = kv[:, :SWA_KV_WIDTH]
    slots_per_piece = MXU_WIDTH // LANES
    q_slots = [sq_ref[j // slots_per_piece, :, (j % slots_per_piece) * LANES:
                      (j % slots_per_piece + 1) * LANES].astype(F32) for j in range(SWA_SLOTS)]
    parts = q_slots + [k]
    ms = _split_mm(jnp.concatenate([p * p for p in parts], axis=0), seg_ref[...])
    qn = [parts[j] * lax.rsqrt(ms[j * SWA_BLOCK:(j + 1) * SWA_BLOCK] + EPS)
          * qgain_ref[:, j * LANES:(j + 1) * LANES] for j in range(SWA_SLOTS)]
    kn = k * lax.rsqrt(ms[SWA_SLOTS * SWA_BLOCK:] + EPS) * kgain_ref[...]

    @pl.when(n == 0)
    def _():
        band_ref[0:SWA_BLOCK, :] = jnp.zeros((SWA_BLOCK, 2 * SWA_KV_WIDTH), F32)

    @pl.when(n > 0)
    def _():
        band_ref[0:SWA_BLOCK, :] = band_ref[SWA_BLOCK:SWA_BAND, :]

    band_ref[SWA_BLOCK:SWA_BAND, :SWA_KV_WIDTH] = kn
    band_ref[SWA_BLOCK:SWA_BAND, SWA_KV_WIDTH:] = kv[:, SWA_KV_WIDTH:]
    k = band_ref[:, :SWA_KV_WIDTH]
    v = band_ref[:, SWA_KV_WIDTH:]

    lane = lax.broadcasted_iota(jnp.int32, (SWA_BAND, LANES), 1)
    low = lane < SWA_HEAD_DIM
    k_rot = pltpu.roll(k, SWA_HEAD_DIM, axis=1)
    v_rot = pltpu.roll(v, SWA_HEAD_DIM, axis=1)
    k_lo = (jnp.where(low, k, 0.0), jnp.where(low, k_rot, 0.0))
    k_hi = (jnp.where(low, 0.0, k_rot), jnp.where(low, 0.0, k))
    v_lo = (jnp.where(low, v, 0.0), jnp.where(low, v_rot, 0.0))
    v_hi = (jnp.where(low, 0.0, v_rot), jnp.where(low, 0.0, v))

    out_low = lax.broadcasted_iota(jnp.int32, (SWA_BLOCK, LANES), 1) < SWA_HEAD_DIM

    slots_per_kv = SWA_SLOTS // SWA_KV_HEADS
    k_bd = [jnp.concatenate([k_lo[kh], k_hi[kh]], axis=0).astype(BF16)
            for kh in range(SWA_KV_HEADS)]
    v_bd = [jnp.concatenate([v_lo[kh], v_hi[kh]], axis=0).astype(BF16)
            for kh in range(SWA_KV_HEADS)]
    logits = [_mm(qn[j], k_bd[j // slots_per_kv], _NT) for j in range(SWA_SLOTS)]
    probs = [[] for _ in range(SWA_SLOTS)]
    inv_den = [[] for _ in range(SWA_SLOTS)]
    for j in range(SWA_SLOTS):
        for r in range(SWA_HEADS_PER_SLOT):
            hd = j * SWA_HEADS_PER_SLOT + r
            lg = logits[j][:, r * SWA_BAND:(r + 1) * SWA_BAND] + bias_ref[0, hd]
            sink = sinks_ref[hd]
            m = jnp.maximum(jnp.max(lg, axis=-1, keepdims=True), sink)
            p = jnp.exp(lg - m)
            den = jnp.sum(p, axis=-1, keepdims=True) + jnp.exp(sink - m)
            probs[j].append(p.astype(BF16))
            inv_den[j].append(1.0 / den)
    outs = [_mm(jnp.concatenate(probs[j], axis=1), v_bd[j // slots_per_kv])
            for j in range(SWA_SLOTS)]
    for j in range(SWA_SLOTS):
        o_ref[0, :, j * LANES:(j + 1) * LANES] = (
            outs[j] * jnp.where(out_low, inv_den[j][0], inv_den[j][1])).astype(o_ref.dtype)


def _swa(rest, batch, sinks, bias_tabs, qgain, kgain, seg):
    B = batch
    S = rest.shape[1] // B
    nb = S // SWA_BLOCK
    kvw = 2 * SWA_KV_WIDTH
    sq_pieces = SWA_WIDTH // MXU_WIDTH
    assert kvw == MXU_WIDTH
    grid_spec = pltpu.PrefetchScalarGridSpec(
        num_scalar_prefetch=1,
        grid=(B, nb),
        in_specs=[
            pl.BlockSpec((sq_pieces, SWA_BLOCK, MXU_WIDTH),
                         lambda b, n, s: (PIECE_SQ // sq_pieces, b * nb + n, 0)),
            pl.BlockSpec((1, SWA_BLOCK, kvw), lambda b, n, s: (PIECE_SKV, b * nb + n, 0)),
            pl.BlockSpec((1, SWA_HEADS, SWA_BLOCK, SWA_BAND),
                         lambda b, n, s: (jnp.minimum(n, 1), 0, 0, 0)),
            pl.BlockSpec((1, SWA_WIDTH), lambda b, n, s: (0, 0)),
            pl.BlockSpec((1, SWA_KV_WIDTH), lambda b, n, s: (0, 0)),
            pl.BlockSpec((LANES, LANES), lambda b, n, s: (0, 0)),
        ],
        out_specs=pl.BlockSpec((1, SWA_BLOCK, SWA_WIDTH), lambda b, n, s: (b, n, 0)),
        scratch_shapes=[pltpu.VMEM((SWA_BAND, kvw), F32)],
    )
    return pl.pallas_call(
        _swa_kernel,
        grid_spec=grid_spec,
        out_shape=jax.ShapeDtypeStruct((B, S, SWA_WIDTH), BF16),
        compiler_params=pltpu.CompilerParams(
            dimension_semantics=("arbitrary", "arbitrary"), vmem_limit_bytes=VMEM_LIMIT),
        name="swa",
    )(sinks, rest, rest, bias_tabs, qgain, kgain, seg)


def _merge_ffn_kernel(x_ref, ydn_ref, yswa_ref, g_ref, wa_ref, wb_ref, wo_ref, gain_ref,
                      wg_ref, wu_ref, wd_ref, o_ref, acc_ref, h2_ref):
    j = pl.program_id(1)

    @pl.when(j == 0)
    def _():
        per_branch = D_MODEL // MXU_WIDTH
        gates = [_sigmoid(jnp.concatenate(
            [g_ref[br * per_branch + p].astype(F32) for p in range(per_branch)], axis=1))
            for br in range(2)]
        ga, gb = gates
        merged = ga * _mm(ydn_ref[...], wa_ref[...]) + gb * _mm(yswa_ref[...], wb_ref[...])
        x1 = x_ref[...] + _mm(merged, wo_ref[...])
        acc_ref[...] = x1
        ms = jnp.mean(x1 * x1, axis=-1, keepdims=True)
        h2_ref[...] = (x1 * lax.rsqrt(ms + EPS) * gain_ref[...]).astype(BF16)

    h2 = h2_ref[...]
    gate = jnp.dot(h2, wg_ref[...], preferred_element_type=F32)
    up = jnp.dot(h2, wu_ref[...], preferred_element_type=F32)
    acc_ref[...] += _mm(_silu(gate) * up, wd_ref[...])

    @pl.when(j == pl.num_programs(1) - 1)
    def _():
        o_ref[...] = acc_ref[...]


def _merge_ffn(x2, ydn, yswa, graw, wa, wb, wo, gain, wg, wu, wd, tm, tf):
    T = x2.shape[0]
    row = lambda w: pl.BlockSpec((tm, w), lambda i, j: (i, 0))
    once = lambda a, b: pl.BlockSpec((a, b), lambda i, j: (0, 0), pipeline_mode=pl.Buffered(1))
    return pl.pallas_call(
        _merge_ffn_kernel,
        grid=(T // tm, D_FF // tf),
        in_specs=[row(D_MODEL), row(DN_WIDTH), row(SWA_WIDTH),
                  pl.BlockSpec((PIECE_SQ - PIECE_GATES, tm, MXU_WIDTH), lambda i, j: (0, i, 0)),
                  once(DN_WIDTH, D_MODEL), once(SWA_WIDTH, D_MODEL), once(D_MODEL, D_MODEL),
                  once(1, D_MODEL),
                  pl.BlockSpec((D_MODEL, tf), lambda i, j: (0, j)),
                  pl.BlockSpec((D_MODEL, tf), lambda i, j: (0, j)),
                  pl.BlockSpec((tf, D_MODEL), lambda i, j: (j, 0))],
        out_specs=row(D_MODEL),
        out_shape=jax.ShapeDtypeStruct((T, D_MODEL), F32),
        scratch_shapes=[pltpu.VMEM((tm, D_MODEL), F32), pltpu.VMEM((tm, D_MODEL), BF16)],
        compiler_params=pltpu.CompilerParams(
            dimension_semantics=("arbitrary", "arbitrary"), vmem_limit_bytes=VMEM_LIMIT),
        name="merge_ffn",
    )(x2, ydn, yswa, graw, wa, wb, wo, gain, wg, wu, wd)


def _t5_bucket_table():
    qi = jnp.arange(SWA_BLOCK)[:, None]
    kj = jnp.arange(SWA_BAND)[None, :]
    dist = SWA_BLOCK + qi - kj
    in_window = (dist >= 0) & (dist < WINDOW)
    n = jnp.maximum(dist, 0)
    max_exact = REL_BUCKETS // 2
    nf = jnp.maximum(n, 1).astype(F32)
    large = max_exact + (jnp.log(nf / max_exact) / math.log(REL_MAX_DIST / max_exact)
                         * (REL_BUCKETS - max_exact)).astype(jnp.int32)
    large = jnp.minimum(large, REL_BUCKETS - 1)
    return jnp.where(n < max_exact, n, large), in_window


def _segment_mean_matrix(width, seg):
    idx = np.arange(width) // seg
    return jnp.asarray((idx[:, None] == idx[None, :]).astype(np.float32) / seg, dtype=BF16)


def kernel(x, attn_norm, w_in, dn_conv, dn_a_log, dn_dt_bias, dn_out_norm, swa_q_norm, swa_k_norm, swa_sinks, rel_bias, w_branch_dn, w_branch_swa, w_out, ffn_norm, w_gate, w_up, w_down):
    B, S, D = x.shape
    T = B * S
    depth = w_in.shape[0]
    bucket, in_window = _t5_bucket_table()
    seg = _segment_mean_matrix(LANES, SWA_HEAD_DIM)
    rel = rel_bias.astype(F32)
    bias_tab = sum(jnp.where(bucket[None] == b, rel[b][:, None, None], 0.0)
                   for b in range(REL_BUCKETS))
    bias_tab = jnp.where(in_window[None], bias_tab, -jnp.inf)
    has_prev = jnp.arange(SWA_BAND)[None, None, :] >= SWA_BLOCK
    bias_tabs = jnp.stack([jnp.where(has_prev, bias_tab, -jnp.inf), bias_tab])

    x2 = x.reshape(T, D)
    for l in range(depth):
        wl = w_in[l].astype(BF16)
        o_ba = W_HEAD
        o_sq = o_ba + 2 * DN_HEADS
        o_g = o_sq + SWA_WIDTH + 2 * SWA_KV_WIDTH
        w_rest = jnp.concatenate([wl[:, o_g:], wl[:, o_sq:o_g]], axis=1)
        w_rest = w_rest.reshape(D, REST_PIECES, MXU_WIDTH).transpose(1, 0, 2)
        w_ba = jnp.pad(wl[:, o_ba:o_sq], ((0, 0), (0, LANES - 2 * DN_HEADS)))
        convw = dn_conv[l].astype(F32).reshape(DN_CONV, QKV_BLOCKS, DN_HEAD_DIM).transpose(1, 0, 2)

        qkv, z, rest, ba = _in_proj(x2, attn_norm[l][None, :], wl, w_rest, w_ba, tm=512)

        pad_lo = jnp.zeros((DN_HEADS,), F32)
        pad_hi = jnp.zeros((LANES - 2 * DN_HEADS,), F32)
        alog_lane = jnp.concatenate([pad_lo, dn_a_log[l].astype(F32), pad_hi])[None, :]
        dtb_lane = jnp.concatenate([pad_lo, dn_dt_bias[l].astype(F32), pad_hi])[None, :]
        y_dn = _deltanet(qkv, z.reshape(B, S, -1), ba.reshape(B, S, -1), convw,
                         alog_lane, dtb_lane, dn_out_norm[l][None, :])

        qgain = jnp.tile(swa_q_norm[l].astype(F32), SWA_HEADS)[None, :] * (SWA_HEAD_DIM ** -0.5)
        kgain = jnp.tile(swa_k_norm[l].astype(F32), SWA_KV_HEADS)[None, :]
        y_swa = _swa(rest, B, swa_sinks[l].astype(F32), bias_tabs, qgain, kgain, seg)

        x2 = _merge_ffn(x2, y_dn.reshape(T, -1), y_swa.reshape(T, -1), rest,
                        w_branch_dn[l].astype(BF16), w_branch_swa[l].astype(BF16),
                        w_out[l].astype(BF16), ffn_norm[l][None, :],
                        w_gate[l].astype(BF16), w_up[l].astype(BF16), w_down[l].astype(BF16),
                        tm=512, tf=1408)
    return x2.reshape(B, S, D)
```

```python
import functools
import math

import numpy as np
import jax
import jax.numpy as jnp
from jax import lax
from jax.experimental import pallas as pl
from jax.experimental.pallas import tpu as pltpu

D_MODEL = 1024
DN_HEADS = 4
DN_HEAD_DIM = 128
DN_WIDTH = DN_HEADS * DN_HEAD_DIM
DN_QKV_WIDTH = 3 * DN_WIDTH
DN_CONV = 4
DN_CHUNK = 64
SWA_HEADS = 8
SWA_KV_HEADS = 2
SWA_HEAD_DIM = 64
SWA_WIDTH = SWA_HEADS * SWA_HEAD_DIM
SWA_KV_WIDTH = SWA_KV_HEADS * SWA_HEAD_DIM
WINDOW = 128
SWA_BLOCK = 128
REL_BUCKETS = 32
REL_MAX_DIST = 128
D_FF = 2816
EPS = 1e-6

LANES = 128
SUBLANES = 8
MXU_WIDTH = 256
VMEM_LIMIT = 56 * 1024 * 1024

W_HEAD = DN_QKV_WIDTH + DN_WIDTH
W_REST = 2 * D_MODEL + SWA_WIDTH + 2 * SWA_KV_WIDTH
REST_PIECES = W_REST // MXU_WIDTH
PIECE_GATES = 0
PIECE_SQ = 2 * D_MODEL // MXU_WIDTH
PIECE_SKV = PIECE_SQ + SWA_WIDTH // MXU_WIDTH
QKV_BLOCKS = DN_QKV_WIDTH // DN_HEAD_DIM

F32 = jnp.float32
BF16 = jnp.bfloat16


def _mm(a, b, dims=(((1,), (0,)), ((), ()))):
    return lax.dot_general(a.astype(BF16), b.astype(BF16), dims, preferred_element_type=F32)


_NT = (((1,), (1,)), ((), ()))
_TN = (((0,), (0,)), ((), ()))


def _sigmoid(x):
    return 1.0 / (1.0 + jnp.exp(-x))


def _silu(x):
    return x * _sigmoid(x)


CONV_TAIL = SUBLANES
CONV_ROWS = 128


def _in_proj_kernel(x_ref, gain_ref, w_head_ref, w_rest_ref, w_ba_ref,
                    qkv_ref, z_ref, rest_ref, ba_ref):
    x = x_ref[...]
    ms = jnp.mean(x * x, axis=-1, keepdims=True)
    h = (x * lax.rsqrt(ms + EPS) * gain_ref[...]).astype(BF16)
    raw = jnp.dot(h, w_head_ref[:, :DN_QKV_WIDTH], preferred_element_type=F32)
    for c in range(QKV_BLOCKS):
        qkv_ref[c] = raw[:, c * DN_HEAD_DIM:(c + 1) * DN_HEAD_DIM]
    z_ref[...] = jnp.dot(h, w_head_ref[:, DN_QKV_WIDTH:], preferred_element_type=F32)
    for p in range(REST_PIECES):
        rest_ref[p] = jnp.dot(h, w_rest_ref[p], preferred_element_type=F32).astype(rest_ref.dtype)
    ba_ref[...] = jnp.dot(h, w_ba_ref[...], preferred_element_type=F32)


def _in_proj(x2, gain, w_in_bf16, w_rest, w_ba, tm):
    T = x2.shape[0]
    once = lambda shape: pl.BlockSpec(shape, lambda i: (0,) * len(shape),
                                      pipeline_mode=pl.Buffered(1))
    return pl.pallas_call(
        _in_proj_kernel,
        grid=(T // tm,),
        in_specs=[
            pl.BlockSpec((tm, D_MODEL), lambda i: (i, 0)),
            pl.BlockSpec((1, D_MODEL), lambda i: (0, 0)),
            once((D_MODEL, W_HEAD)),
            once((REST_PIECES, D_MODEL, MXU_WIDTH)),
            once((D_MODEL, LANES)),
        ],
        out_specs=(
            pl.BlockSpec((QKV_BLOCKS, tm, DN_HEAD_DIM), lambda i: (0, i, 0)),
            pl.BlockSpec((tm, DN_WIDTH), lambda i: (i, 0)),
            pl.BlockSpec((REST_PIECES, tm, MXU_WIDTH), lambda i: (0, i, 0)),
            pl.BlockSpec((tm, LANES), lambda i: (i, 0)),
        ),
        out_shape=(
            jax.ShapeDtypeStruct((QKV_BLOCKS, T, DN_HEAD_DIM), F32),
            jax.ShapeDtypeStruct((T, DN_WIDTH), F32),
            jax.ShapeDtypeStruct((REST_PIECES, T, MXU_WIDTH), BF16),
            jax.ShapeDtypeStruct((T, LANES), F32),
        ),
        compiler_params=pltpu.CompilerParams(
            dimension_semantics=("arbitrary",), vmem_limit_bytes=VMEM_LIMIT),
        name="in_proj",
    )(x2, gain, w_in_bf16, w_rest, w_ba)


DN_TILE = 256
DN_NCH = DN_TILE // DN_CHUNK
DN_LOG_CHUNK = int(math.log2(DN_CHUNK))


def _deltanet_kernel(raw_ref, raw_next_ref, z_ref, ba_ref, convw_ref, alog_ref, dtb_ref, gain_ref,
                     o_ref, state, xbuf, qkv_ref, qkv_next_ref):
    t = pl.program_id(1)
    scale = DN_HEAD_DIM ** -0.5

    def preprocess(dst_ref):
        for c in range(QKV_BLOCKS):
            w = convw_ref[c]
            for r0 in range(0, DN_TILE, CONV_ROWS):
                acc = w[DN_CONV - 1:DN_CONV, :] * xbuf[c, CONV_TAIL + r0:CONV_TAIL + r0 + CONV_ROWS, :]
                for j in range(DN_CONV - 1):
                    off = CONV_TAIL - (DN_CONV - 1) + j + r0
                    acc = acc + w[j:j + 1, :] * xbuf[c, off:off + CONV_ROWS, :]
                act = _silu(acc)
                if c < 2 * DN_HEADS:
                    norm = lax.rsqrt(jnp.sum(act * act, axis=-1, keepdims=True) + EPS)
                    act = act * (norm * scale if c < DN_HEADS else norm)
                dst_ref[c, r0:r0 + CONV_ROWS, :] = act.astype(BF16)

    @pl.when(t == 0)
    def _():
        state[...] = jnp.zeros_like(state)
        xbuf[:, 0:CONV_TAIL, :] = jnp.zeros((QKV_BLOCKS, CONV_TAIL, DN_HEAD_DIM), F32)
        xbuf[:, CONV_TAIL:, :] = raw_ref[...]
        preprocess(qkv_ref)

    xbuf[:, 0:CONV_TAIL, :] = raw_ref[:, DN_TILE - CONV_TAIL:DN_TILE, :]
    xbuf[:, CONV_TAIL:, :] = raw_next_ref[...]
    preprocess(qkv_next_ref)

    ba = ba_ref[0]
    beta = _sigmoid(ba)
    xs = ba + dtb_ref[...]
    softplus = jnp.maximum(xs, 0.0) + jnp.log1p(jnp.exp(-jnp.abs(xs)))
    g = -jnp.exp(alog_ref[...]) * softplus

    row = lax.broadcasted_iota(jnp.int32, (DN_TILE, DN_TILE), 0)
    col = lax.broadcasted_iota(jnp.int32, (DN_TILE, DN_TILE), 1)
    same_chunk = (row >> DN_LOG_CHUNK) == (col >> DN_LOG_CHUNK)
    lower_incl = same_chunk & (row >= col)
    strict_lower = same_chunk & (row > col)
    g_hi = g.astype(BF16)
    g_lo = (g - g_hi.astype(F32)).astype(BF16)
    gc2 = jnp.dot(jnp.where(lower_incl, 1.0, 0.0).astype(BF16),
                  jnp.concatenate([g_hi, g_lo], axis=1), preferred_element_type=F32)
    gc = gc2[:, :LANES] + gc2[:, LANES:]
    g_last = jnp.concatenate(
        [jnp.broadcast_to(gc[(c + 1) * DN_CHUNK - 1:(c + 1) * DN_CHUNK, :], (DN_CHUNK, LANES))
         for c in range(DN_NCH)], axis=0)
    rev = g_last - gc
    gc_t = gc.T
    eye = jnp.where(row == col, 1.0, 0.0)

    heads = range(DN_HEADS)
    gcol = lambda arr, h: arr[:, DN_HEADS + h:DN_HEADS + h + 1]
    q, k, kb, rhs = [], [], [], []
    for h in heads:
        qh = qkv_ref[h].astype(F32)
        kh = qkv_ref[DN_HEADS + h].astype(F32)
        vh = qkv_ref[2 * DN_HEADS + h].astype(F32)
        beta_c = beta[:, h:h + 1]
        q.append(qh)
        k.append(kh)
        kb.append(kh * beta_c)
        rhs.append(jnp.concatenate([vh * beta_c, kb[h] * jnp.exp(gcol(gc, h))], axis=1))
    gram = [_mm(jnp.concatenate([kb[h], q[h]], axis=0), k[h], _NT) for h in heads]
    qk, n_pow, inv = [], [], []
    for h in heads:
        gc_r = gc_t[DN_HEADS + h:DN_HEADS + h + 1, :]
        decay = jnp.exp(jnp.where(lower_incl, gcol(gc, h) - gc_r, -jnp.inf))
        qk.append(gram[h][DN_TILE:] * decay)
        n_pow.append(-jnp.where(strict_lower, gram[h][:DN_TILE] * decay, 0.0))
        inv.append(eye + n_pow[h])
    for _ in range(DN_LOG_CHUNK - 1):
        n_pow = [_mm(n_pow[h], n_pow[h]) for h in heads]
        inv = [inv[h] + _mm(inv[h], n_pow[h]) for h in heads]
    sol = [_mm(inv[h], rhs[h]) for h in heads]
    u = [sol[h][:, :DN_HEAD_DIM] for h in heads]
    wmat = [sol[h][:, DN_HEAD_DIM:] for h in heads]
    qe = [q[h] * jnp.exp(gcol(gc, h)) for h in heads]
    kdec = [k[h] * jnp.exp(gcol(rev, h)) for h in heads]

    s = [state[h] for h in heads]
    v_new = [[] for _ in heads]
    o_inter = [[] for _ in heads]
    for c in range(DN_NCH):
        r0, r1 = c * DN_CHUNK, (c + 1) * DN_CHUNK
        ws = [_mm(jnp.concatenate([wmat[h][r0:r1], qe[h][r0:r1]], axis=0), s[h]) for h in heads]
        for h in heads:
            v_new[h].append(u[h][r0:r1] - ws[h][:DN_CHUNK])
            o_inter[h].append(ws[h][DN_CHUNK:])
        upd = [_mm(kdec[h][r0:r1], v_new[h][c], _TN) for h in heads]
        s = [s[h] * jnp.exp(gc[r1 - 1:r1, DN_HEADS + h:DN_HEADS + h + 1]) + upd[h]
             for h in heads]
    o_intra = [_mm(qk[h], jnp.concatenate(v_new[h], axis=0)) for h in heads]
    for h in heads:
        state[h] = s[h]
        o = jnp.concatenate(o_inter[h], axis=0) + o_intra[h]
        o = o * lax.rsqrt(jnp.mean(o * o, axis=-1, keepdims=True) + EPS) * gain_ref[...]
        zh = z_ref[0, :, h * DN_HEAD_DIM:(h + 1) * DN_HEAD_DIM]
        o_ref[0, :, h * DN_HEAD_DIM:(h + 1) * DN_HEAD_DIM] = (o * _silu(zh)).astype(o_ref.dtype)
    qkv_ref[...] = qkv_next_ref[...]


def _deltanet(raw, z, ba, convw, alog_lane, dtb_lane, out_gain):
    B, S, _ = z.shape
    tiles = S // DN_TILE
    raw_block = (QKV_BLOCKS, DN_TILE, DN_HEAD_DIM)
    return pl.pallas_call(
        _deltanet_kernel,
        grid=(B, tiles),
        in_specs=[
            pl.BlockSpec(raw_block, lambda b, t: (0, b * tiles + t, 0)),
            pl.BlockSpec(raw_block, lambda b, t: (0, b * tiles + jnp.minimum(t + 1, tiles - 1), 0)),
            pl.BlockSpec((1, DN_TILE, DN_WIDTH), lambda b, t: (b, t, 0)),
            pl.BlockSpec((1, DN_TILE, LANES), lambda b, t: (b, t, 0)),
            pl.BlockSpec((QKV_BLOCKS, DN_CONV, DN_HEAD_DIM), lambda b, t: (0, 0, 0)),
            pl.BlockSpec((1, LANES), lambda b, t: (0, 0)),
            pl.BlockSpec((1, LANES), lambda b, t: (0, 0)),
            pl.BlockSpec((1, DN_HEAD_DIM), lambda b, t: (0, 0)),
        ],
        out_specs=pl.BlockSpec((1, DN_TILE, DN_WIDTH), lambda b, t: (b, t, 0)),
        out_shape=jax.ShapeDtypeStruct((B, S, DN_WIDTH), BF16),
        scratch_shapes=[pltpu.VMEM((DN_HEADS, DN_HEAD_DIM, DN_HEAD_DIM), F32),
                        pltpu.VMEM((QKV_BLOCKS, CONV_TAIL + DN_TILE, DN_HEAD_DIM), F32),
                        pltpu.VMEM(raw_block, BF16), pltpu.VMEM(raw_block, BF16)],
        compiler_params=pltpu.CompilerParams(
            dimension_semantics=("arbitrary", "arbitrary"), vmem_limit_bytes=VMEM_LIMIT),
        name="deltanet",
    )(raw, raw, z, ba, convw, alog_lane, dtb_lane, out_gain)


SWA_BAND = 2 * SWA_BLOCK
SWA_SLOTS = SWA_WIDTH // LANES
SWA_HEADS_PER_SLOT = LANES // SWA_HEAD_DIM


def _split_mm(x, w):
    rows = x.shape[0]
    hi = x.astype(BF16)
    lo = (x - hi.astype(F32)).astype(BF16)
    r = jnp.dot(jnp.concatenate([hi, lo], axis=0), w, preferred_element_type=F32)
    return r[:rows] + r[rows:]


def _swa_kernel(sinks_ref, sq_ref, kv_ref, bias_ref, qgain_ref, kgain_ref, seg_ref, o_ref,
                band_ref):
    n = pl.program_id(1)
    kv = kv_ref[0].astype(F32)
    k = kv[:, :SWA_KV_WIDTH]
    slots_per_piece = MXU_WIDTH // LANES
    q_slots = [sq_ref[j // slots_per_piece, :, (j % slots_per_piece) * LANES:
                      (j % slots_per_piece + 1) * LANES].astype(F32) for j in range(SWA_SLOTS)]
    parts = q_slots + [k]
    ms = _split_mm(jnp.concatenate([p * p for p in parts], axis=0), seg_ref[...])
    qn = [parts[j] * lax.rsqrt(ms[j * SWA_BLOCK:(j + 1) * SWA_BLOCK] + EPS)
          * qgain_ref[:, j * LANES:(j + 1) * LANES] for j in range(SWA_SLOTS)]
    kn = k * lax.rsqrt(ms[SWA_SLOTS * SWA_BLOCK:] + EPS) * kgain_ref[...]

    @pl.when(n == 0)
    def _():
        band_ref[0:SWA_BLOCK, :] = jnp.zeros((SWA_BLOCK, 2 * SWA_KV_WIDTH), F32)

    @pl.when(n > 0)
    def _():
        band_ref[0:SWA_BLOCK, :] = band_ref[SWA_BLOCK:SWA_BAND, :]

    band_ref[SWA_BLOCK:SWA_BAND, :SWA_KV_WIDTH] = kn
    band_ref[SWA_BLOCK:SWA_BAND, SWA_KV_WIDTH:] = kv[:, SWA_KV_WIDTH:]
    k = band_ref[:, :SWA_KV_WIDTH]
    v = band_ref[:, SWA_KV_WIDTH:]

    lane = lax.broadcasted_iota(jnp.int32, (SWA_BAND, LANES), 1)
    low = lane < SWA_HEAD_DIM
    k_rot = pltpu.roll(k, SWA_HEAD_DIM, axis=1)
    v_rot = pltpu.roll(v, SWA_HEAD_DIM, axis=1)
    k_lo = (jnp.where(low, k, 0.0), jnp.where(low, k_rot, 0.0))
    k_hi = (jnp.where(low, 0.0, k_rot), jnp.where(low, 0.0, k))
    v_lo = (jnp.where(low, v, 0.0), jnp.where(low, v_rot, 0.0))
    v_hi = (jnp.where(low, 0.0, v_rot), jnp.where(low, 0.0, v))

    out_low = lax.broadcasted_iota(jnp.int32, (SWA_BLOCK, LANES), 1) < SWA_HEAD_DIM

    slots_per_kv = SWA_SLOTS // SWA_KV_HEADS
    k_bd = [jnp.concatenate([k_lo[kh], k_hi[kh]], axis=0).astype(BF16)
            for kh in range(SWA_KV_HEADS)]
    v_bd = [jnp.concatenate([v_lo[kh], v_hi[kh]], axis=0).astype(BF16)
            for kh in range(SWA_KV_HEADS)]
    logits = [_mm(qn[j], k_bd[j // slots_per_kv], _NT) for j in range(SWA_SLOTS)]
    probs = [[] for _ in range(SWA_SLOTS)]
    inv_den = [[] for _ in range(SWA_SLOTS)]
    for j in range(SWA_SLOTS):
        for r in range(SWA_HEADS_PER_SLOT):
            hd = j * SWA_HEADS_PER_SLOT + r
            lg = logits[j][:, r * SWA_BAND:(r + 1) * SWA_BAND] + bias_ref[0, hd]
            sink = sinks_ref[hd]
            m = jnp.maximum(jnp.max(lg, axis=-1, keepdims=True), sink)
            p = jnp.exp(lg - m)
            den = jnp.sum(p, axis=-1, keepdims=True) + jnp.exp(sink - m)
            probs[j].append(p.astype(BF16))
            inv_den[j].append(1.0 / den)
    outs = [_mm(jnp.concatenate(probs[j], axis=1), v_bd[j // slots_per_kv])
            for j in range(SWA_SLOTS)]
    for j in range(SWA_SLOTS):
        o_ref[0, :, j * LANES:(j + 1) * LANES] = (
            outs[j] * jnp.where(out_low, inv_den[j][0], inv_den[j][1])).astype(o_ref.dtype)


def _swa(rest, batch, sinks, bias_tabs, qgain, kgain, seg):
    B = batch
    S = rest.shape[1] // B
    nb = S // SWA_BLOCK
    kvw = 2 * SWA_KV_WIDTH
    sq_pieces = SWA_WIDTH // MXU_WIDTH
    assert kvw == MXU_WIDTH
    grid_spec = pltpu.PrefetchScalarGridSpec(
        num_scalar_prefetch=1,
        grid=(B, nb),
        in_specs=[
            pl.BlockSpec((sq_pieces, SWA_BLOCK, MXU_WIDTH),
                         lambda b, n, s: (PIECE_SQ // sq_pieces, b * nb + n, 0)),
            pl.BlockSpec((1, SWA_BLOCK, kvw), lambda b, n, s: (PIECE_SKV, b * nb + n, 0)),
            pl.BlockSpec((1, SWA_HEADS, SWA_BLOCK, SWA_BAND),
                         lambda b, n, s: (jnp.minimum(n, 1), 0, 0, 0)),
            pl.BlockSpec((1, SWA_WIDTH), lambda b, n, s: (0, 0)),
            pl.BlockSpec((1, SWA_KV_WIDTH), lambda b, n, s: (0, 0)),
            pl.BlockSpec((LANES, LANES), lambda b, n, s: (0, 0)),
        ],
        out_specs=pl.BlockSpec((1, SWA_BLOCK, SWA_WIDTH), lambda b, n, s: (b, n, 0)),
        scratch_shapes=[pltpu.VMEM((SWA_BAND, kvw), F32)],
    )
    return pl.pallas_call(
        _swa_kernel,
        grid_spec=grid_spec,
        out_shape=jax.ShapeDtypeStruct((B, S, SWA_WIDTH), BF16),
        compiler_params=pltpu.CompilerParams(
            dimension_semantics=("arbitrary", "arbitrary"), vmem_limit_bytes=VMEM_LIMIT),
        name="swa",
    )(sinks, rest, rest, bias_tabs, qgain, kgain, seg)


def _merge_ffn_kernel(x_ref, ydn_ref, yswa_ref, g_ref, wa_ref, wb_ref, wo_ref, gain_ref,
                      wg_ref, wu_ref, wd_ref, o_ref, act_ref):
    per_branch = D_MODEL // MXU_WIDTH
    ga, gb = [_sigmoid(jnp.concatenate(
        [g_ref[br * per_branch + p].astype(F32) for p in range(per_branch)], axis=1))
        for br in range(2)]
    merged = ga * _mm(ydn_ref[...], wa_ref[...]) + gb * _mm(yswa_ref[...], wb_ref[...])
    x1 = x_ref[...] + _mm(merged, wo_ref[...])
    ms = jnp.mean(x1 * x1, axis=-1, keepdims=True)
    h2 = (x1 * lax.rsqrt(ms + EPS) * gain_ref[...]).astype(BF16)
    for c0 in range(0, D_FF, MXU_WIDTH):
        gate = jnp.dot(h2, wg_ref[:, c0:c0 + MXU_WIDTH], preferred_element_type=F32)
        up = jnp.dot(h2, wu_ref[:, c0:c0 + MXU_WIDTH], preferred_element_type=F32)
        act_ref[:, c0:c0 + MXU_WIDTH] = (_silu(gate) * up).astype(BF16)
    o_ref[...] = x1 + jnp.dot(act_ref[...], wd_ref[...], preferred_element_type=F32)


def _merge_ffn(x2, ydn, yswa, graw, wa, wb, wo, gain, wg, wu, wd, tm):
    T = x2.shape[0]
    row = lambda w: pl.BlockSpec((tm, w), lambda i: (i, 0))
    once = lambda a, b: pl.BlockSpec((a, b), lambda i: (0, 0), pipeline_mode=pl.Buffered(1))
    return pl.pallas_call(
        _merge_ffn_kernel,
        grid=(T // tm,),
        in_specs=[row(D_MODEL), row(DN_WIDTH), row(SWA_WIDTH),
                  pl.BlockSpec((PIECE_SQ - PIECE_GATES, tm, MXU_WIDTH), lambda i: (0, i, 0)),
                  once(DN_WIDTH, D_MODEL), once(SWA_WIDTH, D_MODEL), once(D_MODEL, D_MODEL),
                  once(1, D_MODEL),
                  once(D_MODEL, D_FF), once(D_MODEL, D_FF), once(D_FF, D_MODEL)],
        out_specs=row(D_MODEL),
        out_shape=jax.ShapeDtypeStruct((T, D_MODEL), F32),
        scratch_shapes=[pltpu.VMEM((tm, D_FF), BF16)],
        compiler_params=pltpu.CompilerParams(
            dimension_semantics=("arbitrary",), vmem_limit_bytes=VMEM_LIMIT),
        name="merge_ffn",
    )(x2, ydn, yswa, graw, wa, wb, wo, gain, wg, wu, wd)


def _t5_bucket_table():
    qi = jnp.arange(SWA_BLOCK)[:, None]
    kj = jnp.arange(SWA_BAND)[None, :]
    dist = SWA_BLOCK + qi - kj
    in_window = (dist >= 0) & (dist < WINDOW)
    n = jnp.maximum(dist, 0)
    max_exact = REL_BUCKETS // 2
    nf = jnp.maximum(n, 1).astype(F32)
    large = max_exact + (jnp.log(nf / max_exact) / math.log(REL_MAX_DIST / max_exact)
                         * (REL_BUCKETS - max_exact)).astype(jnp.int32)
    large = jnp.minimum(large, REL_BUCKETS - 1)
    return jnp.where(n < max_exact, n, large), in_window


def _segment_mean_matrix(width, seg):
    idx = np.arange(width) // seg
    return jnp.asarray((idx[:, None] == idx[None, :]).astype(np.float32) / seg, dtype=BF16)


def kernel(x, attn_norm, w_in, dn_conv, dn_a_log, dn_dt_bias, dn_out_norm, swa_q_norm, swa_k_norm, swa_sinks, rel_bias, w_branch_dn, w_branch_swa, w_out, ffn_norm, w_gate, w_up, w_down):
    B, S, D = x.shape
    T = B * S
    depth = w_in.shape[0]
    bucket, in_window = _t5_bucket_table()
    seg = _segment_mean_matrix(LANES, SWA_HEAD_DIM)
    rel = rel_bias.astype(F32)
    bias_tab = sum(jnp.where(bucket[None] == b, rel[b][:, None, None], 0.0)
                   for b in range(REL_BUCKETS))
    bias_tab = jnp.where(in_window[None], bias_tab, -jnp.inf)
    has_prev = jnp.arange(SWA_BAND)[None, None, :] >= SWA_BLOCK
    bias_tabs = jnp.stack([jnp.where(has_prev, bias_tab, -jnp.inf), bias_tab])

    x2 = x.reshape(T, D)
    for l in range(depth):
        wl = w_in[l].astype(BF16)
        o_ba = W_HEAD
        o_sq = o_ba + 2 * DN_HEADS
        o_g = o_sq + SWA_WIDTH + 2 * SWA_KV_WIDTH
        w_rest = jnp.concatenate([wl[:, o_g:], wl[:, o_sq:o_g]], axis=1)
        w_rest = w_rest.reshape(D, REST_PIECES, MXU_WIDTH).transpose(1, 0, 2)
        w_ba = jnp.pad(wl[:, o_ba:o_sq], ((0, 0), (0, LANES - 2 * DN_HEADS)))
        convw = dn_conv[l].astype(F32).reshape(DN_CONV, QKV_BLOCKS, DN_HEAD_DIM).transpose(1, 0, 2)

        qkv, z, rest, ba = _in_proj(x2, attn_norm[l][None, :], wl, w_rest, w_ba, tm=512)

        pad_lo = jnp.zeros((DN_HEADS,), F32)
        pad_hi = jnp.zeros((LANES - 2 * DN_HEADS,), F32)
        alog_lane = jnp.concatenate([pad_lo, dn_a_log[l].astype(F32), pad_hi])[None, :]
        dtb_lane = jnp.concatenate([pad_lo, dn_dt_bias[l].astype(F32), pad_hi])[None, :]
        y_dn = _deltanet(qkv, z.reshape(B, S, -1), ba.reshape(B, S, -1), convw,
                         alog_lane, dtb_lane, dn_out_norm[l][None, :])

        qgain = jnp.tile(swa_q_norm[l].astype(F32), SWA_HEADS)[None, :] * (SWA_HEAD_DIM ** -0.5)
        kgain = jnp.tile(swa_k_norm[l].astype(F32), SWA_KV_HEADS)[None, :]
        y_swa = _swa(rest, B, swa_sinks[l].astype(F32), bias_tabs, qgain, kgain, seg)

        x2 = _merge_ffn(x2, y_dn.reshape(T, -1), y_swa.reshape(T, -1), rest,
                        w_branch_dn[l].astype(BF16), w_branch_swa[l].astype(BF16),
                        w_out[l].astype(BF16), ffn_norm[l][None, :],
                        w_gate[l].astype(BF16), w_up[l].astype(BF16), w_down[l].astype(BF16),
                        tm=512)
    return x2.reshape(B, S, D)
```

```python
import functools
import math

import numpy as np
import jax
import jax.numpy as jnp
from jax import lax
from jax.experimental import pallas as pl
from jax.experimental.pallas import tpu as pltpu

D_MODEL = 1024
DN_HEADS = 4
DN_HEAD_DIM = 128
DN_WIDTH = DN_HEADS * DN_HEAD_DIM
DN_QKV_WIDTH = 3 * DN_WIDTH
DN_CONV = 4
DN_CHUNK = 64
SWA_HEADS = 8
SWA_KV_HEADS = 2
SWA_HEAD_DIM = 64
SWA_WIDTH = SWA_HEADS * SWA_HEAD_DIM
SWA_KV_WIDTH = SWA_KV_HEADS * SWA_HEAD_DIM
WINDOW = 128
SWA_BLOCK = 128
REL_BUCKETS = 32
REL_MAX_DIST = 128
D_FF = 2816
EPS = 1e-6

LANES = 128
SUBLANES = 8
MXU_WIDTH = 256
VMEM_LIMIT = 56 * 1024 * 1024

W_HEAD = DN_QKV_WIDTH + DN_WIDTH
W_REST = 2 * D_MODEL + SWA_WIDTH + 2 * SWA_KV_WIDTH
REST_PIECES = W_REST // MXU_WIDTH
PIECE_GATES = 0
PIECE_SQ = 2 * D_MODEL // MXU_WIDTH
PIECE_SKV = PIECE_SQ + SWA_WIDTH // MXU_WIDTH
QKV_BLOCKS = DN_QKV_WIDTH // DN_HEAD_DIM

F32 = jnp.float32
BF16 = jnp.bfloat16


def _mm(a, b, dims=(((1,), (0,)), ((), ()))):
    return lax.dot_general(a.astype(BF16), b.astype(BF16), dims, preferred_element_type=F32)


_NT = (((1,), (1,)), ((), ()))
_TN = (((0,), (0,)), ((), ()))


def _sigmoid(x):
    return 1.0 / (1.0 + jnp.exp(-x))


def _silu(x):
    return x * _sigmoid(x)


CONV_TAIL = SUBLANES
CONV_ROWS = 128


def _in_proj_kernel(x_ref, gain_ref, w_head_ref, w_rest_ref, w_ba_ref,
                    qkv_ref, z_ref, rest_ref, ba_ref):
    x = x_ref[...]
    ms = jnp.mean(x * x, axis=-1, keepdims=True)
    h = (x * lax.rsqrt(ms + EPS) * gain_ref[...]).astype(BF16)
    raw = jnp.dot(h, w_head_ref[:, :DN_QKV_WIDTH], preferred_element_type=F32)
    for c in range(QKV_BLOCKS):
        qkv_ref[c] = raw[:, c * DN_HEAD_DIM:(c + 1) * DN_HEAD_DIM]
    z_ref[...] = jnp.dot(h, w_head_ref[:, DN_QKV_WIDTH:], preferred_element_type=F32)
    for p in range(REST_PIECES):
        rest_ref[p] = jnp.dot(h, w_rest_ref[p], preferred_element_type=F32).astype(rest_ref.dtype)
    ba_ref[...] = jnp.dot(h, w_ba_ref[...], preferred_element_type=F32)


def _in_proj(x2, gain, w_in_bf16, w_rest, w_ba, tm):
    T = x2.shape[0]
    once = lambda shape: pl.BlockSpec(shape, lambda i: (0,) * len(shape),
                                      pipeline_mode=pl.Buffered(1))
    return pl.pallas_call(
        _in_proj_kernel,
        grid=(T // tm,),
        in_specs=[
            pl.BlockSpec((tm, D_MODEL), lambda i: (i, 0)),
            pl.BlockSpec((1, D_MODEL), lambda i: (0, 0)),
            once((D_MODEL, W_HEAD)),
            once((REST_PIECES, D_MODEL, MXU_WIDTH)),
            once((D_MODEL, LANES)),
        ],
        out_specs=(
            pl.BlockSpec((QKV_BLOCKS, tm, DN_HEAD_DIM), lambda i: (0, i, 0)),
            pl.BlockSpec((tm, DN_WIDTH), lambda i: (i, 0)),
            pl.BlockSpec((REST_PIECES, tm, MXU_WIDTH), lambda i: (0, i, 0)),
            pl.BlockSpec((tm, LANES), lambda i: (i, 0)),
        ),
        out_shape=(
            jax.ShapeDtypeStruct((QKV_BLOCKS, T, DN_HEAD_DIM), F32),
            jax.ShapeDtypeStruct((T, DN_WIDTH), F32),
            jax.ShapeDtypeStruct((REST_PIECES, T, MXU_WIDTH), BF16),
            jax.ShapeDtypeStruct((T, LANES), F32),
        ),
        compiler_params=pltpu.CompilerParams(
            dimension_semantics=("arbitrary",), vmem_limit_bytes=VMEM_LIMIT),
        name="in_proj",
    )(x2, gain, w_in_bf16, w_rest, w_ba)


DN_TILE = 256
DN_NCH = DN_TILE // DN_CHUNK
DN_LOG_CHUNK = int(math.log2(DN_CHUNK))


def _deltanet_kernel(raw_ref, raw_next_ref, z_ref, ba_ref, convw_ref, alog_ref, dtb_ref, gain_ref,
                     o_ref, state, xbuf, qkv_ref, qkv_next_ref):
    t = pl.program_id(1)
    scale = DN_HEAD_DIM ** -0.5

    def preprocess(dst_ref):
        for c in range(QKV_BLOCKS):
            w = convw_ref[c]
            for r0 in range(0, DN_TILE, CONV_ROWS):
                acc = w[DN_CONV - 1:DN_CONV, :] * xbuf[c, CONV_TAIL + r0:CONV_TAIL + r0 + CONV_ROWS, :]
                for j in range(DN_CONV - 1):
                    off = CONV_TAIL - (DN_CONV - 1) + j + r0
                    acc = acc + w[j:j + 1, :] * xbuf[c, off:off + CONV_ROWS, :]
                act = _silu(acc)
                if c < 2 * DN_HEADS:
                    norm = lax.rsqrt(jnp.sum(act * act, axis=-1, keepdims=True) + EPS)
                    act = act * (norm * scale if c < DN_HEADS else norm)
                dst_ref[c, r0:r0 + CONV_ROWS, :] = act.astype(BF16)

    @pl.when(t == 0)
    def _():
        state[...] = jnp.zeros_like(state)
        xbuf[:, 0:CONV_TAIL, :] = jnp.zeros((QKV_BLOCKS, CONV_TAIL, DN_HEAD_DIM), F32)
        xbuf[:, CONV_TAIL:, :] = raw_ref[...]
        preprocess(qkv_ref)

    xbuf[:, 0:CONV_TAIL, :] = raw_ref[:, DN_TILE - CONV_TAIL:DN_TILE, :]
    xbuf[:, CONV_TAIL:, :] = raw_next_ref[...]
    preprocess(qkv_next_ref)

    ba = ba_ref[0]
    beta = _sigmoid(ba)
    xs = ba + dtb_ref[...]
    softplus = jnp.maximum(xs, 0.0) + jnp.log1p(jnp.exp(-jnp.abs(xs)))
    g = -jnp.exp(alog_ref[...]) * softplus

    row = lax.broadcasted_iota(jnp.int32, (DN_TILE, DN_TILE), 0)
    col = lax.broadcasted_iota(jnp.int32, (DN_TILE, DN_TILE), 1)
    same_chunk = (row >> DN_LOG_CHUNK) == (col >> DN_LOG_CHUNK)
    lower_incl = same_chunk & (row >= col)
    strict_lower = same_chunk & (row > col)
    g_hi = g.astype(BF16)
    g_lo = (g - g_hi.astype(F32)).astype(BF16)
    gc2 = jnp.dot(jnp.where(lower_incl, 1.0, 0.0).astype(BF16),
                  jnp.concatenate([g_hi, g_lo], axis=1), preferred_element_type=F32)
    gc = gc2[:, :LANES] + gc2[:, LANES:]
    g_last = jnp.concatenate(
        [jnp.broadcast_to(gc[(c + 1) * DN_CHUNK - 1:(c + 1) * DN_CHUNK, :], (DN_CHUNK, LANES))
         for c in range(DN_NCH)], axis=0)
    rev = g_last - gc
    gc_t = gc.T
    eye = jnp.where(row == col, 1.0, 0.0)

    heads = range(DN_HEADS)
    gcol = lambda arr, h: arr[:, DN_HEADS + h:DN_HEADS + h + 1]
    q, k, kb, rhs = [], [], [], []
    for h in heads:
        qh = qkv_ref[h].astype(F32)
        kh = qkv_ref[DN_HEADS + h].astype(F32)
        vh = qkv_ref[2 * DN_HEADS + h].astype(F32)
        beta_c = beta[:, h:h + 1]
        q.append(qh)
        k.append(kh)
        kb.append(kh * beta_c)
        rhs.append(jnp.concatenate([vh * beta_c, kb[h] * jnp.exp(gcol(gc, h))], axis=1))
    gram = [_mm(jnp.concatenate([kb[h], q[h]], axis=0), k[h], _NT) for h in heads]
    qk, n_pow, inv = [], [], []
    for h in heads:
        gc_r = gc_t[DN_HEADS + h:DN_HEADS + h + 1, :]
        decay = jnp.exp(jnp.where(lower_incl, gcol(gc, h) - gc_r, -jnp.inf))
        qk.append(gram[h][DN_TILE:] * decay)
        n_pow.append(-jnp.where(strict_lower, gram[h][:DN_TILE] * decay, 0.0))
        inv.append(eye + n_pow[h])
    for _ in range(DN_LOG_CHUNK - 1):
        n_pow = [_mm(n_pow[h], n_pow[h]) for h in heads]
        inv = [inv[h] + _mm(inv[h], n_pow[h]) for h in heads]
    sol = [_mm(inv[h], rhs[h]) for h in heads]
    u = [sol[h][:, :DN_HEAD_DIM] for h in heads]
    wmat = [sol[h][:, DN_HEAD_DIM:] for h in heads]
    qe = [q[h] * jnp.exp(gcol(gc, h)) for h in heads]
    kdec = [k[h] * jnp.exp(gcol(rev, h)) for h in heads]

    s = [state[h] for h in heads]
    v_new = [[] for _ in heads]
    o_inter = [[] for _ in heads]
    for c in range(DN_NCH):
        r0, r1 = c * DN_CHUNK, (c + 1) * DN_CHUNK
        ws = [_mm(jnp.concatenate([wmat[h][r0:r1], qe[h][r0:r1]], axis=0), s[h]) for h in heads]
        for h in heads:
            v_new[h].append(u[h][r0:r1] - ws[h][:DN_CHUNK])
            o_inter[h].append(ws[h][DN_CHUNK:])
        upd = [_mm(kdec[h][r0:r1], v_new[h][c], _TN) for h in heads]
        s = [s[h] * jnp.exp(gc[r1 - 1:r1, DN_HEADS + h:DN_HEADS + h + 1]) + upd[h]
             for h in heads]
    o_intra = [_mm(qk[h], jnp.concatenate(v_new[h], axis=0)) for h in heads]
    for h in heads:
        state[h] = s[h]
        o = jnp.concatenate(o_inter[h], axis=0) + o_intra[h]
        o = o * lax.rsqrt(jnp.mean(o * o, axis=-1, keepdims=True) + EPS) * gain_ref[...]
        zh = z_ref[0, :, h * DN_HEAD_DIM:(h + 1) * DN_HEAD_DIM]
        o_ref[0, :, h * DN_HEAD_DIM:(h + 1) * DN_HEAD_DIM] = (o * _silu(zh)).astype(o_ref.dtype)
    qkv_ref[...] = qkv_next_ref[...]


def _deltanet(raw, z, ba, convw, alog_lane, dtb_lane, out_gain):
    B, S, _ = z.shape
    tiles = S // DN_TILE
    raw_block = (QKV_BLOCKS, DN_TILE, DN_HEAD_DIM)
    return pl.pallas_call(
        _deltanet_kernel,
        grid=(B, tiles),
        in_specs=[
            pl.BlockSpec(raw_block, lambda b, t: (0, b * tiles + t, 0)),
            pl.BlockSpec(raw_block, lambda b, t: (0, b * tiles + jnp.minimum(t + 1, tiles - 1), 0)),
            pl.BlockSpec((1, DN_TILE, DN_WIDTH), lambda b, t: (b, t, 0)),
            pl.BlockSpec((1, DN_TILE, LANES), lambda b, t: (b, t, 0)),
            pl.BlockSpec((QKV_BLOCKS, DN_CONV, DN_HEAD_DIM), lambda b, t: (0, 0, 0)),
            pl.BlockSpec((1, LANES), lambda b, t: (0, 0)),
            pl.BlockSpec((1, LANES), lambda b, t: (0, 0)),
            pl.BlockSpec((1, DN_HEAD_DIM), lambda b, t: (0, 0)),
        ],
        out_specs=pl.BlockSpec((1, DN_TILE, DN_WIDTH), lambda b, t: (b, t, 0)),
        out_shape=jax.ShapeDtypeStruct((B, S, DN_WIDTH), BF16),
        scratch_shapes=[pltpu.VMEM((DN_HEADS, DN_HEAD_DIM, DN_HEAD_DIM), F32),
                        pltpu.VMEM((QKV_BLOCKS, CONV_TAIL + DN_TILE, DN_HEAD_DIM), F32),
                        pltpu.VMEM(raw_block, BF16), pltpu.VMEM(raw_block, BF16)],
        compiler_params=pltpu.CompilerParams(
            dimension_semantics=("arbitrary", "arbitrary"), vmem_limit_bytes=VMEM_LIMIT),
        name="deltanet",
    )(raw, raw, z, ba, convw, alog_lane, dtb_lane, out_gain)


SWA_BAND = 2 * SWA_BLOCK
SWA_SLOTS = SWA_WIDTH // LANES
SWA_HEADS_PER_SLOT = LANES // SWA_HEAD_DIM


SWA_STEP = 2
SWA_ROWS = SWA_STEP * SWA_BLOCK


def _swa_kernel(sinks_ref, sq_ref, kv_ref, bias_ref, qgain_ref, kgain_ref, seg_ref, o_ref,
                band_ref):
    n = pl.program_id(1)
    kv = kv_ref[0].astype(F32)
    k = kv[:, :SWA_KV_WIDTH]
    slots_per_piece = MXU_WIDTH // LANES
    q_slots = [sq_ref[j // slots_per_piece, :, (j % slots_per_piece) * LANES:
                      (j % slots_per_piece + 1) * LANES].astype(F32) for j in range(SWA_SLOTS)]
    parts = q_slots + [k]
    ms = _mm(jnp.concatenate([p * p for p in parts], axis=0), seg_ref[...])
    qn = [parts[j] * lax.rsqrt(ms[j * SWA_ROWS:(j + 1) * SWA_ROWS] + EPS)
          * qgain_ref[:, j * LANES:(j + 1) * LANES] for j in range(SWA_SLOTS)]
    kn = k * lax.rsqrt(ms[SWA_SLOTS * SWA_ROWS:] + EPS) * kgain_ref[...]

    @pl.when(n == 0)
    def _():
        band_ref[0:SWA_BLOCK, :] = jnp.zeros((SWA_BLOCK, 2 * SWA_KV_WIDTH), F32)

    @pl.when(n > 0)
    def _():
        band_ref[0:SWA_BLOCK, :] = band_ref[SWA_ROWS:SWA_ROWS + SWA_BLOCK, :]

    band_ref[SWA_BLOCK:, :SWA_KV_WIDTH] = kn
    band_ref[SWA_BLOCK:, SWA_KV_WIDTH:] = kv[:, SWA_KV_WIDTH:]
    k = band_ref[:, :SWA_KV_WIDTH]
    v = band_ref[:, SWA_KV_WIDTH:]

    lane = lax.broadcasted_iota(jnp.int32, (SWA_BLOCK + SWA_ROWS, LANES), 1)
    low = lane < SWA_HEAD_DIM
    k_rot = pltpu.roll(k, SWA_HEAD_DIM, axis=1)
    v_rot = pltpu.roll(v, SWA_HEAD_DIM, axis=1)
    k_lo = (jnp.where(low, k, 0.0).astype(BF16), jnp.where(low, k_rot, 0.0).astype(BF16))
    k_hi = (jnp.where(low, 0.0, k_rot).astype(BF16), jnp.where(low, 0.0, k).astype(BF16))
    v_lo = (jnp.where(low, v, 0.0).astype(BF16), jnp.where(low, v_rot, 0.0).astype(BF16))
    v_hi = (jnp.where(low, 0.0, v_rot).astype(BF16), jnp.where(low, 0.0, v).astype(BF16))

    out_low = lax.broadcasted_iota(jnp.int32, (SWA_BLOCK, LANES), 1) < SWA_HEAD_DIM

    slots_per_kv = SWA_SLOTS // SWA_KV_HEADS
    units = [(u, j) for u in range(SWA_STEP) for j in range(SWA_SLOTS)]
    band_of = lambda arr, u: arr[u * SWA_BLOCK:u * SWA_BLOCK + SWA_BAND]
    k_bd = {(u, kh): jnp.concatenate([band_of(k_lo[kh], u), band_of(k_hi[kh], u)], axis=0)
            for u in range(SWA_STEP) for kh in range(SWA_KV_HEADS)}
    v_bd = {(u, kh): jnp.concatenate([band_of(v_lo[kh], u), band_of(v_hi[kh], u)], axis=0)
            for u in range(SWA_STEP) for kh in range(SWA_KV_HEADS)}
    logits = {(u, j): _mm(qn[j][u * SWA_BLOCK:(u + 1) * SWA_BLOCK], k_bd[u, j // slots_per_kv], _NT)
              for u, j in units}
    first_tab = jnp.minimum(n, 1)
    probs = {}
    inv_den = {}
    for u, j in units:
        for r in range(SWA_HEADS_PER_SLOT):
            hd = j * SWA_HEADS_PER_SLOT + r
            bias = bias_ref[first_tab, hd] if u == 0 else bias_ref[1, hd]
            lg = logits[u, j][:, r * SWA_BAND:(r + 1) * SWA_BAND] + bias
            sink = sinks_ref[hd]
            m = jnp.maximum(jnp.max(lg, axis=-1, keepdims=True), sink)
            p = jnp.exp(lg - m)
            den = jnp.sum(p, axis=-1, keepdims=True) + jnp.exp(sink - m)
            probs[u, j, r] = p.astype(BF16)
            inv_den[u, j, r] = 1.0 / den
    outs = {(u, j): _mm(jnp.concatenate([probs[u, j, r] for r in range(SWA_HEADS_PER_SLOT)],
                                        axis=1), v_bd[u, j // slots_per_kv])
            for u, j in units}
    for u, j in units:
        o_ref[0, u * SWA_BLOCK:(u + 1) * SWA_BLOCK, j * LANES:(j + 1) * LANES] = (
            outs[u, j] * jnp.where(out_low, inv_den[u, j, 0], inv_den[u, j, 1])).astype(o_ref.dtype)


def _swa(rest, batch, sinks, bias_tabs, qgain, kgain, seg):
    B = batch
    S = rest.shape[1] // B
    steps = S // SWA_ROWS
    kvw = 2 * SWA_KV_WIDTH
    sq_pieces = SWA_WIDTH // MXU_WIDTH
    assert kvw == MXU_WIDTH
    grid_spec = pltpu.PrefetchScalarGridSpec(
        num_scalar_prefetch=1,
        grid=(B, steps),
        in_specs=[
            pl.BlockSpec((sq_pieces, SWA_ROWS, MXU_WIDTH),
                         lambda b, n, s: (PIECE_SQ // sq_pieces, b * steps + n, 0)),
            pl.BlockSpec((1, SWA_ROWS, kvw), lambda b, n, s: (PIECE_SKV, b * steps + n, 0)),
            pl.BlockSpec((2, SWA_HEADS, SWA_BLOCK, SWA_BAND), lambda b, n, s: (0, 0, 0, 0)),
            pl.BlockSpec((1, SWA_WIDTH), lambda b, n, s: (0, 0)),
            pl.BlockSpec((1, SWA_KV_WIDTH), lambda b, n, s: (0, 0)),
            pl.BlockSpec((LANES, LANES), lambda b, n, s: (0, 0)),
        ],
        out_specs=pl.BlockSpec((1, SWA_ROWS, SWA_WIDTH), lambda b, n, s: (b, n, 0)),
        scratch_shapes=[pltpu.VMEM((SWA_BLOCK + SWA_ROWS, kvw), F32)],
    )
    return pl.pallas_call(
        _swa_kernel,
        grid_spec=grid_spec,
        out_shape=jax.ShapeDtypeStruct((B, S, SWA_WIDTH), BF16),
        compiler_params=pltpu.CompilerParams(
            dimension_semantics=("arbitrary", "arbitrary"), vmem_limit_bytes=VMEM_LIMIT),
        name="swa",
    )(sinks, rest, rest, bias_tabs, qgain, kgain, seg)


def _merge_ffn_kernel(x_ref, ydn_ref, yswa_ref, g_ref, wa_ref, wb_ref, wo_ref, gain_ref,
                      wg_ref, wu_ref, wd_ref, o_ref, act_ref):
    per_branch = D_MODEL // MXU_WIDTH
    ga, gb = [_sigmoid(jnp.concatenate(
        [g_ref[br * per_branch + p].astype(F32) for p in range(per_branch)], axis=1))
        for br in range(2)]
    merged = ga * _mm(ydn_ref[...], wa_ref[...]) + gb * _mm(yswa_ref[...], wb_ref[...])
    x1 = x_ref[...] + _mm(merged, wo_ref[...])
    ms = jnp.mean(x1 * x1, axis=-1, keepdims=True)
    h2 = (x1 * lax.rsqrt(ms + EPS) * gain_ref[...]).astype(BF16)
    for c0 in range(0, D_FF, MXU_WIDTH):
        gate = jnp.dot(h2, wg_ref[:, c0:c0 + MXU_WIDTH], preferred_element_type=F32)
        up = jnp.dot(h2, wu_ref[:, c0:c0 + MXU_WIDTH], preferred_element_type=F32)
        act_ref[:, c0:c0 + MXU_WIDTH] = (_silu(gate) * up).astype(BF16)
    o_ref[...] = x1 + jnp.dot(act_ref[...], wd_ref[...], preferred_element_type=F32)


def _merge_ffn(x2, ydn, yswa, graw, wa, wb, wo, gain, wg, wu, wd, tm):
    T = x2.shape[0]
    row = lambda w: pl.BlockSpec((tm, w), lambda i: (i, 0))
    once = lambda a, b: pl.BlockSpec((a, b), lambda i: (0, 0), pipeline_mode=pl.Buffered(1))
    return pl.pallas_call(
        _merge_ffn_kernel,
        grid=(T // tm,),
        in_specs=[row(D_MODEL), row(DN_WIDTH), row(SWA_WIDTH),
                  pl.BlockSpec((PIECE_SQ - PIECE_GATES, tm, MXU_WIDTH), lambda i: (0, i, 0)),
                  once(DN_WIDTH, D_MODEL), once(SWA_WIDTH, D_MODEL), once(D_MODEL, D_MODEL),
                  once(1, D_MODEL),
                  once(D_MODEL, D_FF), once(D_MODEL, D_FF), once(D_FF, D_MODEL)],
        out_specs=row(D_MODEL),
        out_shape=jax.ShapeDtypeStruct((T, D_MODEL), F32),
        scratch_shapes=[pltpu.VMEM((tm, D_FF), BF16)],
        compiler_params=pltpu.CompilerParams(
            dimension_semantics=("arbitrary",), vmem_limit_bytes=VMEM_LIMIT),
        name="merge_ffn",
    )(x2, ydn, yswa, graw, wa, wb, wo, gain, wg, wu, wd)


def _t5_bucket_table():
    qi = jnp.arange(SWA_BLOCK)[:, None]
    kj = jnp.arange(SWA_BAND)[None, :]
    dist = SWA_BLOCK + qi - kj
    in_window = (dist >= 0) & (dist < WINDOW)
    n = jnp.maximum(dist, 0)
    max_exact = REL_BUCKETS // 2
    nf = jnp.maximum(n, 1).astype(F32)
    large = max_exact + (jnp.log(nf / max_exact) / math.log(REL_MAX_DIST / max_exact)
                         * (REL_BUCKETS - max_exact)).astype(jnp.int32)
    large = jnp.minimum(large, REL_BUCKETS - 1)
    return jnp.where(n < max_exact, n, large), in_window


def _segment_mean_matrix(width, seg):
    idx = np.arange(width) // seg
    return jnp.asarray((idx[:, None] == idx[None, :]).astype(np.float32) / seg, dtype=BF16)


def kernel(x, attn_norm, w_in, dn_conv, dn_a_log, dn_dt_bias, dn_out_norm, swa_q_norm, swa_k_norm, swa_sinks, rel_bias, w_branch_dn, w_branch_swa, w_out, ffn_norm, w_gate, w_up, w_down):
    B, S, D = x.shape
    T = B * S
    depth = w_in.shape[0]
    bucket, in_window = _t5_bucket_table()
    seg = _segment_mean_matrix(LANES, SWA_HEAD_DIM)
    rel = rel_bias.astype(F32)
    bias_tab = sum(jnp.where(bucket[None] == b, rel[b][:, None, None], 0.0)
                   for b in range(REL_BUCKETS))
    bias_tab = jnp.where(in_window[None], bias_tab, -jnp.inf)
    has_prev = jnp.arange(SWA_BAND)[None, None, :] >= SWA_BLOCK
    bias_tabs = jnp.stack([jnp.where(has_prev, bias_tab, -jnp.inf), bias_tab])

    x2 = x.reshape(T, D)
    for l in range(depth):
        wl = w_in[l].astype(BF16)
        o_ba = W_HEAD
        o_sq = o_ba + 2 * DN_HEADS
        o_g = o_sq + SWA_WIDTH + 2 * SWA_KV_WIDTH
        w_rest = jnp.concatenate([wl[:, o_g:], wl[:, o_sq:o_g]], axis=1)
        w_rest = w_rest.reshape(D, REST_PIECES, MXU_WIDTH).transpose(1, 0, 2)
        w_ba = jnp.pad(wl[:, o_ba:o_sq], ((0, 0), (0, LANES - 2 * DN_HEADS)))
        convw = dn_conv[l].astype(F32).reshape(DN_CONV, QKV_BLOCKS, DN_HEAD_DIM).transpose(1, 0, 2)

        qkv, z, rest, ba = _in_proj(x2, attn_norm[l][None, :], wl, w_rest, w_ba, tm=512)

        pad_lo = jnp.zeros((DN_HEADS,), F32)
        pad_hi = jnp.zeros((LANES - 2 * DN_HEADS,), F32)
        alog_lane = jnp.concatenate([pad_lo, dn_a_log[l].astype(F32), pad_hi])[None, :]
        dtb_lane = jnp.concatenate([pad_lo, dn_dt_bias[l].astype(F32), pad_hi])[None, :]
        y_dn = _deltanet(qkv, z.reshape(B, S, -1), ba.reshape(B, S, -1), convw,
                         alog_lane, dtb_lane, dn_out_norm[l][None, :])

        qgain = jnp.tile(swa_q_norm[l].astype(F32), SWA_HEADS)[None, :] * (SWA_HEAD_DIM ** -0.5)
        kgain = jnp.tile(swa_k_norm[l].astype(F32), SWA_KV_HEADS)[None, :]
        y_swa = _swa(rest, B, swa_sinks[l].astype(F32), bias_tabs, qgain, kgain, seg)

        x2 = _merge_ffn(x2, y_dn.reshape(T, -1), y_swa.reshape(T, -1), rest,
                        w_branch_dn[l].astype(BF16), w_branch_swa[l].astype(BF16),
                        w_out[l].astype(BF16), ffn_norm[l][None, :],
                        w_gate[l].astype(BF16), w_up[l].astype(BF16), w_down[l].astype(BF16),
                        tm=512)
    return x2.reshape(B, S, D)
```

```python
import functools
import math

import numpy as np
import jax
import jax.numpy as jnp
from jax import lax
from jax.experimental import pallas as pl
from jax.experimental.pallas import tpu as pltpu

D_MODEL = 1024
DN_HEADS = 4
DN_HEAD_DIM = 128
DN_WIDTH = DN_HEADS * DN_HEAD_DIM
DN_QKV_WIDTH = 3 * DN_WIDTH
DN_CONV = 4
DN_CHUNK = 64
SWA_HEADS = 8
SWA_KV_HEADS = 2
SWA_HEAD_DIM = 64
SWA_WIDTH = SWA_HEADS * SWA_HEAD_DIM
SWA_KV_WIDTH = SWA_KV_HEADS * SWA_HEAD_DIM
WINDOW = 128
SWA_BLOCK = 128
REL_BUCKETS = 32
REL_MAX_DIST = 128
D_FF = 2816
EPS = 1e-6

LANES = 128
SUBLANES = 8
MXU_WIDTH = 256
VMEM_LIMIT = 56 * 1024 * 1024

W_HEAD = DN_QKV_WIDTH + DN_WIDTH
W_REST = 2 * D_MODEL + SWA_WIDTH + 2 * SWA_KV_WIDTH
REST_PIECES = W_REST // MXU_WIDTH
PIECE_GATES = 0
PIECE_SQ = 2 * D_MODEL // MXU_WIDTH
PIECE_SKV = PIECE_SQ + SWA_WIDTH // MXU_WIDTH
QKV_BLOCKS = DN_QKV_WIDTH // DN_HEAD_DIM

F32 = jnp.float32
BF16 = jnp.bfloat16


def _mm(a, b, dims=(((1,), (0,)), ((), ()))):
    return lax.dot_general(a.astype(BF16), b.astype(BF16), dims, preferred_element_type=F32)


_NT = (((1,), (1,)), ((), ()))
_TN = (((0,), (0,)), ((), ()))


def _sigmoid(x):
    return 1.0 / (1.0 + jnp.exp(-x))


def _silu(x):
    return x * _sigmoid(x)


CONV_TAIL = SUBLANES
CONV_ROWS = 128


def _in_proj_kernel(x_ref, gain_ref, w_head_ref, w_rest_ref, w_ba_ref,
                    qkv_ref, z_ref, rest_ref, ba_ref):
    x = x_ref[...]
    ms = jnp.mean(x * x, axis=-1, keepdims=True)
    h = (x * lax.rsqrt(ms + EPS) * gain_ref[...]).astype(BF16)
    raw = jnp.dot(h, w_head_ref[:, :DN_QKV_WIDTH], preferred_element_type=F32)
    for c in range(QKV_BLOCKS):
        qkv_ref[c] = raw[:, c * DN_HEAD_DIM:(c + 1) * DN_HEAD_DIM]
    z_ref[...] = jnp.dot(h, w_head_ref[:, DN_QKV_WIDTH:], preferred_element_type=F32)
    for p in range(REST_PIECES):
        rest_ref[p] = jnp.dot(h, w_rest_ref[p], preferred_element_type=F32).astype(rest_ref.dtype)
    ba_ref[...] = jnp.dot(h, w_ba_ref[...], preferred_element_type=F32)


def _in_proj(x2, gain, w_in_bf16, w_rest, w_ba, tm):
    T = x2.shape[0]
    once = lambda shape: pl.BlockSpec(shape, lambda i: (0,) * len(shape),
                                      pipeline_mode=pl.Buffered(1))
    return pl.pallas_call(
        _in_proj_kernel,
        grid=(T // tm,),
        in_specs=[
            pl.BlockSpec((tm, D_MODEL), lambda i: (i, 0)),
            pl.BlockSpec((1, D_MODEL), lambda i: (0, 0)),
            once((D_MODEL, W_HEAD)),
            once((REST_PIECES, D_MODEL, MXU_WIDTH)),
            once((D_MODEL, LANES)),
        ],
        out_specs=(
            pl.BlockSpec((QKV_BLOCKS, tm, DN_HEAD_DIM), lambda i: (0, i, 0)),
            pl.BlockSpec((tm, DN_WIDTH), lambda i: (i, 0)),
            pl.BlockSpec((REST_PIECES, tm, MXU_WIDTH), lambda i: (0, i, 0)),
            pl.BlockSpec((tm, LANES), lambda i: (i, 0)),
        ),
        out_shape=(
            jax.ShapeDtypeStruct((QKV_BLOCKS, T, DN_HEAD_DIM), F32),
            jax.ShapeDtypeStruct((T, DN_WIDTH), F32),
            jax.ShapeDtypeStruct((REST_PIECES, T, MXU_WIDTH), BF16),
            jax.ShapeDtypeStruct((T, LANES), F32),
        ),
        compiler_params=pltpu.CompilerParams(
            dimension_semantics=("arbitrary",), vmem_limit_bytes=VMEM_LIMIT),
        name="in_proj",
    )(x2, gain, w_in_bf16, w_rest, w_ba)


DN_TILE = 256
DN_NCH = DN_TILE // DN_CHUNK
DN_LOG_CHUNK = int(math.log2(DN_CHUNK))
DN_STREAMS = 2


def _deltanet_kernel(raw_ref, raw_next_ref, z_ref, ba_ref, convw_ref, alog_ref, dtb_ref, gain_ref,
                     o_ref, state, xbuf, qkv_ref, qkv_next_ref):
    t = pl.program_id(1)
    scale = DN_HEAD_DIM ** -0.5
    streams = range(DN_STREAMS)

    def preprocess(s, dst_ref):
        for c in range(QKV_BLOCKS):
            w = convw_ref[c]
            for r0 in range(0, DN_TILE, CONV_ROWS):
                acc = w[DN_CONV - 1:DN_CONV, :] * xbuf[s, c, CONV_TAIL + r0:CONV_TAIL + r0 + CONV_ROWS, :]
                for j in range(DN_CONV - 1):
                    off = CONV_TAIL - (DN_CONV - 1) + j + r0
                    acc = acc + w[j:j + 1, :] * xbuf[s, c, off:off + CONV_ROWS, :]
                act = _silu(acc)
                if c < 2 * DN_HEADS:
                    norm = lax.rsqrt(jnp.sum(act * act, axis=-1, keepdims=True) + EPS)
                    act = act * (norm * scale if c < DN_HEADS else norm)
                dst_ref[s, c, r0:r0 + CONV_ROWS, :] = act.astype(BF16)

    @pl.when(t == 0)
    def _():
        state[...] = jnp.zeros_like(state)
        for s in streams:
            xbuf[s, :, 0:CONV_TAIL, :] = jnp.zeros((QKV_BLOCKS, CONV_TAIL, DN_HEAD_DIM), F32)
            xbuf[s, :, CONV_TAIL:, :] = raw_ref[:, s]
            preprocess(s, qkv_ref)

    for s in streams:
        xbuf[s, :, 0:CONV_TAIL, :] = raw_ref[:, s, DN_TILE - CONV_TAIL:DN_TILE, :]
        xbuf[s, :, CONV_TAIL:, :] = raw_next_ref[:, s]
        preprocess(s, qkv_next_ref)

    row = lax.broadcasted_iota(jnp.int32, (DN_TILE, DN_TILE), 0)
    col = lax.broadcasted_iota(jnp.int32, (DN_TILE, DN_TILE), 1)
    same_chunk = (row >> DN_LOG_CHUNK) == (col >> DN_LOG_CHUNK)
    lower_incl = same_chunk & (row >= col)
    strict_lower = same_chunk & (row > col)
    cumsum_mat = jnp.where(lower_incl, 1.0, 0.0).astype(BF16)
    eye = jnp.where(row == col, 1.0, 0.0)

    beta, gc, rev, gc_t = [], [], [], []
    for s in streams:
        ba = ba_ref[s]
        beta.append(_sigmoid(ba))
        xs = ba + dtb_ref[...]
        softplus = jnp.maximum(xs, 0.0) + jnp.log1p(jnp.exp(-jnp.abs(xs)))
        g = -jnp.exp(alog_ref[...]) * softplus
        g_hi = g.astype(BF16)
        g_lo = (g - g_hi.astype(F32)).astype(BF16)
        gc2 = jnp.dot(cumsum_mat, jnp.concatenate([g_hi, g_lo], axis=1),
                      preferred_element_type=F32)
        gc.append(gc2[:, :LANES] + gc2[:, LANES:])
        g_last = jnp.concatenate(
            [jnp.broadcast_to(gc[s][(c + 1) * DN_CHUNK - 1:(c + 1) * DN_CHUNK, :], (DN_CHUNK, LANES))
             for c in range(DN_NCH)], axis=0)
        rev.append(g_last - gc[s])
        gc_t.append(gc[s].T)

    units = [(s, h) for s in streams for h in range(DN_HEADS)]
    gcol = lambda arr, h: arr[:, DN_HEADS + h:DN_HEADS + h + 1]
    q, k, kb, rhs = {}, {}, {}, {}
    for s, h in units:
        q[s, h] = qkv_ref[s, h].astype(F32)
        k[s, h] = qkv_ref[s, DN_HEADS + h].astype(F32)
        vh = qkv_ref[s, 2 * DN_HEADS + h].astype(F32)
        beta_c = beta[s][:, h:h + 1]
        kb[s, h] = k[s, h] * beta_c
        rhs[s, h] = jnp.concatenate([vh * beta_c, kb[s, h] * jnp.exp(gcol(gc[s], h))], axis=1)
    gram = {p: _mm(jnp.concatenate([kb[p], q[p]], axis=0), k[p], _NT) for p in units}
    qk, n_pow, inv = {}, {}, {}
    for s, h in units:
        gc_r = gc_t[s][DN_HEADS + h:DN_HEADS + h + 1, :]
        decay = jnp.exp(jnp.where(lower_incl, gcol(gc[s], h) - gc_r, -jnp.inf))
        qk[s, h] = gram[s, h][DN_TILE:] * decay
        n_pow[s, h] = -jnp.where(strict_lower, gram[s, h][:DN_TILE] * decay, 0.0)
        inv[s, h] = eye + n_pow[s, h]
    for _ in range(DN_LOG_CHUNK - 1):
        n_pow = {p: _mm(n_pow[p], n_pow[p]) for p in units}
        inv = {p: inv[p] + _mm(inv[p], n_pow[p]) for p in units}
    sol = {p: _mm(inv[p], rhs[p]) for p in units}
    u = {p: sol[p][:, :DN_HEAD_DIM] for p in units}
    wmat = {p: sol[p][:, DN_HEAD_DIM:] for p in units}
    qe = {(s, h): q[s, h] * jnp.exp(gcol(gc[s], h)) for s, h in units}
    kdec = {(s, h): k[s, h] * jnp.exp(gcol(rev[s], h)) for s, h in units}

    st = {(s, h): state[s, h] for s, h in units}
    v_new = {p: [] for p in units}
    o_inter = {p: [] for p in units}
    for c in range(DN_NCH):
        r0, r1 = c * DN_CHUNK, (c + 1) * DN_CHUNK
        ws = {p: _mm(jnp.concatenate([wmat[p][r0:r1], qe[p][r0:r1]], axis=0), st[p]) for p in units}
        for p in units:
            v_new[p].append(u[p][r0:r1] - ws[p][:DN_CHUNK])
            o_inter[p].append(ws[p][DN_CHUNK:])
        upd = {p: _mm(kdec[p][r0:r1], v_new[p][c], _TN) for p in units}
        st = {(s, h): st[s, h] * jnp.exp(gc[s][r1 - 1:r1, DN_HEADS + h:DN_HEADS + h + 1]) + upd[s, h]
              for s, h in units}
    o_intra = {p: _mm(qk[p], jnp.concatenate(v_new[p], axis=0)) for p in units}
    for s, h in units:
        state[s, h] = st[s, h]
        o = jnp.concatenate(o_inter[s, h], axis=0) + o_intra[s, h]
        o = o * lax.rsqrt(jnp.mean(o * o, axis=-1, keepdims=True) + EPS) * gain_ref[...]
        zh = z_ref[s, :, h * DN_HEAD_DIM:(h + 1) * DN_HEAD_DIM]
        o_ref[s, :, h * DN_HEAD_DIM:(h + 1) * DN_HEAD_DIM] = (o * _silu(zh)).astype(o_ref.dtype)
    qkv_ref[...] = qkv_next_ref[...]


def _deltanet(raw, z, ba, convw, alog_lane, dtb_lane, out_gain):
    B, S, _ = z.shape
    tiles = S // DN_TILE
    raw = raw.reshape(QKV_BLOCKS, B, S, DN_HEAD_DIM)
    raw_block = (QKV_BLOCKS, DN_STREAMS, DN_TILE, DN_HEAD_DIM)
    qkv_block = (DN_STREAMS, QKV_BLOCKS, DN_TILE, DN_HEAD_DIM)
    return pl.pallas_call(
        _deltanet_kernel,
        grid=(B // DN_STREAMS, tiles),
        in_specs=[
            pl.BlockSpec(raw_block, lambda b, t: (0, b, t, 0)),
            pl.BlockSpec(raw_block, lambda b, t: (0, b, jnp.minimum(t + 1, tiles - 1), 0)),
            pl.BlockSpec((DN_STREAMS, DN_TILE, DN_WIDTH), lambda b, t: (b, t, 0)),
            pl.BlockSpec((DN_STREAMS, DN_TILE, LANES), lambda b, t: (b, t, 0)),
            pl.BlockSpec((QKV_BLOCKS, DN_CONV, DN_HEAD_DIM), lambda b, t: (0, 0, 0)),
            pl.BlockSpec((1, LANES), lambda b, t: (0, 0)),
            pl.BlockSpec((1, LANES), lambda b, t: (0, 0)),
            pl.BlockSpec((1, DN_HEAD_DIM), lambda b, t: (0, 0)),
        ],
        out_specs=pl.BlockSpec((DN_STREAMS, DN_TILE, DN_WIDTH), lambda b, t: (b, t, 0)),
        out_shape=jax.ShapeDtypeStruct((B, S, DN_WIDTH), BF16),
        scratch_shapes=[
            pltpu.VMEM((DN_STREAMS, DN_HEADS, DN_HEAD_DIM, DN_HEAD_DIM), F32),
            pltpu.VMEM((DN_STREAMS, QKV_BLOCKS, CONV_TAIL + DN_TILE, DN_HEAD_DIM), F32),
            pltpu.VMEM(qkv_block, BF16), pltpu.VMEM(qkv_block, BF16)],
        compiler_params=pltpu.CompilerParams(
            dimension_semantics=("arbitrary", "arbitrary"), vmem_limit_bytes=VMEM_LIMIT),
        name="deltanet",
    )(raw, raw, z, ba, convw, alog_lane, dtb_lane, out_gain)


SWA_BAND = 2 * SWA_BLOCK
SWA_SLOTS = SWA_WIDTH // LANES
SWA_HEADS_PER_SLOT = LANES // SWA_HEAD_DIM


SWA_STEP = 4
SWA_ROWS = SWA_STEP * SWA_BLOCK


def _swa_kernel(sinks_ref, sq_ref, kv_ref, bias_ref, qgain_ref, kgain_ref, seg_ref, o_ref,
                band_ref):
    n = pl.program_id(1)
    kv = kv_ref[0].astype(F32)
    k = kv[:, :SWA_KV_WIDTH]
    slots_per_piece = MXU_WIDTH // LANES
    q_slots = [sq_ref[j // slots_per_piece, :, (j % slots_per_piece) * LANES:
                      (j % slots_per_piece + 1) * LANES].astype(F32) for j in range(SWA_SLOTS)]
    parts = q_slots + [k]
    ms = _mm(jnp.concatenate([p * p for p in parts], axis=0), seg_ref[...])
    qn = [parts[j] * lax.rsqrt(ms[j * SWA_ROWS:(j + 1) * SWA_ROWS] + EPS)
          * qgain_ref[:, j * LANES:(j + 1) * LANES] for j in range(SWA_SLOTS)]
    kn = k * lax.rsqrt(ms[SWA_SLOTS * SWA_ROWS:] + EPS) * kgain_ref[...]

    @pl.when(n == 0)
    def _():
        band_ref[0:SWA_BLOCK, :] = jnp.zeros((SWA_BLOCK, 2 * SWA_KV_WIDTH), F32)

    @pl.when(n > 0)
    def _():
        band_ref[0:SWA_BLOCK, :] = band_ref[SWA_ROWS:SWA_ROWS + SWA_BLOCK, :]

    band_ref[SWA_BLOCK:, :SWA_KV_WIDTH] = kn
    band_ref[SWA_BLOCK:, SWA_KV_WIDTH:] = kv[:, SWA_KV_WIDTH:]
    k = band_ref[:, :SWA_KV_WIDTH]
    v = band_ref[:, SWA_KV_WIDTH:]

    lane = lax.broadcasted_iota(jnp.int32, (SWA_BLOCK + SWA_ROWS, LANES), 1)
    low = lane < SWA_HEAD_DIM
    k_rot = pltpu.roll(k, SWA_HEAD_DIM, axis=1)
    v_rot = pltpu.roll(v, SWA_HEAD_DIM, axis=1)
    k_lo = (jnp.where(low, k, 0.0).astype(BF16), jnp.where(low, k_rot, 0.0).astype(BF16))
    k_hi = (jnp.where(low, 0.0, k_rot).astype(BF16), jnp.where(low, 0.0, k).astype(BF16))
    v_lo = (jnp.where(low, v, 0.0).astype(BF16), jnp.where(low, v_rot, 0.0).astype(BF16))
    v_hi = (jnp.where(low, 0.0, v_rot).astype(BF16), jnp.where(low, 0.0, v).astype(BF16))

    out_low = lax.broadcasted_iota(jnp.int32, (SWA_BLOCK, LANES), 1) < SWA_HEAD_DIM

    slots_per_kv = SWA_SLOTS // SWA_KV_HEADS
    units = [(u, j) for u in range(SWA_STEP) for j in range(SWA_SLOTS)]
    band_of = lambda arr, u: arr[u * SWA_BLOCK:u * SWA_BLOCK + SWA_BAND]
    k_bd = {(u, kh): jnp.concatenate([band_of(k_lo[kh], u), band_of(k_hi[kh], u)], axis=0)
            for u in range(SWA_STEP) for kh in range(SWA_KV_HEADS)}
    v_bd = {(u, kh): jnp.concatenate([band_of(v_lo[kh], u), band_of(v_hi[kh], u)], axis=0)
            for u in range(SWA_STEP) for kh in range(SWA_KV_HEADS)}
    logits = {(u, j): _mm(qn[j][u * SWA_BLOCK:(u + 1) * SWA_BLOCK], k_bd[u, j // slots_per_kv], _NT)
              for u, j in units}
    first_tab = jnp.minimum(n, 1)
    probs = {}
    inv_den = {}
    for u, j in units:
        for r in range(SWA_HEADS_PER_SLOT):
            hd = j * SWA_HEADS_PER_SLOT + r
            bias = bias_ref[first_tab, hd] if u == 0 else bias_ref[1, hd]
            lg = logits[u, j][:, r * SWA_BAND:(r + 1) * SWA_BAND] + bias
            sink = sinks_ref[hd]
            m = jnp.maximum(jnp.max(lg, axis=-1, keepdims=True), sink)
            p = jnp.exp(lg - m)
            den = jnp.sum(p, axis=-1, keepdims=True) + jnp.exp(sink - m)
            probs[u, j, r] = p.astype(BF16)
            inv_den[u, j, r] = 1.0 / den
    outs = {(u, j): _mm(jnp.concatenate([probs[u, j, r] for r in range(SWA_HEADS_PER_SLOT)],
                                        axis=1), v_bd[u, j // slots_per_kv])
            for u, j in units}
    for u, j in units:
        o_ref[0, u * SWA_BLOCK:(u + 1) * SWA_BLOCK, j * LANES:(j + 1) * LANES] = (
            outs[u, j] * jnp.where(out_low, inv_den[u, j, 0], inv_den[u, j, 1])).astype(o_ref.dtype)


def _swa(rest, batch, sinks, bias_tabs, qgain, kgain, seg):
    B = batch
    S = rest.shape[1] // B
    steps = S // SWA_ROWS
    kvw = 2 * SWA_KV_WIDTH
    sq_pieces = SWA_WIDTH // MXU_WIDTH
    assert kvw == MXU_WIDTH
    grid_spec = pltpu.PrefetchScalarGridSpec(
        num_scalar_prefetch=1,
        grid=(B, steps),
        in_specs=[
            pl.BlockSpec((sq_pieces, SWA_ROWS, MXU_WIDTH),
                         lambda b, n, s: (PIECE_SQ // sq_pieces, b * steps + n, 0)),
            pl.BlockSpec((1, SWA_ROWS, kvw), lambda b, n, s: (PIECE_SKV, b * steps + n, 0)),
            pl.BlockSpec((2, SWA_HEADS, SWA_BLOCK, SWA_BAND), lambda b, n, s: (0, 0, 0, 0)),
            pl.BlockSpec((1, SWA_WIDTH), lambda b, n, s: (0, 0)),
            pl.BlockSpec((1, SWA_KV_WIDTH), lambda b, n, s: (0, 0)),
            pl.BlockSpec((LANES, LANES), lambda b, n, s: (0, 0)),
        ],
        out_specs=pl.BlockSpec((1, SWA_ROWS, SWA_WIDTH), lambda b, n, s: (b, n, 0)),
        scratch_shapes=[pltpu.VMEM((SWA_BLOCK + SWA_ROWS, kvw), F32)],
    )
    return pl.pallas_call(
        _swa_kernel,
        grid_spec=grid_spec,
        out_shape=jax.ShapeDtypeStruct((B, S, SWA_WIDTH), BF16),
        compiler_params=pltpu.CompilerParams(
            dimension_semantics=("arbitrary", "arbitrary"), vmem_limit_bytes=VMEM_LIMIT),
        name="swa",
    )(sinks, rest, rest, bias_tabs, qgain, kgain, seg)


def _merge_ffn_kernel(x_ref, ydn_ref, yswa_ref, g_ref, wa_ref, wb_ref, wo_ref, gain_ref,
                      wg_ref, wu_ref, wd_ref, o_ref, act_ref):
    per_branch = D_MODEL // MXU_WIDTH
    ga, gb = [_sigmoid(jnp.concatenate(
        [g_ref[br * per_branch + p].astype(F32) for p in range(per_branch)], axis=1))
        for br in range(2)]
    merged = ga * _mm(ydn_ref[...], wa_ref[...]) + gb * _mm(yswa_ref[...], wb_ref[...])
    x1 = x_ref[...] + _mm(merged, wo_ref[...])
    ms = jnp.mean(x1 * x1, axis=-1, keepdims=True)
    h2 = (x1 * lax.rsqrt(ms + EPS) * gain_ref[...]).astype(BF16)
    for c0 in range(0, D_FF, MXU_WIDTH):
        gate = jnp.dot(h2, wg_ref[:, c0:c0 + MXU_WIDTH], preferred_element_type=F32)
        up = jnp.dot(h2, wu_ref[:, c0:c0 + MXU_WIDTH], preferred_element_type=F32)
        act_ref[:, c0:c0 + MXU_WIDTH] = (_silu(gate) * up).astype(BF16)
    o_ref[...] = x1 + jnp.dot(act_ref[...], wd_ref[...], preferred_element_type=F32)


def _merge_ffn(x2, ydn, yswa, graw, wa, wb, wo, gain, wg, wu, wd, tm):
    T = x2.shape[0]
    row = lambda w: pl.BlockSpec((tm, w), lambda i: (i, 0))
    once = lambda a, b: pl.BlockSpec((a, b), lambda i: (0, 0), pipeline_mode=pl.Buffered(1))
    return pl.pallas_call(
        _merge_ffn_kernel,
        grid=(T // tm,),
        in_specs=[row(D_MODEL), row(DN_WIDTH), row(SWA_WIDTH),
                  pl.BlockSpec((PIECE_SQ - PIECE_GATES, tm, MXU_WIDTH), lambda i: (0, i, 0)),
                  once(DN_WIDTH, D_MODEL), once(SWA_WIDTH, D_MODEL), once(D_MODEL, D_MODEL),
                  once(1, D_MODEL),
                  once(D_MODEL, D_FF), once(D_MODEL, D_FF), once(D_FF, D_MODEL)],
        out_specs=row(D_MODEL),
        out_shape=jax.ShapeDtypeStruct((T, D_MODEL), F32),
        scratch_shapes=[pltpu.VMEM((tm, D_FF), BF16)],
        compiler_params=pltpu.CompilerParams(
            dimension_semantics=("arbitrary",), vmem_limit_bytes=VMEM_LIMIT),
        name="merge_ffn",
    )(x2, ydn, yswa, graw, wa, wb, wo, gain, wg, wu, wd)


def _t5_bucket_table():
    qi = jnp.arange(SWA_BLOCK)[:, None]
    kj = jnp.arange(SWA_BAND)[None, :]
    dist = SWA_BLOCK + qi - kj
    in_window = (dist >= 0) & (dist < WINDOW)
    n = jnp.maximum(dist, 0)
    max_exact = REL_BUCKETS // 2
    nf = jnp.maximum(n, 1).astype(F32)
    large = max_exact + (jnp.log(nf / max_exact) / math.log(REL_MAX_DIST / max_exact)
                         * (REL_BUCKETS - max_exact)).astype(jnp.int32)
    large = jnp.minimum(large, REL_BUCKETS - 1)
    return jnp.where(n < max_exact, n, large), in_window


def _segment_mean_matrix(width, seg):
    idx = np.arange(width) // seg
    return jnp.asarray((idx[:, None] == idx[None, :]).astype(np.float32) / seg, dtype=BF16)


def kernel(x, attn_norm, w_in, dn_conv, dn_a_log, dn_dt_bias, dn_out_norm, swa_q_norm, swa_k_norm, swa_sinks, rel_bias, w_branch_dn, w_branch_swa, w_out, ffn_norm, w_gate, w_up, w_down):
    B, S, D = x.shape
    T = B * S
    depth = w_in.shape[0]
    bucket, in_window = _t5_bucket_table()
    seg = _segment_mean_matrix(LANES, SWA_HEAD_DIM)
    rel = rel_bias.astype(F32)
    bias_tab = sum(jnp.where(bucket[None] == b, rel[b][:, None, None], 0.0)
                   for b in range(REL_BUCKETS))
    bias_tab = jnp.where(in_window[None], bias_tab, -jnp.inf)
    has_prev = jnp.arange(SWA_BAND)[None, None, :] >= SWA_BLOCK
    bias_tabs = jnp.stack([jnp.where(has_prev, bias_tab, -jnp.inf), bias_tab])

    x2 = x.reshape(T, D)
    for l in range(depth):
        wl = w_in[l].astype(BF16)
        o_ba = W_HEAD
        o_sq = o_ba + 2 * DN_HEADS
        o_g = o_sq + SWA_WIDTH + 2 * SWA_KV_WIDTH
        w_rest = jnp.concatenate([wl[:, o_g:], wl[:, o_sq:o_g]], axis=1)
        w_rest = w_rest.reshape(D, REST_PIECES, MXU_WIDTH).transpose(1, 0, 2)
        w_ba = jnp.pad(wl[:, o_ba:o_sq], ((0, 0), (0, LANES - 2 * DN_HEADS)))
        convw = dn_conv[l].astype(F32).reshape(DN_CONV, QKV_BLOCKS, DN_HEAD_DIM).transpose(1, 0, 2)

        qkv, z, rest, ba = _in_proj(x2, attn_norm[l][None, :], wl, w_rest, w_ba, tm=512)

        pad_lo = jnp.zeros((DN_HEADS,), F32)
        pad_hi = jnp.zeros((LANES - 2 * DN_HEADS,), F32)
        alog_lane = jnp.concatenate([pad_lo, dn_a_log[l].astype(F32), pad_hi])[None, :]
        dtb_lane = jnp.concatenate([pad_lo, dn_dt_bias[l].astype(F32), pad_hi])[None, :]
        y_dn = _deltanet(qkv, z.reshape(B, S, -1), ba.reshape(B, S, -1), convw,
                         alog_lane, dtb_lane, dn_out_norm[l][None, :])

        qgain = jnp.tile(swa_q_norm[l].astype(F32), SWA_HEADS)[None, :] * (SWA_HEAD_DIM ** -0.5)
        kgain = jnp.tile(swa_k_norm[l].astype(F32), SWA_KV_HEADS)[None, :]
        y_swa = _swa(rest, B, swa_sinks[l].astype(F32), bias_tabs, qgain, kgain, seg)

        x2 = _merge_ffn(x2, y_dn.reshape(T, -1), y_swa.reshape(T, -1), rest,
                        w_branch_dn[l].astype(BF16), w_branch_swa[l].astype(BF16),
                        w_out[l].astype(BF16), ffn_norm[l][None, :],
                        w_gate[l].astype(BF16), w_up[l].astype(BF16), w_down[l].astype(BF16),
                        tm=512)
    return x2.reshape(B, S, D)
```

```python
import functools
import math

import numpy as np
import jax
import jax.numpy as jnp
from jax import lax
from jax.experimental import pallas as pl
from jax.experimental.pallas import tpu as pltpu

D_MODEL = 1024
DN_HEADS = 4
DN_HEAD_DIM = 128
DN_WIDTH = DN_HEADS * DN_HEAD_DIM
DN_QKV_WIDTH = 3 * DN_WIDTH
DN_CONV = 4
DN_CHUNK = 64
SWA_HEADS = 8
SWA_KV_HEADS = 2
SWA_HEAD_DIM = 64
SWA_WIDTH = SWA_HEADS * SWA_HEAD_DIM
SWA_KV_WIDTH = SWA_KV_HEADS * SWA_HEAD_DIM
WINDOW = 128
SWA_BLOCK = 128
REL_BUCKETS = 32
REL_MAX_DIST = 128
D_FF = 2816
EPS = 1e-6

LANES = 128
SUBLANES = 8
MXU_WIDTH = 256
VMEM_LIMIT = 56 * 1024 * 1024

W_HEAD = DN_QKV_WIDTH + DN_WIDTH
W_REST = 2 * D_MODEL + SWA_WIDTH + 2 * SWA_KV_WIDTH
REST_PIECES = W_REST // MXU_WIDTH
PIECE_GATES = 0
PIECE_SQ = 2 * D_MODEL // MXU_WIDTH
PIECE_SKV = PIECE_SQ + SWA_WIDTH // MXU_WIDTH
QKV_BLOCKS = DN_QKV_WIDTH // DN_HEAD_DIM

F32 = jnp.float32
BF16 = jnp.bfloat16


def _mm(a, b, dims=(((1,), (0,)), ((), ()))):
    return lax.dot_general(a.astype(BF16), b.astype(BF16), dims, preferred_element_type=F32)


_NT = (((1,), (1,)), ((), ()))
_TN = (((0,), (0,)), ((), ()))


def _sigmoid(x):
    return 1.0 / (1.0 + jnp.exp(-x))


def _silu(x):
    return x * _sigmoid(x)


CONV_TAIL = SUBLANES
CONV_ROWS = 128


def _in_proj_kernel(x_ref, gain_ref, w_head_ref, w_rest_ref, w_ba_ref,
                    qkv_ref, z_ref, rest_ref, ba_ref):
    x = x_ref[...]
    ms = jnp.mean(x * x, axis=-1, keepdims=True)
    h = (x * lax.rsqrt(ms + EPS) * gain_ref[...]).astype(BF16)
    raw = jnp.dot(h, w_head_ref[:, :DN_QKV_WIDTH], preferred_element_type=F32)
    for c in range(QKV_BLOCKS):
        qkv_ref[c] = raw[:, c * DN_HEAD_DIM:(c + 1) * DN_HEAD_DIM]
    z_ref[...] = jnp.dot(h, w_head_ref[:, DN_QKV_WIDTH:], preferred_element_type=F32)
    for p in range(REST_PIECES):
        rest_ref[p] = jnp.dot(h, w_rest_ref[p], preferred_element_type=F32).astype(rest_ref.dtype)
    ba_ref[...] = jnp.dot(h, w_ba_ref[...], preferred_element_type=F32)


def _in_proj(x2, gain, w_in_bf16, w_rest, w_ba, tm):
    T = x2.shape[0]
    once = lambda shape: pl.BlockSpec(shape, lambda i: (0,) * len(shape),
                                      pipeline_mode=pl.Buffered(1))
    return pl.pallas_call(
        _in_proj_kernel,
        grid=(T // tm,),
        in_specs=[
            pl.BlockSpec((tm, D_MODEL), lambda i: (i, 0)),
            pl.BlockSpec((1, D_MODEL), lambda i: (0, 0)),
            once((D_MODEL, W_HEAD)),
            once((REST_PIECES, D_MODEL, MXU_WIDTH)),
            once((D_MODEL, LANES)),
        ],
        out_specs=(
            pl.BlockSpec((QKV_BLOCKS, tm, DN_HEAD_DIM), lambda i: (0, i, 0)),
            pl.BlockSpec((tm, DN_WIDTH), lambda i: (i, 0)),
            pl.BlockSpec((REST_PIECES, tm, MXU_WIDTH), lambda i: (0, i, 0)),
            pl.BlockSpec((tm, LANES), lambda i: (i, 0)),
        ),
        out_shape=(
            jax.ShapeDtypeStruct((QKV_BLOCKS, T, DN_HEAD_DIM), F32),
            jax.ShapeDtypeStruct((T, DN_WIDTH), F32),
            jax.ShapeDtypeStruct((REST_PIECES, T, MXU_WIDTH), BF16),
            jax.ShapeDtypeStruct((T, LANES), F32),
        ),
        compiler_params=pltpu.CompilerParams(
            dimension_semantics=("arbitrary",), vmem_limit_bytes=VMEM_LIMIT),
        name="in_proj",
    )(x2, gain, w_in_bf16, w_rest, w_ba)


DN_TILE = 128
DN_NCH = DN_TILE // DN_CHUNK
DN_LOG_CHUNK = int(math.log2(DN_CHUNK))
DN_STREAMS = 4


def _deltanet_kernel(raw_ref, raw_next_ref, z_ref, ba_ref, convw_ref, alog_ref, dtb_ref, gain_ref,
                     o_ref, state, xbuf, qkv_ref, qkv_next_ref):
    t = pl.program_id(1)
    scale = DN_HEAD_DIM ** -0.5
    streams = range(DN_STREAMS)

    def preprocess(s, dst_ref):
        for c in range(QKV_BLOCKS):
            w = convw_ref[c]
            for r0 in range(0, DN_TILE, CONV_ROWS):
                acc = w[DN_CONV - 1:DN_CONV, :] * xbuf[s, c, CONV_TAIL + r0:CONV_TAIL + r0 + CONV_ROWS, :]
                for j in range(DN_CONV - 1):
                    off = CONV_TAIL - (DN_CONV - 1) + j + r0
                    acc = acc + w[j:j + 1, :] * xbuf[s, c, off:off + CONV_ROWS, :]
                act = _silu(acc)
                if c < 2 * DN_HEADS:
                    norm = lax.rsqrt(jnp.sum(act * act, axis=-1, keepdims=True) + EPS)
                    act = act * (norm * scale if c < DN_HEADS else norm)
                dst_ref[s, c, r0:r0 + CONV_ROWS, :] = act.astype(BF16)

    @pl.when(t == 0)
    def _():
        state[...] = jnp.zeros_like(state)
        for s in streams:
            xbuf[s, :, 0:CONV_TAIL, :] = jnp.zeros((QKV_BLOCKS, CONV_TAIL, DN_HEAD_DIM), F32)
            xbuf[s, :, CONV_TAIL:, :] = raw_ref[:, s]
            preprocess(s, qkv_ref)

    for s in streams:
        xbuf[s, :, 0:CONV_TAIL, :] = raw_ref[:, s, DN_TILE - CONV_TAIL:DN_TILE, :]
        xbuf[s, :, CONV_TAIL:, :] = raw_next_ref[:, s]
        preprocess(s, qkv_next_ref)

    row = lax.broadcasted_iota(jnp.int32, (DN_TILE, DN_TILE), 0)
    col = lax.broadcasted_iota(jnp.int32, (DN_TILE, DN_TILE), 1)
    same_chunk = (row >> DN_LOG_CHUNK) == (col >> DN_LOG_CHUNK)
    lower_incl = same_chunk & (row >= col)
    strict_lower = same_chunk & (row > col)
    cumsum_mat = jnp.where(lower_incl, 1.0, 0.0).astype(BF16)
    eye = jnp.where(row == col, 1.0, 0.0)

    beta, gc, rev, gc_t = [], [], [], []
    for s in streams:
        ba = ba_ref[s]
        beta.append(_sigmoid(ba))
        xs = ba + dtb_ref[...]
        softplus = jnp.maximum(xs, 0.0) + jnp.log1p(jnp.exp(-jnp.abs(xs)))
        g = -jnp.exp(alog_ref[...]) * softplus
        g_hi = g.astype(BF16)
        g_lo = (g - g_hi.astype(F32)).astype(BF16)
        gc2 = jnp.dot(cumsum_mat, jnp.concatenate([g_hi, g_lo], axis=1),
                      preferred_element_type=F32)
        gc.append(gc2[:, :LANES] + gc2[:, LANES:])
        g_last = jnp.concatenate(
            [jnp.broadcast_to(gc[s][(c + 1) * DN_CHUNK - 1:(c + 1) * DN_CHUNK, :], (DN_CHUNK, LANES))
             for c in range(DN_NCH)], axis=0)
        rev.append(g_last - gc[s])
        gc_t.append(gc[s].T)

    units = [(s, h) for s in streams for h in range(DN_HEADS)]
    gcol = lambda arr, h: arr[:, DN_HEADS + h:DN_HEADS + h + 1]
    q, k, kb, rhs = {}, {}, {}, {}
    for s, h in units:
        q[s, h] = qkv_ref[s, h].astype(F32)
        k[s, h] = qkv_ref[s, DN_HEADS + h].astype(F32)
        vh = qkv_ref[s, 2 * DN_HEADS + h].astype(F32)
        beta_c = beta[s][:, h:h + 1]
        kb[s, h] = k[s, h] * beta_c
        rhs[s, h] = jnp.concatenate([vh * beta_c, kb[s, h] * jnp.exp(gcol(gc[s], h))], axis=1)
    gram = {p: _mm(jnp.concatenate([kb[p], q[p]], axis=0), k[p], _NT) for p in units}
    qk, n_pow, inv = {}, {}, {}
    for s, h in units:
        gc_r = gc_t[s][DN_HEADS + h:DN_HEADS + h + 1, :]
        decay = jnp.exp(jnp.where(lower_incl, gcol(gc[s], h) - gc_r, -jnp.inf))
        qk[s, h] = gram[s, h][DN_TILE:] * decay
        n_pow[s, h] = -jnp.where(strict_lower, gram[s, h][:DN_TILE] * decay, 0.0)
        inv[s, h] = eye + n_pow[s, h]
    for _ in range(DN_LOG_CHUNK - 1):
        n_pow = {p: _mm(n_pow[p], n_pow[p]) for p in units}
        inv = {p: inv[p] + _mm(inv[p], n_pow[p]) for p in units}
    sol = {p: _mm(inv[p], rhs[p]) for p in units}
    u = {p: sol[p][:, :DN_HEAD_DIM] for p in units}
    wmat = {p: sol[p][:, DN_HEAD_DIM:] for p in units}
    qe = {(s, h): q[s, h] * jnp.exp(gcol(gc[s], h)) for s, h in units}
    kdec = {(s, h): k[s, h] * jnp.exp(gcol(rev[s], h)) for s, h in units}

    st = {(s, h): state[s, h] for s, h in units}
    v_new = {p: [] for p in units}
    o_inter = {p: [] for p in units}
    for c in range(DN_NCH):
        r0, r1 = c * DN_CHUNK, (c + 1) * DN_CHUNK
        ws = {p: _mm(jnp.concatenate([wmat[p][r0:r1], qe[p][r0:r1]], axis=0), st[p]) for p in units}
        for p in units:
            v_new[p].append(u[p][r0:r1] - ws[p][:DN_CHUNK])
            o_inter[p].append(ws[p][DN_CHUNK:])
        upd = {p: _mm(kdec[p][r0:r1], v_new[p][c], _TN) for p in units}
        st = {(s, h): st[s, h] * jnp.exp(gc[s][r1 - 1:r1, DN_HEADS + h:DN_HEADS + h + 1]) + upd[s, h]
              for s, h in units}
    o_intra = {p: _mm(qk[p], jnp.concatenate(v_new[p], axis=0)) for p in units}
    for s, h in units:
        state[s, h] = st[s, h]
        o = jnp.concatenate(o_inter[s, h], axis=0) + o_intra[s, h]
        o = o * lax.rsqrt(jnp.mean(o * o, axis=-1, keepdims=True) + EPS) * gain_ref[...]
        zh = z_ref[s, :, h * DN_HEAD_DIM:(h + 1) * DN_HEAD_DIM]
        o_ref[s, :, h * DN_HEAD_DIM:(h + 1) * DN_HEAD_DIM] = (o * _silu(zh)).astype(o_ref.dtype)
    qkv_ref[...] = qkv_next_ref[...]


def _deltanet(raw, z, ba, convw, alog_lane, dtb_lane, out_gain):
    B, S, _ = z.shape
    tiles = S // DN_TILE
    raw = raw.reshape(QKV_BLOCKS, B, S, DN_HEAD_DIM)
    raw_block = (QKV_BLOCKS, DN_STREAMS, DN_TILE, DN_HEAD_DIM)
    qkv_block = (DN_STREAMS, QKV_BLOCKS, DN_TILE, DN_HEAD_DIM)
    return pl.pallas_call(
        _deltanet_kernel,
        grid=(B // DN_STREAMS, tiles),
        in_specs=[
            pl.BlockSpec(raw_block, lambda b, t: (0, b, t, 0)),
            pl.BlockSpec(raw_block, lambda b, t: (0, b, jnp.minimum(t + 1, tiles - 1), 0)),
            pl.BlockSpec((DN_STREAMS, DN_TILE, DN_WIDTH), lambda b, t: (b, t, 0)),
            pl.BlockSpec((DN_STREAMS, DN_TILE, LANES), lambda b, t: (b, t, 0)),
            pl.BlockSpec((QKV_BLOCKS, DN_CONV, DN_HEAD_DIM), lambda b, t: (0, 0, 0)),
            pl.BlockSpec((1, LANES), lambda b, t: (0, 0)),
            pl.BlockSpec((1, LANES), lambda b, t: (0, 0)),
            pl.BlockSpec((1, DN_HEAD_DIM), lambda b, t: (0, 0)),
        ],
        out_specs=pl.BlockSpec((DN_STREAMS, DN_TILE, DN_WIDTH), lambda b, t: (b, t, 0)),
        out_shape=jax.ShapeDtypeStruct((B, S, DN_WIDTH), BF16),
        scratch_shapes=[
            pltpu.VMEM((DN_STREAMS, DN_HEADS, DN_HEAD_DIM, DN_HEAD_DIM), F32),
            pltpu.VMEM((DN_STREAMS, QKV_BLOCKS, CONV_TAIL + DN_TILE, DN_HEAD_DIM), F32),
            pltpu.VMEM(qkv_block, BF16), pltpu.VMEM(qkv_block, BF16)],
        compiler_params=pltpu.CompilerParams(
            dimension_semantics=("arbitrary", "arbitrary"), vmem_limit_bytes=VMEM_LIMIT),
        name="deltanet",
    )(raw, raw, z, ba, convw, alog_lane, dtb_lane, out_gain)


SWA_BAND = 2 * SWA_BLOCK
SWA_SLOTS = SWA_WIDTH // LANES
SWA_HEADS_PER_SLOT = LANES // SWA_HEAD_DIM


SWA_STEP = 4
SWA_ROWS = SWA_STEP * SWA_BLOCK


def _swa_kernel(sinks_ref, sq_ref, kv_ref, bias_ref, qgain_ref, kgain_ref, seg_ref, o_ref,
                band_ref):
    n = pl.program_id(1)
    kv = kv_ref[0].astype(F32)
    k = kv[:, :SWA_KV_WIDTH]
    slots_per_piece = MXU_WIDTH // LANES
    q_slots = [sq_ref[j // slots_per_piece, :, (j % slots_per_piece) * LANES:
                      (j % slots_per_piece + 1) * LANES].astype(F32) for j in range(SWA_SLOTS)]
    parts = q_slots + [k]
    ms = _mm(jnp.concatenate([p * p for p in parts], axis=0), seg_ref[...])
    qn = [parts[j] * lax.rsqrt(ms[j * SWA_ROWS:(j + 1) * SWA_ROWS] + EPS)
          * qgain_ref[:, j * LANES:(j + 1) * LANES] for j in range(SWA_SLOTS)]
    kn = k * lax.rsqrt(ms[SWA_SLOTS * SWA_ROWS:] + EPS) * kgain_ref[...]

    @pl.when(n == 0)
    def _():
        band_ref[0:SWA_BLOCK, :] = jnp.zeros((SWA_BLOCK, 2 * SWA_KV_WIDTH), F32)

    @pl.when(n > 0)
    def _():
        band_ref[0:SWA_BLOCK, :] = band_ref[SWA_ROWS:SWA_ROWS + SWA_BLOCK, :]

    band_ref[SWA_BLOCK:, :SWA_KV_WIDTH] = kn
    band_ref[SWA_BLOCK:, SWA_KV_WIDTH:] = kv[:, SWA_KV_WIDTH:]
    k = band_ref[:, :SWA_KV_WIDTH]
    v = band_ref[:, SWA_KV_WIDTH:]

    lane = lax.broadcasted_iota(jnp.int32, (SWA_BLOCK + SWA_ROWS, LANES), 1)
    low = lane < SWA_HEAD_DIM
    k_rot = pltpu.roll(k, SWA_HEAD_DIM, axis=1)
    v_rot = pltpu.roll(v, SWA_HEAD_DIM, axis=1)
    k_lo = (jnp.where(low, k, 0.0).astype(BF16), jnp.where(low, k_rot, 0.0).astype(BF16))
    k_hi = (jnp.where(low, 0.0, k_rot).astype(BF16), jnp.where(low, 0.0, k).astype(BF16))
    v_lo = (jnp.where(low, v, 0.0).astype(BF16), jnp.where(low, v_rot, 0.0).astype(BF16))
    v_hi = (jnp.where(low, 0.0, v_rot).astype(BF16), jnp.where(low, 0.0, v).astype(BF16))

    out_low = lax.broadcasted_iota(jnp.int32, (SWA_BLOCK, LANES), 1) < SWA_HEAD_DIM

    slots_per_kv = SWA_SLOTS // SWA_KV_HEADS
    units = [(u, j) for u in range(SWA_STEP) for j in range(SWA_SLOTS)]
    band_of = lambda arr, u: arr[u * SWA_BLOCK:u * SWA_BLOCK + SWA_BAND]
    k_bd = {(u, kh): jnp.concatenate([band_of(k_lo[kh], u), band_of(k_hi[kh], u)], axis=0)
            for u in range(SWA_STEP) for kh in range(SWA_KV_HEADS)}
    v_bd = {(u, kh): jnp.concatenate([band_of(v_lo[kh], u), band_of(v_hi[kh], u)], axis=0)
            for u in range(SWA_STEP) for kh in range(SWA_KV_HEADS)}
    logits = {(u, j): _mm(qn[j][u * SWA_BLOCK:(u + 1) * SWA_BLOCK], k_bd[u, j // slots_per_kv], _NT)
              for u, j in units}
    first_tab = jnp.minimum(n, 1)
    probs = {}
    inv_den = {}
    for u, j in units:
        for r in range(SWA_HEADS_PER_SLOT):
            hd = j * SWA_HEADS_PER_SLOT + r
            bias = bias_ref[first_tab, hd] if u == 0 else bias_ref[1, hd]
            lg = logits[u, j][:, r * SWA_BAND:(r + 1) * SWA_BAND] + bias
            sink = sinks_ref[hd]
            m = jnp.maximum(jnp.max(lg, axis=-1, keepdims=True), sink)
            p = jnp.exp(lg - m)
            den = jnp.sum(p, axis=-1, keepdims=True) + jnp.exp(sink - m)
            probs[u, j, r] = p.astype(BF16)
            inv_den[u, j, r] = 1.0 / den
    outs = {(u, j): _mm(jnp.concatenate([probs[u, j, r] for r in range(SWA_HEADS_PER_SLOT)],
                                        axis=1), v_bd[u, j // slots_per_kv])
            for u, j in units}
    for u, j in units:
        o_ref[0, u * SWA_BLOCK:(u + 1) * SWA_BLOCK, j * LANES:(j + 1) * LANES] = (
            outs[u, j] * jnp.where(out_low, inv_den[u, j, 0], inv_den[u, j, 1])).astype(o_ref.dtype)


def _swa(rest, batch, sinks, bias_tabs, qgain, kgain, seg):
    B = batch
    S = rest.shape[1] // B
    steps = S // SWA_ROWS
    kvw = 2 * SWA_KV_WIDTH
    sq_pieces = SWA_WIDTH // MXU_WIDTH
    assert kvw == MXU_WIDTH
    grid_spec = pltpu.PrefetchScalarGridSpec(
        num_scalar_prefetch=1,
        grid=(B, steps),
        in_specs=[
            pl.BlockSpec((sq_pieces, SWA_ROWS, MXU_WIDTH),
                         lambda b, n, s: (PIECE_SQ // sq_pieces, b * steps + n, 0)),
            pl.BlockSpec((1, SWA_ROWS, kvw), lambda b, n, s: (PIECE_SKV, b * steps + n, 0)),
            pl.BlockSpec((2, SWA_HEADS, SWA_BLOCK, SWA_BAND), lambda b, n, s: (0, 0, 0, 0)),
            pl.BlockSpec((1, SWA_WIDTH), lambda b, n, s: (0, 0)),
            pl.BlockSpec((1, SWA_KV_WIDTH), lambda b, n, s: (0, 0)),
            pl.BlockSpec((LANES, LANES), lambda b, n, s: (0, 0)),
        ],
        out_specs=pl.BlockSpec((1, SWA_ROWS, SWA_WIDTH), lambda b, n, s: (b, n, 0)),
        scratch_shapes=[pltpu.VMEM((SWA_BLOCK + SWA_ROWS, kvw), F32)],
    )
    return pl.pallas_call(
        _swa_kernel,
        grid_spec=grid_spec,
        out_shape=jax.ShapeDtypeStruct((B, S, SWA_WIDTH), BF16),
        compiler_params=pltpu.CompilerParams(
            dimension_semantics=("arbitrary", "arbitrary"), vmem_limit_bytes=VMEM_LIMIT),
        name="swa",
    )(sinks, rest, rest, bias_tabs, qgain, kgain, seg)


def _merge_ffn_kernel(x_ref, ydn_ref, yswa_ref, g_ref, wa_ref, wb_ref, wo_ref, gain_ref,
                      wg_ref, wu_ref, wd_ref, o_ref, act_ref):
    per_branch = D_MODEL // MXU_WIDTH
    ga, gb = [_sigmoid(jnp.concatenate(
        [g_ref[br * per_branch + p].astype(F32) for p in range(per_branch)], axis=1))
        for br in range(2)]
    merged = ga * _mm(ydn_ref[...], wa_ref[...]) + gb * _mm(yswa_ref[...], wb_ref[...])
    x1 = x_ref[...] + _mm(merged, wo_ref[...])
    ms = jnp.mean(x1 * x1, axis=-1, keepdims=True)
    h2 = (x1 * lax.rsqrt(ms + EPS) * gain_ref[...]).astype(BF16)
    for c0 in range(0, D_FF, MXU_WIDTH):
        gate = jnp.dot(h2, wg_ref[:, c0:c0 + MXU_WIDTH], preferred_element_type=F32)
        up = jnp.dot(h2, wu_ref[:, c0:c0 + MXU_WIDTH], preferred_element_type=F32)
        act_ref[:, c0:c0 + MXU_WIDTH] = (_silu(gate) * up).astype(BF16)
    o_ref[...] = x1 + jnp.dot(act_ref[...], wd_ref[...], preferred_element_type=F32)


def _merge_ffn(x2, ydn, yswa, graw, wa, wb, wo, gain, wg, wu, wd, tm):
    T = x2.shape[0]
    row = lambda w: pl.BlockSpec((tm, w), lambda i: (i, 0))
    once = lambda a, b: pl.BlockSpec((a, b), lambda i: (0, 0), pipeline_mode=pl.Buffered(1))
    return pl.pallas_call(
        _merge_ffn_kernel,
        grid=(T // tm,),
        in_specs=[row(D_MODEL), row(DN_WIDTH), row(SWA_WIDTH),
                  pl.BlockSpec((PIECE_SQ - PIECE_GATES, tm, MXU_WIDTH), lambda i: (0, i, 0)),
                  once(DN_WIDTH, D_MODEL), once(SWA_WIDTH, D_MODEL), once(D_MODEL, D_MODEL),
                  once(1, D_MODEL),
                  once(D_MODEL, D_FF), once(D_MODEL, D_FF), once(D_FF, D_MODEL)],
        out_specs=row(D_MODEL),
        out_shape=jax.ShapeDtypeStruct((T, D_MODEL), F32),
        scratch_shapes=[pltpu.VMEM((tm, D_FF), BF16)],
        compiler_params=pltpu.CompilerParams(
            dimension_semantics=("arbitrary",), vmem_limit_bytes=VMEM_LIMIT),
        name="merge_ffn",
    )(x2, ydn, yswa, graw, wa, wb, wo, gain, wg, wu, wd)


def _t5_bucket_table():
    qi = jnp.arange(SWA_BLOCK)[:, None]
    kj = jnp.arange(SWA_BAND)[None, :]
    dist = SWA_BLOCK + qi - kj
    in_window = (dist >= 0) & (dist < WINDOW)
    n = jnp.maximum(dist, 0)
    max_exact = REL_BUCKETS // 2
    nf = jnp.maximum(n, 1).astype(F32)
    large = max_exact + (jnp.log(nf / max_exact) / math.log(REL_MAX_DIST / max_exact)
                         * (REL_BUCKETS - max_exact)).astype(jnp.int32)
    large = jnp.minimum(large, REL_BUCKETS - 1)
    return jnp.where(n < max_exact, n, large), in_window


def _segment_mean_matrix(width, seg):
    idx = np.arange(width) // seg
    return jnp.asarray((idx[:, None] == idx[None, :]).astype(np.float32) / seg, dtype=BF16)


def kernel(x, attn_norm, w_in, dn_conv, dn_a_log, dn_dt_bias, dn_out_norm, swa_q_norm, swa_k_norm, swa_sinks, rel_bias, w_branch_dn, w_branch_swa, w_out, ffn_norm, w_gate, w_up, w_down):
    B, S, D = x.shape
    T = B * S
    depth = w_in.shape[0]
    bucket, in_window = _t5_bucket_table()
    seg = _segment_mean_matrix(LANES, SWA_HEAD_DIM)
    rel = rel_bias.astype(F32)
    bias_tab = sum(jnp.where(bucket[None] == b, rel[b][:, None, None], 0.0)
                   for b in range(REL_BUCKETS))
    bias_tab = jnp.where(in_window[None], bias_tab, -jnp.inf)
    has_prev = jnp.arange(SWA_BAND)[None, None, :] >= SWA_BLOCK
    bias_tabs = jnp.stack([jnp.where(has_prev, bias_tab, -jnp.inf), bias_tab])

    x2 = x.reshape(T, D)
    for l in range(depth):
        wl = w_in[l].astype(BF16)
        o_ba = W_HEAD
        o_sq = o_ba + 2 * DN_HEADS
        o_g = o_sq + SWA_WIDTH + 2 * SWA_KV_WIDTH
        w_rest = jnp.concatenate([wl[:, o_g:], wl[:, o_sq:o_g]], axis=1)
        w_rest = w_rest.reshape(D, REST_PIECES, MXU_WIDTH).transpose(1, 0, 2)
        w_ba = jnp.pad(wl[:, o_ba:o_sq], ((0, 0), (0, LANES - 2 * DN_HEADS)))
        convw = dn_conv[l].astype(F32).reshape(DN_CONV, QKV_BLOCKS, DN_HEAD_DIM).transpose(1, 0, 2)

        qkv, z, rest, ba = _in_proj(x2, attn_norm[l][None, :], wl, w_rest, w_ba, tm=512)

        pad_lo = jnp.zeros((DN_HEADS,), F32)
        pad_hi = jnp.zeros((LANES - 2 * DN_HEADS,), F32)
        alog_lane = jnp.concatenate([pad_lo, dn_a_log[l].astype(F32), pad_hi])[None, :]
        dtb_lane = jnp.concatenate([pad_lo, dn_dt_bias[l].astype(F32), pad_hi])[None, :]
        y_dn = _deltanet(qkv, z.reshape(B, S, -1), ba.reshape(B, S, -1), convw,
                         alog_lane, dtb_lane, dn_out_norm[l][None, :])

        qgain = jnp.tile(swa_q_norm[l].astype(F32), SWA_HEADS)[None, :] * (SWA_HEAD_DIM ** -0.5)
        kgain = jnp.tile(swa_k_norm[l].astype(F32), SWA_KV_HEADS)[None, :]
        y_swa = _swa(rest, B, swa_sinks[l].astype(F32), bias_tabs, qgain, kgain, seg)

        x2 = _merge_ffn(x2, y_dn.reshape(T, -1), y_swa.reshape(T, -1), rest,
                        w_branch_dn[l].astype(BF16), w_branch_swa[l].astype(BF16),
                        w_out[l].astype(BF16), ffn_norm[l][None, :],
                        w_gate[l].astype(BF16), w_up[l].astype(BF16), w_down[l].astype(BF16),
                        tm=512)
    return x2.reshape(B, S, D)
```

```python
import functools
import math

import numpy as np
import jax
import jax.numpy as jnp
from jax import lax
from jax.experimental import pallas as pl
from jax.experimental.pallas import tpu as pltpu

D_MODEL = 1024
DN_HEADS = 4
DN_HEAD_DIM = 128
DN_WIDTH = DN_HEADS * DN_HEAD_DIM
DN_QKV_WIDTH = 3 * DN_WIDTH
DN_CONV = 4
DN_CHUNK = 64
SWA_HEADS = 8
SWA_KV_HEADS = 2
SWA_HEAD_DIM = 64
SWA_WIDTH = SWA_HEADS * SWA_HEAD_DIM
SWA_KV_WIDTH = SWA_KV_HEADS * SWA_HEAD_DIM
WINDOW = 128
SWA_BLOCK = 128
REL_BUCKETS = 32
REL_MAX_DIST = 128
D_FF = 2816
EPS = 1e-6

LANES = 128
SUBLANES = 8
MXU_WIDTH = 256
VMEM_LIMIT = 56 * 1024 * 1024

W_HEAD = DN_QKV_WIDTH + DN_WIDTH
W_REST = 2 * D_MODEL + SWA_WIDTH + 2 * SWA_KV_WIDTH
REST_PIECES = W_REST // MXU_WIDTH
PIECE_GATES = 0
PIECE_SQ = 2 * D_MODEL // MXU_WIDTH
PIECE_SKV = PIECE_SQ + SWA_WIDTH // MXU_WIDTH
QKV_BLOCKS = DN_QKV_WIDTH // DN_HEAD_DIM

F32 = jnp.float32
BF16 = jnp.bfloat16


def _mm(a, b, dims=(((1,), (0,)), ((), ()))):
    return lax.dot_general(a.astype(BF16), b.astype(BF16), dims, preferred_element_type=F32)


_NT = (((1,), (1,)), ((), ()))
_TN = (((0,), (0,)), ((), ()))


def _sigmoid(x):
    return 0.5 * jnp.tanh(0.5 * x) + 0.5


def _silu(x):
    h = 0.5 * x
    return h * jnp.tanh(h) + h


CONV_TAIL = SUBLANES
CONV_ROWS = 128


def _in_proj_kernel(x_ref, gain_ref, w_head_ref, w_rest_ref, w_ba_ref,
                    qkv_ref, z_ref, rest_ref, ba_ref):
    x = x_ref[...]
    ms = jnp.mean(x * x, axis=-1, keepdims=True)
    h = (x * lax.rsqrt(ms + EPS) * gain_ref[...]).astype(BF16)
    raw = jnp.dot(h, w_head_ref[:, :DN_QKV_WIDTH], preferred_element_type=F32)
    for c in range(QKV_BLOCKS):
        qkv_ref[c] = raw[:, c * DN_HEAD_DIM:(c + 1) * DN_HEAD_DIM]
    z_ref[...] = jnp.dot(h, w_head_ref[:, DN_QKV_WIDTH:], preferred_element_type=F32)
    for p in range(REST_PIECES):
        rest_ref[p] = jnp.dot(h, w_rest_ref[p], preferred_element_type=F32).astype(rest_ref.dtype)
    ba_ref[...] = jnp.dot(h, w_ba_ref[...], preferred_element_type=F32)


def _in_proj(x2, gain, w_in_bf16, w_rest, w_ba, tm):
    T = x2.shape[0]
    once = lambda shape: pl.BlockSpec(shape, lambda i: (0,) * len(shape),
                                      pipeline_mode=pl.Buffered(1))
    return pl.pallas_call(
        _in_proj_kernel,
        grid=(T // tm,),
        in_specs=[
            pl.BlockSpec((tm, D_MODEL), lambda i: (i, 0)),
            pl.BlockSpec((1, D_MODEL), lambda i: (0, 0)),
            once((D_MODEL, W_HEAD)),
            once((REST_PIECES, D_MODEL, MXU_WIDTH)),
            once((D_MODEL, LANES)),
        ],
        out_specs=(
            pl.BlockSpec((QKV_BLOCKS, tm, DN_HEAD_DIM), lambda i: (0, i, 0)),
            pl.BlockSpec((tm, DN_WIDTH), lambda i: (i, 0)),
            pl.BlockSpec((REST_PIECES, tm, MXU_WIDTH), lambda i: (0, i, 0)),
            pl.BlockSpec((tm, LANES), lambda i: (i, 0)),
        ),
        out_shape=(
            jax.ShapeDtypeStruct((QKV_BLOCKS, T, DN_HEAD_DIM), F32),
            jax.ShapeDtypeStruct((T, DN_WIDTH), F32),
            jax.ShapeDtypeStruct((REST_PIECES, T, MXU_WIDTH), BF16),
            jax.ShapeDtypeStruct((T, LANES), F32),
        ),
        compiler_params=pltpu.CompilerParams(
            dimension_semantics=("arbitrary",), vmem_limit_bytes=VMEM_LIMIT),
        name="in_proj",
    )(x2, gain, w_in_bf16, w_rest, w_ba)


DN_TILE = 128
DN_NCH = DN_TILE // DN_CHUNK
DN_LOG_CHUNK = int(math.log2(DN_CHUNK))
DN_STREAMS = 4


def _deltanet_kernel(raw_ref, raw_next_ref, z_ref, ba_ref, convw_ref, alog_ref, dtb_ref, gain_ref,
                     o_ref, state, xbuf, qkv_ref, qkv_next_ref):
    t = pl.program_id(1)
    scale = DN_HEAD_DIM ** -0.5
    streams = range(DN_STREAMS)

    def preprocess(s, dst_ref):
        for c in range(QKV_BLOCKS):
            w = convw_ref[c]
            for r0 in range(0, DN_TILE, CONV_ROWS):
                acc = w[DN_CONV - 1:DN_CONV, :] * xbuf[s, c, CONV_TAIL + r0:CONV_TAIL + r0 + CONV_ROWS, :]
                for j in range(DN_CONV - 1):
                    off = CONV_TAIL - (DN_CONV - 1) + j + r0
                    acc = acc + w[j:j + 1, :] * xbuf[s, c, off:off + CONV_ROWS, :]
                act = _silu(acc)
                if c < 2 * DN_HEADS:
                    norm = lax.rsqrt(jnp.sum(act * act, axis=-1, keepdims=True) + EPS)
                    act = act * (norm * scale if c < DN_HEADS else norm)
                dst_ref[s, c, r0:r0 + CONV_ROWS, :] = act.astype(BF16)

    @pl.when(t == 0)
    def _():
        state[...] = jnp.zeros_like(state)
        for s in streams:
            xbuf[s, :, 0:CONV_TAIL, :] = jnp.zeros((QKV_BLOCKS, CONV_TAIL, DN_HEAD_DIM), F32)
            xbuf[s, :, CONV_TAIL:, :] = raw_ref[:, s]
            preprocess(s, qkv_ref)

    for s in streams:
        xbuf[s, :, 0:CONV_TAIL, :] = raw_ref[:, s, DN_TILE - CONV_TAIL:DN_TILE, :]
        xbuf[s, :, CONV_TAIL:, :] = raw_next_ref[:, s]
        preprocess(s, qkv_next_ref)

    row = lax.broadcasted_iota(jnp.int32, (DN_TILE, DN_TILE), 0)
    col = lax.broadcasted_iota(jnp.int32, (DN_TILE, DN_TILE), 1)
    same_chunk = (row >> DN_LOG_CHUNK) == (col >> DN_LOG_CHUNK)
    lower_incl = same_chunk & (row >= col)
    strict_lower = same_chunk & (row > col)
    cumsum_mat = jnp.where(lower_incl, 1.0, 0.0).astype(BF16)
    eye = jnp.where(row == col, 1.0, 0.0)

    beta, gc, rev, gc_t = [], [], [], []
    for s in streams:
        ba = ba_ref[s]
        beta.append(_sigmoid(ba))
        xs = ba + dtb_ref[...]
        softplus = jnp.maximum(xs, 0.0) + jnp.log1p(jnp.exp(-jnp.abs(xs)))
        g = -jnp.exp(alog_ref[...]) * softplus
        g_hi = g.astype(BF16)
        g_lo = (g - g_hi.astype(F32)).astype(BF16)
        gc2 = jnp.dot(cumsum_mat, jnp.concatenate([g_hi, g_lo], axis=1),
                      preferred_element_type=F32)
        gc.append(gc2[:, :LANES] + gc2[:, LANES:])
        g_last = jnp.concatenate(
            [jnp.broadcast_to(gc[s][(c + 1) * DN_CHUNK - 1:(c + 1) * DN_CHUNK, :], (DN_CHUNK, LANES))
             for c in range(DN_NCH)], axis=0)
        rev.append(g_last - gc[s])
        gc_t.append(gc[s].T)

    units = [(s, h) for s in streams for h in range(DN_HEADS)]
    gcol = lambda arr, h: arr[:, DN_HEADS + h:DN_HEADS + h + 1]
    q, k, kb, rhs = {}, {}, {}, {}
    for s, h in units:
        q[s, h] = qkv_ref[s, h].astype(F32)
        k[s, h] = qkv_ref[s, DN_HEADS + h].astype(F32)
        vh = qkv_ref[s, 2 * DN_HEADS + h].astype(F32)
        beta_c = beta[s][:, h:h + 1]
        kb[s, h] = k[s, h] * beta_c
        rhs[s, h] = jnp.concatenate([vh * beta_c, kb[s, h] * jnp.exp(gcol(gc[s], h))], axis=1)
    gram = {p: _mm(jnp.concatenate([kb[p], q[p]], axis=0), k[p], _NT) for p in units}
    qk, n_pow, inv = {}, {}, {}
    for s, h in units:
        gc_r = gc_t[s][DN_HEADS + h:DN_HEADS + h + 1, :]
        decay = jnp.exp(jnp.where(lower_incl, gcol(gc[s], h) - gc_r, -jnp.inf))
        qk[s, h] = gram[s, h][DN_TILE:] * decay
        n_pow[s, h] = -jnp.where(strict_lower, gram[s, h][:DN_TILE] * decay, 0.0)
        inv[s, h] = eye + n_pow[s, h]
    for _ in range(DN_LOG_CHUNK - 1):
        n_pow = {p: _mm(n_pow[p], n_pow[p]) for p in units}
        inv = {p: inv[p] + _mm(inv[p], n_pow[p]) for p in units}
    sol = {p: _mm(inv[p], rhs[p]) for p in units}
    u = {p: sol[p][:, :DN_HEAD_DIM] for p in units}
    wmat = {p: sol[p][:, DN_HEAD_DIM:] for p in units}
    qe = {(s, h): q[s, h] * jnp.exp(gcol(gc[s], h)) for s, h in units}
    kdec = {(s, h): k[s, h] * jnp.exp(gcol(rev[s], h)) for s, h in units}

    st = {(s, h): state[s, h] for s, h in units}
    v_new = {p: [] for p in units}
    o_inter = {p: [] for p in units}
    for c in range(DN_NCH):
        r0, r1 = c * DN_CHUNK, (c + 1) * DN_CHUNK
        ws = {p: _mm(jnp.concatenate([wmat[p][r0:r1], qe[p][r0:r1]], axis=0), st[p]) for p in units}
        for p in units:
            v_new[p].append(u[p][r0:r1] - ws[p][:DN_CHUNK])
            o_inter[p].append(ws[p][DN_CHUNK:])
        upd = {p: _mm(kdec[p][r0:r1], v_new[p][c], _TN) for p in units}
        st = {(s, h): st[s, h] * jnp.exp(gc[s][r1 - 1:r1, DN_HEADS + h:DN_HEADS + h + 1]) + upd[s, h]
              for s, h in units}
    o_intra = {p: _mm(qk[p], jnp.concatenate(v_new[p], axis=0)) for p in units}
    for s, h in units:
        state[s, h] = st[s, h]
        o = jnp.concatenate(o_inter[s, h], axis=0) + o_intra[s, h]
        o = o * lax.rsqrt(jnp.mean(o * o, axis=-1, keepdims=True) + EPS) * gain_ref[...]
        zh = z_ref[s, :, h * DN_HEAD_DIM:(h + 1) * DN_HEAD_DIM]
        o_ref[s, :, h * DN_HEAD_DIM:(h + 1) * DN_HEAD_DIM] = (o * _silu(zh)).astype(o_ref.dtype)
    qkv_ref[...] = qkv_next_ref[...]


def _deltanet(raw, z, ba, convw, alog_lane, dtb_lane, out_gain):
    B, S, _ = z.shape
    tiles = S // DN_TILE
    raw = raw.reshape(QKV_BLOCKS, B, S, DN_HEAD_DIM)
    raw_block = (QKV_BLOCKS, DN_STREAMS, DN_TILE, DN_HEAD_DIM)
    qkv_block = (DN_STREAMS, QKV_BLOCKS, DN_TILE, DN_HEAD_DIM)
    return pl.pallas_call(
        _deltanet_kernel,
        grid=(B // DN_STREAMS, tiles),
        in_specs=[
            pl.BlockSpec(raw_block, lambda b, t: (0, b, t, 0)),
            pl.BlockSpec(raw_block, lambda b, t: (0, b, jnp.minimum(t + 1, tiles - 1), 0)),
            pl.BlockSpec((DN_STREAMS, DN_TILE, DN_WIDTH), lambda b, t: (b, t, 0)),
            pl.BlockSpec((DN_STREAMS, DN_TILE, LANES), lambda b, t: (b, t, 0)),
            pl.BlockSpec((QKV_BLOCKS, DN_CONV, DN_HEAD_DIM), lambda b, t: (0, 0, 0)),
            pl.BlockSpec((1, LANES), lambda b, t: (0, 0)),
            pl.BlockSpec((1, LANES), lambda b, t: (0, 0)),
            pl.BlockSpec((1, DN_HEAD_DIM), lambda b, t: (0, 0)),
        ],
        out_specs=pl.BlockSpec((DN_STREAMS, DN_TILE, DN_WIDTH), lambda b, t: (b, t, 0)),
        out_shape=jax.ShapeDtypeStruct((B, S, DN_WIDTH), BF16),
        scratch_shapes=[
            pltpu.VMEM((DN_STREAMS, DN_HEADS, DN_HEAD_DIM, DN_HEAD_DIM), F32),
            pltpu.VMEM((DN_STREAMS, QKV_BLOCKS, CONV_TAIL + DN_TILE, DN_HEAD_DIM), F32),
            pltpu.VMEM(qkv_block, BF16), pltpu.VMEM(qkv_block, BF16)],
        compiler_params=pltpu.CompilerParams(
            dimension_semantics=("arbitrary", "arbitrary"), vmem_limit_bytes=VMEM_LIMIT),
        name="deltanet",
    )(raw, raw, z, ba, convw, alog_lane, dtb_lane, out_gain)


SWA_BAND = 2 * SWA_BLOCK
SWA_SLOTS = SWA_WIDTH // LANES
SWA_HEADS_PER_SLOT = LANES // SWA_HEAD_DIM


SWA_STEP = 4
SWA_ROWS = SWA_STEP * SWA_BLOCK


def _swa_kernel(sinks_ref, sq_ref, kv_ref, bias_ref, qgain_ref, kgain_ref, seg_ref, o_ref,
                band_ref):
    n = pl.program_id(1)
    kv = kv_ref[0].astype(F32)
    k = kv[:, :SWA_KV_WIDTH]
    slots_per_piece = MXU_WIDTH // LANES
    q_slots = [sq_ref[j // slots_per_piece, :, (j % slots_per_piece) * LANES:
                      (j % slots_per_piece + 1) * LANES].astype(F32) for j in range(SWA_SLOTS)]
    parts = q_slots + [k]
    ms = _mm(jnp.concatenate([p * p for p in parts], axis=0), seg_ref[...])
    qn = [parts[j] * lax.rsqrt(ms[j * SWA_ROWS:(j + 1) * SWA_ROWS] + EPS)
          * qgain_ref[:, j * LANES:(j + 1) * LANES] for j in range(SWA_SLOTS)]
    kn = k * lax.rsqrt(ms[SWA_SLOTS * SWA_ROWS:] + EPS) * kgain_ref[...]

    @pl.when(n == 0)
    def _():
        band_ref[0:SWA_BLOCK, :] = jnp.zeros((SWA_BLOCK, 2 * SWA_KV_WIDTH), F32)

    @pl.when(n > 0)
    def _():
        band_ref[0:SWA_BLOCK, :] = band_ref[SWA_ROWS:SWA_ROWS + SWA_BLOCK, :]

    band_ref[SWA_BLOCK:, :SWA_KV_WIDTH] = kn
    band_ref[SWA_BLOCK:, SWA_KV_WIDTH:] = kv[:, SWA_KV_WIDTH:]
    k = band_ref[:, :SWA_KV_WIDTH]
    v = band_ref[:, SWA_KV_WIDTH:]

    lane = lax.broadcasted_iota(jnp.int32, (SWA_BLOCK + SWA_ROWS, LANES), 1)
    low = lane < SWA_HEAD_DIM
    k_rot = pltpu.roll(k, SWA_HEAD_DIM, axis=1)
    v_rot = pltpu.roll(v, SWA_HEAD_DIM, axis=1)
    k_lo = (jnp.where(low, k, 0.0).astype(BF16), jnp.where(low, k_rot, 0.0).astype(BF16))
    k_hi = (jnp.where(low, 0.0, k_rot).astype(BF16), jnp.where(low, 0.0, k).astype(BF16))
    v_lo = (jnp.where(low, v, 0.0).astype(BF16), jnp.where(low, v_rot, 0.0).astype(BF16))
    v_hi = (jnp.where(low, 0.0, v_rot).astype(BF16), jnp.where(low, 0.0, v).astype(BF16))

    out_low = lax.broadcasted_iota(jnp.int32, (SWA_BLOCK, LANES), 1) < SWA_HEAD_DIM

    slots_per_kv = SWA_SLOTS // SWA_KV_HEADS
    units = [(u, j) for u in range(SWA_STEP) for j in range(SWA_SLOTS)]
    band_of = lambda arr, u: arr[u * SWA_BLOCK:u * SWA_BLOCK + SWA_BAND]
    k_bd = {(u, kh): jnp.concatenate([band_of(k_lo[kh], u), band_of(k_hi[kh], u)], axis=0)
            for u in range(SWA_STEP) for kh in range(SWA_KV_HEADS)}
    v_bd = {(u, kh): jnp.concatenate([band_of(v_lo[kh], u), band_of(v_hi[kh], u)], axis=0)
            for u in range(SWA_STEP) for kh in range(SWA_KV_HEADS)}
    logits = {(u, j): _mm(qn[j][u * SWA_BLOCK:(u + 1) * SWA_BLOCK], k_bd[u, j // slots_per_kv], _NT)
              for u, j in units}
    first_tab = jnp.minimum(n, 1)
    probs = {}
    inv_den = {}
    for u, j in units:
        for r in range(SWA_HEADS_PER_SLOT):
            hd = j * SWA_HEADS_PER_SLOT + r
            bias = bias_ref[first_tab, hd] if u == 0 else bias_ref[1, hd]
            lg = logits[u, j][:, r * SWA_BAND:(r + 1) * SWA_BAND] + bias
            sink = sinks_ref[hd]
            m = jnp.maximum(jnp.max(lg, axis=-1, keepdims=True), sink)
            p = jnp.exp(lg - m)
            den = jnp.sum(p, axis=-1, keepdims=True) + jnp.exp(sink - m)
            probs[u, j, r] = p.astype(BF16)
            inv_den[u, j, r] = 1.0 / den
    outs = {(u, j): _mm(jnp.concatenate([probs[u, j, r] for r in range(SWA_HEADS_PER_SLOT)],
                                        axis=1), v_bd[u, j // slots_per_kv])
            for u, j in units}
    for u, j in units:
        o_ref[0, u * SWA_BLOCK:(u + 1) * SWA_BLOCK, j * LANES:(j + 1) * LANES] = (
            outs[u, j] * jnp.where(out_low, inv_den[u, j, 0], inv_den[u, j, 1])).astype(o_ref.dtype)


def _swa(rest, batch, sinks, bias_tabs, qgain, kgain, seg):
    B = batch
    S = rest.shape[1] // B
    steps = S // SWA_ROWS
    kvw = 2 * SWA_KV_WIDTH
    sq_pieces = SWA_WIDTH // MXU_WIDTH
    assert kvw == MXU_WIDTH
    grid_spec = pltpu.PrefetchScalarGridSpec(
        num_scalar_prefetch=1,
        grid=(B, steps),
        in_specs=[
            pl.BlockSpec((sq_pieces, SWA_ROWS, MXU_WIDTH),
                         lambda b, n, s: (PIECE_SQ // sq_pieces, b * steps + n, 0)),
            pl.BlockSpec((1, SWA_ROWS, kvw), lambda b, n, s: (PIECE_SKV, b * steps + n, 0)),
            pl.BlockSpec((2, SWA_HEADS, SWA_BLOCK, SWA_BAND), lambda b, n, s: (0, 0, 0, 0)),
            pl.BlockSpec((1, SWA_WIDTH), lambda b, n, s: (0, 0)),
            pl.BlockSpec((1, SWA_KV_WIDTH), lambda b, n, s: (0, 0)),
            pl.BlockSpec((LANES, LANES), lambda b, n, s: (0, 0)),
        ],
        out_specs=pl.BlockSpec((1, SWA_ROWS, SWA_WIDTH), lambda b, n, s: (b, n, 0)),
        scratch_shapes=[pltpu.VMEM((SWA_BLOCK + SWA_ROWS, kvw), F32)],
    )
    return pl.pallas_call(
        _swa_kernel,
        grid_spec=grid_spec,
        out_shape=jax.ShapeDtypeStruct((B, S, SWA_WIDTH), BF16),
        compiler_params=pltpu.CompilerParams(
            dimension_semantics=("arbitrary", "arbitrary"), vmem_limit_bytes=VMEM_LIMIT),
        name="swa",
    )(sinks, rest, rest, bias_tabs, qgain, kgain, seg)


def _merge_ffn_kernel(x_ref, ydn_ref, yswa_ref, g_ref, wa_ref, wb_ref, wo_ref, gain_ref,
                      wg_ref, wu_ref, wd_ref, o_ref, act_ref):
    per_branch = D_MODEL // MXU_WIDTH
    ga, gb = [_sigmoid(jnp.concatenate(
        [g_ref[br * per_branch + p].astype(F32) for p in range(per_branch)], axis=1))
        for br in range(2)]
    merged = ga * _mm(ydn_ref[...], wa_ref[...]) + gb * _mm(yswa_ref[...], wb_ref[...])
    x1 = x_ref[...] + _mm(merged, wo_ref[...])
    ms = jnp.mean(x1 * x1, axis=-1, keepdims=True)
    h2 = (x1 * lax.rsqrt(ms + EPS) * gain_ref[...]).astype(BF16)
    for c0 in range(0, D_FF, MXU_WIDTH):
        gate = jnp.dot(h2, wg_ref[:, c0:c0 + MXU_WIDTH], preferred_element_type=F32)
        up = jnp.dot(h2, wu_ref[:, c0:c0 + MXU_WIDTH], preferred_element_type=F32)
        act_ref[:, c0:c0 + MXU_WIDTH] = (_silu(gate) * up).astype(BF16)
    o_ref[...] = x1 + jnp.dot(act_ref[...], wd_ref[...], preferred_element_type=F32)


def _merge_ffn(x2, ydn, yswa, graw, wa, wb, wo, gain, wg, wu, wd, tm):
    T = x2.shape[0]
    row = lambda w: pl.BlockSpec((tm, w), lambda i: (i, 0))
    once = lambda a, b: pl.BlockSpec((a, b), lambda i: (0, 0), pipeline_mode=pl.Buffered(1))
    return pl.pallas_call(
        _merge_ffn_kernel,
        grid=(T // tm,),
        in_specs=[row(D_MODEL), row(DN_WIDTH), row(SWA_WIDTH),
                  pl.BlockSpec((PIECE_SQ - PIECE_GATES, tm, MXU_WIDTH), lambda i: (0, i, 0)),
                  once(DN_WIDTH, D_MODEL), once(SWA_WIDTH, D_MODEL), once(D_MODEL, D_MODEL),
                  once(1, D_MODEL),
                  once(D_MODEL, D_FF), once(D_MODEL, D_FF), once(D_FF, D_MODEL)],
        out_specs=row(D_MODEL),
        out_shape=jax.ShapeDtypeStruct((T, D_MODEL), F32),
        scratch_shapes=[pltpu.VMEM((tm, D_FF), BF16)],
        compiler_params=pltpu.CompilerParams(
            dimension_semantics=("arbitrary",), vmem_limit_bytes=VMEM_LIMIT),
        name="merge_ffn",
    )(x2, ydn, yswa, graw, wa, wb, wo, gain, wg, wu, wd)


def _t5_bucket_table():
    qi = jnp.arange(SWA_BLOCK)[:, None]
    kj = jnp.arange(SWA_BAND)[None, :]
    dist = SWA_BLOCK + qi - kj
    in_window = (dist >= 0) & (dist < WINDOW)
    n = jnp.maximum(dist, 0)
    max_exact = REL_BUCKETS // 2
    nf = jnp.maximum(n, 1).astype(F32)
    large = max_exact + (jnp.log(nf / max_exact) / math.log(REL_MAX_DIST / max_exact)
                         * (REL_BUCKETS - max_exact)).astype(jnp.int32)
    large = jnp.minimum(large, REL_BUCKETS - 1)
    return jnp.where(n < max_exact, n, large), in_window


def _segment_mean_matrix(width, seg):
    idx = np.arange(width) // seg
    return jnp.asarray((idx[:, None] == idx[None, :]).astype(np.float32) / seg, dtype=BF16)


def kernel(x, attn_norm, w_in, dn_conv, dn_a_log, dn_dt_bias, dn_out_norm, swa_q_norm, swa_k_norm, swa_sinks, rel_bias, w_branch_dn, w_branch_swa, w_out, ffn_norm, w_gate, w_up, w_down):
    B, S, D = x.shape
    T = B * S
    depth = w_in.shape[0]
    bucket, in_window = _t5_bucket_table()
    seg = _segment_mean_matrix(LANES, SWA_HEAD_DIM)
    rel = rel_bias.astype(F32)
    bias_tab = sum(jnp.where(bucket[None] == b, rel[b][:, None, None], 0.0)
                   for b in range(REL_BUCKETS))
    bias_tab = jnp.where(in_window[None], bias_tab, -jnp.inf)
    has_prev = jnp.arange(SWA_BAND)[None, None, :] >= SWA_BLOCK
    bias_tabs = jnp.stack([jnp.where(has_prev, bias_tab, -jnp.inf), bias_tab])

    x2 = x.reshape(T, D)
    for l in range(depth):
        wl = w_in[l].astype(BF16)
        o_ba = W_HEAD
        o_sq = o_ba + 2 * DN_HEADS
        o_g = o_sq + SWA_WIDTH + 2 * SWA_KV_WIDTH
        w_rest = jnp.concatenate([wl[:, o_g:], wl[:, o_sq:o_g]], axis=1)
        w_rest = w_rest.reshape(D, REST_PIECES, MXU_WIDTH).transpose(1, 0, 2)
        w_ba = jnp.pad(wl[:, o_ba:o_sq], ((0, 0), (0, LANES - 2 * DN_HEADS)))
        convw = dn_conv[l].astype(F32).reshape(DN_CONV, QKV_BLOCKS, DN_HEAD_DIM).transpose(1, 0, 2)

        qkv, z, rest, ba = _in_proj(x2, attn_norm[l][None, :], wl, w_rest, w_ba, tm=512)

        pad_lo = jnp.zeros((DN_HEADS,), F32)
        pad_hi = jnp.zeros((LANES - 2 * DN_HEADS,), F32)
        alog_lane = jnp.concatenate([pad_lo, dn_a_log[l].astype(F32), pad_hi])[None, :]
        dtb_lane = jnp.concatenate([pad_lo, dn_dt_bias[l].astype(F32), pad_hi])[None, :]
        y_dn = _deltanet(qkv, z.reshape(B, S, -1), ba.reshape(B, S, -1), convw,
                         alog_lane, dtb_lane, dn_out_norm[l][None, :])

        qgain = jnp.tile(swa_q_norm[l].astype(F32), SWA_HEADS)[None, :] * (SWA_HEAD_DIM ** -0.5)
        kgain = jnp.tile(swa_k_norm[l].astype(F32), SWA_KV_HEADS)[None, :]
        y_swa = _swa(rest, B, swa_sinks[l].astype(F32), bias_tabs, qgain, kgain, seg)

        x2 = _merge_ffn(x2, y_dn.reshape(T, -1), y_swa.reshape(T, -1), rest,
                        w_branch_dn[l].astype(BF16), w_branch_swa[l].astype(BF16),
                        w_out[l].astype(BF16), ffn_norm[l][None, :],
                        w_gate[l].astype(BF16), w_up[l].astype(BF16), w_down[l].astype(BF16),
                        tm=512)
    return x2.reshape(B, S, D)
```

```python
import functools
import math

import numpy as np
import jax
import jax.numpy as jnp
from jax import lax
from jax.experimental import pallas as pl
from jax.experimental.pallas import tpu as pltpu

D_MODEL = 1024
DN_HEADS = 4
DN_HEAD_DIM = 128
DN_WIDTH = DN_HEADS * DN_HEAD_DIM
DN_QKV_WIDTH = 3 * DN_WIDTH
DN_CONV = 4
DN_CHUNK = 64
SWA_HEADS = 8
SWA_KV_HEADS = 2
SWA_HEAD_DIM = 64
SWA_WIDTH = SWA_HEADS * SWA_HEAD_DIM
SWA_KV_WIDTH = SWA_KV_HEADS * SWA_HEAD_DIM
WINDOW = 128
SWA_BLOCK = 128
REL_BUCKETS = 32
REL_MAX_DIST = 128
D_FF = 2816
EPS = 1e-6

LANES = 128
SUBLANES = 8
MXU_WIDTH = 256
VMEM_LIMIT = 56 * 1024 * 1024

W_HEAD = DN_QKV_WIDTH + DN_WIDTH
W_BA = 2 * DN_HEADS
W_SWA = SWA_WIDTH + 2 * SWA_KV_WIDTH
W_GATES = 2 * D_MODEL
D_IN = W_HEAD + W_BA + W_SWA + W_GATES
SWA_PIECES = W_SWA // MXU_WIDTH
GATE_PIECES = W_GATES // MXU_WIDTH
QKV_BLOCKS = DN_QKV_WIDTH // DN_HEAD_DIM

F32 = jnp.float32
BF16 = jnp.bfloat16


def _mm(a, b, dims=(((1,), (0,)), ((), ()))):
    return lax.dot_general(a.astype(BF16), b.astype(BF16), dims, preferred_element_type=F32)


_NT = (((1,), (1,)), ((), ()))
_TN = (((0,), (0,)), ((), ()))


def _sigmoid(x):
    return 0.5 * jnp.tanh(0.5 * x) + 0.5


def _silu(x):
    h = 0.5 * x
    return h * jnp.tanh(h) + h


CONV_TAIL = SUBLANES
CONV_ROWS = 128


W_CAST_ROWS = 256


def _in_proj_kernel(x_ref, gain_ref, wt_ref, qkv_ref, z_ref, swa_ref, gates_ref, ba_ref,
                    w_head, w_ba, w_rest):
    @pl.when(pl.program_id(0) == 0)
    def _():
        for r0 in range(0, W_HEAD, W_CAST_ROWS):
            w_head[r0:r0 + W_CAST_ROWS, :] = wt_ref[r0:r0 + W_CAST_ROWS, :].astype(BF16)
        w_ba[...] = jnp.zeros_like(w_ba)
        w_ba[0:W_BA, :] = wt_ref[W_HEAD:W_HEAD + W_BA, :].astype(BF16)
        base = W_HEAD + W_BA
        for r0 in range(0, W_SWA + W_GATES, W_CAST_ROWS):
            w_rest[r0:r0 + W_CAST_ROWS, :] = wt_ref[base + r0:base + r0 + W_CAST_ROWS, :].astype(BF16)

    x = x_ref[...]
    ms = jnp.mean(x * x, axis=-1, keepdims=True)
    h = (x * lax.rsqrt(ms + EPS) * gain_ref[...]).astype(BF16)
    proj = lambda w: lax.dot_general(h, w, _NT, preferred_element_type=F32)
    raw = proj(w_head[0:DN_QKV_WIDTH, :])
    for c in range(QKV_BLOCKS):
        qkv_ref[c] = raw[:, c * DN_HEAD_DIM:(c + 1) * DN_HEAD_DIM]
    z_ref[...] = proj(w_head[DN_QKV_WIDTH:W_HEAD, :])
    for p in range(SWA_PIECES):
        swa_ref[p] = proj(w_rest[p * MXU_WIDTH:(p + 1) * MXU_WIDTH, :]).astype(swa_ref.dtype)
    for p in range(GATE_PIECES):
        r0 = W_SWA + p * MXU_WIDTH
        gates_ref[p] = proj(w_rest[r0:r0 + MXU_WIDTH, :]).astype(gates_ref.dtype)
    ba_ref[...] = proj(w_ba[...])


def _in_proj(x2, gain, w_t, tm):
    T = x2.shape[0]
    return pl.pallas_call(
        _in_proj_kernel,
        grid=(T // tm,),
        in_specs=[
            pl.BlockSpec((tm, D_MODEL), lambda i: (i, 0)),
            pl.BlockSpec((1, D_MODEL), lambda i: (0, 0)),
            pl.BlockSpec((D_IN, D_MODEL), lambda i: (0, 0), pipeline_mode=pl.Buffered(1)),
        ],
        out_specs=(
            pl.BlockSpec((QKV_BLOCKS, tm, DN_HEAD_DIM), lambda i: (0, i, 0)),
            pl.BlockSpec((tm, DN_WIDTH), lambda i: (i, 0)),
            pl.BlockSpec((SWA_PIECES, tm, MXU_WIDTH), lambda i: (0, i, 0)),
            pl.BlockSpec((GATE_PIECES, tm, MXU_WIDTH), lambda i: (0, i, 0)),
            pl.BlockSpec((tm, LANES), lambda i: (i, 0)),
        ),
        out_shape=(
            jax.ShapeDtypeStruct((QKV_BLOCKS, T, DN_HEAD_DIM), F32),
            jax.ShapeDtypeStruct((T, DN_WIDTH), F32),
            jax.ShapeDtypeStruct((SWA_PIECES, T, MXU_WIDTH), BF16),
            jax.ShapeDtypeStruct((GATE_PIECES, T, MXU_WIDTH), BF16),
            jax.ShapeDtypeStruct((T, LANES), F32),
        ),
        scratch_shapes=[pltpu.VMEM((W_HEAD, D_MODEL), BF16),
                        pltpu.VMEM((LANES, D_MODEL), BF16),
                        pltpu.VMEM((W_SWA + W_GATES, D_MODEL), BF16)],
        compiler_params=pltpu.CompilerParams(
            dimension_semantics=("arbitrary",), vmem_limit_bytes=VMEM_LIMIT),
        name="in_proj",
    )(x2, gain, w_t)


DN_TILE = 128
DN_NCH = DN_TILE // DN_CHUNK
DN_LOG_CHUNK = int(math.log2(DN_CHUNK))
DN_STREAMS = 4


def _deltanet_kernel(raw_ref, raw_next_ref, z_ref, ba_ref, convw_ref, alog_ref, dtb_ref, gain_ref,
                     o_ref, state, xbuf, qkv_ref, qkv_next_ref):
    t = pl.program_id(1)
    scale = DN_HEAD_DIM ** -0.5
    streams = range(DN_STREAMS)

    def preprocess(s, dst_ref):
        for c in range(QKV_BLOCKS):
            w = convw_ref[c]
            for r0 in range(0, DN_TILE, CONV_ROWS):
                acc = w[DN_CONV - 1:DN_CONV, :] * xbuf[s, c, CONV_TAIL + r0:CONV_TAIL + r0 + CONV_ROWS, :]
                for j in range(DN_CONV - 1):
                    off = CONV_TAIL - (DN_CONV - 1) + j + r0
                    acc = acc + w[j:j + 1, :] * xbuf[s, c, off:off + CONV_ROWS, :]
                act = _silu(acc)
                if c < 2 * DN_HEADS:
                    norm = lax.rsqrt(jnp.sum(act * act, axis=-1, keepdims=True) + EPS)
                    act = act * (norm * scale if c < DN_HEADS else norm)
                dst_ref[s, c, r0:r0 + CONV_ROWS, :] = act.astype(BF16)

    @pl.when(t == 0)
    def _():
        state[...] = jnp.zeros_like(state)
        for s in streams:
            xbuf[s, :, 0:CONV_TAIL, :] = jnp.zeros((QKV_BLOCKS, CONV_TAIL, DN_HEAD_DIM), F32)
            xbuf[s, :, CONV_TAIL:, :] = raw_ref[:, s]
            preprocess(s, qkv_ref)

    for s in streams:
        xbuf[s, :, 0:CONV_TAIL, :] = raw_ref[:, s, DN_TILE - CONV_TAIL:DN_TILE, :]
        xbuf[s, :, CONV_TAIL:, :] = raw_next_ref[:, s]
        preprocess(s, qkv_next_ref)

    row = lax.broadcasted_iota(jnp.int32, (DN_TILE, DN_TILE), 0)
    col = lax.broadcasted_iota(jnp.int32, (DN_TILE, DN_TILE), 1)
    same_chunk = (row >> DN_LOG_CHUNK) == (col >> DN_LOG_CHUNK)
    lower_incl = same_chunk & (row >= col)
    strict_lower = same_chunk & (row > col)
    cumsum_mat = jnp.where(lower_incl, 1.0, 0.0).astype(BF16)
    eye = jnp.where(row == col, 1.0, 0.0)

    beta, gc, rev, gc_t = [], [], [], []
    for s in streams:
        ba = ba_ref[s]
        beta.append(_sigmoid(ba))
        xs = ba + dtb_ref[...]
        softplus = jnp.maximum(xs, 0.0) + jnp.log1p(jnp.exp(-jnp.abs(xs)))
        g = -jnp.exp(alog_ref[...]) * softplus
        g_hi = g.astype(BF16)
        g_lo = (g - g_hi.astype(F32)).astype(BF16)
        gc2 = jnp.dot(cumsum_mat, jnp.concatenate([g_hi, g_lo], axis=1),
                      preferred_element_type=F32)
        gc.append(gc2[:, :LANES] + gc2[:, LANES:])
        g_last = jnp.concatenate(
            [jnp.broadcast_to(gc[s][(c + 1) * DN_CHUNK - 1:(c + 1) * DN_CHUNK, :], (DN_CHUNK, LANES))
             for c in range(DN_NCH)], axis=0)
        rev.append(g_last - gc[s])
        gc_t.append(gc[s].T)

    units = [(s, h) for s in streams for h in range(DN_HEADS)]
    gcol = lambda arr, h: arr[:, DN_HEADS + h:DN_HEADS + h + 1]
    q, k, kb, rhs = {}, {}, {}, {}
    for s, h in units:
        q[s, h] = qkv_ref[s, h].astype(F32)
        k[s, h] = qkv_ref[s, DN_HEADS + h].astype(F32)
        vh = qkv_ref[s, 2 * DN_HEADS + h].astype(F32)
        beta_c = beta[s][:, h:h + 1]
        kb[s, h] = k[s, h] * beta_c
        rhs[s, h] = jnp.concatenate([vh * beta_c, kb[s, h] * jnp.exp(gcol(gc[s], h))], axis=1)
    gram = {p: _mm(jnp.concatenate([kb[p], q[p]], axis=0), k[p], _NT) for p in units}
    qk, n_pow, inv = {}, {}, {}
    for s, h in units:
        gc_r = gc_t[s][DN_HEADS + h:DN_HEADS + h + 1, :]
        decay = jnp.exp(jnp.where(lower_incl, gcol(gc[s], h) - gc_r, -jnp.inf))
        qk[s, h] = gram[s, h][DN_TILE:] * decay
        n_pow[s, h] = -jnp.where(strict_lower, gram[s, h][:DN_TILE] * decay, 0.0)
        inv[s, h] = eye + n_pow[s, h]
    for _ in range(DN_LOG_CHUNK - 1):
        n_pow = {p: _mm(n_pow[p], n_pow[p]) for p in units}
        inv = {p: inv[p] + _mm(inv[p], n_pow[p]) for p in units}
    sol = {p: _mm(inv[p], rhs[p]) for p in units}
    u = {p: sol[p][:, :DN_HEAD_DIM] for p in units}
    wmat = {p: sol[p][:, DN_HEAD_DIM:] for p in units}
    qe = {(s, h): q[s, h] * jnp.exp(gcol(gc[s], h)) for s, h in units}
    kdec = {(s, h): k[s, h] * jnp.exp(gcol(rev[s], h)) for s, h in units}

    st = {(s, h): state[s, h] for s, h in units}
    v_new = {p: [] for p in units}
    o_inter = {p: [] for p in units}
    for c in range(DN_NCH):
        r0, r1 = c * DN_CHUNK, (c + 1) * DN_CHUNK
        ws = {p: _mm(jnp.concatenate([wmat[p][r0:r1], qe[p][r0:r1]], axis=0), st[p]) for p in units}
        for p in units:
            v_new[p].append(u[p][r0:r1] - ws[p][:DN_CHUNK])
            o_inter[p].append(ws[p][DN_CHUNK:])
        upd = {p: _mm(kdec[p][r0:r1], v_new[p][c], _TN) for p in units}
        st = {(s, h): st[s, h] * jnp.exp(gc[s][r1 - 1:r1, DN_HEADS + h:DN_HEADS + h + 1]) + upd[s, h]
              for s, h in units}
    o_intra = {p: _mm(qk[p], jnp.concatenate(v_new[p], axis=0)) for p in units}
    for s, h in units:
        state[s, h] = st[s, h]
        o = jnp.concatenate(o_inter[s, h], axis=0) + o_intra[s, h]
        o = o * lax.rsqrt(jnp.mean(o * o, axis=-1, keepdims=True) + EPS) * gain_ref[...]
        zh = z_ref[s, :, h * DN_HEAD_DIM:(h + 1) * DN_HEAD_DIM]
        o_ref[s, :, h * DN_HEAD_DIM:(h + 1) * DN_HEAD_DIM] = (o * _silu(zh)).astype(o_ref.dtype)
    qkv_ref[...] = qkv_next_ref[...]


def _deltanet(raw, z, ba, convw, alog_lane, dtb_lane, out_gain):
    B, S, _ = z.shape
    tiles = S // DN_TILE
    raw = raw.reshape(QKV_BLOCKS, B, S, DN_HEAD_DIM)
    raw_block = (QKV_BLOCKS, DN_STREAMS, DN_TILE, DN_HEAD_DIM)
    qkv_block = (DN_STREAMS, QKV_BLOCKS, DN_TILE, DN_HEAD_DIM)
    return pl.pallas_call(
        _deltanet_kernel,
        grid=(B // DN_STREAMS, tiles),
        in_specs=[
            pl.BlockSpec(raw_block, lambda b, t: (0, b, t, 0)),
            pl.BlockSpec(raw_block, lambda b, t: (0, b, jnp.minimum(t + 1, tiles - 1), 0)),
            pl.BlockSpec((DN_STREAMS, DN_TILE, DN_WIDTH), lambda b, t: (b, t, 0)),
            pl.BlockSpec((DN_STREAMS, DN_TILE, LANES), lambda b, t: (b, t, 0)),
            pl.BlockSpec((QKV_BLOCKS, DN_CONV, DN_HEAD_DIM), lambda b, t: (0, 0, 0)),
            pl.BlockSpec((1, LANES), lambda b, t: (0, 0)),
            pl.BlockSpec((1, LANES), lambda b, t: (0, 0)),
            pl.BlockSpec((1, DN_HEAD_DIM), lambda b, t: (0, 0)),
        ],
        out_specs=pl.BlockSpec((DN_STREAMS, DN_TILE, DN_WIDTH), lambda b, t: (b, t, 0)),
        out_shape=jax.ShapeDtypeStruct((B, S, DN_WIDTH), BF16),
        scratch_shapes=[
            pltpu.VMEM((DN_STREAMS, DN_HEADS, DN_HEAD_DIM, DN_HEAD_DIM), F32),
            pltpu.VMEM((DN_STREAMS, QKV_BLOCKS, CONV_TAIL + DN_TILE, DN_HEAD_DIM), F32),
            pltpu.VMEM(qkv_block, BF16), pltpu.VMEM(qkv_block, BF16)],
        compiler_params=pltpu.CompilerParams(
            dimension_semantics=("arbitrary", "arbitrary"), vmem_limit_bytes=VMEM_LIMIT),
        name="deltanet",
    )(raw, raw, z, ba, convw, alog_lane, dtb_lane, out_gain)


SWA_BAND = 2 * SWA_BLOCK
SWA_SLOTS = SWA_WIDTH // LANES
SWA_HEADS_PER_SLOT = LANES // SWA_HEAD_DIM


SWA_STEP = 4
SWA_ROWS = SWA_STEP * SWA_BLOCK


def _swa_kernel(sinks_ref, sq_ref, kv_ref, bias_ref, qgain_ref, kgain_ref, seg_ref, o_ref,
                band_ref):
    n = pl.program_id(1)
    kv = kv_ref[0].astype(F32)
    k = kv[:, :SWA_KV_WIDTH]
    slots_per_piece = MXU_WIDTH // LANES
    q_slots = [sq_ref[j // slots_per_piece, :, (j % slots_per_piece) * LANES:
                      (j % slots_per_piece + 1) * LANES].astype(F32) for j in range(SWA_SLOTS)]
    parts = q_slots + [k]
    ms = _mm(jnp.concatenate([p * p for p in parts], axis=0), seg_ref[...])
    qn = [parts[j] * lax.rsqrt(ms[j * SWA_ROWS:(j + 1) * SWA_ROWS] + EPS)
          * qgain_ref[:, j * LANES:(j + 1) * LANES] for j in range(SWA_SLOTS)]
    kn = k * lax.rsqrt(ms[SWA_SLOTS * SWA_ROWS:] + EPS) * kgain_ref[...]

    @pl.when(n == 0)
    def _():
        band_ref[0:SWA_BLOCK, :] = jnp.zeros((SWA_BLOCK, 2 * SWA_KV_WIDTH), F32)

    @pl.when(n > 0)
    def _():
        band_ref[0:SWA_BLOCK, :] = band_ref[SWA_ROWS:SWA_ROWS + SWA_BLOCK, :]

    band_ref[SWA_BLOCK:, :SWA_KV_WIDTH] = kn
    band_ref[SWA_BLOCK:, SWA_KV_WIDTH:] = kv[:, SWA_KV_WIDTH:]
    k = band_ref[:, :SWA_KV_WIDTH]
    v = band_ref[:, SWA_KV_WIDTH:]

    lane = lax.broadcasted_iota(jnp.int32, (SWA_BLOCK + SWA_ROWS, LANES), 1)
    low = lane < SWA_HEAD_DIM
    k_rot = pltpu.roll(k, SWA_HEAD_DIM, axis=1)
    v_rot = pltpu.roll(v, SWA_HEAD_DIM, axis=1)
    k_lo = (jnp.where(low, k, 0.0).astype(BF16), jnp.where(low, k_rot, 0.0).astype(BF16))
    k_hi = (jnp.where(low, 0.0, k_rot).astype(BF16), jnp.where(low, 0.0, k).astype(BF16))
    v_lo = (jnp.where(low, v, 0.0).astype(BF16), jnp.where(low, v_rot, 0.0).astype(BF16))
    v_hi = (jnp.where(low, 0.0, v_rot).astype(BF16), jnp.where(low, 0.0, v).astype(BF16))

    out_low = lax.broadcasted_iota(jnp.int32, (SWA_BLOCK, LANES), 1) < SWA_HEAD_DIM

    slots_per_kv = SWA_SLOTS // SWA_KV_HEADS
    units = [(u, j) for u in range(SWA_STEP) for j in range(SWA_SLOTS)]
    band_of = lambda arr, u: arr[u * SWA_BLOCK:u * SWA_BLOCK + SWA_BAND]
    k_bd = {(u, kh): jnp.concatenate([band_of(k_lo[kh], u), band_of(k_hi[kh], u)], axis=0)
            for u in range(SWA_STEP) for kh in range(SWA_KV_HEADS)}
    v_bd = {(u, kh): jnp.concatenate([band_of(v_lo[kh], u), band_of(v_hi[kh], u)], axis=0)
            for u in range(SWA_STEP) for kh in range(SWA_KV_HEADS)}
    logits = {(u, j): _mm(qn[j][u * SWA_BLOCK:(u + 1) * SWA_BLOCK], k_bd[u, j // slots_per_kv], _NT)
              for u, j in units}
    first_tab = jnp.minimum(n, 1)
    probs = {}
    inv_den = {}
    for u, j in units:
        for r in range(SWA_HEADS_PER_SLOT):
            hd = j * SWA_HEADS_PER_SLOT + r
            bias = bias_ref[first_tab, hd] if u == 0 else bias_ref[1, hd]
            lg = logits[u, j][:, r * SWA_BAND:(r + 1) * SWA_BAND] + bias
            sink = sinks_ref[hd]
            m = jnp.maximum(jnp.max(lg, axis=-1, keepdims=True), sink)
            p = jnp.exp(lg - m)
            den = jnp.sum(p, axis=-1, keepdims=True) + jnp.exp(sink - m)
            probs[u, j, r] = p.astype(BF16)
            inv_den[u, j, r] = 1.0 / den
    outs = {(u, j): _mm(jnp.concatenate([probs[u, j, r] for r in range(SWA_HEADS_PER_SLOT)],
                                        axis=1), v_bd[u, j // slots_per_kv])
            for u, j in units}
    for u, j in units:
        o_ref[0, u * SWA_BLOCK:(u + 1) * SWA_BLOCK, j * LANES:(j + 1) * LANES] = (
            outs[u, j] * jnp.where(out_low, inv_den[u, j, 0], inv_den[u, j, 1])).astype(o_ref.dtype)


def _swa(rest, batch, sinks, bias_tabs, qgain, kgain, seg):
    B = batch
    S = rest.shape[1] // B
    steps = S // SWA_ROWS
    kvw = 2 * SWA_KV_WIDTH
    sq_pieces = SWA_WIDTH // MXU_WIDTH
    assert kvw == MXU_WIDTH
    grid_spec = pltpu.PrefetchScalarGridSpec(
        num_scalar_prefetch=1,
        grid=(B, steps),
        in_specs=[
            pl.BlockSpec((sq_pieces, SWA_ROWS, MXU_WIDTH),
                         lambda b, n, s: (0, b * steps + n, 0)),
            pl.BlockSpec((1, SWA_ROWS, kvw), lambda b, n, s: (sq_pieces, b * steps + n, 0)),
            pl.BlockSpec((2, SWA_HEADS, SWA_BLOCK, SWA_BAND), lambda b, n, s: (0, 0, 0, 0)),
            pl.BlockSpec((1, SWA_WIDTH), lambda b, n, s: (0, 0)),
            pl.BlockSpec((1, SWA_KV_WIDTH), lambda b, n, s: (0, 0)),
            pl.BlockSpec((LANES, LANES), lambda b, n, s: (0, 0)),
        ],
        out_specs=pl.BlockSpec((1, SWA_ROWS, SWA_WIDTH), lambda b, n, s: (b, n, 0)),
        scratch_shapes=[pltpu.VMEM((SWA_BLOCK + SWA_ROWS, kvw), F32)],
    )
    return pl.pallas_call(
        _swa_kernel,
        grid_spec=grid_spec,
        out_shape=jax.ShapeDtypeStruct((B, S, SWA_WIDTH), BF16),
        compiler_params=pltpu.CompilerParams(
            dimension_semantics=("arbitrary", "arbitrary"), vmem_limit_bytes=VMEM_LIMIT),
        name="swa",
    )(sinks, rest, rest, bias_tabs, qgain, kgain, seg)


def _merge_ffn_kernel(x_ref, ydn_ref, yswa_ref, g_ref, wa_ref, wb_ref, wo_ref, gain_ref,
                      wg_ref, wu_ref, wd_ref, o_ref, act_ref):
    per_branch = D_MODEL // MXU_WIDTH
    ga, gb = [_sigmoid(jnp.concatenate(
        [g_ref[br * per_branch + p].astype(F32) for p in range(per_branch)], axis=1))
        for br in range(2)]
    merged = ga * _mm(ydn_ref[...], wa_ref[...]) + gb * _mm(yswa_ref[...], wb_ref[...])
    x1 = x_ref[...] + _mm(merged, wo_ref[...])
    ms = jnp.mean(x1 * x1, axis=-1, keepdims=True)
    h2 = (x1 * lax.rsqrt(ms + EPS) * gain_ref[...]).astype(BF16)
    for c0 in range(0, D_FF, MXU_WIDTH):
        gate = jnp.dot(h2, wg_ref[:, c0:c0 + MXU_WIDTH], preferred_element_type=F32)
        up = jnp.dot(h2, wu_ref[:, c0:c0 + MXU_WIDTH], preferred_element_type=F32)
        act_ref[:, c0:c0 + MXU_WIDTH] = (_silu(gate) * up).astype(BF16)
    o_ref[...] = x1 + jnp.dot(act_ref[...], wd_ref[...], preferred_element_type=F32)


def _merge_ffn(x2, ydn, yswa, graw, wa, wb, wo, gain, wg, wu, wd, tm):
    T = x2.shape[0]
    row = lambda w: pl.BlockSpec((tm, w), lambda i: (i, 0))
    once = lambda a, b: pl.BlockSpec((a, b), lambda i: (0, 0), pipeline_mode=pl.Buffered(1))
    return pl.pallas_call(
        _merge_ffn_kernel,
        grid=(T // tm,),
        in_specs=[row(D_MODEL), row(DN_WIDTH), row(SWA_WIDTH),
                  pl.BlockSpec((GATE_PIECES, tm, MXU_WIDTH), lambda i: (0, i, 0)),
                  once(DN_WIDTH, D_MODEL), once(SWA_WIDTH, D_MODEL), once(D_MODEL, D_MODEL),
                  once(1, D_MODEL),
                  once(D_MODEL, D_FF), once(D_MODEL, D_FF), once(D_FF, D_MODEL)],
        out_specs=row(D_MODEL),
        out_shape=jax.ShapeDtypeStruct((T, D_MODEL), F32),
        scratch_shapes=[pltpu.VMEM((tm, D_FF), BF16)],
        compiler_params=pltpu.CompilerParams(
            dimension_semantics=("arbitrary",), vmem_limit_bytes=VMEM_LIMIT),
        name="merge_ffn",
    )(x2, ydn, yswa, graw, wa, wb, wo, gain, wg, wu, wd)


def _t5_bucket_table():
    qi = jnp.arange(SWA_BLOCK)[:, None]
    kj = jnp.arange(SWA_BAND)[None, :]
    dist = SWA_BLOCK + qi - kj
    in_window = (dist >= 0) & (dist < WINDOW)
    n = jnp.maximum(dist, 0)
    max_exact = REL_BUCKETS // 2
    nf = jnp.maximum(n, 1).astype(F32)
    large = max_exact + (jnp.log(nf / max_exact) / math.log(REL_MAX_DIST / max_exact)
                         * (REL_BUCKETS - max_exact)).astype(jnp.int32)
    large = jnp.minimum(large, REL_BUCKETS - 1)
    return jnp.where(n < max_exact, n, large), in_window


def _segment_mean_matrix(width, seg):
    idx = np.arange(width) // seg
    return jnp.asarray((idx[:, None] == idx[None, :]).astype(np.float32) / seg, dtype=BF16)


def kernel(x, attn_norm, w_in, dn_conv, dn_a_log, dn_dt_bias, dn_out_norm, swa_q_norm, swa_k_norm, swa_sinks, rel_bias, w_branch_dn, w_branch_swa, w_out, ffn_norm, w_gate, w_up, w_down):
    B, S, D = x.shape
    T = B * S
    depth = w_in.shape[0]
    bucket, in_window = _t5_bucket_table()
    seg = _segment_mean_matrix(LANES, SWA_HEAD_DIM)
    rel = rel_bias.astype(F32)
    bias_tab = sum(jnp.where(bucket[None] == b, rel[b][:, None, None], 0.0)
                   for b in range(REL_BUCKETS))
    bias_tab = jnp.where(in_window[None], bias_tab, -jnp.inf)
    has_prev = jnp.arange(SWA_BAND)[None, None, :] >= SWA_BLOCK
    bias_tabs = jnp.stack([jnp.where(has_prev, bias_tab, -jnp.inf), bias_tab])

    x2 = x.reshape(T, D)
    for l in range(depth):
        convw = dn_conv[l].astype(F32).reshape(DN_CONV, QKV_BLOCKS, DN_HEAD_DIM).transpose(1, 0, 2)

        qkv, z, sqkv, gates, ba = _in_proj(x2, attn_norm[l][None, :],
                                           jnp.swapaxes(w_in[l], 0, 1).astype(F32), tm=512)

        pad_lo = jnp.zeros((DN_HEADS,), F32)
        pad_hi = jnp.zeros((LANES - 2 * DN_HEADS,), F32)
        alog_lane = jnp.concatenate([pad_lo, dn_a_log[l].astype(F32), pad_hi])[None, :]
        dtb_lane = jnp.concatenate([pad_lo, dn_dt_bias[l].astype(F32), pad_hi])[None, :]
        y_dn = _deltanet(qkv, z.reshape(B, S, -1), ba.reshape(B, S, -1), convw,
                         alog_lane, dtb_lane, dn_out_norm[l][None, :])

        qgain = jnp.tile(swa_q_norm[l].astype(F32), SWA_HEADS)[None, :] * (SWA_HEAD_DIM ** -0.5)
        kgain = jnp.tile(swa_k_norm[l].astype(F32), SWA_KV_HEADS)[None, :]
        y_swa = _swa(sqkv, B, swa_sinks[l].astype(F32), bias_tabs, qgain, kgain, seg)

        x2 = _merge_ffn(x2, y_dn.reshape(T, -1), y_swa.reshape(T, -1), gates,
                        w_branch_dn[l].astype(BF16), w_branch_swa[l].astype(BF16),
                        w_out[l].astype(BF16), ffn_norm[l][None, :],
                        w_gate[l].astype(BF16), w_up[l].astype(BF16), w_down[l].astype(BF16),
                        tm=512)
    return x2.reshape(B, S, D)
```

```python
import functools
import math

import numpy as np
import jax
import jax.numpy as jnp
from jax import lax
from jax.experimental import pallas as pl
from jax.experimental.pallas import tpu as pltpu

D_MODEL = 1024
DN_HEADS = 4
DN_HEAD_DIM = 128
DN_WIDTH = DN_HEADS * DN_HEAD_DIM
DN_QKV_WIDTH = 3 * DN_WIDTH
DN_CONV = 4
DN_CHUNK = 64
SWA_HEADS = 8
SWA_KV_HEADS = 2
SWA_HEAD_DIM = 64
SWA_WIDTH = SWA_HEADS * SWA_HEAD_DIM
SWA_KV_WIDTH = SWA_KV_HEADS * SWA_HEAD_DIM
WINDOW = 128
SWA_BLOCK = 128
REL_BUCKETS = 32
REL_MAX_DIST = 128
D_FF = 2816
EPS = 1e-6

LANES = 128
SUBLANES = 8
MXU_WIDTH = 256
VMEM_LIMIT = 56 * 1024 * 1024

W_HEAD = DN_QKV_WIDTH + DN_WIDTH
W_BA = 2 * DN_HEADS
W_SWA = SWA_WIDTH + 2 * SWA_KV_WIDTH
W_GATES = 2 * D_MODEL
D_IN = W_HEAD + W_BA + W_SWA + W_GATES
SWA_PIECES = W_SWA // MXU_WIDTH
GATE_PIECES = W_GATES // MXU_WIDTH
QKV_BLOCKS = DN_QKV_WIDTH // DN_HEAD_DIM

F32 = jnp.float32
BF16 = jnp.bfloat16


def _mm(a, b, dims=(((1,), (0,)), ((), ()))):
    return lax.dot_general(a.astype(BF16), b.astype(BF16), dims, preferred_element_type=F32)


_NT = (((1,), (1,)), ((), ()))
_TN = (((0,), (0,)), ((), ()))


def _sigmoid(x):
    return 0.5 * jnp.tanh(0.5 * x) + 0.5


def _silu(x):
    h = 0.5 * x
    return h * jnp.tanh(h) + h


CONV_TAIL = SUBLANES
CONV_ROWS = 128


W_CAST_ROWS = 256


def _in_proj_kernel(x_ref, gain_ref, wt_ref, *refs, n_cast):
    cast_src = refs[:n_cast]
    qkv_ref, z_ref, swa_ref, gates_ref, ba_ref = refs[n_cast:n_cast + 5]
    cast_dst = refs[n_cast + 5:2 * n_cast + 5]
    w_head, w_ba, w_rest = refs[2 * n_cast + 5:]
    for src, dst in zip(cast_src, cast_dst):
        dst[...] = src[...].astype(BF16)

    @pl.when(pl.program_id(0) == 0)
    def _():
        for r0 in range(0, W_HEAD, W_CAST_ROWS):
            w_head[r0:r0 + W_CAST_ROWS, :] = wt_ref[r0:r0 + W_CAST_ROWS, :].astype(BF16)
        w_ba[...] = jnp.zeros_like(w_ba)
        w_ba[0:W_BA, :] = wt_ref[W_HEAD:W_HEAD + W_BA, :].astype(BF16)
        base = W_HEAD + W_BA
        for r0 in range(0, W_SWA + W_GATES, W_CAST_ROWS):
            w_rest[r0:r0 + W_CAST_ROWS, :] = wt_ref[base + r0:base + r0 + W_CAST_ROWS, :].astype(BF16)

    x = x_ref[...]
    ms = jnp.mean(x * x, axis=-1, keepdims=True)
    h = (x * lax.rsqrt(ms + EPS) * gain_ref[...]).astype(BF16)
    proj = lambda w: lax.dot_general(h, w, _NT, preferred_element_type=F32)
    raw = proj(w_head[0:DN_QKV_WIDTH, :])
    for c in range(QKV_BLOCKS):
        qkv_ref[c] = raw[:, c * DN_HEAD_DIM:(c + 1) * DN_HEAD_DIM]
    z_ref[...] = proj(w_head[DN_QKV_WIDTH:W_HEAD, :])
    for p in range(SWA_PIECES):
        swa_ref[p] = proj(w_rest[p * MXU_WIDTH:(p + 1) * MXU_WIDTH, :]).astype(swa_ref.dtype)
    for p in range(GATE_PIECES):
        r0 = W_SWA + p * MXU_WIDTH
        gates_ref[p] = proj(w_rest[r0:r0 + MXU_WIDTH, :]).astype(gates_ref.dtype)
    ba_ref[...] = proj(w_ba[...])


BF16_ROWS = 2 * SUBLANES


def _cast_block_rows(rows, steps):
    for blocks in range(steps, 0, -1):
        if steps % blocks == 0 and rows % (blocks * BF16_ROWS) == 0:
            return rows // blocks
    raise ValueError(f"no bf16-tile-aligned split of {rows} rows over {steps} steps")


def _in_proj(x2, gain, w_t, later_weights, tm):
    T = x2.shape[0]
    steps = T // tm
    cast_specs = []
    for w in later_weights:
        rows = _cast_block_rows(w.shape[0], steps)
        repeat = steps // (w.shape[0] // rows)
        cast_specs.append(pl.BlockSpec((rows, w.shape[1]), lambda i, repeat=repeat: (i // repeat, 0)))
    outs = pl.pallas_call(
        functools.partial(_in_proj_kernel, n_cast=len(later_weights)),
        grid=(steps,),
        in_specs=[
            pl.BlockSpec((tm, D_MODEL), lambda i: (i, 0)),
            pl.BlockSpec((1, D_MODEL), lambda i: (0, 0)),
            pl.BlockSpec((D_IN, D_MODEL), lambda i: (0, 0), pipeline_mode=pl.Buffered(1)),
        ] + cast_specs,
        out_specs=[
            pl.BlockSpec((QKV_BLOCKS, tm, DN_HEAD_DIM), lambda i: (0, i, 0)),
            pl.BlockSpec((tm, DN_WIDTH), lambda i: (i, 0)),
            pl.BlockSpec((SWA_PIECES, tm, MXU_WIDTH), lambda i: (0, i, 0)),
            pl.BlockSpec((GATE_PIECES, tm, MXU_WIDTH), lambda i: (0, i, 0)),
            pl.BlockSpec((tm, LANES), lambda i: (i, 0)),
        ] + cast_specs,
        out_shape=[
            jax.ShapeDtypeStruct((QKV_BLOCKS, T, DN_HEAD_DIM), F32),
            jax.ShapeDtypeStruct((T, DN_WIDTH), F32),
            jax.ShapeDtypeStruct((SWA_PIECES, T, MXU_WIDTH), BF16),
            jax.ShapeDtypeStruct((GATE_PIECES, T, MXU_WIDTH), BF16),
            jax.ShapeDtypeStruct((T, LANES), F32),
        ] + [jax.ShapeDtypeStruct(w.shape, BF16) for w in later_weights],
        scratch_shapes=[pltpu.VMEM((W_HEAD, D_MODEL), BF16),
                        pltpu.VMEM((LANES, D_MODEL), BF16),
                        pltpu.VMEM((W_SWA + W_GATES, D_MODEL), BF16)],
        compiler_params=pltpu.CompilerParams(
            dimension_semantics=("arbitrary",), vmem_limit_bytes=VMEM_LIMIT),
        name="in_proj",
    )(x2, gain, w_t, *later_weights)
    return outs[:5], outs[5:]


DN_TILE = 128
DN_NCH = DN_TILE // DN_CHUNK
DN_LOG_CHUNK = int(math.log2(DN_CHUNK))
DN_STREAMS = 4


def _deltanet_kernel(raw_ref, raw_next_ref, z_ref, ba_ref, convw_ref, alog_ref, dtb_ref, gain_ref,
                     o_ref, state, xbuf, qkv_ref, qkv_next_ref):
    t = pl.program_id(1)
    scale = DN_HEAD_DIM ** -0.5
    streams = range(DN_STREAMS)

    def preprocess(s, dst_ref):
        for c in range(QKV_BLOCKS):
            w = convw_ref[c]
            for r0 in range(0, DN_TILE, CONV_ROWS):
                acc = w[DN_CONV - 1:DN_CONV, :] * xbuf[s, c, CONV_TAIL + r0:CONV_TAIL + r0 + CONV_ROWS, :]
                for j in range(DN_CONV - 1):
                    off = CONV_TAIL - (DN_CONV - 1) + j + r0
                    acc = acc + w[j:j + 1, :] * xbuf[s, c, off:off + CONV_ROWS, :]
                act = _silu(acc)
                if c < 2 * DN_HEADS:
                    norm = lax.rsqrt(jnp.sum(act * act, axis=-1, keepdims=True) + EPS)
                    act = act * (norm * scale if c < DN_HEADS else norm)
                dst_ref[s, c, r0:r0 + CONV_ROWS, :] = act.astype(BF16)

    @pl.when(t == 0)
    def _():
        state[...] = jnp.zeros_like(state)
        for s in streams:
            xbuf[s, :, 0:CONV_TAIL, :] = jnp.zeros((QKV_BLOCKS, CONV_TAIL, DN_HEAD_DIM), F32)
            xbuf[s, :, CONV_TAIL:, :] = raw_ref[:, s]
            preprocess(s, qkv_ref)

    for s in streams:
        xbuf[s, :, 0:CONV_TAIL, :] = raw_ref[:, s, DN_TILE - CONV_TAIL:DN_TILE, :]
        xbuf[s, :, CONV_TAIL:, :] = raw_next_ref[:, s]
        preprocess(s, qkv_next_ref)

    row = lax.broadcasted_iota(jnp.int32, (DN_TILE, DN_TILE), 0)
    col = lax.broadcasted_iota(jnp.int32, (DN_TILE, DN_TILE), 1)
    same_chunk = (row >> DN_LOG_CHUNK) == (col >> DN_LOG_CHUNK)
    lower_incl = same_chunk & (row >= col)
    strict_lower = same_chunk & (row > col)
    cumsum_mat = jnp.where(lower_incl, 1.0, 0.0).astype(BF16)
    eye = jnp.where(row == col, 1.0, 0.0)

    beta, gc, rev, gc_t = [], [], [], []
    for s in streams:
        ba = ba_ref[s]
        beta.append(_sigmoid(ba))
        xs = ba + dtb_ref[...]
        softplus = jnp.maximum(xs, 0.0) + jnp.log1p(jnp.exp(-jnp.abs(xs)))
        g = -jnp.exp(alog_ref[...]) * softplus
        g_hi = g.astype(BF16)
        g_lo = (g - g_hi.astype(F32)).astype(BF16)
        gc2 = jnp.dot(cumsum_mat, jnp.concatenate([g_hi, g_lo], axis=1),
                      preferred_element_type=F32)
        gc.append(gc2[:, :LANES] + gc2[:, LANES:])
        g_last = jnp.concatenate(
            [jnp.broadcast_to(gc[s][(c + 1) * DN_CHUNK - 1:(c + 1) * DN_CHUNK, :], (DN_CHUNK, LANES))
             for c in range(DN_NCH)], axis=0)
        rev.append(g_last - gc[s])
        gc_t.append(gc[s].T)

    units = [(s, h) for s in streams for h in range(DN_HEADS)]
    gcol = lambda arr, h: arr[:, DN_HEADS + h:DN_HEADS + h + 1]
    q, k, kb, rhs = {}, {}, {}, {}
    for s, h in units:
        q[s, h] = qkv_ref[s, h].astype(F32)
        k[s, h] = qkv_ref[s, DN_HEADS + h].astype(F32)
        vh = qkv_ref[s, 2 * DN_HEADS + h].astype(F32)
        beta_c = beta[s][:, h:h + 1]
        kb[s, h] = k[s, h] * beta_c
        rhs[s, h] = jnp.concatenate([vh * beta_c, kb[s, h] * jnp.exp(gcol(gc[s], h))], axis=1)
    gram = {p: _mm(jnp.concatenate([kb[p], q[p]], axis=0), k[p], _NT) for p in units}
    qk, n_pow, inv = {}, {}, {}
    for s, h in units:
        gc_r = gc_t[s][DN_HEADS + h:DN_HEADS + h + 1, :]
        decay = jnp.exp(jnp.where(lower_incl, gcol(gc[s], h) - gc_r, -jnp.inf))
        qk[s, h] = gram[s, h][DN_TILE:] * decay
        n_pow[s, h] = -jnp.where(strict_lower, gram[s, h][:DN_TILE] * decay, 0.0)
        inv[s, h] = eye + n_pow[s, h]
    for _ in range(DN_LOG_CHUNK - 1):
        n_pow = {p: _mm(n_pow[p], n_pow[p]) for p in units}
        inv = {p: inv[p] + _mm(inv[p], n_pow[p]) for p in units}
    sol = {p: _mm(inv[p], rhs[p]) for p in units}
    u = {p: sol[p][:, :DN_HEAD_DIM] for p in units}
    wmat = {p: sol[p][:, DN_HEAD_DIM:] for p in units}
    qe = {(s, h): q[s, h] * jnp.exp(gcol(gc[s], h)) for s, h in units}
    kdec = {(s, h): k[s, h] * jnp.exp(gcol(rev[s], h)) for s, h in units}

    st = {(s, h): state[s, h] for s, h in units}
    v_new = {p: [] for p in units}
    o_inter = {p: [] for p in units}
    for c in range(DN_NCH):
        r0, r1 = c * DN_CHUNK, (c + 1) * DN_CHUNK
        ws = {p: _mm(jnp.concatenate([wmat[p][r0:r1], qe[p][r0:r1]], axis=0), st[p]) for p in units}
        for p in units:
            v_new[p].append(u[p][r0:r1] - ws[p][:DN_CHUNK])
            o_inter[p].append(ws[p][DN_CHUNK:])
        upd = {p: _mm(kdec[p][r0:r1], v_new[p][c], _TN) for p in units}
        st = {(s, h): st[s, h] * jnp.exp(gc[s][r1 - 1:r1, DN_HEADS + h:DN_HEADS + h + 1]) + upd[s, h]
              for s, h in units}
    o_intra = {p: _mm(qk[p], jnp.concatenate(v_new[p], axis=0)) for p in units}
    for s, h in units:
        state[s, h] = st[s, h]
        o = jnp.concatenate(o_inter[s, h], axis=0) + o_intra[s, h]
        o = o * lax.rsqrt(jnp.mean(o * o, axis=-1, keepdims=True) + EPS) * gain_ref[...]
        zh = z_ref[s, :, h * DN_HEAD_DIM:(h + 1) * DN_HEAD_DIM]
        o_ref[s, :, h * DN_HEAD_DIM:(h + 1) * DN_HEAD_DIM] = (o * _silu(zh)).astype(o_ref.dtype)
    qkv_ref[...] = qkv_next_ref[...]


def _deltanet(raw, z, ba, convw, alog_lane, dtb_lane, out_gain):
    B, S, _ = z.shape
    tiles = S // DN_TILE
    raw = raw.reshape(QKV_BLOCKS, B, S, DN_HEAD_DIM)
    raw_block = (QKV_BLOCKS, DN_STREAMS, DN_TILE, DN_HEAD_DIM)
    qkv_block = (DN_STREAMS, QKV_BLOCKS, DN_TILE, DN_HEAD_DIM)
    return pl.pallas_call(
        _deltanet_kernel,
        grid=(B // DN_STREAMS, tiles),
        in_specs=[
            pl.BlockSpec(raw_block, lambda b, t: (0, b, t, 0)),
            pl.BlockSpec(raw_block, lambda b, t: (0, b, jnp.minimum(t + 1, tiles - 1), 0)),
            pl.BlockSpec((DN_STREAMS, DN_TILE, DN_WIDTH), lambda b, t: (b, t, 0)),
            pl.BlockSpec((DN_STREAMS, DN_TILE, LANES), lambda b, t: (b, t, 0)),
            pl.BlockSpec((QKV_BLOCKS, DN_CONV, DN_HEAD_DIM), lambda b, t: (0, 0, 0)),
            pl.BlockSpec((1, LANES), lambda b, t: (0, 0)),
            pl.BlockSpec((1, LANES), lambda b, t: (0, 0)),
            pl.BlockSpec((1, DN_HEAD_DIM), lambda b, t: (0, 0)),
        ],
        out_specs=pl.BlockSpec((DN_STREAMS, DN_TILE, DN_WIDTH), lambda b, t: (b, t, 0)),
        out_shape=jax.ShapeDtypeStruct((B, S, DN_WIDTH), BF16),
        scratch_shapes=[
            pltpu.VMEM((DN_STREAMS, DN_HEADS, DN_HEAD_DIM, DN_HEAD_DIM), F32),
            pltpu.VMEM((DN_STREAMS, QKV_BLOCKS, CONV_TAIL + DN_TILE, DN_HEAD_DIM), F32),
            pltpu.VMEM(qkv_block, BF16), pltpu.VMEM(qkv_block, BF16)],
        compiler_params=pltpu.CompilerParams(
            dimension_semantics=("arbitrary", "arbitrary"), vmem_limit_bytes=VMEM_LIMIT),
        name="deltanet",
    )(raw, raw, z, ba, convw, alog_lane, dtb_lane, out_gain)


SWA_BAND = 2 * SWA_BLOCK
SWA_SLOTS = SWA_WIDTH // LANES
SWA_HEADS_PER_SLOT = LANES // SWA_HEAD_DIM


SWA_STEP = 4
SWA_ROWS = SWA_STEP * SWA_BLOCK


def _swa_kernel(sinks_ref, sq_ref, kv_ref, bias_ref, qgain_ref, kgain_ref, seg_ref, o_ref,
                band_ref):
    n = pl.program_id(1)
    kv = kv_ref[0].astype(F32)
    k = kv[:, :SWA_KV_WIDTH]
    slots_per_piece = MXU_WIDTH // LANES
    q_slots = [sq_ref[j // slots_per_piece, :, (j % slots_per_piece) * LANES:
                      (j % slots_per_piece + 1) * LANES].astype(F32) for j in range(SWA_SLOTS)]
    parts = q_slots + [k]
    ms = _mm(jnp.concatenate([p * p for p in parts], axis=0), seg_ref[...])
    qn = [parts[j] * lax.rsqrt(ms[j * SWA_ROWS:(j + 1) * SWA_ROWS] + EPS)
          * qgain_ref[:, j * LANES:(j + 1) * LANES] for j in range(SWA_SLOTS)]
    kn = k * lax.rsqrt(ms[SWA_SLOTS * SWA_ROWS:] + EPS) * kgain_ref[...]

    @pl.when(n == 0)
    def _():
        band_ref[0:SWA_BLOCK, :] = jnp.zeros((SWA_BLOCK, 2 * SWA_KV_WIDTH), F32)

    @pl.when(n > 0)
    def _():
        band_ref[0:SWA_BLOCK, :] = band_ref[SWA_ROWS:SWA_ROWS + SWA_BLOCK, :]

    band_ref[SWA_BLOCK:, :SWA_KV_WIDTH] = kn
    band_ref[SWA_BLOCK:, SWA_KV_WIDTH:] = kv[:, SWA_KV_WIDTH:]
    k = band_ref[:, :SWA_KV_WIDTH]
    v = band_ref[:, SWA_KV_WIDTH:]

    lane = lax.broadcasted_iota(jnp.int32, (SWA_BLOCK + SWA_ROWS, LANES), 1)
    low = lane < SWA_HEAD_DIM
    k_rot = pltpu.roll(k, SWA_HEAD_DIM, axis=1)
    v_rot = pltpu.roll(v, SWA_HEAD_DIM, axis=1)
    k_lo = (jnp.where(low, k, 0.0).astype(BF16), jnp.where(low, k_rot, 0.0).astype(BF16))
    k_hi = (jnp.where(low, 0.0, k_rot).astype(BF16), jnp.where(low, 0.0, k).astype(BF16))
    v_lo = (jnp.where(low, v, 0.0).astype(BF16), jnp.where(low, v_rot, 0.0).astype(BF16))
    v_hi = (jnp.where(low, 0.0, v_rot).astype(BF16), jnp.where(low, 0.0, v).astype(BF16))

    out_low = lax.broadcasted_iota(jnp.int32, (SWA_BLOCK, LANES), 1) < SWA_HEAD_DIM

    slots_per_kv = SWA_SLOTS // SWA_KV_HEADS
    units = [(u, j) for u in range(SWA_STEP) for j in range(SWA_SLOTS)]
    band_of = lambda arr, u: arr[u * SWA_BLOCK:u * SWA_BLOCK + SWA_BAND]
    k_bd = {(u, kh): jnp.concatenate([band_of(k_lo[kh], u), band_of(k_hi[kh], u)], axis=0)
            for u in range(SWA_STEP) for kh in range(SWA_KV_HEADS)}
    v_bd = {(u, kh): jnp.concatenate([band_of(v_lo[kh], u), band_of(v_hi[kh], u)], axis=0)
            for u in range(SWA_STEP) for kh in range(SWA_KV_HEADS)}
    logits = {(u, j): _mm(qn[j][u * SWA_BLOCK:(u + 1) * SWA_BLOCK], k_bd[u, j // slots_per_kv], _NT)
              for u, j in units}
    first_tab = jnp.minimum(n, 1)
    probs = {}
    inv_den = {}
    for u, j in units:
        for r in range(SWA_HEADS_PER_SLOT):
            hd = j * SWA_HEADS_PER_SLOT + r
            bias = bias_ref[first_tab, hd] if u == 0 else bias_ref[1, hd]
            lg = logits[u, j][:, r * SWA_BAND:(r + 1) * SWA_BAND] + bias
            sink = sinks_ref[hd]
            m = jnp.maximum(jnp.max(lg, axis=-1, keepdims=True), sink)
            p = jnp.exp(lg - m)
            den = jnp.sum(p, axis=-1, keepdims=True) + jnp.exp(sink - m)
            probs[u, j, r] = p.astype(BF16)
            inv_den[u, j, r] = 1.0 / den
    outs = {(u, j): _mm(jnp.concatenate([probs[u, j, r] for r in range(SWA_HEADS_PER_SLOT)],
                                        axis=1), v_bd[u, j // slots_per_kv])
            for u, j in units}
    for u, j in units:
        o_ref[0, u * SWA_BLOCK:(u + 1) * SWA_BLOCK, j * LANES:(j + 1) * LANES] = (
            outs[u, j] * jnp.where(out_low, inv_den[u, j, 0], inv_den[u, j, 1])).astype(o_ref.dtype)


def _swa(rest, batch, sinks, bias_tabs, qgain, kgain, seg):
    B = batch
    S = rest.shape[1] // B
    steps = S // SWA_ROWS
    kvw = 2 * SWA_KV_WIDTH
    sq_pieces = SWA_WIDTH // MXU_WIDTH
    assert kvw == MXU_WIDTH
    grid_spec = pltpu.PrefetchScalarGridSpec(
        num_scalar_prefetch=1,
        grid=(B, steps),
        in_specs=[
            pl.BlockSpec((sq_pieces, SWA_ROWS, MXU_WIDTH),
                         lambda b, n, s: (0, b * steps + n, 0)),
            pl.BlockSpec((1, SWA_ROWS, kvw), lambda b, n, s: (sq_pieces, b * steps + n, 0)),
            pl.BlockSpec((2, SWA_HEADS, SWA_BLOCK, SWA_BAND), lambda b, n, s: (0, 0, 0, 0)),
            pl.BlockSpec((1, SWA_WIDTH), lambda b, n, s: (0, 0)),
            pl.BlockSpec((1, SWA_KV_WIDTH), lambda b, n, s: (0, 0)),
            pl.BlockSpec((LANES, LANES), lambda b, n, s: (0, 0)),
        ],
        out_specs=pl.BlockSpec((1, SWA_ROWS, SWA_WIDTH), lambda b, n, s: (b, n, 0)),
        scratch_shapes=[pltpu.VMEM((SWA_BLOCK + SWA_ROWS, kvw), F32)],
    )
    return pl.pallas_call(
        _swa_kernel,
        grid_spec=grid_spec,
        out_shape=jax.ShapeDtypeStruct((B, S, SWA_WIDTH), BF16),
        compiler_params=pltpu.CompilerParams(
            dimension_semantics=("arbitrary", "arbitrary"), vmem_limit_bytes=VMEM_LIMIT),
        name="swa",
    )(sinks, rest, rest, bias_tabs, qgain, kgain, seg)


def _merge_ffn_kernel(x_ref, ydn_ref, yswa_ref, g_ref, wa_ref, wb_ref, wo_ref, gain_ref,
                      wg_ref, wu_ref, wd_ref, o_ref, act_ref):
    per_branch = D_MODEL // MXU_WIDTH
    ga, gb = [_sigmoid(jnp.concatenate(
        [g_ref[br * per_branch + p].astype(F32) for p in range(per_branch)], axis=1))
        for br in range(2)]
    merged = ga * _mm(ydn_ref[...], wa_ref[...]) + gb * _mm(yswa_ref[...], wb_ref[...])
    x1 = x_ref[...] + _mm(merged, wo_ref[...])
    ms = jnp.mean(x1 * x1, axis=-1, keepdims=True)
    h2 = (x1 * lax.rsqrt(ms + EPS) * gain_ref[...]).astype(BF16)
    for c0 in range(0, D_FF, MXU_WIDTH):
        gate = jnp.dot(h2, wg_ref[:, c0:c0 + MXU_WIDTH], preferred_element_type=F32)
        up = jnp.dot(h2, wu_ref[:, c0:c0 + MXU_WIDTH], preferred_element_type=F32)
        act_ref[:, c0:c0 + MXU_WIDTH] = (_silu(gate) * up).astype(BF16)
    o_ref[...] = x1 + jnp.dot(act_ref[...], wd_ref[...], preferred_element_type=F32)


def _merge_ffn(x2, ydn, yswa, graw, wa, wb, wo, gain, wg, wu, wd, tm):
    T = x2.shape[0]
    row = lambda w: pl.BlockSpec((tm, w), lambda i: (i, 0))
    once = lambda a, b: pl.BlockSpec((a, b), lambda i: (0, 0), pipeline_mode=pl.Buffered(1))
    return pl.pallas_call(
        _merge_ffn_kernel,
        grid=(T // tm,),
        in_specs=[row(D_MODEL), row(DN_WIDTH), row(SWA_WIDTH),
                  pl.BlockSpec((GATE_PIECES, tm, MXU_WIDTH), lambda i: (0, i, 0)),
                  once(DN_WIDTH, D_MODEL), once(SWA_WIDTH, D_MODEL), once(D_MODEL, D_MODEL),
                  once(1, D_MODEL),
                  once(D_MODEL, D_FF), once(D_MODEL, D_FF), once(D_FF, D_MODEL)],
        out_specs=row(D_MODEL),
        out_shape=jax.ShapeDtypeStruct((T, D_MODEL), F32),
        scratch_shapes=[pltpu.VMEM((tm, D_FF), BF16)],
        compiler_params=pltpu.CompilerParams(
            dimension_semantics=("arbitrary",), vmem_limit_bytes=VMEM_LIMIT),
        name="merge_ffn",
    )(x2, ydn, yswa, graw, wa, wb, wo, gain, wg, wu, wd)


def _t5_bucket_table():
    qi = jnp.arange(SWA_BLOCK)[:, None]
    kj = jnp.arange(SWA_BAND)[None, :]
    dist = SWA_BLOCK + qi - kj
    in_window = (dist >= 0) & (dist < WINDOW)
    n = jnp.maximum(dist, 0)
    max_exact = REL_BUCKETS // 2
    nf = jnp.maximum(n, 1).astype(F32)
    large = max_exact + (jnp.log(nf / max_exact) / math.log(REL_MAX_DIST / max_exact)
                         * (REL_BUCKETS - max_exact)).astype(jnp.int32)
    large = jnp.minimum(large, REL_BUCKETS - 1)
    return jnp.where(n < max_exact, n, large), in_window


def _segment_mean_matrix(width, seg):
    idx = np.arange(width) // seg
    return jnp.asarray((idx[:, None] == idx[None, :]).astype(np.float32) / seg, dtype=BF16)


def kernel(x, attn_norm, w_in, dn_conv, dn_a_log, dn_dt_bias, dn_out_norm, swa_q_norm, swa_k_norm, swa_sinks, rel_bias, w_branch_dn, w_branch_swa, w_out, ffn_norm, w_gate, w_up, w_down):
    B, S, D = x.shape
    T = B * S
    depth = w_in.shape[0]
    bucket, in_window = _t5_bucket_table()
    seg = _segment_mean_matrix(LANES, SWA_HEAD_DIM)
    rel = rel_bias.astype(F32)
    bias_tab = sum(jnp.where(bucket[None] == b, rel[b][:, None, None], 0.0)
                   for b in range(REL_BUCKETS))
    bias_tab = jnp.where(in_window[None], bias_tab, -jnp.inf)
    has_prev = jnp.arange(SWA_BAND)[None, None, :] >= SWA_BLOCK
    bias_tabs = jnp.stack([jnp.where(has_prev, bias_tab, -jnp.inf), bias_tab])

    x2 = x.reshape(T, D)
    for l in range(depth):
        convw = dn_conv[l].astype(F32).reshape(DN_CONV, QKV_BLOCKS, DN_HEAD_DIM).transpose(1, 0, 2)

        later = [w[l].astype(F32) for w in (w_branch_dn, w_branch_swa, w_out, w_gate, w_up, w_down)]
        (qkv, z, sqkv, gates, ba), later_bf16 = _in_proj(
            x2, attn_norm[l][None, :], jnp.swapaxes(w_in[l], 0, 1).astype(F32), later, tm=512)

        pad_lo = jnp.zeros((DN_HEADS,), F32)
        pad_hi = jnp.zeros((LANES - 2 * DN_HEADS,), F32)
        alog_lane = jnp.concatenate([pad_lo, dn_a_log[l].astype(F32), pad_hi])[None, :]
        dtb_lane = jnp.concatenate([pad_lo, dn_dt_bias[l].astype(F32), pad_hi])[None, :]
        y_dn = _deltanet(qkv, z.reshape(B, S, -1), ba.reshape(B, S, -1), convw,
                         alog_lane, dtb_lane, dn_out_norm[l][None, :])

        qgain = jnp.tile(swa_q_norm[l].astype(F32), SWA_HEADS)[None, :] * (SWA_HEAD_DIM ** -0.5)
        kgain = jnp.tile(swa_k_norm[l].astype(F32), SWA_KV_HEADS)[None, :]
        y_swa = _swa(sqkv, B, swa_sinks[l].astype(F32), bias_tabs, qgain, kgain, seg)

        wa, wb, wo, wg, wu, wd = later_bf16
        x2 = _merge_ffn(x2, y_dn.reshape(T, -1), y_swa.reshape(T, -1), gates,
                        wa, wb, wo, ffn_norm[l][None, :], wg, wu, wd, tm=512)
    return x2.reshape(B, S, D)
```

```python
import functools
import math

import numpy as np
import jax
import jax.numpy as jnp
from jax import lax
from jax.experimental import pallas as pl
from jax.experimental.pallas import tpu as pltpu

D_MODEL = 1024
DN_HEADS = 4
DN_HEAD_DIM = 128
DN_WIDTH = DN_HEADS * DN_HEAD_DIM
DN_QKV_WIDTH = 3 * DN_WIDTH
DN_CONV = 4
DN_CHUNK = 64
SWA_HEADS = 8
SWA_KV_HEADS = 2
SWA_HEAD_DIM = 64
SWA_WIDTH = SWA_HEADS * SWA_HEAD_DIM
SWA_KV_WIDTH = SWA_KV_HEADS * SWA_HEAD_DIM
WINDOW = 128
SWA_BLOCK = 128
REL_BUCKETS = 32
REL_MAX_DIST = 128
D_FF = 2816
EPS = 1e-6

LANES = 128
SUBLANES = 8
MXU_WIDTH = 256
VMEM_LIMIT = 58 * 1024 * 1024

W_HEAD = DN_QKV_WIDTH + DN_WIDTH
W_BA = 2 * DN_HEADS
W_SWA = SWA_WIDTH + 2 * SWA_KV_WIDTH
W_GATES = 2 * D_MODEL
D_IN = W_HEAD + W_BA + W_SWA + W_GATES
SWA_PIECES = W_SWA // MXU_WIDTH
GATE_PIECES = W_GATES // MXU_WIDTH
QKV_BLOCKS = DN_QKV_WIDTH // DN_HEAD_DIM

F32 = jnp.float32
BF16 = jnp.bfloat16


def _mm(a, b, dims=(((1,), (0,)), ((), ()))):
    return lax.dot_general(a.astype(BF16), b.astype(BF16), dims, preferred_element_type=F32)


_NT = (((1,), (1,)), ((), ()))
_TN = (((0,), (0,)), ((), ()))


def _sigmoid(x):
    return 0.5 * jnp.tanh(0.5 * x) + 0.5


def _silu(x):
    h = 0.5 * x
    return h * jnp.tanh(h) + h


CONV_TAIL = SUBLANES
CONV_ROWS = 128


W_CAST_ROWS = 256


def _in_proj_kernel(sinks_ref, x_ref, gain_ref, wt_ref, bias_ref, qgain_ref, kgain_ref, seg_ref,
                    *refs, n_cast, tiles_per_seq):
    cast_src = refs[:n_cast]
    qkv_ref, z_ref, yswa_ref, gates_ref, ba_ref = refs[n_cast:n_cast + 5]
    cast_dst = refs[n_cast + 5:2 * n_cast + 5]
    w_head, w_ba, w_rest, band_ref = refs[2 * n_cast + 5:]
    i = pl.program_id(0)
    for src, dst in zip(cast_src, cast_dst):
        dst[...] = src[...].astype(BF16)

    @pl.when(i == 0)
    def _():
        band_ref[...] = jnp.zeros_like(band_ref)
        for r0 in range(0, W_HEAD, W_CAST_ROWS):
            w_head[r0:r0 + W_CAST_ROWS, :] = wt_ref[r0:r0 + W_CAST_ROWS, :].astype(BF16)
        w_ba[...] = jnp.zeros_like(w_ba)
        w_ba[0:W_BA, :] = wt_ref[W_HEAD:W_HEAD + W_BA, :].astype(BF16)
        base = W_HEAD + W_BA
        for r0 in range(0, W_SWA + W_GATES, W_CAST_ROWS):
            w_rest[r0:r0 + W_CAST_ROWS, :] = wt_ref[base + r0:base + r0 + W_CAST_ROWS, :].astype(BF16)

    x = x_ref[...]
    ms = jnp.mean(x * x, axis=-1, keepdims=True)
    h = (x * lax.rsqrt(ms + EPS) * gain_ref[...]).astype(BF16)
    proj = lambda w: lax.dot_general(h, w, _NT, preferred_element_type=F32)

    def dn_qkv():
        raw = proj(w_head[0:DN_QKV_WIDTH, :])
        for c in range(QKV_BLOCKS):
            qkv_ref[c] = raw[:, c * DN_HEAD_DIM:(c + 1) * DN_HEAD_DIM].astype(qkv_ref.dtype)

    def dn_z():
        z_ref[...] = proj(w_head[DN_QKV_WIDTH:W_HEAD, :]).astype(z_ref.dtype)

    def gate_logits():
        for p in range(GATE_PIECES):
            r0 = W_SWA + p * MXU_WIDTH
            gates_ref[p] = proj(w_rest[r0:r0 + MXU_WIDTH, :]).astype(gates_ref.dtype)

    def beta_decay():
        ba_ref[...] = proj(w_ba[...])

    sw = [proj(w_rest[p * MXU_WIDTH:(p + 1) * MXU_WIDTH, :]) for p in range(SWA_PIECES)]
    slots_per_piece = MXU_WIDTH // LANES
    q_slots = [sw[j // slots_per_piece][:, (j % slots_per_piece) * LANES:
                                        (j % slots_per_piece + 1) * LANES] for j in range(SWA_SLOTS)]
    _swa_attend(i % tiles_per_seq == 0, q_slots, sw[SWA_WIDTH // MXU_WIDTH], sinks_ref, bias_ref,
                qgain_ref, kgain_ref, seg_ref, band_ref, yswa_ref,
                between=(dn_qkv, dn_z, gate_logits, beta_decay))


BF16_ROWS = 2 * SUBLANES


def _cast_block_rows(rows, steps):
    for blocks in range(steps, 0, -1):
        if steps % blocks == 0 and rows % (blocks * BF16_ROWS) == 0:
            return rows // blocks
    raise ValueError(f"no bf16-tile-aligned split of {rows} rows over {steps} steps")


def _in_proj(x2, gain, w_t, swa_params, later_weights, seq_len):
    sinks, bias_tabs, qgain, kgain, seg = swa_params
    T = x2.shape[0]
    tm = SWA_ROWS
    steps = T // tm
    const = lambda shape: pl.BlockSpec(shape, lambda i, s: (0,) * len(shape),
                                       pipeline_mode=pl.Buffered(1))
    cast_specs = []
    for w in later_weights:
        rows = _cast_block_rows(w.shape[0], steps)
        repeat = steps // (w.shape[0] // rows)
        cast_specs.append(pl.BlockSpec((rows, w.shape[1]),
                                       lambda i, s, repeat=repeat: (i // repeat, 0)))
    grid_spec = pltpu.PrefetchScalarGridSpec(
        num_scalar_prefetch=1,
        grid=(steps,),
        in_specs=[
            pl.BlockSpec((tm, D_MODEL), lambda i, s: (i, 0)),
            const((1, D_MODEL)),
            const((D_IN, D_MODEL)),
            const((2, SWA_HEADS, SWA_BLOCK, SWA_BAND)),
            const((1, SWA_WIDTH)),
            const((1, SWA_KV_WIDTH)),
            const((LANES, LANES)),
        ] + cast_specs,
        out_specs=[
            pl.BlockSpec((QKV_BLOCKS, tm, DN_HEAD_DIM), lambda i, s: (0, i, 0)),
            pl.BlockSpec((tm, DN_WIDTH), lambda i, s: (i, 0)),
            pl.BlockSpec((tm, SWA_WIDTH), lambda i, s: (i, 0)),
            pl.BlockSpec((GATE_PIECES, tm, MXU_WIDTH), lambda i, s: (0, i, 0)),
            pl.BlockSpec((tm, LANES), lambda i, s: (i, 0)),
        ] + cast_specs,
        scratch_shapes=[pltpu.VMEM((W_HEAD, D_MODEL), BF16),
                        pltpu.VMEM((LANES, D_MODEL), BF16),
                        pltpu.VMEM((W_SWA + W_GATES, D_MODEL), BF16),
                        pltpu.VMEM((SWA_BLOCK + SWA_ROWS, 2 * SWA_KV_WIDTH), F32)],
    )
    outs = pl.pallas_call(
        functools.partial(_in_proj_kernel, n_cast=len(later_weights),
                          tiles_per_seq=seq_len // tm),
        grid_spec=grid_spec,
        out_shape=[
            jax.ShapeDtypeStruct((QKV_BLOCKS, T, DN_HEAD_DIM), BF16),
            jax.ShapeDtypeStruct((T, DN_WIDTH), BF16),
            jax.ShapeDtypeStruct((T, SWA_WIDTH), BF16),
            jax.ShapeDtypeStruct((GATE_PIECES, T, MXU_WIDTH), BF16),
            jax.ShapeDtypeStruct((T, LANES), F32),
        ] + [jax.ShapeDtypeStruct(w.shape, BF16) for w in later_weights],
        compiler_params=pltpu.CompilerParams(
            dimension_semantics=("arbitrary",), vmem_limit_bytes=VMEM_LIMIT),
        name="in_proj",
    )(sinks, x2, gain, w_t, bias_tabs, qgain, kgain, seg, *later_weights)
    return outs[:5], outs[5:]


DN_TILE = 128
DN_NCH = DN_TILE // DN_CHUNK
DN_LOG_CHUNK = int(math.log2(DN_CHUNK))
DN_STREAMS = 4


def _deltanet_kernel(raw_ref, raw_next_ref, z_ref, ba_ref, convw_ref, alog_ref, dtb_ref, gain_ref,
                     o_ref, state, xbuf, qkv_ref, qkv_next_ref):
    t = pl.program_id(1)
    scale = DN_HEAD_DIM ** -0.5
    streams = range(DN_STREAMS)

    def preprocess(s, dst_ref):
        for c in range(QKV_BLOCKS):
            w = convw_ref[c]
            for r0 in range(0, DN_TILE, CONV_ROWS):
                acc = w[DN_CONV - 1:DN_CONV, :] * xbuf[s, c, CONV_TAIL + r0:CONV_TAIL + r0 + CONV_ROWS, :]
                for j in range(DN_CONV - 1):
                    off = CONV_TAIL - (DN_CONV - 1) + j + r0
                    acc = acc + w[j:j + 1, :] * xbuf[s, c, off:off + CONV_ROWS, :]
                act = _silu(acc)
                if c < 2 * DN_HEADS:
                    norm = lax.rsqrt(jnp.sum(act * act, axis=-1, keepdims=True) + EPS)
                    act = act * (norm * scale if c < DN_HEADS else norm)
                dst_ref[s, c, r0:r0 + CONV_ROWS, :] = act.astype(BF16)

    @pl.when(t == 0)
    def _():
        state[...] = jnp.zeros_like(state)
        for s in streams:
            xbuf[s, :, 0:CONV_TAIL, :] = jnp.zeros((QKV_BLOCKS, CONV_TAIL, DN_HEAD_DIM), F32)
            xbuf[s, :, CONV_TAIL:, :] = raw_ref[:, s].astype(F32)
            preprocess(s, qkv_ref)

    for s in streams:
        last_rows = raw_ref[:, s, DN_TILE - BF16_ROWS:DN_TILE, :].astype(F32)
        xbuf[s, :, 0:CONV_TAIL, :] = last_rows[:, BF16_ROWS - CONV_TAIL:, :]
        xbuf[s, :, CONV_TAIL:, :] = raw_next_ref[:, s].astype(F32)
        preprocess(s, qkv_next_ref)

    row = lax.broadcasted_iota(jnp.int32, (DN_TILE, DN_TILE), 0)
    col = lax.broadcasted_iota(jnp.int32, (DN_TILE, DN_TILE), 1)
    same_chunk = (row >> DN_LOG_CHUNK) == (col >> DN_LOG_CHUNK)
    lower_incl = same_chunk & (row >= col)
    strict_lower = same_chunk & (row > col)
    cumsum_mat = jnp.where(lower_incl, 1.0, 0.0).astype(BF16)
    eye = jnp.where(row == col, 1.0, 0.0)

    beta, gc, rev, gc_t = [], [], [], []
    for s in streams:
        ba = ba_ref[s]
        beta.append(_sigmoid(ba))
        xs = ba + dtb_ref[...]
        softplus = jnp.maximum(xs, 0.0) + jnp.log1p(jnp.exp(-jnp.abs(xs)))
        g = -jnp.exp(alog_ref[...]) * softplus
        g_hi = g.astype(BF16)
        g_lo = (g - g_hi.astype(F32)).astype(BF16)
        gc2 = jnp.dot(cumsum_mat, jnp.concatenate([g_hi, g_lo], axis=1),
                      preferred_element_type=F32)
        gc.append(gc2[:, :LANES] + gc2[:, LANES:])
        g_last = jnp.concatenate(
            [jnp.broadcast_to(gc[s][(c + 1) * DN_CHUNK - 1:(c + 1) * DN_CHUNK, :], (DN_CHUNK, LANES))
             for c in range(DN_NCH)], axis=0)
        rev.append(g_last - gc[s])
        gc_t.append(gc[s].T)

    units = [(s, h) for s in streams for h in range(DN_HEADS)]
    gcol = lambda arr, h: arr[:, DN_HEADS + h:DN_HEADS + h + 1]
    q, k, kb, rhs = {}, {}, {}, {}
    for s, h in units:
        q[s, h] = qkv_ref[s, h].astype(F32)
        k[s, h] = qkv_ref[s, DN_HEADS + h].astype(F32)
        vh = qkv_ref[s, 2 * DN_HEADS + h].astype(F32)
        beta_c = beta[s][:, h:h + 1]
        kb[s, h] = k[s, h] * beta_c
        rhs[s, h] = jnp.concatenate([vh * beta_c, kb[s, h] * jnp.exp(gcol(gc[s], h))], axis=1)
    gram = {p: _mm(jnp.concatenate([kb[p], q[p]], axis=0), k[p], _NT) for p in units}
    qk, n_pow, inv = {}, {}, {}
    for s, h in units:
        gc_r = gc_t[s][DN_HEADS + h:DN_HEADS + h + 1, :]
        decay = jnp.exp(jnp.where(lower_incl, gcol(gc[s], h) - gc_r, -jnp.inf))
        qk[s, h] = gram[s, h][DN_TILE:] * decay
        n_pow[s, h] = -jnp.where(strict_lower, gram[s, h][:DN_TILE] * decay, 0.0)
        inv[s, h] = eye + n_pow[s, h]
    for _ in range(DN_LOG_CHUNK - 1):
        n_pow = {p: _mm(n_pow[p], n_pow[p]) for p in units}
        inv = {p: inv[p] + _mm(inv[p], n_pow[p]) for p in units}
    sol = {p: _mm(inv[p], rhs[p]) for p in units}
    u = {p: sol[p][:, :DN_HEAD_DIM] for p in units}
    wmat = {p: sol[p][:, DN_HEAD_DIM:] for p in units}
    qe = {(s, h): q[s, h] * jnp.exp(gcol(gc[s], h)) for s, h in units}
    kdec = {(s, h): k[s, h] * jnp.exp(gcol(rev[s], h)) for s, h in units}

    st = {(s, h): state[s, h] for s, h in units}
    v_new = {p: [] for p in units}
    o_inter = {p: [] for p in units}
    for c in range(DN_NCH):
        r0, r1 = c * DN_CHUNK, (c + 1) * DN_CHUNK
        ws = {p: _mm(jnp.concatenate([wmat[p][r0:r1], qe[p][r0:r1]], axis=0), st[p]) for p in units}
        for p in units:
            v_new[p].append(u[p][r0:r1] - ws[p][:DN_CHUNK])
            o_inter[p].append(ws[p][DN_CHUNK:])
        upd = {p: _mm(kdec[p][r0:r1], v_new[p][c], _TN) for p in units}
        st = {(s, h): st[s, h] * jnp.exp(gc[s][r1 - 1:r1, DN_HEADS + h:DN_HEADS + h + 1]) + upd[s, h]
              for s, h in units}
    o_intra = {p: _mm(qk[p], jnp.concatenate(v_new[p], axis=0)) for p in units}
    for s, h in units:
        state[s, h] = st[s, h]
        o = jnp.concatenate(o_inter[s, h], axis=0) + o_intra[s, h]
        o = o * lax.rsqrt(jnp.mean(o * o, axis=-1, keepdims=True) + EPS) * gain_ref[...]
        zh = z_ref[s, :, h * DN_HEAD_DIM:(h + 1) * DN_HEAD_DIM].astype(F32)
        o_ref[s, :, h * DN_HEAD_DIM:(h + 1) * DN_HEAD_DIM] = (o * _silu(zh)).astype(o_ref.dtype)
    qkv_ref[...] = qkv_next_ref[...]


def _deltanet(raw, z, ba, convw, alog_lane, dtb_lane, out_gain):
    B, S, _ = z.shape
    tiles = S // DN_TILE
    raw = raw.reshape(QKV_BLOCKS, B, S, DN_HEAD_DIM)
    raw_block = (QKV_BLOCKS, DN_STREAMS, DN_TILE, DN_HEAD_DIM)
    qkv_block = (DN_STREAMS, QKV_BLOCKS, DN_TILE, DN_HEAD_DIM)
    return pl.pallas_call(
        _deltanet_kernel,
        grid=(B // DN_STREAMS, tiles),
        in_specs=[
            pl.BlockSpec(raw_block, lambda b, t: (0, b, t, 0)),
            pl.BlockSpec(raw_block, lambda b, t: (0, b, jnp.minimum(t + 1, tiles - 1), 0)),
            pl.BlockSpec((DN_STREAMS, DN_TILE, DN_WIDTH), lambda b, t: (b, t, 0)),
            pl.BlockSpec((DN_STREAMS, DN_TILE, LANES), lambda b, t: (b, t, 0)),
            pl.BlockSpec((QKV_BLOCKS, DN_CONV, DN_HEAD_DIM), lambda b, t: (0, 0, 0)),
            pl.BlockSpec((1, LANES), lambda b, t: (0, 0)),
            pl.BlockSpec((1, LANES), lambda b, t: (0, 0)),
            pl.BlockSpec((1, DN_HEAD_DIM), lambda b, t: (0, 0)),
        ],
        out_specs=pl.BlockSpec((DN_STREAMS, DN_TILE, DN_WIDTH), lambda b, t: (b, t, 0)),
        out_shape=jax.ShapeDtypeStruct((B, S, DN_WIDTH), BF16),
        scratch_shapes=[
            pltpu.VMEM((DN_STREAMS, DN_HEADS, DN_HEAD_DIM, DN_HEAD_DIM), F32),
            pltpu.VMEM((DN_STREAMS, QKV_BLOCKS, CONV_TAIL + DN_TILE, DN_HEAD_DIM), F32),
            pltpu.VMEM(qkv_block, BF16), pltpu.VMEM(qkv_block, BF16)],
        compiler_params=pltpu.CompilerParams(
            dimension_semantics=("arbitrary", "arbitrary"), vmem_limit_bytes=VMEM_LIMIT),
        name="deltanet",
    )(raw, raw, z, ba, convw, alog_lane, dtb_lane, out_gain)


SWA_BAND = 2 * SWA_BLOCK
SWA_SLOTS = SWA_WIDTH // LANES
SWA_HEADS_PER_SLOT = LANES // SWA_HEAD_DIM


SWA_STEP = 4
SWA_ROWS = SWA_STEP * SWA_BLOCK


def _swa_attend(first, q_slots, kv, sinks_ref, bias_ref, qgain_ref, kgain_ref, seg_ref, band_ref,
                o_ref, between):
    k = kv[:, :SWA_KV_WIDTH]
    parts = q_slots + [k]
    sq = jnp.concatenate([p * p for p in parts], axis=0).astype(BF16)
    between[0]()
    ms = _mm(sq, seg_ref[...])
    qn = [parts[j] * lax.rsqrt(ms[j * SWA_ROWS:(j + 1) * SWA_ROWS] + EPS)
          * qgain_ref[:, j * LANES:(j + 1) * LANES] for j in range(SWA_SLOTS)]
    kn = k * lax.rsqrt(ms[SWA_SLOTS * SWA_ROWS:] + EPS) * kgain_ref[...]

    band_ref[0:SWA_BLOCK, :] = jnp.where(first, 0.0, band_ref[SWA_ROWS:SWA_ROWS + SWA_BLOCK, :])
    band_ref[SWA_BLOCK:, :SWA_KV_WIDTH] = kn
    band_ref[SWA_BLOCK:, SWA_KV_WIDTH:] = kv[:, SWA_KV_WIDTH:]
    k = band_ref[:, :SWA_KV_WIDTH]
    v = band_ref[:, SWA_KV_WIDTH:]

    lane = lax.broadcasted_iota(jnp.int32, (SWA_BLOCK + SWA_ROWS, LANES), 1)
    low = lane < SWA_HEAD_DIM
    k_rot = pltpu.roll(k, SWA_HEAD_DIM, axis=1)
    v_rot = pltpu.roll(v, SWA_HEAD_DIM, axis=1)
    k_lo = (jnp.where(low, k, 0.0).astype(BF16), jnp.where(low, k_rot, 0.0).astype(BF16))
    k_hi = (jnp.where(low, 0.0, k_rot).astype(BF16), jnp.where(low, 0.0, k).astype(BF16))
    v_lo = (jnp.where(low, v, 0.0).astype(BF16), jnp.where(low, v_rot, 0.0).astype(BF16))
    v_hi = (jnp.where(low, 0.0, v_rot).astype(BF16), jnp.where(low, 0.0, v).astype(BF16))

    out_low = lax.broadcasted_iota(jnp.int32, (SWA_BLOCK, LANES), 1) < SWA_HEAD_DIM

    slots_per_kv = SWA_SLOTS // SWA_KV_HEADS
    units = [(u, j) for u in range(SWA_STEP) for j in range(SWA_SLOTS)]
    band_of = lambda arr, u: arr[u * SWA_BLOCK:u * SWA_BLOCK + SWA_BAND]
    k_bd = {(u, kh): jnp.concatenate([band_of(k_lo[kh], u), band_of(k_hi[kh], u)], axis=0)
            for u in range(SWA_STEP) for kh in range(SWA_KV_HEADS)}
    v_bd = {(u, kh): jnp.concatenate([band_of(v_lo[kh], u), band_of(v_hi[kh], u)], axis=0)
            for u in range(SWA_STEP) for kh in range(SWA_KV_HEADS)}
    qn = [qn[j].astype(BF16) for j in range(SWA_SLOTS)]
    between[1]()
    logits = {(u, j): _mm(qn[j][u * SWA_BLOCK:(u + 1) * SWA_BLOCK], k_bd[u, j // slots_per_kv], _NT)
              for u, j in units}
    first_tab = jnp.where(first, 0, 1)
    probs = {}
    inv_den = {}
    for u, j in units:
        for r in range(SWA_HEADS_PER_SLOT):
            hd = j * SWA_HEADS_PER_SLOT + r
            bias = bias_ref[first_tab, hd] if u == 0 else bias_ref[1, hd]
            lg = logits[u, j][:, r * SWA_BAND:(r + 1) * SWA_BAND] + bias
            sink = sinks_ref[hd]
            m = jnp.maximum(jnp.max(lg, axis=-1, keepdims=True), sink)
            p = jnp.exp(lg - m)
            den = jnp.sum(p, axis=-1, keepdims=True) + jnp.exp(sink - m)
            probs[u, j, r] = p.astype(BF16)
            inv_den[u, j, r] = 1.0 / den
    between[2]()
    outs = {(u, j): _mm(jnp.concatenate([probs[u, j, r] for r in range(SWA_HEADS_PER_SLOT)],
                                        axis=1), v_bd[u, j // slots_per_kv])
            for u, j in units}
    between[3]()
    for u, j in units:
        o_ref[u * SWA_BLOCK:(u + 1) * SWA_BLOCK, j * LANES:(j + 1) * LANES] = (
            outs[u, j] * jnp.where(out_low, inv_den[u, j, 0], inv_den[u, j, 1])).astype(o_ref.dtype)


def _merge_ffn_kernel(x_ref, ydn_ref, yswa_ref, g_ref, wa_ref, wb_ref, wo_ref, gain_ref,
                      wg_ref, wu_ref, wd_ref, o_ref, act_ref):
    per_branch = D_MODEL // MXU_WIDTH
    ga, gb = [_sigmoid(jnp.concatenate(
        [g_ref[br * per_branch + p].astype(F32) for p in range(per_branch)], axis=1))
        for br in range(2)]
    merged = ga * _mm(ydn_ref[...], wa_ref[...]) + gb * _mm(yswa_ref[...], wb_ref[...])
    x1 = x_ref[...] + _mm(merged, wo_ref[...])
    ms = jnp.mean(x1 * x1, axis=-1, keepdims=True)
    h2 = (x1 * lax.rsqrt(ms + EPS) * gain_ref[...]).astype(BF16)
    for c0 in range(0, D_FF, MXU_WIDTH):
        gate = jnp.dot(h2, wg_ref[:, c0:c0 + MXU_WIDTH], preferred_element_type=F32)
        up = jnp.dot(h2, wu_ref[:, c0:c0 + MXU_WIDTH], preferred_element_type=F32)
        act_ref[:, c0:c0 + MXU_WIDTH] = (_silu(gate) * up).astype(BF16)
    o_ref[...] = x1 + jnp.dot(act_ref[...], wd_ref[...], preferred_element_type=F32)


def _merge_ffn(x2, ydn, yswa, graw, wa, wb, wo, gain, wg, wu, wd, tm):
    T = x2.shape[0]
    row = lambda w: pl.BlockSpec((tm, w), lambda i: (i, 0))
    once = lambda a, b: pl.BlockSpec((a, b), lambda i: (0, 0), pipeline_mode=pl.Buffered(1))
    return pl.pallas_call(
        _merge_ffn_kernel,
        grid=(T // tm,),
        in_specs=[row(D_MODEL), row(DN_WIDTH), row(SWA_WIDTH),
                  pl.BlockSpec((GATE_PIECES, tm, MXU_WIDTH), lambda i: (0, i, 0)),
                  once(DN_WIDTH, D_MODEL), once(SWA_WIDTH, D_MODEL), once(D_MODEL, D_MODEL),
                  once(1, D_MODEL),
                  once(D_MODEL, D_FF), once(D_MODEL, D_FF), once(D_FF, D_MODEL)],
        out_specs=row(D_MODEL),
        out_shape=jax.ShapeDtypeStruct((T, D_MODEL), F32),
        scratch_shapes=[pltpu.VMEM((tm, D_FF), BF16)],
        compiler_params=pltpu.CompilerParams(
            dimension_semantics=("arbitrary",), vmem_limit_bytes=VMEM_LIMIT),
        name="merge_ffn",
    )(x2, ydn, yswa, graw, wa, wb, wo, gain, wg, wu, wd)


def _t5_bucket_table():
    qi = jnp.arange(SWA_BLOCK)[:, None]
    kj = jnp.arange(SWA_BAND)[None, :]
    dist = SWA_BLOCK + qi - kj
    in_window = (dist >= 0) & (dist < WINDOW)
    n = jnp.maximum(dist, 0)
    max_exact = REL_BUCKETS // 2
    nf = jnp.maximum(n, 1).astype(F32)
    large = max_exact + (jnp.log(nf / max_exact) / math.log(REL_MAX_DIST / max_exact)
                         * (REL_BUCKETS - max_exact)).astype(jnp.int32)
    large = jnp.minimum(large, REL_BUCKETS - 1)
    return jnp.where(n < max_exact, n, large), in_window


def _segment_mean_matrix(width, seg):
    idx = np.arange(width) // seg
    return jnp.asarray((idx[:, None] == idx[None, :]).astype(np.float32) / seg, dtype=BF16)


def kernel(x, attn_norm, w_in, dn_conv, dn_a_log, dn_dt_bias, dn_out_norm, swa_q_norm, swa_k_norm, swa_sinks, rel_bias, w_branch_dn, w_branch_swa, w_out, ffn_norm, w_gate, w_up, w_down):
    B, S, D = x.shape
    T = B * S
    depth = w_in.shape[0]
    bucket, in_window = _t5_bucket_table()
    seg = _segment_mean_matrix(LANES, SWA_HEAD_DIM)
    rel = rel_bias.astype(F32)
    bias_tab = sum(jnp.where(bucket[None] == b, rel[b][:, None, None], 0.0)
                   for b in range(REL_BUCKETS))
    bias_tab = jnp.where(in_window[None], bias_tab, -jnp.inf)
    has_prev = jnp.arange(SWA_BAND)[None, None, :] >= SWA_BLOCK
    bias_tabs = jnp.stack([jnp.where(has_prev, bias_tab, -jnp.inf), bias_tab])

    x2 = x.reshape(T, D)
    for l in range(depth):
        convw = dn_conv[l].astype(F32).reshape(DN_CONV, QKV_BLOCKS, DN_HEAD_DIM).transpose(1, 0, 2)

        later = [w[l].astype(F32) for w in (w_branch_dn, w_branch_swa, w_out, w_gate, w_up, w_down)]
        qgain = jnp.tile(swa_q_norm[l].astype(F32), SWA_HEADS)[None, :] * (SWA_HEAD_DIM ** -0.5)
        kgain = jnp.tile(swa_k_norm[l].astype(F32), SWA_KV_HEADS)[None, :]
        swa_params = (swa_sinks[l].astype(F32), bias_tabs, qgain, kgain, seg)
        (qkv, z, y_swa, gates, ba), later_bf16 = _in_proj(
            x2, attn_norm[l][None, :], jnp.swapaxes(w_in[l], 0, 1).astype(F32), swa_params,
            later, seq_len=S)

        pad_lo = jnp.zeros((DN_HEADS,), F32)
        pad_hi = jnp.zeros((LANES - 2 * DN_HEADS,), F32)
        alog_lane = jnp.concatenate([pad_lo, dn_a_log[l].astype(F32), pad_hi])[None, :]
        dtb_lane = jnp.concatenate([pad_lo, dn_dt_bias[l].astype(F32), pad_hi])[None, :]
        y_dn = _deltanet(qkv, z.reshape(B, S, -1), ba.reshape(B, S, -1), convw,
                         alog_lane, dtb_lane, dn_out_norm[l][None, :])

        wa, wb, wo, wg, wu, wd = later_bf16
        x2 = _merge_ffn(x2, y_dn.reshape(T, -1), y_swa, gates,
                        wa, wb, wo, ffn_norm[l][None, :], wg, wu, wd, tm=512)
    return x2.reshape(B, S, D)
```

```python
import functools
import math

import numpy as np
import jax
import jax.numpy as jnp
from jax import lax
from jax.experimental import pallas as pl
from jax.experimental.pallas import tpu as pltpu

D_MODEL = 1024
DN_HEADS = 4
DN_HEAD_DIM = 128
DN_WIDTH = DN_HEADS * DN_HEAD_DIM
DN_QKV_WIDTH = 3 * DN_WIDTH
DN_CONV = 4
DN_CHUNK = 64
SWA_HEADS = 8
SWA_KV_HEADS = 2
SWA_HEAD_DIM = 64
SWA_WIDTH = SWA_HEADS * SWA_HEAD_DIM
SWA_KV_WIDTH = SWA_KV_HEADS * SWA_HEAD_DIM
WINDOW = 128
SWA_BLOCK = 128
REL_BUCKETS = 32
REL_MAX_DIST = 128
D_FF = 2816
EPS = 1e-6

LANES = 128
SUBLANES = 8
MXU_WIDTH = 256
VMEM_LIMIT = 58 * 1024 * 1024

W_HEAD = DN_QKV_WIDTH + DN_WIDTH
W_BA = 2 * DN_HEADS
W_SWA = SWA_WIDTH + 2 * SWA_KV_WIDTH
W_GATES = 2 * D_MODEL
D_IN = W_HEAD + W_BA + W_SWA + W_GATES
SWA_PIECES = W_SWA // MXU_WIDTH
GATE_PIECES = W_GATES // MXU_WIDTH
QKV_BLOCKS = DN_QKV_WIDTH // DN_HEAD_DIM

F32 = jnp.float32
BF16 = jnp.bfloat16


def _mm(a, b, dims=(((1,), (0,)), ((), ()))):
    return lax.dot_general(a.astype(BF16), b.astype(BF16), dims, preferred_element_type=F32)


_NT = (((1,), (1,)), ((), ()))
_TN = (((0,), (0,)), ((), ()))


def _sigmoid(x):
    return 0.5 * jnp.tanh(0.5 * x) + 0.5


def _silu(x):
    h = 0.5 * x
    return h * jnp.tanh(h) + h


CONV_TAIL = SUBLANES
CONV_ROWS = 128


W_CAST_ROWS = 256


def _in_proj_kernel(sinks_ref, x_ref, gain_ref, wt_ref, bias_ref, qgain_ref, kgain_ref, seg_ref,
                    qkv_ref, z_ref, yswa_ref, gates_ref, ba_ref, w_head, w_ba, w_rest, band_ref,
                    *, tiles_per_seq):
    i = pl.program_id(0)

    @pl.when(i == 0)
    def _():
        band_ref[...] = jnp.zeros_like(band_ref)
        for r0 in range(0, W_HEAD, W_CAST_ROWS):
            w_head[r0:r0 + W_CAST_ROWS, :] = wt_ref[r0:r0 + W_CAST_ROWS, :].astype(BF16)
        w_ba[...] = jnp.zeros_like(w_ba)
        w_ba[0:W_BA, :] = wt_ref[W_HEAD:W_HEAD + W_BA, :].astype(BF16)
        base = W_HEAD + W_BA
        for r0 in range(0, W_SWA + W_GATES, W_CAST_ROWS):
            w_rest[r0:r0 + W_CAST_ROWS, :] = wt_ref[base + r0:base + r0 + W_CAST_ROWS, :].astype(BF16)

    x = x_ref[...]
    ms = jnp.mean(x * x, axis=-1, keepdims=True)
    h = (x * lax.rsqrt(ms + EPS) * gain_ref[...]).astype(BF16)
    proj = lambda w: lax.dot_general(h, w, _NT, preferred_element_type=F32)

    def dn_qkv():
        raw = proj(w_head[0:DN_QKV_WIDTH, :])
        for c in range(QKV_BLOCKS):
            qkv_ref[c] = raw[:, c * DN_HEAD_DIM:(c + 1) * DN_HEAD_DIM].astype(qkv_ref.dtype)

    def dn_z():
        z_ref[...] = proj(w_head[DN_QKV_WIDTH:W_HEAD, :]).astype(z_ref.dtype)

    def gate_logits():
        for p in range(GATE_PIECES):
            r0 = W_SWA + p * MXU_WIDTH
            gates_ref[p] = proj(w_rest[r0:r0 + MXU_WIDTH, :]).astype(gates_ref.dtype)

    def beta_decay():
        ba_ref[...] = proj(w_ba[...])

    sw = [proj(w_rest[p * MXU_WIDTH:(p + 1) * MXU_WIDTH, :]) for p in range(SWA_PIECES)]
    slots_per_piece = MXU_WIDTH // LANES
    q_slots = [sw[j // slots_per_piece][:, (j % slots_per_piece) * LANES:
                                        (j % slots_per_piece + 1) * LANES] for j in range(SWA_SLOTS)]
    _swa_attend(i % tiles_per_seq == 0, q_slots, sw[SWA_WIDTH // MXU_WIDTH], sinks_ref, bias_ref,
                qgain_ref, kgain_ref, seg_ref, band_ref, yswa_ref,
                between=(dn_qkv, dn_z, gate_logits, beta_decay))


BF16_ROWS = 2 * SUBLANES


def _cast_specs(weights, steps):
    specs = []
    for w in weights:
        blocks = next(b for b in range(steps, 0, -1)
                      if steps % b == 0 and w.shape[0] % (b * BF16_ROWS) == 0)
        repeat = steps // blocks
        specs.append(pl.BlockSpec((w.shape[0] // blocks, w.shape[1]),
                                  lambda *idx, repeat=repeat: (idx[-1] // repeat, 0)))
    return specs


def _in_proj(x2, gain, w_t, swa_params, seq_len):
    sinks, bias_tabs, qgain, kgain, seg = swa_params
    T = x2.shape[0]
    tm = SWA_ROWS
    steps = T // tm
    const = lambda shape: pl.BlockSpec(shape, lambda i, s: (0,) * len(shape),
                                       pipeline_mode=pl.Buffered(1))
    grid_spec = pltpu.PrefetchScalarGridSpec(
        num_scalar_prefetch=1,
        grid=(steps,),
        in_specs=[
            pl.BlockSpec((tm, D_MODEL), lambda i, s: (i, 0)),
            const((1, D_MODEL)),
            const((D_IN, D_MODEL)),
            const((2, SWA_HEADS, SWA_BLOCK, SWA_BAND)),
            const((1, SWA_WIDTH)),
            const((1, SWA_KV_WIDTH)),
            const((LANES, LANES)),
        ],
        out_specs=[
            pl.BlockSpec((QKV_BLOCKS, tm, DN_HEAD_DIM), lambda i, s: (0, i, 0)),
            pl.BlockSpec((tm, DN_WIDTH), lambda i, s: (i, 0)),
            pl.BlockSpec((tm, SWA_WIDTH), lambda i, s: (i, 0)),
            pl.BlockSpec((GATE_PIECES, tm, MXU_WIDTH), lambda i, s: (0, i, 0)),
            pl.BlockSpec((tm, LANES), lambda i, s: (i, 0)),
        ],
        scratch_shapes=[pltpu.VMEM((W_HEAD, D_MODEL), BF16),
                        pltpu.VMEM((LANES, D_MODEL), BF16),
                        pltpu.VMEM((W_SWA + W_GATES, D_MODEL), BF16),
                        pltpu.VMEM((SWA_BLOCK + SWA_ROWS, 2 * SWA_KV_WIDTH), F32)],
    )
    return pl.pallas_call(
        functools.partial(_in_proj_kernel, tiles_per_seq=seq_len // tm),
        grid_spec=grid_spec,
        out_shape=[
            jax.ShapeDtypeStruct((QKV_BLOCKS, T, DN_HEAD_DIM), F32),
            jax.ShapeDtypeStruct((T, DN_WIDTH), F32),
            jax.ShapeDtypeStruct((T, SWA_WIDTH), BF16),
            jax.ShapeDtypeStruct((GATE_PIECES, T, MXU_WIDTH), BF16),
            jax.ShapeDtypeStruct((T, LANES), F32),
        ],
        compiler_params=pltpu.CompilerParams(
            dimension_semantics=("arbitrary",), vmem_limit_bytes=VMEM_LIMIT),
        name="in_proj",
    )(sinks, x2, gain, w_t, bias_tabs, qgain, kgain, seg)


DN_TILE = 128
DN_NCH = DN_TILE // DN_CHUNK
DN_LOG_CHUNK = int(math.log2(DN_CHUNK))
DN_STREAMS = 4


def _deltanet_kernel(raw_ref, raw_next_ref, z_ref, ba_ref, convw_ref, alog_ref, dtb_ref, gain_ref,
                     *refs, n_cast):
    cast_src = refs[:n_cast]
    o_ref = refs[n_cast]
    cast_dst = refs[n_cast + 1:2 * n_cast + 1]
    state, xbuf, qkv_ref, qkv_next_ref = refs[2 * n_cast + 1:]
    for src, dst in zip(cast_src, cast_dst):
        dst[...] = src[...].astype(BF16)
    t = pl.program_id(1)
    scale = DN_HEAD_DIM ** -0.5
    streams = range(DN_STREAMS)

    def preprocess(s, dst_ref):
        for c in range(QKV_BLOCKS):
            w = convw_ref[c]
            for r0 in range(0, DN_TILE, CONV_ROWS):
                acc = w[DN_CONV - 1:DN_CONV, :] * xbuf[s, c, CONV_TAIL + r0:CONV_TAIL + r0 + CONV_ROWS, :]
                for j in range(DN_CONV - 1):
                    off = CONV_TAIL - (DN_CONV - 1) + j + r0
                    acc = acc + w[j:j + 1, :] * xbuf[s, c, off:off + CONV_ROWS, :]
                act = _silu(acc)
                if c < 2 * DN_HEADS:
                    norm = lax.rsqrt(jnp.sum(act * act, axis=-1, keepdims=True) + EPS)
                    act = act * (norm * scale if c < DN_HEADS else norm)
                dst_ref[s, c, r0:r0 + CONV_ROWS, :] = act.astype(BF16)

    @pl.when(t == 0)
    def _():
        state[...] = jnp.zeros_like(state)
        for s in streams:
            xbuf[s, :, 0:CONV_TAIL, :] = jnp.zeros((QKV_BLOCKS, CONV_TAIL, DN_HEAD_DIM), F32)
            xbuf[s, :, CONV_TAIL:, :] = raw_ref[:, s]
            preprocess(s, qkv_ref)

    for s in streams:
        xbuf[s, :, 0:CONV_TAIL, :] = raw_ref[:, s, DN_TILE - CONV_TAIL:DN_TILE, :]
        xbuf[s, :, CONV_TAIL:, :] = raw_next_ref[:, s]
        preprocess(s, qkv_next_ref)

    row = lax.broadcasted_iota(jnp.int32, (DN_TILE, DN_TILE), 0)
    col = lax.broadcasted_iota(jnp.int32, (DN_TILE, DN_TILE), 1)
    same_chunk = (row >> DN_LOG_CHUNK) == (col >> DN_LOG_CHUNK)
    lower_incl = same_chunk & (row >= col)
    strict_lower = same_chunk & (row > col)
    cumsum_mat = jnp.where(lower_incl, 1.0, 0.0).astype(BF16)
    eye = jnp.where(row == col, 1.0, 0.0)

    beta, gc, rev, gc_t = [], [], [], []
    for s in streams:
        ba = ba_ref[s]
        beta.append(_sigmoid(ba))
        xs = ba + dtb_ref[...]
        softplus = jnp.maximum(xs, 0.0) + jnp.log1p(jnp.exp(-jnp.abs(xs)))
        g = -jnp.exp(alog_ref[...]) * softplus
        g_hi = g.astype(BF16)
        g_lo = (g - g_hi.astype(F32)).astype(BF16)
        gc2 = jnp.dot(cumsum_mat, jnp.concatenate([g_hi, g_lo], axis=1),
                      preferred_element_type=F32)
        gc.append(gc2[:, :LANES] + gc2[:, LANES:])
        g_last = jnp.concatenate(
            [jnp.broadcast_to(gc[s][(c + 1) * DN_CHUNK - 1:(c + 1) * DN_CHUNK, :], (DN_CHUNK, LANES))
             for c in range(DN_NCH)], axis=0)
        rev.append(g_last - gc[s])
        gc_t.append(gc[s].T)

    units = [(s, h) for s in streams for h in range(DN_HEADS)]
    gcol = lambda arr, h: arr[:, DN_HEADS + h:DN_HEADS + h + 1]
    q, k, kb, rhs = {}, {}, {}, {}
    for s, h in units:
        q[s, h] = qkv_ref[s, h].astype(F32)
        k[s, h] = qkv_ref[s, DN_HEADS + h].astype(F32)
        vh = qkv_ref[s, 2 * DN_HEADS + h].astype(F32)
        beta_c = beta[s][:, h:h + 1]
        kb[s, h] = k[s, h] * beta_c
        rhs[s, h] = jnp.concatenate([vh * beta_c, kb[s, h] * jnp.exp(gcol(gc[s], h))], axis=1)
    gram = {p: _mm(jnp.concatenate([kb[p], q[p]], axis=0), k[p], _NT) for p in units}
    qk, n_pow, inv = {}, {}, {}
    for s, h in units:
        gc_r = gc_t[s][DN_HEADS + h:DN_HEADS + h + 1, :]
        decay = jnp.exp(jnp.where(lower_incl, gcol(gc[s], h) - gc_r, -jnp.inf))
        qk[s, h] = gram[s, h][DN_TILE:] * decay
        n_pow[s, h] = -jnp.where(strict_lower, gram[s, h][:DN_TILE] * decay, 0.0)
        inv[s, h] = eye + n_pow[s, h]
    for _ in range(DN_LOG_CHUNK - 1):
        n_pow = {p: _mm(n_pow[p], n_pow[p]) for p in units}
        inv = {p: inv[p] + _mm(inv[p], n_pow[p]) for p in units}
    sol = {p: _mm(inv[p], rhs[p]) for p in units}
    u = {p: sol[p][:, :DN_HEAD_DIM] for p in units}
    wmat = {p: sol[p][:, DN_HEAD_DIM:] for p in units}
    qe = {(s, h): q[s, h] * jnp.exp(gcol(gc[s], h)) for s, h in units}
    kdec = {(s, h): k[s, h] * jnp.exp(gcol(rev[s], h)) for s, h in units}

    st = {(s, h): state[s, h] for s, h in units}
    v_new = {p: [] for p in units}
    o_inter = {p: [] for p in units}
    for c in range(DN_NCH):
        r0, r1 = c * DN_CHUNK, (c + 1) * DN_CHUNK
        ws = {p: _mm(jnp.concatenate([wmat[p][r0:r1], qe[p][r0:r1]], axis=0), st[p]) for p in units}
        for p in units:
            v_new[p].append(u[p][r0:r1] - ws[p][:DN_CHUNK])
            o_inter[p].append(ws[p][DN_CHUNK:])
        upd = {p: _mm(kdec[p][r0:r1], v_new[p][c], _TN) for p in units}
        st = {(s, h): st[s, h] * jnp.exp(gc[s][r1 - 1:r1, DN_HEADS + h:DN_HEADS + h + 1]) + upd[s, h]
              for s, h in units}
    o_intra = {p: _mm(qk[p], jnp.concatenate(v_new[p], axis=0)) for p in units}
    for s, h in units:
        state[s, h] = st[s, h]
        o = jnp.concatenate(o_inter[s, h], axis=0) + o_intra[s, h]
        o = o * lax.rsqrt(jnp.mean(o * o, axis=-1, keepdims=True) + EPS) * gain_ref[...]
        zh = z_ref[s, :, h * DN_HEAD_DIM:(h + 1) * DN_HEAD_DIM].astype(F32)
        o_ref[s, :, h * DN_HEAD_DIM:(h + 1) * DN_HEAD_DIM] = (o * _silu(zh)).astype(o_ref.dtype)
    qkv_ref[...] = qkv_next_ref[...]


def _deltanet(raw, z, ba, convw, alog_lane, dtb_lane, out_gain, later_weights):
    B, S, _ = z.shape
    tiles = S // DN_TILE
    raw = raw.reshape(QKV_BLOCKS, B, S, DN_HEAD_DIM)
    raw_block = (QKV_BLOCKS, DN_STREAMS, DN_TILE, DN_HEAD_DIM)
    qkv_block = (DN_STREAMS, QKV_BLOCKS, DN_TILE, DN_HEAD_DIM)
    cast_specs = _cast_specs(later_weights, tiles)
    outs = pl.pallas_call(
        functools.partial(_deltanet_kernel, n_cast=len(later_weights)),
        grid=(B // DN_STREAMS, tiles),
        in_specs=[
            pl.BlockSpec(raw_block, lambda b, t: (0, b, t, 0)),
            pl.BlockSpec(raw_block, lambda b, t: (0, b, jnp.minimum(t + 1, tiles - 1), 0)),
            pl.BlockSpec((DN_STREAMS, DN_TILE, DN_WIDTH), lambda b, t: (b, t, 0)),
            pl.BlockSpec((DN_STREAMS, DN_TILE, LANES), lambda b, t: (b, t, 0)),
            pl.BlockSpec((QKV_BLOCKS, DN_CONV, DN_HEAD_DIM), lambda b, t: (0, 0, 0)),
            pl.BlockSpec((1, LANES), lambda b, t: (0, 0)),
            pl.BlockSpec((1, LANES), lambda b, t: (0, 0)),
            pl.BlockSpec((1, DN_HEAD_DIM), lambda b, t: (0, 0)),
        ] + cast_specs,
        out_specs=[pl.BlockSpec((DN_STREAMS, DN_TILE, DN_WIDTH), lambda b, t: (b, t, 0))] + cast_specs,
        out_shape=[jax.ShapeDtypeStruct((B, S, DN_WIDTH), BF16)]
        + [jax.ShapeDtypeStruct(w.shape, BF16) for w in later_weights],
        scratch_shapes=[
            pltpu.VMEM((DN_STREAMS, DN_HEADS, DN_HEAD_DIM, DN_HEAD_DIM), F32),
            pltpu.VMEM((DN_STREAMS, QKV_BLOCKS, CONV_TAIL + DN_TILE, DN_HEAD_DIM), F32),
            pltpu.VMEM(qkv_block, BF16), pltpu.VMEM(qkv_block, BF16)],
        compiler_params=pltpu.CompilerParams(
            dimension_semantics=("arbitrary", "arbitrary"), vmem_limit_bytes=VMEM_LIMIT),
        name="deltanet",
    )(raw, raw, z, ba, convw, alog_lane, dtb_lane, out_gain, *later_weights)
    return outs[0], outs[1:]


SWA_BAND = 2 * SWA_BLOCK
SWA_SLOTS = SWA_WIDTH // LANES
SWA_HEADS_PER_SLOT = LANES // SWA_HEAD_DIM


SWA_STEP = 4
SWA_ROWS = SWA_STEP * SWA_BLOCK


def _swa_attend(first, q_slots, kv, sinks_ref, bias_ref, qgain_ref, kgain_ref, seg_ref, band_ref,
                o_ref, between):
    k = kv[:, :SWA_KV_WIDTH]
    parts = q_slots + [k]
    sq = jnp.concatenate([p * p for p in parts], axis=0).astype(BF16)
    between[0]()
    ms = _mm(sq, seg_ref[...])
    qn = [parts[j] * lax.rsqrt(ms[j * SWA_ROWS:(j + 1) * SWA_ROWS] + EPS)
          * qgain_ref[:, j * LANES:(j + 1) * LANES] for j in range(SWA_SLOTS)]
    kn = k * lax.rsqrt(ms[SWA_SLOTS * SWA_ROWS:] + EPS) * kgain_ref[...]

    band_ref[0:SWA_BLOCK, :] = jnp.where(first, 0.0, band_ref[SWA_ROWS:SWA_ROWS + SWA_BLOCK, :])
    band_ref[SWA_BLOCK:, :SWA_KV_WIDTH] = kn
    band_ref[SWA_BLOCK:, SWA_KV_WIDTH:] = kv[:, SWA_KV_WIDTH:]
    k = band_ref[:, :SWA_KV_WIDTH]
    v = band_ref[:, SWA_KV_WIDTH:]

    lane = lax.broadcasted_iota(jnp.int32, (SWA_BLOCK + SWA_ROWS, LANES), 1)
    low = lane < SWA_HEAD_DIM
    k_rot = pltpu.roll(k, SWA_HEAD_DIM, axis=1)
    v_rot = pltpu.roll(v, SWA_HEAD_DIM, axis=1)
    k_lo = (jnp.where(low, k, 0.0).astype(BF16), jnp.where(low, k_rot, 0.0).astype(BF16))
    k_hi = (jnp.where(low, 0.0, k_rot).astype(BF16), jnp.where(low, 0.0, k).astype(BF16))
    v_lo = (jnp.where(low, v, 0.0).astype(BF16), jnp.where(low, v_rot, 0.0).astype(BF16))
    v_hi = (jnp.where(low, 0.0, v_rot).astype(BF16), jnp.where(low, 0.0, v).astype(BF16))

    out_low = lax.broadcasted_iota(jnp.int32, (SWA_BLOCK, LANES), 1) < SWA_HEAD_DIM

    slots_per_kv = SWA_SLOTS // SWA_KV_HEADS
    units = [(u, j) for u in range(SWA_STEP) for j in range(SWA_SLOTS)]
    band_of = lambda arr, u: arr[u * SWA_BLOCK:u * SWA_BLOCK + SWA_BAND]
    k_bd = {(u, kh): jnp.concatenate([band_of(k_lo[kh], u), band_of(k_hi[kh], u)], axis=0)
            for u in range(SWA_STEP) for kh in range(SWA_KV_HEADS)}
    v_bd = {(u, kh): jnp.concatenate([band_of(v_lo[kh], u), band_of(v_hi[kh], u)], axis=0)
            for u in range(SWA_STEP) for kh in range(SWA_KV_HEADS)}
    qn = [qn[j].astype(BF16) for j in range(SWA_SLOTS)]
    between[1]()
    logits = {(u, j): _mm(qn[j][u * SWA_BLOCK:(u + 1) * SWA_BLOCK], k_bd[u, j // slots_per_kv], _NT)
              for u, j in units}
    first_tab = jnp.where(first, 0, 1)
    probs = {}
    inv_den = {}
    for u, j in units:
        for r in range(SWA_HEADS_PER_SLOT):
            hd = j * SWA_HEADS_PER_SLOT + r
            bias = bias_ref[first_tab, hd] if u == 0 else bias_ref[1, hd]
            lg = logits[u, j][:, r * SWA_BAND:(r + 1) * SWA_BAND] + bias
            sink = sinks_ref[hd]
            m = jnp.maximum(jnp.max(lg, axis=-1, keepdims=True), sink)
            p = jnp.exp(lg - m)
            den = jnp.sum(p, axis=-1, keepdims=True) + jnp.exp(sink - m)
            probs[u, j, r] = p.astype(BF16)
            inv_den[u, j, r] = 1.0 / den
    between[2]()
    outs = {(u, j): _mm(jnp.concatenate([probs[u, j, r] for r in range(SWA_HEADS_PER_SLOT)],
                                        axis=1), v_bd[u, j // slots_per_kv])
            for u, j in units}
    between[3]()
    for u, j in units:
        o_ref[u * SWA_BLOCK:(u + 1) * SWA_BLOCK, j * LANES:(j + 1) * LANES] = (
            outs[u, j] * jnp.where(out_low, inv_den[u, j, 0], inv_den[u, j, 1])).astype(o_ref.dtype)


def _merge_ffn_kernel(x_ref, ydn_ref, yswa_ref, g_ref, wa_ref, wb_ref, wo_ref, gain_ref,
                      wg_ref, wu_ref, wd_ref, o_ref, act_ref):
    per_branch = D_MODEL // MXU_WIDTH
    ga, gb = [_sigmoid(jnp.concatenate(
        [g_ref[br * per_branch + p].astype(F32) for p in range(per_branch)], axis=1))
        for br in range(2)]
    merged = ga * _mm(ydn_ref[...], wa_ref[...]) + gb * _mm(yswa_ref[...], wb_ref[...])
    x1 = x_ref[...] + _mm(merged, wo_ref[...])
    ms = jnp.mean(x1 * x1, axis=-1, keepdims=True)
    h2 = (x1 * lax.rsqrt(ms + EPS) * gain_ref[...]).astype(BF16)
    for c0 in range(0, D_FF, MXU_WIDTH):
        gate = jnp.dot(h2, wg_ref[:, c0:c0 + MXU_WIDTH], preferred_element_type=F32)
        up = jnp.dot(h2, wu_ref[:, c0:c0 + MXU_WIDTH], preferred_element_type=F32)
        act_ref[:, c0:c0 + MXU_WIDTH] = (_silu(gate) * up).astype(BF16)
    o_ref[...] = x1 + jnp.dot(act_ref[...], wd_ref[...], preferred_element_type=F32)


def _merge_ffn(x2, ydn, yswa, graw, wa, wb, wo, gain, wg, wu, wd, tm):
    T = x2.shape[0]
    row = lambda w: pl.BlockSpec((tm, w), lambda i: (i, 0))
    once = lambda a, b: pl.BlockSpec((a, b), lambda i: (0, 0), pipeline_mode=pl.Buffered(1))
    return pl.pallas_call(
        _merge_ffn_kernel,
        grid=(T // tm,),
        in_specs=[row(D_MODEL), row(DN_WIDTH), row(SWA_WIDTH),
                  pl.BlockSpec((GATE_PIECES, tm, MXU_WIDTH), lambda i: (0, i, 0)),
                  once(DN_WIDTH, D_MODEL), once(SWA_WIDTH, D_MODEL), once(D_MODEL, D_MODEL),
                  once(1, D_MODEL),
                  once(D_MODEL, D_FF), once(D_MODEL, D_FF), once(D_FF, D_MODEL)],
        out_specs=row(D_MODEL),
        out_shape=jax.ShapeDtypeStruct((T, D_MODEL), F32),
        scratch_shapes=[pltpu.VMEM((tm, D_FF), BF16)],
        compiler_params=pltpu.CompilerParams(
            dimension_semantics=("arbitrary",), vmem_limit_bytes=VMEM_LIMIT),
        name="merge_ffn",
    )(x2, ydn, yswa, graw, wa, wb, wo, gain, wg, wu, wd)


def _t5_bucket_table():
    qi = jnp.arange(SWA_BLOCK)[:, None]
    kj = jnp.arange(SWA_BAND)[None, :]
    dist = SWA_BLOCK + qi - kj
    in_window = (dist >= 0) & (dist < WINDOW)
    n = jnp.maximum(dist, 0)
    max_exact = REL_BUCKETS // 2
    nf = jnp.maximum(n, 1).astype(F32)
    large = max_exact + (jnp.log(nf / max_exact) / math.log(REL_MAX_DIST / max_exact)
                         * (REL_BUCKETS - max_exact)).astype(jnp.int32)
    large = jnp.minimum(large, REL_BUCKETS - 1)
    return jnp.where(n < max_exact, n, large), in_window


def _segment_mean_matrix(width, seg):
    idx = np.arange(width) // seg
    return jnp.asarray((idx[:, None] == idx[None, :]).astype(np.float32) / seg, dtype=BF16)


def kernel(x, attn_norm, w_in, dn_conv, dn_a_log, dn_dt_bias, dn_out_norm, swa_q_norm, swa_k_norm, swa_sinks, rel_bias, w_branch_dn, w_branch_swa, w_out, ffn_norm, w_gate, w_up, w_down):
    B, S, D = x.shape
    T = B * S
    depth = w_in.shape[0]
    bucket, in_window = _t5_bucket_table()
    seg = _segment_mean_matrix(LANES, SWA_HEAD_DIM)
    rel = rel_bias.astype(F32)
    bias_tab = sum(jnp.where(bucket[None] == b, rel[b][:, None, None], 0.0)
                   for b in range(REL_BUCKETS))
    bias_tab = jnp.where(in_window[None], bias_tab, -jnp.inf)
    has_prev = jnp.arange(SWA_BAND)[None, None, :] >= SWA_BLOCK
    bias_tabs = jnp.stack([jnp.where(has_prev, bias_tab, -jnp.inf), bias_tab])

    x2 = x.reshape(T, D)
    for l in range(depth):
        convw = dn_conv[l].astype(F32).reshape(DN_CONV, QKV_BLOCKS, DN_HEAD_DIM).transpose(1, 0, 2)

        later = [w[l].astype(F32) for w in (w_branch_dn, w_branch_swa, w_out, w_gate, w_up, w_down)]
        qgain = jnp.tile(swa_q_norm[l].astype(F32), SWA_HEADS)[None, :] * (SWA_HEAD_DIM ** -0.5)
        kgain = jnp.tile(swa_k_norm[l].astype(F32), SWA_KV_HEADS)[None, :]
        swa_params = (swa_sinks[l].astype(F32), bias_tabs, qgain, kgain, seg)
        qkv, z, y_swa, gates, ba = _in_proj(
            x2, attn_norm[l][None, :], jnp.swapaxes(w_in[l], 0, 1).astype(F32), swa_params,
            seq_len=S)

        pad_lo = jnp.zeros((DN_HEADS,), F32)
        pad_hi = jnp.zeros((LANES - 2 * DN_HEADS,), F32)
        alog_lane = jnp.concatenate([pad_lo, dn_a_log[l].astype(F32), pad_hi])[None, :]
        dtb_lane = jnp.concatenate([pad_lo, dn_dt_bias[l].astype(F32), pad_hi])[None, :]
        y_dn, later_bf16 = _deltanet(qkv, z.reshape(B, S, -1), ba.reshape(B, S, -1), convw,
                                     alog_lane, dtb_lane, dn_out_norm[l][None, :], later)

        wa, wb, wo, wg, wu, wd = later_bf16
        x2 = _merge_ffn(x2, y_dn.reshape(T, -1), y_swa, gates,
                        wa, wb, wo, ffn_norm[l][None, :], wg, wu, wd, tm=512)
    return x2.reshape(B, S, D)
```

```python
import functools
import math

import jax
import jax.numpy as jnp
from jax import lax
from jax.experimental import pallas as pl
from jax.experimental.pallas import tpu as pltpu

D_MODEL = 1024
DN_HEADS = 4
DN_HEAD_DIM = 128
DN_WIDTH = DN_HEADS * DN_HEAD_DIM
DN_QKV_WIDTH = 3 * DN_WIDTH
DN_CONV = 4
DN_CHUNK = 64
SWA_HEADS = 8
SWA_KV_HEADS = 2
SWA_HEAD_DIM = 64
SWA_WIDTH = SWA_HEADS * SWA_HEAD_DIM
SWA_KV_WIDTH = SWA_KV_HEADS * SWA_HEAD_DIM
WINDOW = 128
SWA_BLOCK = 128
REL_BUCKETS = 32
REL_MAX_DIST = 128
D_FF = 2816
EPS = 1e-6

LANES = 128
SUBLANES = 8
MXU_WIDTH = 256
VMEM_LIMIT = 58 * 1024 * 1024

W_HEAD = DN_QKV_WIDTH + DN_WIDTH
W_BA = 2 * DN_HEADS
W_SWA = SWA_WIDTH + 2 * SWA_KV_WIDTH
W_GATES = 2 * D_MODEL
D_IN = W_HEAD + W_BA + W_SWA + W_GATES
SWA_PIECES = W_SWA // MXU_WIDTH
GATE_PIECES = W_GATES // MXU_WIDTH
QKV_BLOCKS = DN_QKV_WIDTH // DN_HEAD_DIM

F32 = jnp.float32
BF16 = jnp.bfloat16


def _mm(a, b, dims=(((1,), (0,)), ((), ()))):
    return lax.dot_general(a.astype(BF16), b.astype(BF16), dims, preferred_element_type=F32)


_NT = (((1,), (1,)), ((), ()))
_TN = (((0,), (0,)), ((), ()))


def _sigmoid(x):
    return 0.5 * jnp.tanh(0.5 * x) + 0.5


def _silu(x):
    h = 0.5 * x
    return h * jnp.tanh(h) + h


CONV_TAIL = SUBLANES
CONV_ROWS = 128


W_CAST_ROWS = 256


def _in_proj_kernel(sinks_ref, x_ref, gain_ref, wt_ref, bias_ref, qgain_ref, kgain_ref,
                    qkv_ref, z_ref, yswa_ref, gates_ref, ba_ref, w_head, w_ba, w_rest, band_ref,
                    *, tiles_per_seq):
    i = pl.program_id(0)

    @pl.when(i == 0)
    def _():
        band_ref[...] = jnp.zeros_like(band_ref)
        for r0 in range(0, W_HEAD, W_CAST_ROWS):
            w_head[r0:r0 + W_CAST_ROWS, :] = wt_ref[r0:r0 + W_CAST_ROWS, :].astype(BF16)
        w_ba[...] = jnp.zeros_like(w_ba)
        w_ba[0:W_BA, :] = wt_ref[W_HEAD:W_HEAD + W_BA, :].astype(BF16)
        base = W_HEAD + W_BA
        for r0 in range(0, W_SWA + W_GATES, W_CAST_ROWS):
            w_rest[r0:r0 + W_CAST_ROWS, :] = wt_ref[base + r0:base + r0 + W_CAST_ROWS, :].astype(BF16)

    x = x_ref[...]
    ms = jnp.mean(x * x, axis=-1, keepdims=True)
    h = (x * lax.rsqrt(ms + EPS) * gain_ref[...]).astype(BF16)
    proj = lambda w: lax.dot_general(h, w, _NT, preferred_element_type=F32)

    def dn_qkv():
        raw = proj(w_head[0:DN_QKV_WIDTH, :])
        for c in range(QKV_BLOCKS):
            qkv_ref[c] = raw[:, c * DN_HEAD_DIM:(c + 1) * DN_HEAD_DIM].astype(qkv_ref.dtype)

    def dn_z():
        z_ref[...] = proj(w_head[DN_QKV_WIDTH:W_HEAD, :]).astype(z_ref.dtype)

    def gate_logits():
        for p in range(GATE_PIECES):
            r0 = W_SWA + p * MXU_WIDTH
            gates_ref[p] = proj(w_rest[r0:r0 + MXU_WIDTH, :]).astype(gates_ref.dtype)

    def beta_decay():
        ba_ref[...] = proj(w_ba[...])

    sw = [proj(w_rest[p * MXU_WIDTH:(p + 1) * MXU_WIDTH, :]) for p in range(SWA_PIECES)]
    slots_per_piece = MXU_WIDTH // LANES
    q_slots = [sw[j // slots_per_piece][:, (j % slots_per_piece) * LANES:
                                        (j % slots_per_piece + 1) * LANES] for j in range(SWA_SLOTS)]
    _swa_attend(i % tiles_per_seq == 0, q_slots, sw[SWA_WIDTH // MXU_WIDTH], sinks_ref, bias_ref,
                qgain_ref, kgain_ref, band_ref, yswa_ref,
                between=(dn_qkv, dn_z, gate_logits, beta_decay))


BF16_ROWS = 2 * SUBLANES


def _cast_specs(weights, steps):
    specs = []
    for w in weights:
        blocks = next(b for b in range(steps, 0, -1)
                      if steps % b == 0 and w.shape[0] % (b * BF16_ROWS) == 0)
        repeat = steps // blocks
        specs.append(pl.BlockSpec((w.shape[0] // blocks, w.shape[1]),
                                  lambda *idx, repeat=repeat: (idx[-1] // repeat, 0)))
    return specs


def _in_proj(x2, gain, w_t, swa_params, seq_len):
    sinks, bias_tabs, qgain, kgain = swa_params
    T = x2.shape[0]
    tm = SWA_ROWS
    steps = T // tm
    const = lambda shape: pl.BlockSpec(shape, lambda i, s: (0,) * len(shape),
                                       pipeline_mode=pl.Buffered(1))
    grid_spec = pltpu.PrefetchScalarGridSpec(
        num_scalar_prefetch=1,
        grid=(steps,),
        in_specs=[
            pl.BlockSpec((tm, D_MODEL), lambda i, s: (i, 0)),
            const((1, D_MODEL)),
            const((D_IN, D_MODEL)),
            const((2, SWA_HEADS, SWA_BLOCK, SWA_BAND)),
            const((1, SWA_WIDTH)),
            const((1, SWA_KV_WIDTH)),
        ],
        out_specs=[
            pl.BlockSpec((QKV_BLOCKS, tm, DN_HEAD_DIM), lambda i, s: (0, i, 0)),
            pl.BlockSpec((tm, DN_WIDTH), lambda i, s: (i, 0)),
            pl.BlockSpec((tm, SWA_WIDTH), lambda i, s: (i, 0)),
            pl.BlockSpec((GATE_PIECES, tm, MXU_WIDTH), lambda i, s: (0, i, 0)),
            pl.BlockSpec((tm, LANES), lambda i, s: (i, 0)),
        ],
        scratch_shapes=[pltpu.VMEM((W_HEAD, D_MODEL), BF16),
                        pltpu.VMEM((LANES, D_MODEL), BF16),
                        pltpu.VMEM((W_SWA + W_GATES, D_MODEL), BF16),
                        pltpu.VMEM((SWA_BLOCK + SWA_ROWS, 2 * SWA_KV_WIDTH), F32)],
    )
    return pl.pallas_call(
        functools.partial(_in_proj_kernel, tiles_per_seq=seq_len // tm),
        grid_spec=grid_spec,
        out_shape=[
            jax.ShapeDtypeStruct((QKV_BLOCKS, T, DN_HEAD_DIM), F32),
            jax.ShapeDtypeStruct((T, DN_WIDTH), F32),
            jax.ShapeDtypeStruct((T, SWA_WIDTH), BF16),
            jax.ShapeDtypeStruct((GATE_PIECES, T, MXU_WIDTH), BF16),
            jax.ShapeDtypeStruct((T, LANES), F32),
        ],
        compiler_params=pltpu.CompilerParams(
            dimension_semantics=("arbitrary",), vmem_limit_bytes=VMEM_LIMIT),
        name="in_proj",
    )(sinks, x2, gain, w_t, bias_tabs, qgain, kgain)


DN_TILE = 128
DN_NCH = DN_TILE // DN_CHUNK
DN_LOG_CHUNK = int(math.log2(DN_CHUNK))
DN_STREAMS = 4


def _deltanet_kernel(raw_ref, raw_next_ref, z_ref, ba_ref, convw_ref, alog_ref, dtb_ref, gain_ref,
                     *refs, n_cast):
    cast_src = refs[:n_cast]
    o_ref = refs[n_cast]
    cast_dst = refs[n_cast + 1:2 * n_cast + 1]
    state, xbuf, qkv_ref, qkv_next_ref = refs[2 * n_cast + 1:]
    for src, dst in zip(cast_src, cast_dst):
        dst[...] = src[...].astype(BF16)
    t = pl.program_id(1)
    scale = DN_HEAD_DIM ** -0.5
    streams = range(DN_STREAMS)

    def preprocess(s, dst_ref):
        for c in range(QKV_BLOCKS):
            w = convw_ref[c]
            for r0 in range(0, DN_TILE, CONV_ROWS):
                acc = w[DN_CONV - 1:DN_CONV, :] * xbuf[s, c, CONV_TAIL + r0:CONV_TAIL + r0 + CONV_ROWS, :]
                for j in range(DN_CONV - 1):
                    off = CONV_TAIL - (DN_CONV - 1) + j + r0
                    acc = acc + w[j:j + 1, :] * xbuf[s, c, off:off + CONV_ROWS, :]
                act = _silu(acc)
                if c < 2 * DN_HEADS:
                    norm = lax.rsqrt(jnp.sum(act * act, axis=-1, keepdims=True) + EPS)
                    act = act * (norm * scale if c < DN_HEADS else norm)
                dst_ref[s, c, r0:r0 + CONV_ROWS, :] = act.astype(BF16)

    @pl.when(t == 0)
    def _():
        state[...] = jnp.zeros_like(state)
        for s in streams:
            xbuf[s, :, 0:CONV_TAIL, :] = jnp.zeros((QKV_BLOCKS, CONV_TAIL, DN_HEAD_DIM), F32)
            xbuf[s, :, CONV_TAIL:, :] = raw_ref[:, s]
            preprocess(s, qkv_ref)

    for s in streams:
        xbuf[s, :, 0:CONV_TAIL, :] = raw_ref[:, s, DN_TILE - CONV_TAIL:DN_TILE, :]
        xbuf[s, :, CONV_TAIL:, :] = raw_next_ref[:, s]
        preprocess(s, qkv_next_ref)

    row = lax.broadcasted_iota(jnp.int32, (DN_TILE, DN_TILE), 0)
    col = lax.broadcasted_iota(jnp.int32, (DN_TILE, DN_TILE), 1)
    same_chunk = (row >> DN_LOG_CHUNK) == (col >> DN_LOG_CHUNK)
    lower_incl = same_chunk & (row >= col)
    strict_lower = same_chunk & (row > col)
    cumsum_mat = jnp.where(lower_incl, 1.0, 0.0).astype(BF16)
    eye = jnp.where(row == col, 1.0, 0.0)

    beta, gc, rev, gc_t = [], [], [], []
    for s in streams:
        ba = ba_ref[s]
        beta.append(_sigmoid(ba))
        xs = ba + dtb_ref[...]
        softplus = jnp.maximum(xs, 0.0) + jnp.log1p(jnp.exp(-jnp.abs(xs)))
        g = -jnp.exp(alog_ref[...]) * softplus
        g_hi = g.astype(BF16)
        g_lo = (g - g_hi.astype(F32)).astype(BF16)
        gc2 = jnp.dot(cumsum_mat, jnp.concatenate([g_hi, g_lo], axis=1),
                      preferred_element_type=F32)
        gc.append(gc2[:, :LANES] + gc2[:, LANES:])
        g_last = jnp.concatenate(
            [jnp.broadcast_to(gc[s][(c + 1) * DN_CHUNK - 1:(c + 1) * DN_CHUNK, :], (DN_CHUNK, LANES))
             for c in range(DN_NCH)], axis=0)
        rev.append(g_last - gc[s])
        gc_t.append(gc[s].T)

    units = [(s, h) for s in streams for h in range(DN_HEADS)]
    gcol = lambda arr, h: arr[:, DN_HEADS + h:DN_HEADS + h + 1]
    q, k, kb, rhs = {}, {}, {}, {}
    for s, h in units:
        q[s, h] = qkv_ref[s, h].astype(F32)
        k[s, h] = qkv_ref[s, DN_HEADS + h].astype(F32)
        vh = qkv_ref[s, 2 * DN_HEADS + h].astype(F32)
        beta_c = beta[s][:, h:h + 1]
        kb[s, h] = k[s, h] * beta_c
        rhs[s, h] = jnp.concatenate([vh * beta_c, kb[s, h] * jnp.exp(gcol(gc[s], h))], axis=1)
    gram = {p: _mm(jnp.concatenate([kb[p], q[p]], axis=0), k[p], _NT) for p in units}
    qk, n_pow, inv = {}, {}, {}
    for s, h in units:
        gc_r = gc_t[s][DN_HEADS + h:DN_HEADS + h + 1, :]
        decay = jnp.exp(jnp.where(lower_incl, gcol(gc[s], h) - gc_r, -jnp.inf))
        qk[s, h] = gram[s, h][DN_TILE:] * decay
        n_pow[s, h] = -jnp.where(strict_lower, gram[s, h][:DN_TILE] * decay, 0.0)
        inv[s, h] = eye + n_pow[s, h]
    for _ in range(DN_LOG_CHUNK - 1):
        n_pow = {p: _mm(n_pow[p], n_pow[p]) for p in units}
        inv = {p: inv[p] + _mm(inv[p], n_pow[p]) for p in units}
    sol = {p: _mm(inv[p], rhs[p]) for p in units}
    u = {p: sol[p][:, :DN_HEAD_DIM] for p in units}
    wmat = {p: sol[p][:, DN_HEAD_DIM:] for p in units}
    qe = {(s, h): q[s, h] * jnp.exp(gcol(gc[s], h)) for s, h in units}
    kdec = {(s, h): k[s, h] * jnp.exp(gcol(rev[s], h)) for s, h in units}

    st = {(s, h): state[s, h] for s, h in units}
    v_new = {p: [] for p in units}
    o_inter = {p: [] for p in units}
    for c in range(DN_NCH):
        r0, r1 = c * DN_CHUNK, (c + 1) * DN_CHUNK
        ws = {p: _mm(jnp.concatenate([wmat[p][r0:r1], qe[p][r0:r1]], axis=0), st[p]) for p in units}
        for p in units:
            v_new[p].append(u[p][r0:r1] - ws[p][:DN_CHUNK])
            o_inter[p].append(ws[p][DN_CHUNK:])
        upd = {p: _mm(kdec[p][r0:r1], v_new[p][c], _TN) for p in units}
        st = {(s, h): st[s, h] * jnp.exp(gc[s][r1 - 1:r1, DN_HEADS + h:DN_HEADS + h + 1]) + upd[s, h]
              for s, h in units}
    o_intra = {p: _mm(qk[p], jnp.concatenate(v_new[p], axis=0)) for p in units}
    for s, h in units:
        state[s, h] = st[s, h]
        o = jnp.concatenate(o_inter[s, h], axis=0) + o_intra[s, h]
        o = o * lax.rsqrt(jnp.mean(o * o, axis=-1, keepdims=True) + EPS) * gain_ref[...]
        zh = z_ref[s, :, h * DN_HEAD_DIM:(h + 1) * DN_HEAD_DIM].astype(F32)
        o_ref[s, :, h * DN_HEAD_DIM:(h + 1) * DN_HEAD_DIM] = (o * _silu(zh)).astype(o_ref.dtype)
    qkv_ref[...] = qkv_next_ref[...]


def _deltanet(raw, z, ba, convw, alog_lane, dtb_lane, out_gain, later_weights):
    B, S, _ = z.shape
    tiles = S // DN_TILE
    raw = raw.reshape(QKV_BLOCKS, B, S, DN_HEAD_DIM)
    raw_block = (QKV_BLOCKS, DN_STREAMS, DN_TILE, DN_HEAD_DIM)
    qkv_block = (DN_STREAMS, QKV_BLOCKS, DN_TILE, DN_HEAD_DIM)
    cast_specs = _cast_specs(later_weights, tiles)
    outs = pl.pallas_call(
        functools.partial(_deltanet_kernel, n_cast=len(later_weights)),
        grid=(B // DN_STREAMS, tiles),
        in_specs=[
            pl.BlockSpec(raw_block, lambda b, t: (0, b, t, 0)),
            pl.BlockSpec(raw_block, lambda b, t: (0, b, jnp.minimum(t + 1, tiles - 1), 0)),
            pl.BlockSpec((DN_STREAMS, DN_TILE, DN_WIDTH), lambda b, t: (b, t, 0)),
            pl.BlockSpec((DN_STREAMS, DN_TILE, LANES), lambda b, t: (b, t, 0)),
            pl.BlockSpec((QKV_BLOCKS, DN_CONV, DN_HEAD_DIM), lambda b, t: (0, 0, 0)),
            pl.BlockSpec((1, LANES), lambda b, t: (0, 0)),
            pl.BlockSpec((1, LANES), lambda b, t: (0, 0)),
            pl.BlockSpec((1, DN_HEAD_DIM), lambda b, t: (0, 0)),
        ] + cast_specs,
        out_specs=[pl.BlockSpec((DN_STREAMS, DN_TILE, DN_WIDTH), lambda b, t: (b, t, 0))] + cast_specs,
        out_shape=[jax.ShapeDtypeStruct((B, S, DN_WIDTH), BF16)]
        + [jax.ShapeDtypeStruct(w.shape, BF16) for w in later_weights],
        scratch_shapes=[
            pltpu.VMEM((DN_STREAMS, DN_HEADS, DN_HEAD_DIM, DN_HEAD_DIM), F32),
            pltpu.VMEM((DN_STREAMS, QKV_BLOCKS, CONV_TAIL + DN_TILE, DN_HEAD_DIM), F32),
            pltpu.VMEM(qkv_block, BF16), pltpu.VMEM(qkv_block, BF16)],
        compiler_params=pltpu.CompilerParams(
            dimension_semantics=("arbitrary", "arbitrary"), vmem_limit_bytes=VMEM_LIMIT),
        name="deltanet",
    )(raw, raw, z, ba, convw, alog_lane, dtb_lane, out_gain, *later_weights)
    return outs[0], outs[1:]


SWA_BAND = 2 * SWA_BLOCK
SWA_SLOTS = SWA_WIDTH // LANES
SWA_HEADS_PER_SLOT = LANES // SWA_HEAD_DIM


SWA_STEP = 4
SWA_ROWS = SWA_STEP * SWA_BLOCK


def _swa_attend(first, q_slots, kv, sinks_ref, bias_ref, qgain_ref, kgain_ref, band_ref, o_ref,
                between):
    k = kv[:, :SWA_KV_WIDTH]
    between[0]()
    half = lax.broadcasted_iota(jnp.int32, (SWA_ROWS, LANES), 1) < SWA_HEAD_DIM

    def head_norm(p):
        p2 = p * p
        s_lo = jnp.sum(jnp.where(half, p2, 0.0), axis=-1, keepdims=True)
        s_hi = jnp.sum(jnp.where(half, 0.0, p2), axis=-1, keepdims=True)
        r_lo = lax.rsqrt(s_lo * (1.0 / SWA_HEAD_DIM) + EPS)
        r_hi = lax.rsqrt(s_hi * (1.0 / SWA_HEAD_DIM) + EPS)
        return p * jnp.where(half, r_lo, r_hi)

    qn = [head_norm(q_slots[j]) * qgain_ref[:, j * LANES:(j + 1) * LANES]
          for j in range(SWA_SLOTS)]
    kn = head_norm(k) * kgain_ref[...]

    band_ref[0:SWA_BLOCK, :] = jnp.where(first, 0.0, band_ref[SWA_ROWS:SWA_ROWS + SWA_BLOCK, :])
    band_ref[SWA_BLOCK:, :SWA_KV_WIDTH] = kn
    band_ref[SWA_BLOCK:, SWA_KV_WIDTH:] = kv[:, SWA_KV_WIDTH:]
    k = band_ref[:, :SWA_KV_WIDTH]
    v = band_ref[:, SWA_KV_WIDTH:]

    lane = lax.broadcasted_iota(jnp.int32, (SWA_BLOCK + SWA_ROWS, LANES), 1)
    low = lane < SWA_HEAD_DIM
    k_rot = pltpu.roll(k, SWA_HEAD_DIM, axis=1)
    v_rot = pltpu.roll(v, SWA_HEAD_DIM, axis=1)
    k_lo = (jnp.where(low, k, 0.0).astype(BF16), jnp.where(low, k_rot, 0.0).astype(BF16))
    k_hi = (jnp.where(low, 0.0, k_rot).astype(BF16), jnp.where(low, 0.0, k).astype(BF16))
    v_lo = (jnp.where(low, v, 0.0).astype(BF16), jnp.where(low, v_rot, 0.0).astype(BF16))
    v_hi = (jnp.where(low, 0.0, v_rot).astype(BF16), jnp.where(low, 0.0, v).astype(BF16))

    out_low = lax.broadcasted_iota(jnp.int32, (SWA_BLOCK, LANES), 1) < SWA_HEAD_DIM

    slots_per_kv = SWA_SLOTS // SWA_KV_HEADS
    units = [(u, j) for u in range(SWA_STEP) for j in range(SWA_SLOTS)]
    band_of = lambda arr, u: arr[u * SWA_BLOCK:u * SWA_BLOCK + SWA_BAND]
    k_bd = {(u, kh): jnp.concatenate([band_of(k_lo[kh], u), band_of(k_hi[kh], u)], axis=0)
            for u in range(SWA_STEP) for kh in range(SWA_KV_HEADS)}
    v_bd = {(u, kh): jnp.concatenate([band_of(v_lo[kh], u), band_of(v_hi[kh], u)], axis=0)
            for u in range(SWA_STEP) for kh in range(SWA_KV_HEADS)}
    qn = [qn[j].astype(BF16) for j in range(SWA_SLOTS)]
    between[1]()
    logits = {(u, j): _mm(qn[j][u * SWA_BLOCK:(u + 1) * SWA_BLOCK], k_bd[u, j // slots_per_kv], _NT)
              for u, j in units}
    first_tab = jnp.where(first, 0, 1)
    probs = {}
    inv_den = {}
    for u, j in units:
        for r in range(SWA_HEADS_PER_SLOT):
            hd = j * SWA_HEADS_PER_SLOT + r
            bias = bias_ref[first_tab, hd] if u == 0 else bias_ref[1, hd]
            lg = logits[u, j][:, r * SWA_BAND:(r + 1) * SWA_BAND] + bias
            sink = sinks_ref[hd]
            m = jnp.maximum(jnp.max(lg, axis=-1, keepdims=True), sink)
            p = jnp.exp(lg - m)
            den = jnp.sum(p, axis=-1, keepdims=True) + jnp.exp(sink - m)
            probs[u, j, r] = p.astype(BF16)
            inv_den[u, j, r] = 1.0 / den
    between[2]()
    outs = {(u, j): _mm(jnp.concatenate([probs[u, j, r] for r in range(SWA_HEADS_PER_SLOT)],
                                        axis=1), v_bd[u, j // slots_per_kv])
            for u, j in units}
    between[3]()
    for u, j in units:
        o_ref[u * SWA_BLOCK:(u + 1) * SWA_BLOCK, j * LANES:(j + 1) * LANES] = (
            outs[u, j] * jnp.where(out_low, inv_den[u, j, 0], inv_den[u, j, 1])).astype(o_ref.dtype)


def _merge_ffn_kernel(x_ref, ydn_ref, yswa_ref, g_ref, wa_ref, wb_ref, wo_ref, gain_ref,
                      wg_ref, wu_ref, wd_ref, o_ref, act_ref):
    per_branch = D_MODEL // MXU_WIDTH
    ga, gb = [_sigmoid(jnp.concatenate(
        [g_ref[br * per_branch + p].astype(F32) for p in range(per_branch)], axis=1))
        for br in range(2)]
    merged = ga * _mm(ydn_ref[...], wa_ref[...]) + gb * _mm(yswa_ref[...], wb_ref[...])
    x1 = x_ref[...] + _mm(merged, wo_ref[...])
    ms = jnp.mean(x1 * x1, axis=-1, keepdims=True)
    h2 = (x1 * lax.rsqrt(ms + EPS) * gain_ref[...]).astype(BF16)
    for c0 in range(0, D_FF, MXU_WIDTH):
        gate = jnp.dot(h2, wg_ref[:, c0:c0 + MXU_WIDTH], preferred_element_type=F32)
        up = jnp.dot(h2, wu_ref[:, c0:c0 + MXU_WIDTH], preferred_element_type=F32)
        act_ref[:, c0:c0 + MXU_WIDTH] = (_silu(gate) * up).astype(BF16)
    o_ref[...] = x1 + jnp.dot(act_ref[...], wd_ref[...], preferred_element_type=F32)


def _merge_ffn(x2, ydn, yswa, graw, wa, wb, wo, gain, wg, wu, wd, tm):
    T = x2.shape[0]
    row = lambda w: pl.BlockSpec((tm, w), lambda i: (i, 0))
    once = lambda a, b: pl.BlockSpec((a, b), lambda i: (0, 0), pipeline_mode=pl.Buffered(1))
    return pl.pallas_call(
        _merge_ffn_kernel,
        grid=(T // tm,),
        in_specs=[row(D_MODEL), row(DN_WIDTH), row(SWA_WIDTH),
                  pl.BlockSpec((GATE_PIECES, tm, MXU_WIDTH), lambda i: (0, i, 0)),
                  once(DN_WIDTH, D_MODEL), once(SWA_WIDTH, D_MODEL), once(D_MODEL, D_MODEL),
                  once(1, D_MODEL),
                  once(D_MODEL, D_FF), once(D_MODEL, D_FF), once(D_FF, D_MODEL)],
        out_specs=row(D_MODEL),
        out_shape=jax.ShapeDtypeStruct((T, D_MODEL), F32),
        scratch_shapes=[pltpu.VMEM((tm, D_FF), BF16)],
        compiler_params=pltpu.CompilerParams(
            dimension_semantics=("arbitrary",), vmem_limit_bytes=VMEM_LIMIT),
        name="merge_ffn",
    )(x2, ydn, yswa, graw, wa, wb, wo, gain, wg, wu, wd)


def _t5_bucket_table():
    qi = jnp.arange(SWA_BLOCK)[:, None]
    kj = jnp.arange(SWA_BAND)[None, :]
    dist = SWA_BLOCK + qi - kj
    in_window = (dist >= 0) & (dist < WINDOW)
    n = jnp.maximum(dist, 0)
    max_exact = REL_BUCKETS // 2
    nf = jnp.maximum(n, 1).astype(F32)
    large = max_exact + (jnp.log(nf / max_exact) / math.log(REL_MAX_DIST / max_exact)
                         * (REL_BUCKETS - max_exact)).astype(jnp.int32)
    large = jnp.minimum(large, REL_BUCKETS - 1)
    return jnp.where(n < max_exact, n, large), in_window


def kernel(x, attn_norm, w_in, dn_conv, dn_a_log, dn_dt_bias, dn_out_norm, swa_q_norm, swa_k_norm, swa_sinks, rel_bias, w_branch_dn, w_branch_swa, w_out, ffn_norm, w_gate, w_up, w_down):
    B, S, D = x.shape
    T = B * S
    depth = w_in.shape[0]
    bucket, in_window = _t5_bucket_table()
    rel = rel_bias.astype(F32)
    bias_tab = sum(jnp.where(bucket[None] == b, rel[b][:, None, None], 0.0)
                   for b in range(REL_BUCKETS))
    bias_tab = jnp.where(in_window[None], bias_tab, -jnp.inf)
    has_prev = jnp.arange(SWA_BAND)[None, None, :] >= SWA_BLOCK
    bias_tabs = jnp.stack([jnp.where(has_prev, bias_tab, -jnp.inf), bias_tab])

    x2 = x.reshape(T, D)
    for l in range(depth):
        convw = dn_conv[l].astype(F32).reshape(DN_CONV, QKV_BLOCKS, DN_HEAD_DIM).transpose(1, 0, 2)

        later = [w[l].astype(F32) for w in (w_branch_dn, w_branch_swa, w_out, w_gate, w_up, w_down)]
        qgain = jnp.tile(swa_q_norm[l].astype(F32), SWA_HEADS)[None, :] * (SWA_HEAD_DIM ** -0.5)
        kgain = jnp.tile(swa_k_norm[l].astype(F32), SWA_KV_HEADS)[None, :]
        swa_params = (swa_sinks[l].astype(F32), bias_tabs, qgain, kgain)
        qkv, z, y_swa, gates, ba = _in_proj(
            x2, attn_norm[l][None, :], jnp.swapaxes(w_in[l], 0, 1).astype(F32), swa_params,
            seq_len=S)

        pad_lo = jnp.zeros((DN_HEADS,), F32)
        pad_hi = jnp.zeros((LANES - 2 * DN_HEADS,), F32)
        alog_lane = jnp.concatenate([pad_lo, dn_a_log[l].astype(F32), pad_hi])[None, :]
        dtb_lane = jnp.concatenate([pad_lo, dn_dt_bias[l].astype(F32), pad_hi])[None, :]
        y_dn, later_bf16 = _deltanet(qkv, z.reshape(B, S, -1), ba.reshape(B, S, -1), convw,
                                     alog_lane, dtb_lane, dn_out_norm[l][None, :], later)

        wa, wb, wo, wg, wu, wd = later_bf16
        x2 = _merge_ffn(x2, y_dn.reshape(T, -1), y_swa, gates,
                        wa, wb, wo, ffn_norm[l][None, :], wg, wu, wd, tm=512)
    return x2.reshape(B, S, D)
```

```python
import functools
import math

import jax
import jax.numpy as jnp
from jax import lax
from jax.experimental import pallas as pl
from jax.experimental.pallas import tpu as pltpu

D_MODEL = 1024
DN_HEADS = 4
DN_HEAD_DIM = 128
DN_WIDTH = DN_HEADS * DN_HEAD_DIM
DN_QKV_WIDTH = 3 * DN_WIDTH
DN_CONV = 4
DN_CHUNK = 64
SWA_HEADS = 8
SWA_KV_HEADS = 2
SWA_HEAD_DIM = 64
SWA_WIDTH = SWA_HEADS * SWA_HEAD_DIM
SWA_KV_WIDTH = SWA_KV_HEADS * SWA_HEAD_DIM
WINDOW = 128
SWA_BLOCK = 128
REL_BUCKETS = 32
REL_MAX_DIST = 128
D_FF = 2816
EPS = 1e-6

LANES = 128
SUBLANES = 8
MXU_WIDTH = 256
VMEM_LIMIT = 58 * 1024 * 1024

W_HEAD = DN_QKV_WIDTH + DN_WIDTH
W_BA = 2 * DN_HEADS
W_SWA = SWA_WIDTH + 2 * SWA_KV_WIDTH
W_GATES = 2 * D_MODEL
D_IN = W_HEAD + W_BA + W_SWA + W_GATES
SWA_PIECES = W_SWA // MXU_WIDTH
GATE_PIECES = W_GATES // MXU_WIDTH
QKV_BLOCKS = DN_QKV_WIDTH // DN_HEAD_DIM

F32 = jnp.float32
BF16 = jnp.bfloat16


def _mm(a, b, dims=(((1,), (0,)), ((), ()))):
    return lax.dot_general(a.astype(BF16), b.astype(BF16), dims, preferred_element_type=F32)


_NT = (((1,), (1,)), ((), ()))
_TN = (((0,), (0,)), ((), ()))


def _sigmoid(x):
    return 0.5 * jnp.tanh(0.5 * x) + 0.5


def _silu(x):
    h = 0.5 * x
    return h * jnp.tanh(h) + h


W_CAST_ROWS = 256


def _in_proj_kernel(sinks_ref, x_ref, gain_ref, wt_ref, bias_ref, qgain_ref, kgain_ref, convw_ref,
                    qkv_ref, z_ref, yswa_ref, gates_ref, ba_ref, w_head, w_ba, w_rest, band_ref,
                    xbuf, *, tiles_per_seq):
    i = pl.program_id(0)
    tm = x_ref.shape[0]
    first = i % tiles_per_seq == 0

    @pl.when(i == 0)
    def _():
        band_ref[...] = jnp.zeros_like(band_ref)
        xbuf[:, tm:, :] = jnp.zeros((QKV_BLOCKS, CONV_TAIL, DN_HEAD_DIM), F32)
        for r0 in range(0, W_HEAD, W_CAST_ROWS):
            w_head[r0:r0 + W_CAST_ROWS, :] = wt_ref[r0:r0 + W_CAST_ROWS, :].astype(BF16)
        w_ba[...] = jnp.zeros_like(w_ba)
        w_ba[0:W_BA, :] = wt_ref[W_HEAD:W_HEAD + W_BA, :].astype(BF16)
        base = W_HEAD + W_BA
        for r0 in range(0, W_SWA + W_GATES, W_CAST_ROWS):
            w_rest[r0:r0 + W_CAST_ROWS, :] = wt_ref[base + r0:base + r0 + W_CAST_ROWS, :].astype(BF16)

    x = x_ref[...]
    ms = jnp.mean(x * x, axis=-1, keepdims=True)
    h = (x * lax.rsqrt(ms + EPS) * gain_ref[...]).astype(BF16)
    proj = lambda w: lax.dot_general(h, w, _NT, preferred_element_type=F32)

    def dn_qkv():
        raw = proj(w_head[0:DN_QKV_WIDTH, :])
        xbuf[:, 0:CONV_TAIL, :] = jnp.where(first, 0.0, xbuf[:, tm:, :])
        for c in range(QKV_BLOCKS):
            xbuf[c, CONV_TAIL:, :] = raw[:, c * DN_HEAD_DIM:(c + 1) * DN_HEAD_DIM]

    def dn_preprocess():
        scale = DN_HEAD_DIM ** -0.5
        for c in range(QKV_BLOCKS):
            w = convw_ref[c]
            for r0 in range(0, tm, CONV_ROWS):
                acc = w[DN_CONV - 1:DN_CONV, :] * xbuf[c, CONV_TAIL + r0:CONV_TAIL + r0 + CONV_ROWS, :]
                for j in range(DN_CONV - 1):
                    off = CONV_TAIL - (DN_CONV - 1) + j + r0
                    acc = acc + w[j:j + 1, :] * xbuf[c, off:off + CONV_ROWS, :]
                act = _silu(acc)
                if c < 2 * DN_HEADS:
                    norm = lax.rsqrt(jnp.sum(act * act, axis=-1, keepdims=True) + EPS)
                    act = act * (norm * scale if c < DN_HEADS else norm)
                qkv_ref[c, r0:r0 + CONV_ROWS, :] = act.astype(qkv_ref.dtype)

    def dn_z():
        z_ref[...] = proj(w_head[DN_QKV_WIDTH:W_HEAD, :]).astype(z_ref.dtype)

    def gates_and_rest():
        for p in range(GATE_PIECES):
            r0 = W_SWA + p * MXU_WIDTH
            gates_ref[p] = proj(w_rest[r0:r0 + MXU_WIDTH, :]).astype(gates_ref.dtype)
        dn_preprocess()
        ba_ref[...] = proj(w_ba[...])

    sw = [proj(w_rest[p * MXU_WIDTH:(p + 1) * MXU_WIDTH, :]) for p in range(SWA_PIECES)]
    slots_per_piece = MXU_WIDTH // LANES
    q_slots = [sw[j // slots_per_piece][:, (j % slots_per_piece) * LANES:
                                        (j % slots_per_piece + 1) * LANES] for j in range(SWA_SLOTS)]
    _swa_attend(first, q_slots, sw[SWA_WIDTH // MXU_WIDTH], sinks_ref, bias_ref,
                qgain_ref, kgain_ref, band_ref, yswa_ref,
                between=(dn_z, lambda: None, dn_qkv, gates_and_rest))


BF16_ROWS = 2 * SUBLANES


def _cast_specs(weights, steps):
    specs = []
    for w in weights:
        blocks = next(b for b in range(steps, 0, -1)
                      if steps % b == 0 and w.shape[0] % (b * BF16_ROWS) == 0)
        repeat = steps // blocks
        specs.append(pl.BlockSpec((w.shape[0] // blocks, w.shape[1]),
                                  lambda *idx, repeat=repeat: (idx[-1] // repeat, 0)))
    return specs


def _in_proj(x2, gain, w_t, swa_params, convw, seq_len):
    sinks, bias_tabs, qgain, kgain = swa_params
    T = x2.shape[0]
    tm = SWA_ROWS
    steps = T // tm
    const = lambda shape: pl.BlockSpec(shape, lambda i, s: (0,) * len(shape),
                                       pipeline_mode=pl.Buffered(1))
    grid_spec = pltpu.PrefetchScalarGridSpec(
        num_scalar_prefetch=1,
        grid=(steps,),
        in_specs=[
            pl.BlockSpec((tm, D_MODEL), lambda i, s: (i, 0)),
            const((1, D_MODEL)),
            const((D_IN, D_MODEL)),
            const((2, SWA_HEADS, SWA_BLOCK, SWA_BAND)),
            const((1, SWA_WIDTH)),
            const((1, SWA_KV_WIDTH)),
            const((QKV_BLOCKS, DN_CONV, DN_HEAD_DIM)),
        ],
        out_specs=[
            pl.BlockSpec((QKV_BLOCKS, tm, DN_HEAD_DIM), lambda i, s: (0, i, 0)),
            pl.BlockSpec((tm, DN_WIDTH), lambda i, s: (i, 0)),
            pl.BlockSpec((tm, SWA_WIDTH), lambda i, s: (i, 0)),
            pl.BlockSpec((GATE_PIECES, tm, MXU_WIDTH), lambda i, s: (0, i, 0)),
            pl.BlockSpec((tm, LANES), lambda i, s: (i, 0)),
        ],
        scratch_shapes=[pltpu.VMEM((W_HEAD, D_MODEL), BF16),
                        pltpu.VMEM((LANES, D_MODEL), BF16),
                        pltpu.VMEM((W_SWA + W_GATES, D_MODEL), BF16),
                        pltpu.VMEM((SWA_BLOCK + SWA_ROWS, 2 * SWA_KV_WIDTH), F32),
                        pltpu.VMEM((QKV_BLOCKS, tm + CONV_TAIL, DN_HEAD_DIM), F32)],
    )
    return pl.pallas_call(
        functools.partial(_in_proj_kernel, tiles_per_seq=seq_len // tm),
        grid_spec=grid_spec,
        out_shape=[
            jax.ShapeDtypeStruct((QKV_BLOCKS, T, DN_HEAD_DIM), BF16),
            jax.ShapeDtypeStruct((T, DN_WIDTH), F32),
            jax.ShapeDtypeStruct((T, SWA_WIDTH), BF16),
            jax.ShapeDtypeStruct((GATE_PIECES, T, MXU_WIDTH), BF16),
            jax.ShapeDtypeStruct((T, LANES), F32),
        ],
        compiler_params=pltpu.CompilerParams(
            dimension_semantics=("arbitrary",), vmem_limit_bytes=VMEM_LIMIT),
        name="in_proj",
    )(sinks, x2, gain, w_t, bias_tabs, qgain, kgain, convw)


DN_TILE = 128
DN_NCH = DN_TILE // DN_CHUNK
DN_LOG_CHUNK = int(math.log2(DN_CHUNK))
DN_STREAMS = 4
CONV_TAIL = SUBLANES
CONV_ROWS = 128


def _deltanet_kernel(qkv_ref, z_ref, ba_ref, alog_ref, dtb_ref, gain_ref, *refs, n_cast):
    cast_src = refs[:n_cast]
    o_ref = refs[n_cast]
    cast_dst = refs[n_cast + 1:2 * n_cast + 1]
    (state,) = refs[2 * n_cast + 1:]
    for src, dst in zip(cast_src, cast_dst):
        dst[...] = src[...].astype(BF16)
    streams = range(DN_STREAMS)

    @pl.when(pl.program_id(1) == 0)
    def _():
        state[...] = jnp.zeros_like(state)

    row = lax.broadcasted_iota(jnp.int32, (DN_TILE, DN_TILE), 0)
    col = lax.broadcasted_iota(jnp.int32, (DN_TILE, DN_TILE), 1)
    same_chunk = (row >> DN_LOG_CHUNK) == (col >> DN_LOG_CHUNK)
    lower_incl = same_chunk & (row >= col)
    strict_lower = same_chunk & (row > col)
    cumsum_mat = jnp.where(lower_incl, 1.0, 0.0).astype(BF16)
    eye = jnp.where(row == col, 1.0, 0.0)

    beta, gc, rev, gc_t = [], [], [], []
    for s in streams:
        ba = ba_ref[s]
        beta.append(_sigmoid(ba))
        xs = ba + dtb_ref[...]
        softplus = jnp.maximum(xs, 0.0) + jnp.log1p(jnp.exp(-jnp.abs(xs)))
        g = -jnp.exp(alog_ref[...]) * softplus
        g_hi = g.astype(BF16)
        g_lo = (g - g_hi.astype(F32)).astype(BF16)
        gc2 = jnp.dot(cumsum_mat, jnp.concatenate([g_hi, g_lo], axis=1),
                      preferred_element_type=F32)
        gc.append(gc2[:, :LANES] + gc2[:, LANES:])
        g_last = jnp.concatenate(
            [jnp.broadcast_to(gc[s][(c + 1) * DN_CHUNK - 1:(c + 1) * DN_CHUNK, :], (DN_CHUNK, LANES))
             for c in range(DN_NCH)], axis=0)
        rev.append(g_last - gc[s])
        gc_t.append(gc[s].T)

    units = [(s, h) for s in streams for h in range(DN_HEADS)]
    gcol = lambda arr, h: arr[:, DN_HEADS + h:DN_HEADS + h + 1]
    q, k, kb, rhs = {}, {}, {}, {}
    for s, h in units:
        q[s, h] = qkv_ref[h, s].astype(F32)
        k[s, h] = qkv_ref[DN_HEADS + h, s].astype(F32)
        vh = qkv_ref[2 * DN_HEADS + h, s].astype(F32)
        beta_c = beta[s][:, h:h + 1]
        kb[s, h] = k[s, h] * beta_c
        rhs[s, h] = jnp.concatenate([vh * beta_c, kb[s, h] * jnp.exp(gcol(gc[s], h))], axis=1)
    gram = {p: _mm(jnp.concatenate([kb[p], q[p]], axis=0), k[p], _NT) for p in units}
    qk, n_pow, inv = {}, {}, {}
    for s, h in units:
        gc_r = gc_t[s][DN_HEADS + h:DN_HEADS + h + 1, :]
        decay = jnp.exp(jnp.where(lower_incl, gcol(gc[s], h) - gc_r, -jnp.inf))
        qk[s, h] = gram[s, h][DN_TILE:] * decay
        n_pow[s, h] = -jnp.where(strict_lower, gram[s, h][:DN_TILE] * decay, 0.0)
        inv[s, h] = eye + n_pow[s, h]
    for _ in range(DN_LOG_CHUNK - 1):
        n_pow = {p: _mm(n_pow[p], n_pow[p]) for p in units}
        inv = {p: inv[p] + _mm(inv[p], n_pow[p]) for p in units}
    sol = {p: _mm(inv[p], rhs[p]) for p in units}
    u = {p: sol[p][:, :DN_HEAD_DIM] for p in units}
    wmat = {p: sol[p][:, DN_HEAD_DIM:] for p in units}
    qe = {(s, h): q[s, h] * jnp.exp(gcol(gc[s], h)) for s, h in units}
    kdec = {(s, h): k[s, h] * jnp.exp(gcol(rev[s], h)) for s, h in units}

    st = {(s, h): state[s, h] for s, h in units}
    v_new = {p: [] for p in units}
    o_inter = {p: [] for p in units}
    for c in range(DN_NCH):
        r0, r1 = c * DN_CHUNK, (c + 1) * DN_CHUNK
        ws = {p: _mm(jnp.concatenate([wmat[p][r0:r1], qe[p][r0:r1]], axis=0), st[p]) for p in units}
        for p in units:
            v_new[p].append(u[p][r0:r1] - ws[p][:DN_CHUNK])
            o_inter[p].append(ws[p][DN_CHUNK:])
        upd = {p: _mm(kdec[p][r0:r1], v_new[p][c], _TN) for p in units}
        st = {(s, h): st[s, h] * jnp.exp(gc[s][r1 - 1:r1, DN_HEADS + h:DN_HEADS + h + 1]) + upd[s, h]
              for s, h in units}
    o_intra = {p: _mm(qk[p], jnp.concatenate(v_new[p], axis=0)) for p in units}
    for s, h in units:
        state[s, h] = st[s, h]
        o = jnp.concatenate(o_inter[s, h], axis=0) + o_intra[s, h]
        o = o * lax.rsqrt(jnp.mean(o * o, axis=-1, keepdims=True) + EPS) * gain_ref[...]
        zh = z_ref[s, :, h * DN_HEAD_DIM:(h + 1) * DN_HEAD_DIM].astype(F32)
        o_ref[s, :, h * DN_HEAD_DIM:(h + 1) * DN_HEAD_DIM] = (o * _silu(zh)).astype(o_ref.dtype)


def _deltanet(qkv, z, ba, alog_lane, dtb_lane, out_gain, later_weights):
    B, S, _ = z.shape
    tiles = S // DN_TILE
    qkv = qkv.reshape(QKV_BLOCKS, B, S, DN_HEAD_DIM)
    cast_specs = _cast_specs(later_weights, tiles)
    outs = pl.pallas_call(
        functools.partial(_deltanet_kernel, n_cast=len(later_weights)),
        grid=(B // DN_STREAMS, tiles),
        in_specs=[
            pl.BlockSpec((QKV_BLOCKS, DN_STREAMS, DN_TILE, DN_HEAD_DIM), lambda b, t: (0, b, t, 0)),
            pl.BlockSpec((DN_STREAMS, DN_TILE, DN_WIDTH), lambda b, t: (b, t, 0)),
            pl.BlockSpec((DN_STREAMS, DN_TILE, LANES), lambda b, t: (b, t, 0)),
            pl.BlockSpec((1, LANES), lambda b, t: (0, 0)),
            pl.BlockSpec((1, LANES), lambda b, t: (0, 0)),
            pl.BlockSpec((1, DN_HEAD_DIM), lambda b, t: (0, 0)),
        ] + cast_specs,
        out_specs=[pl.BlockSpec((DN_STREAMS, DN_TILE, DN_WIDTH), lambda b, t: (b, t, 0))] + cast_specs,
        out_shape=[jax.ShapeDtypeStruct((B, S, DN_WIDTH), BF16)]
        + [jax.ShapeDtypeStruct(w.shape, BF16) for w in later_weights],
        scratch_shapes=[pltpu.VMEM((DN_STREAMS, DN_HEADS, DN_HEAD_DIM, DN_HEAD_DIM), F32)],
        compiler_params=pltpu.CompilerParams(
            dimension_semantics=("arbitrary", "arbitrary"), vmem_limit_bytes=VMEM_LIMIT),
        name="deltanet",
    )(qkv, z, ba, alog_lane, dtb_lane, out_gain, *later_weights)
    return outs[0], outs[1:]


SWA_BAND = 2 * SWA_BLOCK
SWA_SLOTS = SWA_WIDTH // LANES
SWA_HEADS_PER_SLOT = LANES // SWA_HEAD_DIM


SWA_STEP = 4
SWA_ROWS = SWA_STEP * SWA_BLOCK


def _swa_attend(first, q_slots, kv, sinks_ref, bias_ref, qgain_ref, kgain_ref, band_ref, o_ref,
                between):
    k = kv[:, :SWA_KV_WIDTH]
    between[0]()
    half = lax.broadcasted_iota(jnp.int32, (SWA_ROWS, LANES), 1) < SWA_HEAD_DIM

    def head_norm(p):
        p2 = p * p
        s_lo = jnp.sum(jnp.where(half, p2, 0.0), axis=-1, keepdims=True)
        s_hi = jnp.sum(jnp.where(half, 0.0, p2), axis=-1, keepdims=True)
        r_lo = lax.rsqrt(s_lo * (1.0 / SWA_HEAD_DIM) + EPS)
        r_hi = lax.rsqrt(s_hi * (1.0 / SWA_HEAD_DIM) + EPS)
        return p * jnp.where(half, r_lo, r_hi)

    qn = [head_norm(q_slots[j]) * qgain_ref[:, j * LANES:(j + 1) * LANES]
          for j in range(SWA_SLOTS)]
    kn = head_norm(k) * kgain_ref[...]

    band_ref[0:SWA_BLOCK, :] = jnp.where(first, 0.0, band_ref[SWA_ROWS:SWA_ROWS + SWA_BLOCK, :])
    band_ref[SWA_BLOCK:, :SWA_KV_WIDTH] = kn
    band_ref[SWA_BLOCK:, SWA_KV_WIDTH:] = kv[:, SWA_KV_WIDTH:]
    k = band_ref[:, :SWA_KV_WIDTH]
    v = band_ref[:, SWA_KV_WIDTH:]

    lane = lax.broadcasted_iota(jnp.int32, (SWA_BLOCK + SWA_ROWS, LANES), 1)
    low = lane < SWA_HEAD_DIM
    k_rot = pltpu.roll(k, SWA_HEAD_DIM, axis=1)
    v_rot = pltpu.roll(v, SWA_HEAD_DIM, axis=1)
    k_lo = (jnp.where(low, k, 0.0).astype(BF16), jnp.where(low, k_rot, 0.0).astype(BF16))
    k_hi = (jnp.where(low, 0.0, k_rot).astype(BF16), jnp.where(low, 0.0, k).astype(BF16))
    v_lo = (jnp.where(low, v, 0.0).astype(BF16), jnp.where(low, v_rot, 0.0).astype(BF16))
    v_hi = (jnp.where(low, 0.0, v_rot).astype(BF16), jnp.where(low, 0.0, v).astype(BF16))

    out_low = lax.broadcasted_iota(jnp.int32, (SWA_BLOCK, LANES), 1) < SWA_HEAD_DIM

    slots_per_kv = SWA_SLOTS // SWA_KV_HEADS
    units = [(u, j) for u in range(SWA_STEP) for j in range(SWA_SLOTS)]
    band_of = lambda arr, u: arr[u * SWA_BLOCK:u * SWA_BLOCK + SWA_BAND]
    k_bd = {(u, kh): jnp.concatenate([band_of(k_lo[kh], u), band_of(k_hi[kh], u)], axis=0)
            for u in range(SWA_STEP) for kh in range(SWA_KV_HEADS)}
    v_bd = {(u, kh): jnp.concatenate([band_of(v_lo[kh], u), band_of(v_hi[kh], u)], axis=0)
            for u in range(SWA_STEP) for kh in range(SWA_KV_HEADS)}
    qn = [qn[j].astype(BF16) for j in range(SWA_SLOTS)]
    between[1]()
    logits = {(u, j): _mm(qn[j][u * SWA_BLOCK:(u + 1) * SWA_BLOCK], k_bd[u, j // slots_per_kv], _NT)
              for u, j in units}
    first_tab = jnp.where(first, 0, 1)
    probs = {}
    inv_den = {}
    for u, j in units:
        for r in range(SWA_HEADS_PER_SLOT):
            hd = j * SWA_HEADS_PER_SLOT + r
            bias = bias_ref[first_tab, hd] if u == 0 else bias_ref[1, hd]
            lg = logits[u, j][:, r * SWA_BAND:(r + 1) * SWA_BAND] + bias
            sink = sinks_ref[hd]
            m = jnp.maximum(jnp.max(lg, axis=-1, keepdims=True), sink)
            p = jnp.exp(lg - m)
            den = jnp.sum(p, axis=-1, keepdims=True) + jnp.exp(sink - m)
            probs[u, j, r] = p.astype(BF16)
            inv_den[u, j, r] = 1.0 / den
    between[2]()
    outs = {(u, j): _mm(jnp.concatenate([probs[u, j, r] for r in range(SWA_HEADS_PER_SLOT)],
                                        axis=1), v_bd[u, j // slots_per_kv])
            for u, j in units}
    between[3]()
    for u, j in units:
        o_ref[u * SWA_BLOCK:(u + 1) * SWA_BLOCK, j * LANES:(j + 1) * LANES] = (
            outs[u, j] * jnp.where(out_low, inv_den[u, j, 0], inv_den[u, j, 1])).astype(o_ref.dtype)


def _merge_ffn_kernel(x_ref, ydn_ref, yswa_ref, g_ref, wa_ref, wb_ref, wo_ref, gain_ref,
                      wg_ref, wu_ref, wd_ref, o_ref, act_ref):
    per_branch = D_MODEL // MXU_WIDTH
    ga, gb = [_sigmoid(jnp.concatenate(
        [g_ref[br * per_branch + p].astype(F32) for p in range(per_branch)], axis=1))
        for br in range(2)]
    merged = ga * _mm(ydn_ref[...], wa_ref[...]) + gb * _mm(yswa_ref[...], wb_ref[...])
    x1 = x_ref[...] + _mm(merged, wo_ref[...])
    ms = jnp.mean(x1 * x1, axis=-1, keepdims=True)
    h2 = (x1 * lax.rsqrt(ms + EPS) * gain_ref[...]).astype(BF16)
    for c0 in range(0, D_FF, MXU_WIDTH):
        gate = jnp.dot(h2, wg_ref[:, c0:c0 + MXU_WIDTH], preferred_element_type=F32)
        up = jnp.dot(h2, wu_ref[:, c0:c0 + MXU_WIDTH], preferred_element_type=F32)
        act_ref[:, c0:c0 + MXU_WIDTH] = (_silu(gate) * up).astype(BF16)
    o_ref[...] = x1 + jnp.dot(act_ref[...], wd_ref[...], preferred_element_type=F32)


def _merge_ffn(x2, ydn, yswa, graw, wa, wb, wo, gain, wg, wu, wd, tm):
    T = x2.shape[0]
    row = lambda w: pl.BlockSpec((tm, w), lambda i: (i, 0))
    once = lambda a, b: pl.BlockSpec((a, b), lambda i: (0, 0), pipeline_mode=pl.Buffered(1))
    return pl.pallas_call(
        _merge_ffn_kernel,
        grid=(T // tm,),
        in_specs=[row(D_MODEL), row(DN_WIDTH), row(SWA_WIDTH),
                  pl.BlockSpec((GATE_PIECES, tm, MXU_WIDTH), lambda i: (0, i, 0)),
                  once(DN_WIDTH, D_MODEL), once(SWA_WIDTH, D_MODEL), once(D_MODEL, D_MODEL),
                  once(1, D_MODEL),
                  once(D_MODEL, D_FF), once(D_MODEL, D_FF), once(D_FF, D_MODEL)],
        out_specs=row(D_MODEL),
        out_shape=jax.ShapeDtypeStruct((T, D_MODEL), F32),
        scratch_shapes=[pltpu.VMEM((tm, D_FF), BF16)],
        compiler_params=pltpu.CompilerParams(
            dimension_semantics=("arbitrary",), vmem_limit_bytes=VMEM_LIMIT),
        name="merge_ffn",
    )(x2, ydn, yswa, graw, wa, wb, wo, gain, wg, wu, wd)


def _t5_bucket_table():
    qi = jnp.arange(SWA_BLOCK)[:, None]
    kj = jnp.arange(SWA_BAND)[None, :]
    dist = SWA_BLOCK + qi - kj
    in_window = (dist >= 0) & (dist < WINDOW)
    n = jnp.maximum(dist, 0)
    max_exact = REL_BUCKETS // 2
    nf = jnp.maximum(n, 1).astype(F32)
    large = max_exact + (jnp.log(nf / max_exact) / math.log(REL_MAX_DIST / max_exact)
                         * (REL_BUCKETS - max_exact)).astype(jnp.int32)
    large = jnp.minimum(large, REL_BUCKETS - 1)
    return jnp.where(n < max_exact, n, large), in_window


def kernel(x, attn_norm, w_in, dn_conv, dn_a_log, dn_dt_bias, dn_out_norm, swa_q_norm, swa_k_norm, swa_sinks, rel_bias, w_branch_dn, w_branch_swa, w_out, ffn_norm, w_gate, w_up, w_down):
    B, S, D = x.shape
    T = B * S
    depth = w_in.shape[0]
    bucket, in_window = _t5_bucket_table()
    rel = rel_bias.astype(F32)
    bias_tab = sum(jnp.where(bucket[None] == b, rel[b][:, None, None], 0.0)
                   for b in range(REL_BUCKETS))
    bias_tab = jnp.where(in_window[None], bias_tab, -jnp.inf)
    has_prev = jnp.arange(SWA_BAND)[None, None, :] >= SWA_BLOCK
    bias_tabs = jnp.stack([jnp.where(has_prev, bias_tab, -jnp.inf), bias_tab])

    x2 = x.reshape(T, D)
    for l in range(depth):
        convw = dn_conv[l].astype(F32).reshape(DN_CONV, QKV_BLOCKS, DN_HEAD_DIM).transpose(1, 0, 2)

        later = [w[l].astype(F32) for w in (w_branch_dn, w_branch_swa, w_out, w_gate, w_up, w_down)]
        qgain = jnp.tile(swa_q_norm[l].astype(F32), SWA_HEADS)[None, :] * (SWA_HEAD_DIM ** -0.5)
        kgain = jnp.tile(swa_k_norm[l].astype(F32), SWA_KV_HEADS)[None, :]
        swa_params = (swa_sinks[l].astype(F32), bias_tabs, qgain, kgain)
        qkv, z, y_swa, gates, ba = _in_proj(
            x2, attn_norm[l][None, :], jnp.swapaxes(w_in[l], 0, 1).astype(F32), swa_params,
            convw, seq_len=S)

        pad_lo = jnp.zeros((DN_HEADS,), F32)
        pad_hi = jnp.zeros((LANES - 2 * DN_HEADS,), F32)
        alog_lane = jnp.concatenate([pad_lo, dn_a_log[l].astype(F32), pad_hi])[None, :]
        dtb_lane = jnp.concatenate([pad_lo, dn_dt_bias[l].astype(F32), pad_hi])[None, :]
        y_dn, later_bf16 = _deltanet(qkv, z.reshape(B, S, -1), ba.reshape(B, S, -1),
                                     alog_lane, dtb_lane, dn_out_norm[l][None, :], later)

        wa, wb, wo, wg, wu, wd = later_bf16
        x2 = _merge_ffn(x2, y_dn.reshape(T, -1), y_swa, gates,
                        wa, wb, wo, ffn_norm[l][None, :], wg, wu, wd, tm=512)
    return x2.reshape(B, S, D)
```

```python
import functools
import math

import jax
import jax.numpy as jnp
from jax import lax
from jax.experimental import pallas as pl
from jax.experimental.pallas import tpu as pltpu

D_MODEL = 1024
DN_HEADS = 4
DN_HEAD_DIM = 128
DN_WIDTH = DN_HEADS * DN_HEAD_DIM
DN_QKV_WIDTH = 3 * DN_WIDTH
DN_CONV = 4
DN_CHUNK = 64
SWA_HEADS = 8
SWA_KV_HEADS = 2
SWA_HEAD_DIM = 64
SWA_WIDTH = SWA_HEADS * SWA_HEAD_DIM
SWA_KV_WIDTH = SWA_KV_HEADS * SWA_HEAD_DIM
WINDOW = 128
SWA_BLOCK = 128
REL_BUCKETS = 32
REL_MAX_DIST = 128
D_FF = 2816
EPS = 1e-6

LANES = 128
SUBLANES = 8
MXU_WIDTH = 256
VMEM_LIMIT = 58 * 1024 * 1024

W_HEAD = DN_QKV_WIDTH + DN_WIDTH
W_BA = 2 * DN_HEADS
W_SWA = SWA_WIDTH + 2 * SWA_KV_WIDTH
W_GATES = 2 * D_MODEL
D_IN = W_HEAD + W_BA + W_SWA + W_GATES
SWA_PIECES = W_SWA // MXU_WIDTH
GATE_PIECES = W_GATES // MXU_WIDTH
QKV_BLOCKS = DN_QKV_WIDTH // DN_HEAD_DIM

F32 = jnp.float32
BF16 = jnp.bfloat16


def _mm(a, b, dims=(((1,), (0,)), ((), ()))):
    return lax.dot_general(a.astype(BF16), b.astype(BF16), dims, preferred_element_type=F32)


_NT = (((1,), (1,)), ((), ()))
_TN = (((0,), (0,)), ((), ()))


def _sigmoid(x):
    return 0.5 * jnp.tanh(0.5 * x) + 0.5


def _silu(x):
    h = 0.5 * x
    return h * jnp.tanh(h) + h


W_CAST_ROWS = 256


def _in_proj_kernel(sinks_ref, x_ref, gain_ref, wt_ref, bias_ref, qgain_ref, kgain_ref, convw_ref,
                    qkv_ref, z_ref, yswa_ref, gates_ref, ba_ref, w_head, w_ba, w_rest, band_ref,
                    xbuf, *, tiles_per_seq):
    i = pl.program_id(0)
    tm = x_ref.shape[0]
    first = i % tiles_per_seq == 0

    @pl.when(i == 0)
    def _():
        band_ref[...] = jnp.zeros_like(band_ref)
        xbuf[:, tm:, :] = jnp.zeros((QKV_BLOCKS, CONV_TAIL, DN_HEAD_DIM), F32)
        for r0 in range(0, W_HEAD, W_CAST_ROWS):
            w_head[r0:r0 + W_CAST_ROWS, :] = wt_ref[r0:r0 + W_CAST_ROWS, :].astype(BF16)
        w_ba[...] = jnp.zeros_like(w_ba)
        w_ba[0:W_BA, :] = wt_ref[W_HEAD:W_HEAD + W_BA, :].astype(BF16)
        base = W_HEAD + W_BA
        for r0 in range(0, W_SWA + W_GATES, W_CAST_ROWS):
            w_rest[r0:r0 + W_CAST_ROWS, :] = wt_ref[base + r0:base + r0 + W_CAST_ROWS, :].astype(BF16)

    x = x_ref[...]
    ms = jnp.mean(x * x, axis=-1, keepdims=True)
    h = (x * lax.rsqrt(ms + EPS) * gain_ref[...]).astype(BF16)
    proj = lambda w: lax.dot_general(h, w, _NT, preferred_element_type=F32)

    def dn_qkv():
        raw = proj(w_head[0:DN_QKV_WIDTH, :])
        xbuf[:, 0:CONV_TAIL, :] = jnp.where(first, 0.0, xbuf[:, tm:, :])
        for c in range(QKV_BLOCKS):
            xbuf[c, CONV_TAIL:, :] = raw[:, c * DN_HEAD_DIM:(c + 1) * DN_HEAD_DIM]

    def dn_preprocess():
        scale = DN_HEAD_DIM ** -0.5
        for c in range(QKV_BLOCKS):
            w = convw_ref[c]
            for r0 in range(0, tm, CONV_ROWS):
                acc = w[DN_CONV - 1:DN_CONV, :] * xbuf[c, CONV_TAIL + r0:CONV_TAIL + r0 + CONV_ROWS, :]
                for j in range(DN_CONV - 1):
                    off = CONV_TAIL - (DN_CONV - 1) + j + r0
                    acc = acc + w[j:j + 1, :] * xbuf[c, off:off + CONV_ROWS, :]
                act = _silu(acc)
                if c < 2 * DN_HEADS:
                    norm = lax.rsqrt(jnp.sum(act * act, axis=-1, keepdims=True) + EPS)
                    act = act * (norm * scale if c < DN_HEADS else norm)
                qkv_ref[c, r0:r0 + CONV_ROWS, :] = act.astype(qkv_ref.dtype)

    def dn_z():
        z_ref[...] = proj(w_head[DN_QKV_WIDTH:W_HEAD, :]).astype(z_ref.dtype)

    def qkv_and_gates():
        dn_qkv()
        for p in range(GATE_PIECES):
            r0 = W_SWA + p * MXU_WIDTH
            gates_ref[p] = proj(w_rest[r0:r0 + MXU_WIDTH, :]).astype(gates_ref.dtype)

    def preprocess_and_rest():
        dn_preprocess()
        ba_ref[...] = proj(w_ba[...])

    sw = [proj(w_rest[p * MXU_WIDTH:(p + 1) * MXU_WIDTH, :]) for p in range(SWA_PIECES)]
    slots_per_piece = MXU_WIDTH // LANES
    q_slots = [sw[j // slots_per_piece][:, (j % slots_per_piece) * LANES:
                                        (j % slots_per_piece + 1) * LANES] for j in range(SWA_SLOTS)]
    _swa_attend(first, q_slots, sw[SWA_WIDTH // MXU_WIDTH], sinks_ref, bias_ref,
                qgain_ref, kgain_ref, band_ref, yswa_ref,
                between=(dn_z, lambda: None, qkv_and_gates, preprocess_and_rest))


BF16_ROWS = 2 * SUBLANES


def _cast_specs(weights, steps):
    specs = []
    for w in weights:
        blocks = next(b for b in range(steps, 0, -1)
                      if steps % b == 0 and w.shape[0] % (b * BF16_ROWS) == 0)
        repeat = steps // blocks
        specs.append(pl.BlockSpec((w.shape[0] // blocks, w.shape[1]),
                                  lambda *idx, repeat=repeat: (idx[-1] // repeat, 0)))
    return specs


def _in_proj(x2, gain, w_t, swa_params, convw, seq_len):
    sinks, bias_tabs, qgain, kgain = swa_params
    T = x2.shape[0]
    tm = SWA_ROWS
    steps = T // tm
    const = lambda shape: pl.BlockSpec(shape, lambda i, s: (0,) * len(shape),
                                       pipeline_mode=pl.Buffered(1))
    grid_spec = pltpu.PrefetchScalarGridSpec(
        num_scalar_prefetch=1,
        grid=(steps,),
        in_specs=[
            pl.BlockSpec((tm, D_MODEL), lambda i, s: (i, 0)),
            const((1, D_MODEL)),
            const((D_IN, D_MODEL)),
            const((2, SWA_HEADS, SWA_BLOCK, SWA_BAND)),
            const((1, SWA_WIDTH)),
            const((1, SWA_KV_WIDTH)),
            const((QKV_BLOCKS, DN_CONV, DN_HEAD_DIM)),
        ],
        out_specs=[
            pl.BlockSpec((QKV_BLOCKS, tm, DN_HEAD_DIM), lambda i, s: (0, i, 0)),
            pl.BlockSpec((tm, DN_WIDTH), lambda i, s: (i, 0)),
            pl.BlockSpec((tm, SWA_WIDTH), lambda i, s: (i, 0)),
            pl.BlockSpec((GATE_PIECES, tm, MXU_WIDTH), lambda i, s: (0, i, 0)),
            pl.BlockSpec((tm, LANES), lambda i, s: (i, 0)),
        ],
        scratch_shapes=[pltpu.VMEM((W_HEAD, D_MODEL), BF16),
                        pltpu.VMEM((LANES, D_MODEL), BF16),
                        pltpu.VMEM((W_SWA + W_GATES, D_MODEL), BF16),
                        pltpu.VMEM((SWA_BLOCK + SWA_ROWS, 2 * SWA_KV_WIDTH), F32),
                        pltpu.VMEM((QKV_BLOCKS, tm + CONV_TAIL, DN_HEAD_DIM), F32)],
    )
    return pl.pallas_call(
        functools.partial(_in_proj_kernel, tiles_per_seq=seq_len // tm),
        grid_spec=grid_spec,
        out_shape=[
            jax.ShapeDtypeStruct((QKV_BLOCKS, T, DN_HEAD_DIM), BF16),
            jax.ShapeDtypeStruct((T, DN_WIDTH), F32),
            jax.ShapeDtypeStruct((T, SWA_WIDTH), BF16),
            jax.ShapeDtypeStruct((GATE_PIECES, T, MXU_WIDTH), BF16),
            jax.ShapeDtypeStruct((T, LANES), F32),
        ],
        compiler_params=pltpu.CompilerParams(
            dimension_semantics=("arbitrary",), vmem_limit_bytes=VMEM_LIMIT),
        name="in_proj",
    )(sinks, x2, gain, w_t, bias_tabs, qgain, kgain, convw)


DN_TILE = 128
DN_NCH = DN_TILE // DN_CHUNK
DN_LOG_CHUNK = int(math.log2(DN_CHUNK))
DN_STREAMS = 4
CONV_TAIL = SUBLANES
CONV_ROWS = 128


def _deltanet_kernel(qkv_ref, z_ref, ba_ref, alog_ref, dtb_ref, gain_ref, *refs, n_cast):
    cast_src = refs[:n_cast]
    o_ref = refs[n_cast]
    cast_dst = refs[n_cast + 1:2 * n_cast + 1]
    (state,) = refs[2 * n_cast + 1:]
    for src, dst in zip(cast_src, cast_dst):
        dst[...] = src[...].astype(BF16)
    streams = range(DN_STREAMS)

    @pl.when(pl.program_id(1) == 0)
    def _():
        state[...] = jnp.zeros_like(state)

    row = lax.broadcasted_iota(jnp.int32, (DN_TILE, DN_TILE), 0)
    col = lax.broadcasted_iota(jnp.int32, (DN_TILE, DN_TILE), 1)
    same_chunk = (row >> DN_LOG_CHUNK) == (col >> DN_LOG_CHUNK)
    lower_incl = same_chunk & (row >= col)
    strict_lower = same_chunk & (row > col)
    cumsum_mat = jnp.where(lower_incl, 1.0, 0.0).astype(BF16)
    eye = jnp.where(row == col, 1.0, 0.0)

    beta, gc, rev, gc_t = [], [], [], []
    for s in streams:
        ba = ba_ref[s]
        beta.append(_sigmoid(ba))
        xs = ba + dtb_ref[...]
        softplus = jnp.maximum(xs, 0.0) + jnp.log1p(jnp.exp(-jnp.abs(xs)))
        g = -jnp.exp(alog_ref[...]) * softplus
        g_hi = g.astype(BF16)
        g_lo = (g - g_hi.astype(F32)).astype(BF16)
        gc2 = jnp.dot(cumsum_mat, jnp.concatenate([g_hi, g_lo], axis=1),
                      preferred_element_type=F32)
        gc.append(gc2[:, :LANES] + gc2[:, LANES:])
        g_last = jnp.concatenate(
            [jnp.broadcast_to(gc[s][(c + 1) * DN_CHUNK - 1:(c + 1) * DN_CHUNK, :], (DN_CHUNK, LANES))
             for c in range(DN_NCH)], axis=0)
        rev.append(g_last - gc[s])
        gc_t.append(gc[s].T)

    units = [(s, h) for s in streams for h in range(DN_HEADS)]
    gcol = lambda arr, h: arr[:, DN_HEADS + h:DN_HEADS + h + 1]
    q, k, kb, rhs = {}, {}, {}, {}
    for s, h in units:
        q[s, h] = qkv_ref[h, s].astype(F32)
        k[s, h] = qkv_ref[DN_HEADS + h, s].astype(F32)
        vh = qkv_ref[2 * DN_HEADS + h, s].astype(F32)
        beta_c = beta[s][:, h:h + 1]
        kb[s, h] = k[s, h] * beta_c
        rhs[s, h] = jnp.concatenate([vh * beta_c, kb[s, h] * jnp.exp(gcol(gc[s], h))], axis=1)
    gram = {p: _mm(jnp.concatenate([kb[p], q[p]], axis=0), k[p], _NT) for p in units}
    qk, n_pow, inv = {}, {}, {}
    for s, h in units:
        gc_r = gc_t[s][DN_HEADS + h:DN_HEADS + h + 1, :]
        decay = jnp.exp(jnp.where(lower_incl, gcol(gc[s], h) - gc_r, -jnp.inf))
        qk[s, h] = gram[s, h][DN_TILE:] * decay
        n_pow[s, h] = -jnp.where(strict_lower, gram[s, h][:DN_TILE] * decay, 0.0)
        inv[s, h] = eye + n_pow[s, h]
    for _ in range(DN_LOG_CHUNK - 1):
        n_pow = {p: _mm(n_pow[p], n_pow[p]) for p in units}
        inv = {p: inv[p] + _mm(inv[p], n_pow[p]) for p in units}
    sol = {p: _mm(inv[p], rhs[p]) for p in units}
    u = {p: sol[p][:, :DN_HEAD_DIM] for p in units}
    wmat = {p: sol[p][:, DN_HEAD_DIM:] for p in units}
    qe = {(s, h): q[s, h] * jnp.exp(gcol(gc[s], h)) for s, h in units}
    kdec = {(s, h): k[s, h] * jnp.exp(gcol(rev[s], h)) for s, h in units}

    st = {(s, h): state[s, h] for s, h in units}
    v_new = {p: [] for p in units}
    o_inter = {p: [] for p in units}
    for c in range(DN_NCH):
        r0, r1 = c * DN_CHUNK, (c + 1) * DN_CHUNK
        ws = {p: _mm(jnp.concatenate([wmat[p][r0:r1], qe[p][r0:r1]], axis=0), st[p]) for p in units}
        for p in units:
            v_new[p].append(u[p][r0:r1] - ws[p][:DN_CHUNK])
            o_inter[p].append(ws[p][DN_CHUNK:])
        upd = {p: _mm(kdec[p][r0:r1], v_new[p][c], _TN) for p in units}
        st = {(s, h): st[s, h] * jnp.exp(gc[s][r1 - 1:r1, DN_HEADS + h:DN_HEADS + h + 1]) + upd[s, h]
              for s, h in units}
    o_intra = {p: _mm(qk[p], jnp.concatenate(v_new[p], axis=0)) for p in units}
    for s, h in units:
        state[s, h] = st[s, h]
        o = jnp.concatenate(o_inter[s, h], axis=0) + o_intra[s, h]
        o = o * lax.rsqrt(jnp.mean(o * o, axis=-1, keepdims=True) + EPS) * gain_ref[...]
        zh = z_ref[s, :, h * DN_HEAD_DIM:(h + 1) * DN_HEAD_DIM].astype(F32)
        o_ref[s, :, h * DN_HEAD_DIM:(h + 1) * DN_HEAD_DIM] = (o * _silu(zh)).astype(o_ref.dtype)


def _deltanet(qkv, z, ba, alog_lane, dtb_lane, out_gain, later_weights):
    B, S, _ = z.shape
    tiles = S // DN_TILE
    qkv = qkv.reshape(QKV_BLOCKS, B, S, DN_HEAD_DIM)
    cast_specs = _cast_specs(later_weights, tiles)
    outs = pl.pallas_call(
        functools.partial(_deltanet_kernel, n_cast=len(later_weights)),
        grid=(B // DN_STREAMS, tiles),
        in_specs=[
            pl.BlockSpec((QKV_BLOCKS, DN_STREAMS, DN_TILE, DN_HEAD_DIM), lambda b, t: (0, b, t, 0)),
            pl.BlockSpec((DN_STREAMS, DN_TILE, DN_WIDTH), lambda b, t: (b, t, 0)),
            pl.BlockSpec((DN_STREAMS, DN_TILE, LANES), lambda b, t: (b, t, 0)),
            pl.BlockSpec((1, LANES), lambda b, t: (0, 0)),
            pl.BlockSpec((1, LANES), lambda b, t: (0, 0)),
            pl.BlockSpec((1, DN_HEAD_DIM), lambda b, t: (0, 0)),
        ] + cast_specs,
        out_specs=[pl.BlockSpec((DN_STREAMS, DN_TILE, DN_WIDTH), lambda b, t: (b, t, 0))] + cast_specs,
        out_shape=[jax.ShapeDtypeStruct((B, S, DN_WIDTH), BF16)]
        + [jax.ShapeDtypeStruct(w.shape, BF16) for w in later_weights],
        scratch_shapes=[pltpu.VMEM((DN_STREAMS, DN_HEADS, DN_HEAD_DIM, DN_HEAD_DIM), F32)],
        compiler_params=pltpu.CompilerParams(
            dimension_semantics=("arbitrary", "arbitrary"), vmem_limit_bytes=VMEM_LIMIT),
        name="deltanet",
    )(qkv, z, ba, alog_lane, dtb_lane, out_gain, *later_weights)
    return outs[0], outs[1:]


SWA_BAND = 2 * SWA_BLOCK
SWA_SLOTS = SWA_WIDTH // LANES
SWA_HEADS_PER_SLOT = LANES // SWA_HEAD_DIM


SWA_STEP = 4
SWA_ROWS = SWA_STEP * SWA_BLOCK


def _swa_attend(first, q_slots, kv, sinks_ref, bias_ref, qgain_ref, kgain_ref, band_ref, o_ref,
                between):
    k = kv[:, :SWA_KV_WIDTH]
    between[0]()
    half = lax.broadcasted_iota(jnp.int32, (SWA_ROWS, LANES), 1) < SWA_HEAD_DIM

    def head_norm(p):
        p2 = p * p
        s_lo = jnp.sum(jnp.where(half, p2, 0.0), axis=-1, keepdims=True)
        s_hi = jnp.sum(jnp.where(half, 0.0, p2), axis=-1, keepdims=True)
        r_lo = lax.rsqrt(s_lo * (1.0 / SWA_HEAD_DIM) + EPS)
        r_hi = lax.rsqrt(s_hi * (1.0 / SWA_HEAD_DIM) + EPS)
        return p * jnp.where(half, r_lo, r_hi)

    qn = [head_norm(q_slots[j]) * qgain_ref[:, j * LANES:(j + 1) * LANES]
          for j in range(SWA_SLOTS)]
    kn = head_norm(k) * kgain_ref[...]

    band_ref[0:SWA_BLOCK, :] = jnp.where(first, 0.0, band_ref[SWA_ROWS:SWA_ROWS + SWA_BLOCK, :])
    band_ref[SWA_BLOCK:, :SWA_KV_WIDTH] = kn
    band_ref[SWA_BLOCK:, SWA_KV_WIDTH:] = kv[:, SWA_KV_WIDTH:]
    k = band_ref[:, :SWA_KV_WIDTH]
    v = band_ref[:, SWA_KV_WIDTH:]

    lane = lax.broadcasted_iota(jnp.int32, (SWA_BLOCK + SWA_ROWS, LANES), 1)
    low = lane < SWA_HEAD_DIM
    k_rot = pltpu.roll(k, SWA_HEAD_DIM, axis=1)
    v_rot = pltpu.roll(v, SWA_HEAD_DIM, axis=1)
    k_lo = (jnp.where(low, k, 0.0).astype(BF16), jnp.where(low, k_rot, 0.0).astype(BF16))
    k_hi = (jnp.where(low, 0.0, k_rot).astype(BF16), jnp.where(low, 0.0, k).astype(BF16))
    v_lo = (jnp.where(low, v, 0.0).astype(BF16), jnp.where(low, v_rot, 0.0).astype(BF16))
    v_hi = (jnp.where(low, 0.0, v_rot).astype(BF16), jnp.where(low, 0.0, v).astype(BF16))

    out_low = lax.broadcasted_iota(jnp.int32, (SWA_BLOCK, LANES), 1) < SWA_HEAD_DIM

    slots_per_kv = SWA_SLOTS // SWA_KV_HEADS
    units = [(u, j) for u in range(SWA_STEP) for j in range(SWA_SLOTS)]
    band_of = lambda arr, u: arr[u * SWA_BLOCK:u * SWA_BLOCK + SWA_BAND]
    k_bd = {(u, kh): jnp.concatenate([band_of(k_lo[kh], u), band_of(k_hi[kh], u)], axis=0)
            for u in range(SWA_STEP) for kh in range(SWA_KV_HEADS)}
    v_bd = {(u, kh): jnp.concatenate([band_of(v_lo[kh], u), band_of(v_hi[kh], u)], axis=0)
            for u in range(SWA_STEP) for kh in range(SWA_KV_HEADS)}
    qn = [qn[j].astype(BF16) for j in range(SWA_SLOTS)]
    between[1]()
    logits = {(u, j): _mm(qn[j][u * SWA_BLOCK:(u + 1) * SWA_BLOCK], k_bd[u, j // slots_per_kv], _NT)
              for u, j in units}
    first_tab = jnp.where(first, 0, 1)
    probs = {}
    inv_den = {}
    for u, j in units:
        for r in range(SWA_HEADS_PER_SLOT):
            hd = j * SWA_HEADS_PER_SLOT + r
            bias = bias_ref[first_tab, hd] if u == 0 else bias_ref[1, hd]
            lg = logits[u, j][:, r * SWA_BAND:(r + 1) * SWA_BAND] + bias
            sink = sinks_ref[hd]
            m = jnp.maximum(jnp.max(lg, axis=-1, keepdims=True), sink)
            p = jnp.exp(lg - m)
            den = jnp.sum(p, axis=-1, keepdims=True) + jnp.exp(sink - m)
            probs[u, j, r] = p.astype(BF16)
            inv_den[u, j, r] = 1.0 / den
    between[2]()
    outs = {(u, j): _mm(jnp.concatenate([probs[u, j, r] for r in range(SWA_HEADS_PER_SLOT)],
                                        axis=1), v_bd[u, j // slots_per_kv])
            for u, j in units}
    between[3]()
    for u, j in units:
        o_ref[u * SWA_BLOCK:(u + 1) * SWA_BLOCK, j * LANES:(j + 1) * LANES] = (
            outs[u, j] * jnp.where(out_low, inv_den[u, j, 0], inv_den[u, j, 1])).astype(o_ref.dtype)


def _merge_ffn_kernel(x_ref, ydn_ref, yswa_ref, g_ref, wa_ref, wb_ref, wo_ref, gain_ref,
                      wg_ref, wu_ref, wd_ref, o_ref, act_ref):
    per_branch = D_MODEL // MXU_WIDTH
    ga, gb = [_sigmoid(jnp.concatenate(
        [g_ref[br * per_branch + p].astype(F32) for p in range(per_branch)], axis=1))
        for br in range(2)]
    merged = ga * _mm(ydn_ref[...], wa_ref[...]) + gb * _mm(yswa_ref[...], wb_ref[...])
    x1 = x_ref[...] + _mm(merged, wo_ref[...])
    ms = jnp.mean(x1 * x1, axis=-1, keepdims=True)
    h2 = (x1 * lax.rsqrt(ms + EPS) * gain_ref[...]).astype(BF16)
    for c0 in range(0, D_FF, MXU_WIDTH):
        gate = jnp.dot(h2, wg_ref[:, c0:c0 + MXU_WIDTH], preferred_element_type=F32)
        up = jnp.dot(h2, wu_ref[:, c0:c0 + MXU_WIDTH], preferred_element_type=F32)
        act_ref[:, c0:c0 + MXU_WIDTH] = (_silu(gate) * up).astype(BF16)
    o_ref[...] = x1 + jnp.dot(act_ref[...], wd_ref[...], preferred_element_type=F32)


def _merge_ffn(x2, ydn, yswa, graw, wa, wb, wo, gain, wg, wu, wd, tm):
    T = x2.shape[0]
    row = lambda w: pl.BlockSpec((tm, w), lambda i: (i, 0))
    once = lambda a, b: pl.BlockSpec((a, b), lambda i: (0, 0), pipeline_mode=pl.Buffered(1))
    return pl.pallas_call(
        _merge_ffn_kernel,
        grid=(T // tm,),
        in_specs=[row(D_MODEL), row(DN_WIDTH), row(SWA_WIDTH),
                  pl.BlockSpec((GATE_PIECES, tm, MXU_WIDTH), lambda i: (0, i, 0)),
                  once(DN_WIDTH, D_MODEL), once(SWA_WIDTH, D_MODEL), once(D_MODEL, D_MODEL),
                  once(1, D_MODEL),
                  once(D_MODEL, D_FF), once(D_MODEL, D_FF), once(D_FF, D_MODEL)],
        out_specs=row(D_MODEL),
        out_shape=jax.ShapeDtypeStruct((T, D_MODEL), F32),
        scratch_shapes=[pltpu.VMEM((tm, D_FF), BF16)],
        compiler_params=pltpu.CompilerParams(
            dimension_semantics=("arbitrary",), vmem_limit_bytes=VMEM_LIMIT),
        name="merge_ffn",
    )(x2, ydn, yswa, graw, wa, wb, wo, gain, wg, wu, wd)


def _t5_bucket_table():
    qi = jnp.arange(SWA_BLOCK)[:, None]
    kj = jnp.arange(SWA_BAND)[None, :]
    dist = SWA_BLOCK + qi - kj
    in_window = (dist >= 0) & (dist < WINDOW)
    n = jnp.maximum(dist, 0)
    max_exact = REL_BUCKETS // 2
    nf = jnp.maximum(n, 1).astype(F32)
    large = max_exact + (jnp.log(nf / max_exact) / math.log(REL_MAX_DIST / max_exact)
                         * (REL_BUCKETS - max_exact)).astype(jnp.int32)
    large = jnp.minimum(large, REL_BUCKETS - 1)
    return jnp.where(n < max_exact, n, large), in_window


def kernel(x, attn_norm, w_in, dn_conv, dn_a_log, dn_dt_bias, dn_out_norm, swa_q_norm, swa_k_norm, swa_sinks, rel_bias, w_branch_dn, w_branch_swa, w_out, ffn_norm, w_gate, w_up, w_down):
    B, S, D = x.shape
    T = B * S
    depth = w_in.shape[0]
    bucket, in_window = _t5_bucket_table()
    rel = rel_bias.astype(F32)
    bias_tab = sum(jnp.where(bucket[None] == b, rel[b][:, None, None], 0.0)
                   for b in range(REL_BUCKETS))
    bias_tab = jnp.where(in_window[None], bias_tab, -jnp.inf)
    has_prev = jnp.arange(SWA_BAND)[None, None, :] >= SWA_BLOCK
    bias_tabs = jnp.stack([jnp.where(has_prev, bias_tab, -jnp.inf), bias_tab])

    x2 = x.reshape(T, D)
    for l in range(depth):
        convw = dn_conv[l].astype(F32).reshape(DN_CONV, QKV_BLOCKS, DN_HEAD_DIM).transpose(1, 0, 2)

        later = [w[l].astype(F32) for w in (w_branch_dn, w_branch_swa, w_out, w_gate, w_up, w_down)]
        qgain = jnp.tile(swa_q_norm[l].astype(F32), SWA_HEADS)[None, :] * (SWA_HEAD_DIM ** -0.5)
        kgain = jnp.tile(swa_k_norm[l].astype(F32), SWA_KV_HEADS)[None, :]
        swa_params = (swa_sinks[l].astype(F32), bias_tabs, qgain, kgain)
        qkv, z, y_swa, gates, ba = _in_proj(
            x2, attn_norm[l][None, :], jnp.swapaxes(w_in[l], 0, 1).astype(F32), swa_params,
            convw, seq_len=S)

        pad_lo = jnp.zeros((DN_HEADS,), F32)
        pad_hi = jnp.zeros((LANES - 2 * DN_HEADS,), F32)
        alog_lane = jnp.concatenate([pad_lo, dn_a_log[l].astype(F32), pad_hi])[None, :]
        dtb_lane = jnp.concatenate([pad_lo, dn_dt_bias[l].astype(F32), pad_hi])[None, :]
        y_dn, later_bf16 = _deltanet(qkv, z.reshape(B, S, -1), ba.reshape(B, S, -1),
                                     alog_lane, dtb_lane, dn_out_norm[l][None, :], later)

        wa, wb, wo, wg, wu, wd = later_bf16
        x2 = _merge_ffn(x2, y_dn.reshape(T, -1), y_swa, gates,
                        wa, wb, wo, ffn_norm[l][None, :], wg, wu, wd, tm=512)
    return x2.reshape(B, S, D)
```

```python
import functools
import math

import jax
import jax.numpy as jnp
from jax import lax
from jax.experimental import pallas as pl
from jax.experimental.pallas import tpu as pltpu

D_MODEL = 1024
DN_HEADS = 4
DN_HEAD_DIM = 128
DN_WIDTH = DN_HEADS * DN_HEAD_DIM
DN_QKV_WIDTH = 3 * DN_WIDTH
DN_CONV = 4
DN_CHUNK = 64
SWA_HEADS = 8
SWA_KV_HEADS = 2
SWA_HEAD_DIM = 64
SWA_WIDTH = SWA_HEADS * SWA_HEAD_DIM
SWA_KV_WIDTH = SWA_KV_HEADS * SWA_HEAD_DIM
WINDOW = 128
SWA_BLOCK = 128
REL_BUCKETS = 32
REL_MAX_DIST = 128
D_FF = 2816
EPS = 1e-6

LANES = 128
SUBLANES = 8
MXU_WIDTH = 256
VMEM_LIMIT = 58 * 1024 * 1024

W_HEAD = DN_QKV_WIDTH + DN_WIDTH
W_BA = 2 * DN_HEADS
W_SWA = SWA_WIDTH + 2 * SWA_KV_WIDTH
W_GATES = 2 * D_MODEL
D_IN = W_HEAD + W_BA + W_SWA + W_GATES
SWA_PIECES = W_SWA // MXU_WIDTH
GATE_PIECES = W_GATES // MXU_WIDTH
QKV_BLOCKS = DN_QKV_WIDTH // DN_HEAD_DIM

F32 = jnp.float32
BF16 = jnp.bfloat16


def _mm(a, b, dims=(((1,), (0,)), ((), ()))):
    return lax.dot_general(a.astype(BF16), b.astype(BF16), dims, preferred_element_type=F32)


_NT = (((1,), (1,)), ((), ()))
_TN = (((0,), (0,)), ((), ()))


def _sigmoid(x):
    return 0.5 * jnp.tanh(0.5 * x) + 0.5


def _silu(x):
    h = 0.5 * x
    return h * jnp.tanh(h) + h


W_CAST_ROWS = 256
GATES_EARLY = 3


def _in_proj_kernel(sinks_ref, x_ref, gain_ref, wt_ref, bias_ref, qgain_ref, kgain_ref, convw_ref,
                    qkv_ref, z_ref, yswa_ref, gates_ref, ba_ref, w_head, w_ba, w_rest, band_ref,
                    xbuf, *, tiles_per_seq):
    i = pl.program_id(0)
    tm = x_ref.shape[0]
    first = i % tiles_per_seq == 0

    @pl.when(i == 0)
    def _():
        band_ref[...] = jnp.zeros_like(band_ref)
        xbuf[:, tm:, :] = jnp.zeros((QKV_BLOCKS, CONV_TAIL, DN_HEAD_DIM), F32)
        for r0 in range(0, W_HEAD, W_CAST_ROWS):
            w_head[r0:r0 + W_CAST_ROWS, :] = wt_ref[r0:r0 + W_CAST_ROWS, :].astype(BF16)
        w_ba[...] = jnp.zeros_like(w_ba)
        w_ba[0:W_BA, :] = wt_ref[W_HEAD:W_HEAD + W_BA, :].astype(BF16)
        base = W_HEAD + W_BA
        for r0 in range(0, W_SWA + W_GATES, W_CAST_ROWS):
            w_rest[r0:r0 + W_CAST_ROWS, :] = wt_ref[base + r0:base + r0 + W_CAST_ROWS, :].astype(BF16)

    x = x_ref[...]
    ms = jnp.mean(x * x, axis=-1, keepdims=True)
    h = (x * lax.rsqrt(ms + EPS) * gain_ref[...]).astype(BF16)
    proj = lambda w: lax.dot_general(h, w, _NT, preferred_element_type=F32)

    def dn_qkv():
        raw = proj(w_head[0:DN_QKV_WIDTH, :])
        xbuf[:, 0:CONV_TAIL, :] = jnp.where(first, 0.0, xbuf[:, tm:, :])
        for c in range(QKV_BLOCKS):
            xbuf[c, CONV_TAIL:, :] = raw[:, c * DN_HEAD_DIM:(c + 1) * DN_HEAD_DIM]

    def dn_preprocess():
        scale = DN_HEAD_DIM ** -0.5
        for c in range(QKV_BLOCKS):
            w = convw_ref[c]
            for r0 in range(0, tm, CONV_ROWS):
                acc = w[DN_CONV - 1:DN_CONV, :] * xbuf[c, CONV_TAIL + r0:CONV_TAIL + r0 + CONV_ROWS, :]
                for j in range(DN_CONV - 1):
                    off = CONV_TAIL - (DN_CONV - 1) + j + r0
                    acc = acc + w[j:j + 1, :] * xbuf[c, off:off + CONV_ROWS, :]
                act = _silu(acc)
                if c < 2 * DN_HEADS:
                    norm = lax.rsqrt(jnp.sum(act * act, axis=-1, keepdims=True) + EPS)
                    act = act * (norm * scale if c < DN_HEADS else norm)
                qkv_ref[c, r0:r0 + CONV_ROWS, :] = act.astype(qkv_ref.dtype)

    def dn_z():
        z_ref[...] = proj(w_head[DN_QKV_WIDTH:W_HEAD, :]).astype(z_ref.dtype)

    def gate_logits(pieces):
        for p in pieces:
            r0 = W_SWA + p * MXU_WIDTH
            gates_ref[p] = proj(w_rest[r0:r0 + MXU_WIDTH, :]).astype(gates_ref.dtype)

    def gates_a_and_preprocess():
        gate_logits(range(0, GATES_EARLY))
        dn_preprocess()

    def gates_b():
        gate_logits(range(GATES_EARLY, GATE_PIECES))

    def qkv_and_z():
        dn_qkv()
        dn_z()

    def beta_decay():
        ba_ref[...] = proj(w_ba[...])

    sw = [proj(w_rest[p * MXU_WIDTH:(p + 1) * MXU_WIDTH, :]) for p in range(SWA_PIECES)]
    slots_per_piece = MXU_WIDTH // LANES
    q_slots = [sw[j // slots_per_piece][:, (j % slots_per_piece) * LANES:
                                        (j % slots_per_piece + 1) * LANES] for j in range(SWA_SLOTS)]
    _swa_attend(first, q_slots, sw[SWA_WIDTH // MXU_WIDTH], sinks_ref, bias_ref,
                qgain_ref, kgain_ref, band_ref, yswa_ref,
                between=(qkv_and_z, gates_a_and_preprocess, gates_b, beta_decay))


BF16_ROWS = 2 * SUBLANES


def _cast_specs(weights, steps):
    specs = []
    for w in weights:
        blocks = next(b for b in range(steps, 0, -1)
                      if steps % b == 0 and w.shape[0] % (b * BF16_ROWS) == 0)
        repeat = steps // blocks
        specs.append(pl.BlockSpec((w.shape[0] // blocks, w.shape[1]),
                                  lambda *idx, repeat=repeat: (idx[-1] // repeat, 0)))
    return specs


def _in_proj(x2, gain, w_t, swa_params, convw, seq_len):
    sinks, bias_tabs, qgain, kgain = swa_params
    T = x2.shape[0]
    tm = SWA_ROWS
    steps = T // tm
    const = lambda shape: pl.BlockSpec(shape, lambda i, s: (0,) * len(shape),
                                       pipeline_mode=pl.Buffered(1))
    grid_spec = pltpu.PrefetchScalarGridSpec(
        num_scalar_prefetch=1,
        grid=(steps,),
        in_specs=[
            pl.BlockSpec((tm, D_MODEL), lambda i, s: (i, 0)),
            const((1, D_MODEL)),
            const((D_IN, D_MODEL)),
            const((2, SWA_HEADS, SWA_BLOCK, SWA_BAND)),
            const((1, SWA_WIDTH)),
            const((1, SWA_KV_WIDTH)),
            const((QKV_BLOCKS, DN_CONV, DN_HEAD_DIM)),
        ],
        out_specs=[
            pl.BlockSpec((QKV_BLOCKS, tm, DN_HEAD_DIM), lambda i, s: (0, i, 0)),
            pl.BlockSpec((tm, DN_WIDTH), lambda i, s: (i, 0)),
            pl.BlockSpec((tm, SWA_WIDTH), lambda i, s: (i, 0)),
            pl.BlockSpec((GATE_PIECES, tm, MXU_WIDTH), lambda i, s: (0, i, 0)),
            pl.BlockSpec((tm, LANES), lambda i, s: (i, 0)),
        ],
        scratch_shapes=[pltpu.VMEM((W_HEAD, D_MODEL), BF16),
                        pltpu.VMEM((LANES, D_MODEL), BF16),
                        pltpu.VMEM((W_SWA + W_GATES, D_MODEL), BF16),
                        pltpu.VMEM((SWA_BLOCK + SWA_ROWS, 2 * SWA_KV_WIDTH), F32),
                        pltpu.VMEM((QKV_BLOCKS, tm + CONV_TAIL, DN_HEAD_DIM), F32)],
    )
    return pl.pallas_call(
        functools.partial(_in_proj_kernel, tiles_per_seq=seq_len // tm),
        grid_spec=grid_spec,
        out_shape=[
            jax.ShapeDtypeStruct((QKV_BLOCKS, T, DN_HEAD_DIM), BF16),
            jax.ShapeDtypeStruct((T, DN_WIDTH), F32),
            jax.ShapeDtypeStruct((T, SWA_WIDTH), BF16),
            jax.ShapeDtypeStruct((GATE_PIECES, T, MXU_WIDTH), BF16),
            jax.ShapeDtypeStruct((T, LANES), F32),
        ],
        compiler_params=pltpu.CompilerParams(
            dimension_semantics=("arbitrary",), vmem_limit_bytes=VMEM_LIMIT),
        name="in_proj",
    )(sinks, x2, gain, w_t, bias_tabs, qgain, kgain, convw)


DN_TILE = 128
DN_NCH = DN_TILE // DN_CHUNK
DN_LOG_CHUNK = int(math.log2(DN_CHUNK))
DN_STREAMS = 4
DN_GROUP = DN_STREAMS * DN_HEADS
CONV_TAIL = SUBLANES
CONV_ROWS = 128


def _deltanet_kernel(qkv_ref, z_ref, ba_ref, alog_ref, dtb_ref, gain_ref, *refs, n_cast):
    cast_src = refs[:n_cast]
    o_ref = refs[n_cast]
    cast_dst = refs[n_cast + 1:2 * n_cast + 1]
    (state,) = refs[2 * n_cast + 1:]
    for src, dst in zip(cast_src, cast_dst):
        dst[...] = src[...].astype(BF16)
    streams = range(DN_STREAMS)

    @pl.when(pl.program_id(1) == 0)
    def _():
        state[...] = jnp.zeros_like(state)

    row = lax.broadcasted_iota(jnp.int32, (DN_TILE, DN_TILE), 0)
    col = lax.broadcasted_iota(jnp.int32, (DN_TILE, DN_TILE), 1)
    same_chunk = (row >> DN_LOG_CHUNK) == (col >> DN_LOG_CHUNK)
    lower_incl = same_chunk & (row >= col)
    strict_lower = same_chunk & (row > col)
    cumsum_mat = jnp.where(lower_incl, 1.0, 0.0).astype(BF16)
    eye = jnp.where(row == col, 1.0, 0.0)

    beta, gc, rev, gc_t = [], [], [], []
    for s in streams:
        ba = ba_ref[s]
        beta.append(_sigmoid(ba))
        xs = ba + dtb_ref[...]
        softplus = jnp.maximum(xs, 0.0) + jnp.log1p(jnp.exp(-jnp.abs(xs)))
        g = -jnp.exp(alog_ref[...]) * softplus
        g_hi = g.astype(BF16)
        g_lo = (g - g_hi.astype(F32)).astype(BF16)
        gc2 = jnp.dot(cumsum_mat, jnp.concatenate([g_hi, g_lo], axis=1),
                      preferred_element_type=F32)
        gc.append(gc2[:, :LANES] + gc2[:, LANES:])
        g_last = jnp.concatenate(
            [jnp.broadcast_to(gc[s][(c + 1) * DN_CHUNK - 1:(c + 1) * DN_CHUNK, :], (DN_CHUNK, LANES))
             for c in range(DN_NCH)], axis=0)
        rev.append(g_last - gc[s])
        gc_t.append(gc[s].T)

    gcol = lambda arr, h: arr[:, DN_HEADS + h:DN_HEADS + h + 1]

    def advance(units):
        q, k, kb, rhs = {}, {}, {}, {}
        for s, h in units:
            q[s, h] = qkv_ref[h, s].astype(F32)
            k[s, h] = qkv_ref[DN_HEADS + h, s].astype(F32)
            vh = qkv_ref[2 * DN_HEADS + h, s].astype(F32)
            beta_c = beta[s][:, h:h + 1]
            kb[s, h] = k[s, h] * beta_c
            rhs[s, h] = jnp.concatenate([vh * beta_c, kb[s, h] * jnp.exp(gcol(gc[s], h))], axis=1)
        gram = {p: _mm(jnp.concatenate([kb[p], q[p]], axis=0), k[p], _NT) for p in units}
        qk, n_pow, inv = {}, {}, {}
        for s, h in units:
            gc_r = gc_t[s][DN_HEADS + h:DN_HEADS + h + 1, :]
            decay = jnp.exp(jnp.where(lower_incl, gcol(gc[s], h) - gc_r, -jnp.inf))
            qk[s, h] = gram[s, h][DN_TILE:] * decay
            n_pow[s, h] = -jnp.where(strict_lower, gram[s, h][:DN_TILE] * decay, 0.0)
            inv[s, h] = eye + n_pow[s, h]
        for _ in range(DN_LOG_CHUNK - 1):
            n_pow = {p: _mm(n_pow[p], n_pow[p]) for p in units}
            inv = {p: inv[p] + _mm(inv[p], n_pow[p]) for p in units}
        sol = {p: _mm(inv[p], rhs[p]) for p in units}
        u = {p: sol[p][:, :DN_HEAD_DIM] for p in units}
        wmat = {p: sol[p][:, DN_HEAD_DIM:] for p in units}
        qe = {(s, h): q[s, h] * jnp.exp(gcol(gc[s], h)) for s, h in units}
        kdec = {(s, h): k[s, h] * jnp.exp(gcol(rev[s], h)) for s, h in units}

        st = {(s, h): state[s, h] for s, h in units}
        v_new = {p: [] for p in units}
        o_inter = {p: [] for p in units}
        for c in range(DN_NCH):
            r0, r1 = c * DN_CHUNK, (c + 1) * DN_CHUNK
            ws = {p: _mm(jnp.concatenate([wmat[p][r0:r1], qe[p][r0:r1]], axis=0), st[p])
                  for p in units}
            for p in units:
                v_new[p].append(u[p][r0:r1] - ws[p][:DN_CHUNK])
                o_inter[p].append(ws[p][DN_CHUNK:])
            upd = {p: _mm(kdec[p][r0:r1], v_new[p][c], _TN) for p in units}
            st = {(s, h): st[s, h] * jnp.exp(gc[s][r1 - 1:r1, DN_HEADS + h:DN_HEADS + h + 1])
                  + upd[s, h] for s, h in units}
        o_intra = {p: _mm(qk[p], jnp.concatenate(v_new[p], axis=0)) for p in units}
        for s, h in units:
            state[s, h] = st[s, h]
            o = jnp.concatenate(o_inter[s, h], axis=0) + o_intra[s, h]
            o = o * lax.rsqrt(jnp.mean(o * o, axis=-1, keepdims=True) + EPS) * gain_ref[...]
            zh = z_ref[s, :, h * DN_HEAD_DIM:(h + 1) * DN_HEAD_DIM].astype(F32)
            o_ref[s, :, h * DN_HEAD_DIM:(h + 1) * DN_HEAD_DIM] = (o * _silu(zh)).astype(o_ref.dtype)

    all_units = [(s, h) for s in streams for h in range(DN_HEADS)]
    for g0 in range(0, len(all_units), DN_GROUP):
        advance(all_units[g0:g0 + DN_GROUP])


def _deltanet(qkv, z, ba, alog_lane, dtb_lane, out_gain, later_weights):
    B, S, _ = z.shape
    tiles = S // DN_TILE
    qkv = qkv.reshape(QKV_BLOCKS, B, S, DN_HEAD_DIM)
    cast_specs = _cast_specs(later_weights, tiles)
    outs = pl.pallas_call(
        functools.partial(_deltanet_kernel, n_cast=len(later_weights)),
        grid=(B // DN_STREAMS, tiles),
        in_specs=[
            pl.BlockSpec((QKV_BLOCKS, DN_STREAMS, DN_TILE, DN_HEAD_DIM), lambda b, t: (0, b, t, 0)),
            pl.BlockSpec((DN_STREAMS, DN_TILE, DN_WIDTH), lambda b, t: (b, t, 0)),
            pl.BlockSpec((DN_STREAMS, DN_TILE, LANES), lambda b, t: (b, t, 0)),
            pl.BlockSpec((1, LANES), lambda b, t: (0, 0)),
            pl.BlockSpec((1, LANES), lambda b, t: (0, 0)),
            pl.BlockSpec((1, DN_HEAD_DIM), lambda b, t: (0, 0)),
        ] + cast_specs,
        out_specs=[pl.BlockSpec((DN_STREAMS, DN_TILE, DN_WIDTH), lambda b, t: (b, t, 0))] + cast_specs,
        out_shape=[jax.ShapeDtypeStruct((B, S, DN_WIDTH), BF16)]
        + [jax.ShapeDtypeStruct(w.shape, BF16) for w in later_weights],
        scratch_shapes=[pltpu.VMEM((DN_STREAMS, DN_HEADS, DN_HEAD_DIM, DN_HEAD_DIM), F32)],
        compiler_params=pltpu.CompilerParams(
            dimension_semantics=("arbitrary", "arbitrary"), vmem_limit_bytes=VMEM_LIMIT),
        name="deltanet",
    )(qkv, z, ba, alog_lane, dtb_lane, out_gain, *later_weights)
    return outs[0], outs[1:]


SWA_BAND = 2 * SWA_BLOCK
SWA_SLOTS = SWA_WIDTH // LANES
SWA_HEADS_PER_SLOT = LANES // SWA_HEAD_DIM


SWA_STEP = 4
SWA_ROWS = SWA_STEP * SWA_BLOCK


def _swa_attend(first, q_slots, kv, sinks_ref, bias_ref, qgain_ref, kgain_ref, band_ref, o_ref,
                between):
    k = kv[:, :SWA_KV_WIDTH]
    between[0]()
    half = lax.broadcasted_iota(jnp.int32, (SWA_ROWS, LANES), 1) < SWA_HEAD_DIM

    def head_norm(p):
        p2 = p * p
        s_lo = jnp.sum(jnp.where(half, p2, 0.0), axis=-1, keepdims=True)
        s_hi = jnp.sum(jnp.where(half, 0.0, p2), axis=-1, keepdims=True)
        r_lo = lax.rsqrt(s_lo * (1.0 / SWA_HEAD_DIM) + EPS)
        r_hi = lax.rsqrt(s_hi * (1.0 / SWA_HEAD_DIM) + EPS)
        return p * jnp.where(half, r_lo, r_hi)

    qn = [head_norm(q_slots[j]) * qgain_ref[:, j * LANES:(j + 1) * LANES]
          for j in range(SWA_SLOTS)]
    kn = head_norm(k) * kgain_ref[...]

    band_ref[0:SWA_BLOCK, :] = jnp.where(first, 0.0, band_ref[SWA_ROWS:SWA_ROWS + SWA_BLOCK, :])
    band_ref[SWA_BLOCK:, :SWA_KV_WIDTH] = kn
    band_ref[SWA_BLOCK:, SWA_KV_WIDTH:] = kv[:, SWA_KV_WIDTH:]
    k = band_ref[:, :SWA_KV_WIDTH]
    v = band_ref[:, SWA_KV_WIDTH:]

    lane = lax.broadcasted_iota(jnp.int32, (SWA_BLOCK + SWA_ROWS, LANES), 1)
    low = lane < SWA_HEAD_DIM
    k_rot = pltpu.roll(k, SWA_HEAD_DIM, axis=1)
    v_rot = pltpu.roll(v, SWA_HEAD_DIM, axis=1)
    k_lo = (jnp.where(low, k, 0.0).astype(BF16), jnp.where(low, k_rot, 0.0).astype(BF16))
    k_hi = (jnp.where(low, 0.0, k_rot).astype(BF16), jnp.where(low, 0.0, k).astype(BF16))
    v_lo = (jnp.where(low, v, 0.0).astype(BF16), jnp.where(low, v_rot, 0.0).astype(BF16))
    v_hi = (jnp.where(low, 0.0, v_rot).astype(BF16), jnp.where(low, 0.0, v).astype(BF16))

    out_low = lax.broadcasted_iota(jnp.int32, (SWA_BLOCK, LANES), 1) < SWA_HEAD_DIM

    slots_per_kv = SWA_SLOTS // SWA_KV_HEADS
    units = [(u, j) for u in range(SWA_STEP) for j in range(SWA_SLOTS)]
    band_of = lambda arr, u: arr[u * SWA_BLOCK:u * SWA_BLOCK + SWA_BAND]
    k_bd = {(u, kh): jnp.concatenate([band_of(k_lo[kh], u), band_of(k_hi[kh], u)], axis=0)
            for u in range(SWA_STEP) for kh in range(SWA_KV_HEADS)}
    v_bd = {(u, kh): jnp.concatenate([band_of(v_lo[kh], u), band_of(v_hi[kh], u)], axis=0)
            for u in range(SWA_STEP) for kh in range(SWA_KV_HEADS)}
    qn = [qn[j].astype(BF16) for j in range(SWA_SLOTS)]
    between[1]()
    logits = {(u, j): _mm(qn[j][u * SWA_BLOCK:(u + 1) * SWA_BLOCK], k_bd[u, j // slots_per_kv], _NT)
              for u, j in units}
    first_tab = jnp.where(first, 0, 1)
    probs = {}
    inv_den = {}
    for u, j in units:
        for r in range(SWA_HEADS_PER_SLOT):
            hd = j * SWA_HEADS_PER_SLOT + r
            bias = bias_ref[first_tab, hd] if u == 0 else bias_ref[1, hd]
            lg = logits[u, j][:, r * SWA_BAND:(r + 1) * SWA_BAND] + bias
            sink = sinks_ref[hd]
            m = jnp.maximum(jnp.max(lg, axis=-1, keepdims=True), sink)
            p = jnp.exp(lg - m)
            den = jnp.sum(p, axis=-1, keepdims=True) + jnp.exp(sink - m)
            probs[u, j, r] = p.astype(BF16)
            inv_den[u, j, r] = 1.0 / den
    between[2]()
    outs = {(u, j): _mm(jnp.concatenate([probs[u, j, r] for r in range(SWA_HEADS_PER_SLOT)],
                                        axis=1), v_bd[u, j // slots_per_kv])
            for u, j in units}
    between[3]()
    for u, j in units:
        o_ref[u * SWA_BLOCK:(u + 1) * SWA_BLOCK, j * LANES:(j + 1) * LANES] = (
            outs[u, j] * jnp.where(out_low, inv_den[u, j, 0], inv_den[u, j, 1])).astype(o_ref.dtype)


def _merge_ffn_kernel(x_ref, ydn_ref, yswa_ref, g_ref, wa_ref, wb_ref, wo_ref, gain_ref,
                      wg_ref, wu_ref, wd_ref, o_ref, act_ref):
    per_branch = D_MODEL // MXU_WIDTH
    ga, gb = [_sigmoid(jnp.concatenate(
        [g_ref[br * per_branch + p].astype(F32) for p in range(per_branch)], axis=1))
        for br in range(2)]
    merged = ga * _mm(ydn_ref[...], wa_ref[...]) + gb * _mm(yswa_ref[...], wb_ref[...])
    x1 = x_ref[...] + _mm(merged, wo_ref[...])
    ms = jnp.mean(x1 * x1, axis=-1, keepdims=True)
    h2 = (x1 * lax.rsqrt(ms + EPS) * gain_ref[...]).astype(BF16)
    for c0 in range(0, D_FF, MXU_WIDTH):
        gate = jnp.dot(h2, wg_ref[:, c0:c0 + MXU_WIDTH], preferred_element_type=F32)
        up = jnp.dot(h2, wu_ref[:, c0:c0 + MXU_WIDTH], preferred_element_type=F32)
        act_ref[:, c0:c0 + MXU_WIDTH] = (_silu(gate) * up).astype(BF16)
    o_ref[...] = x1 + jnp.dot(act_ref[...], wd_ref[...], preferred_element_type=F32)


def _merge_ffn(x2, ydn, yswa, graw, wa, wb, wo, gain, wg, wu, wd, tm):
    T = x2.shape[0]
    row = lambda w: pl.BlockSpec((tm, w), lambda i: (i, 0))
    once = lambda a, b: pl.BlockSpec((a, b), lambda i: (0, 0), pipeline_mode=pl.Buffered(1))
    return pl.pallas_call(
        _merge_ffn_kernel,
        grid=(T // tm,),
        in_specs=[row(D_MODEL), row(DN_WIDTH), row(SWA_WIDTH),
                  pl.BlockSpec((GATE_PIECES, tm, MXU_WIDTH), lambda i: (0, i, 0)),
                  once(DN_WIDTH, D_MODEL), once(SWA_WIDTH, D_MODEL), once(D_MODEL, D_MODEL),
                  once(1, D_MODEL),
                  once(D_MODEL, D_FF), once(D_MODEL, D_FF), once(D_FF, D_MODEL)],
        out_specs=row(D_MODEL),
        out_shape=jax.ShapeDtypeStruct((T, D_MODEL), F32),
        scratch_shapes=[pltpu.VMEM((tm, D_FF), BF16)],
        compiler_params=pltpu.CompilerParams(
            dimension_semantics=("arbitrary",), vmem_limit_bytes=VMEM_LIMIT),
        name="merge_ffn",
    )(x2, ydn, yswa, graw, wa, wb, wo, gain, wg, wu, wd)


def _t5_bucket_table():
    qi = jnp.arange(SWA_BLOCK)[:, None]
    kj = jnp.arange(SWA_BAND)[None, :]
    dist = SWA_BLOCK + qi - kj
    in_window = (dist >= 0) & (dist < WINDOW)
    n = jnp.maximum(dist, 0)
    max_exact = REL_BUCKETS // 2
    nf = jnp.maximum(n, 1).astype(F32)
    large = max_exact + (jnp.log(nf / max_exact) / math.log(REL_MAX_DIST / max_exact)
                         * (REL_BUCKETS - max_exact)).astype(jnp.int32)
    large = jnp.minimum(large, REL_BUCKETS - 1)
    return jnp.where(n < max_exact, n, large), in_window


def kernel(x, attn_norm, w_in, dn_conv, dn_a_log, dn_dt_bias, dn_out_norm, swa_q_norm, swa_k_norm, swa_sinks, rel_bias, w_branch_dn, w_branch_swa, w_out, ffn_norm, w_gate, w_up, w_down):
    B, S, D = x.shape
    T = B * S
    depth = w_in.shape[0]
    bucket, in_window = _t5_bucket_table()
    rel = rel_bias.astype(F32)
    bias_tab = sum(jnp.where(bucket[None] == b, rel[b][:, None, None], 0.0)
                   for b in range(REL_BUCKETS))
    bias_tab = jnp.where(in_window[None], bias_tab, -jnp.inf)
    has_prev = jnp.arange(SWA_BAND)[None, None, :] >= SWA_BLOCK
    bias_tabs = jnp.stack([jnp.where(has_prev, bias_tab, -jnp.inf), bias_tab])

    x2 = x.reshape(T, D)
    for l in range(depth):
        convw = dn_conv[l].astype(F32).reshape(DN_CONV, QKV_BLOCKS, DN_HEAD_DIM).transpose(1, 0, 2)

        later = [w[l].astype(F32) for w in (w_branch_dn, w_branch_swa, w_out, w_gate, w_up, w_down)]
        qgain = jnp.tile(swa_q_norm[l].astype(F32), SWA_HEADS)[None, :] * (SWA_HEAD_DIM ** -0.5)
        kgain = jnp.tile(swa_k_norm[l].astype(F32), SWA_KV_HEADS)[None, :]
        swa_params = (swa_sinks[l].astype(F32), bias_tabs, qgain, kgain)
        qkv, z, y_swa, gates, ba = _in_proj(
            x2, attn_norm[l][None, :], jnp.swapaxes(w_in[l], 0, 1).astype(F32), swa_params,
            convw, seq_len=S)

        pad_lo = jnp.zeros((DN_HEADS,), F32)
        pad_hi = jnp.zeros((LANES - 2 * DN_HEADS,), F32)
        alog_lane = jnp.concatenate([pad_lo, dn_a_log[l].astype(F32), pad_hi])[None, :]
        dtb_lane = jnp.concatenate([pad_lo, dn_dt_bias[l].astype(F32), pad_hi])[None, :]
        y_dn, later_bf16 = _deltanet(qkv, z.reshape(B, S, -1), ba.reshape(B, S, -1),
                                     alog_lane, dtb_lane, dn_out_norm[l][None, :], later)

        wa, wb, wo, wg, wu, wd = later_bf16
        x2 = _merge_ffn(x2, y_dn.reshape(T, -1), y_swa, gates,
                        wa, wb, wo, ffn_norm[l][None, :], wg, wu, wd, tm=512)
    return x2.reshape(B, S, D)
```

```python
import functools
import math

import jax
import jax.numpy as jnp
from jax import lax
from jax.experimental import pallas as pl
from jax.experimental.pallas import tpu as pltpu

D_MODEL = 1024
DN_HEADS = 4
DN_HEAD_DIM = 128
DN_WIDTH = DN_HEADS * DN_HEAD_DIM
DN_QKV_WIDTH = 3 * DN_WIDTH
DN_CONV = 4
DN_CHUNK = 64
SWA_HEADS = 8
SWA_KV_HEADS = 2
SWA_HEAD_DIM = 64
SWA_WIDTH = SWA_HEADS * SWA_HEAD_DIM
SWA_KV_WIDTH = SWA_KV_HEADS * SWA_HEAD_DIM
WINDOW = 128
SWA_BLOCK = 128
REL_BUCKETS = 32
REL_MAX_DIST = 128
D_FF = 2816
EPS = 1e-6

LANES = 128
SUBLANES = 8
BF16_ROWS = 2 * SUBLANES
MXU_WIDTH = 256
VMEM_LIMIT = 58 * 1024 * 1024

W_HEAD = DN_QKV_WIDTH + DN_WIDTH
W_BA = 2 * DN_HEADS
W_SWA = SWA_WIDTH + 2 * SWA_KV_WIDTH
W_GATES = 2 * D_MODEL
D_IN = W_HEAD + W_BA + W_SWA + W_GATES
SWA_PIECES = W_SWA // MXU_WIDTH
GATE_PIECES = W_GATES // MXU_WIDTH
QKV_BLOCKS = DN_QKV_WIDTH // DN_HEAD_DIM

TILE = 128
STREAMS = 4
ROWS = TILE * STREAMS
DN_NCH = TILE // DN_CHUNK
DN_LOG_CHUNK = int(math.log2(DN_CHUNK))
CONV_TAIL = SUBLANES
SWA_BAND = 2 * SWA_BLOCK
SWA_SLOTS = SWA_WIDTH // LANES
SWA_HEADS_PER_SLOT = LANES // SWA_HEAD_DIM
W_CAST_ROWS = 256
GATES_EARLY = 3

F32 = jnp.float32
BF16 = jnp.bfloat16


def _mm(a, b, dims=(((1,), (0,)), ((), ()))):
    return lax.dot_general(a.astype(BF16), b.astype(BF16), dims, preferred_element_type=F32)


_NT = (((1,), (1,)), ((), ()))
_TN = (((0,), (0,)), ((), ()))


def _sigmoid(x):
    return 0.5 * jnp.tanh(0.5 * x) + 0.5


def _silu(x):
    h = 0.5 * x
    return h * jnp.tanh(h) + h


def _cast_w_kernel(wt_ref, w_head, w_ba, w_rest):
    for r0 in range(0, W_HEAD, W_CAST_ROWS):
        w_head[r0:r0 + W_CAST_ROWS, :] = wt_ref[r0:r0 + W_CAST_ROWS, :].astype(BF16)
    w_ba[...] = jnp.zeros_like(w_ba)
    w_ba[0:W_BA, :] = wt_ref[W_HEAD:W_HEAD + W_BA, :].astype(BF16)
    base = W_HEAD + W_BA
    for r0 in range(0, W_SWA + W_GATES, W_CAST_ROWS):
        w_rest[r0:r0 + W_CAST_ROWS, :] = wt_ref[base + r0:base + r0 + W_CAST_ROWS, :].astype(BF16)


def _cast_w(w_t):
    shapes = (W_HEAD, LANES, W_SWA + W_GATES)
    return pl.pallas_call(
        _cast_w_kernel,
        grid=(1,),
        in_specs=[pl.BlockSpec((D_IN, D_MODEL), lambda i: (0, 0), pipeline_mode=pl.Buffered(1))],
        out_specs=[pl.BlockSpec((r, D_MODEL), lambda i: (0, 0)) for r in shapes],
        out_shape=[jax.ShapeDtypeStruct((r, D_MODEL), BF16) for r in shapes],
        compiler_params=pltpu.CompilerParams(
            dimension_semantics=("arbitrary",), vmem_limit_bytes=VMEM_LIMIT),
        name="cast_w",
    )(w_t)


def _cast_specs(weights, steps, axis):
    specs = []
    for w in weights:
        blocks = next(b for b in range(steps, 0, -1)
                      if steps % b == 0 and w.shape[0] % (b * BF16_ROWS) == 0)
        repeat = steps // blocks
        specs.append(pl.BlockSpec((w.shape[0] // blocks, w.shape[1]),
                                  lambda *idx, repeat=repeat: (idx[axis] // repeat, 0)))
    return specs


def _delta_rule(qkv_scr, ba, z, alog_ref, dtb_ref, gain_ref, state, o_ref, fillers):
    fillers = iter(fillers)
    fill = lambda: next(fillers, lambda: None)()
    streams = range(STREAMS)
    row = lax.broadcasted_iota(jnp.int32, (TILE, TILE), 0)
    col = lax.broadcasted_iota(jnp.int32, (TILE, TILE), 1)
    same_chunk = (row >> DN_LOG_CHUNK) == (col >> DN_LOG_CHUNK)
    lower_incl = same_chunk & (row >= col)
    strict_lower = same_chunk & (row > col)
    cumsum_mat = jnp.where(lower_incl, 1.0, 0.0).astype(BF16)
    eye = jnp.where(row == col, 1.0, 0.0)
    rows_of = lambda s: slice(s * TILE, (s + 1) * TILE)

    beta, gc, rev, gc_t = [], [], [], []
    for s in streams:
        ba_s = ba[rows_of(s)]
        beta.append(_sigmoid(ba_s))
        xs = ba_s + dtb_ref[...]
        softplus = jnp.maximum(xs, 0.0) + jnp.log1p(jnp.exp(-jnp.abs(xs)))
        g = -jnp.exp(alog_ref[...]) * softplus
        g_hi = g.astype(BF16)
        g_lo = (g - g_hi.astype(F32)).astype(BF16)
        gc2 = jnp.dot(cumsum_mat, jnp.concatenate([g_hi, g_lo], axis=1),
                      preferred_element_type=F32)
        gc.append(gc2[:, :LANES] + gc2[:, LANES:])
        g_last = jnp.concatenate(
            [jnp.broadcast_to(gc[s][(c + 1) * DN_CHUNK - 1:(c + 1) * DN_CHUNK, :], (DN_CHUNK, LANES))
             for c in range(DN_NCH)], axis=0)
        rev.append(g_last - gc[s])
        gc_t.append(gc[s].T)

    units = [(s, h) for s in streams for h in range(DN_HEADS)]
    gcol = lambda arr, h: arr[:, DN_HEADS + h:DN_HEADS + h + 1]
    q, k, kb, rhs = {}, {}, {}, {}
    for s, h in units:
        q[s, h] = qkv_scr[h, rows_of(s), :].astype(F32)
        k[s, h] = qkv_scr[DN_HEADS + h, rows_of(s), :].astype(F32)
        vh = qkv_scr[2 * DN_HEADS + h, rows_of(s), :].astype(F32)
        beta_c = beta[s][:, h:h + 1]
        kb[s, h] = k[s, h] * beta_c
        rhs[s, h] = jnp.concatenate([vh * beta_c, kb[s, h] * jnp.exp(gcol(gc[s], h))], axis=1)
    gram = {p: _mm(jnp.concatenate([kb[p], q[p]], axis=0), k[p], _NT) for p in units}
    fill()
    qk, n_pow, inv = {}, {}, {}
    for s, h in units:
        gc_r = gc_t[s][DN_HEADS + h:DN_HEADS + h + 1, :]
        decay = jnp.exp(jnp.where(lower_incl, gcol(gc[s], h) - gc_r, -jnp.inf))
        qk[s, h] = gram[s, h][TILE:] * decay
        n_pow[s, h] = -jnp.where(strict_lower, gram[s, h][:TILE] * decay, 0.0)
        inv[s, h] = eye + n_pow[s, h]
    for _ in range(DN_LOG_CHUNK - 1):
        n_pow = {p: _mm(n_pow[p], n_pow[p]) for p in units}
        inv = {p: inv[p] + _mm(inv[p], n_pow[p]) for p in units}
        fill()
    sol = {p: _mm(inv[p], rhs[p]) for p in units}
    fill()
    u = {p: sol[p][:, :DN_HEAD_DIM] for p in units}
    wmat = {p: sol[p][:, DN_HEAD_DIM:] for p in units}
    qe = {(s, h): q[s, h] * jnp.exp(gcol(gc[s], h)) for s, h in units}
    kdec = {(s, h): k[s, h] * jnp.exp(gcol(rev[s], h)) for s, h in units}

    st = {(s, h): state[s, h] for s, h in units}
    v_new = {p: [] for p in units}
    o_inter = {p: [] for p in units}
    for c in range(DN_NCH):
        r0, r1 = c * DN_CHUNK, (c + 1) * DN_CHUNK
        ws = {p: _mm(jnp.concatenate([wmat[p][r0:r1], qe[p][r0:r1]], axis=0), st[p]) for p in units}
        for p in units:
            v_new[p].append(u[p][r0:r1] - ws[p][:DN_CHUNK])
            o_inter[p].append(ws[p][DN_CHUNK:])
        upd = {p: _mm(kdec[p][r0:r1], v_new[p][c], _TN) for p in units}
        st = {(s, h): st[s, h] * jnp.exp(gc[s][r1 - 1:r1, DN_HEADS + h:DN_HEADS + h + 1]) + upd[s, h]
              for s, h in units}
        fill()
    o_intra = {p: _mm(qk[p], jnp.concatenate(v_new[p], axis=0)) for p in units}
    for f in fillers:
        f()
    for s, h in units:
        state[s, h] = st[s, h]
        o = jnp.concatenate(o_inter[s, h], axis=0) + o_intra[s, h]
        o = o * lax.rsqrt(jnp.mean(o * o, axis=-1, keepdims=True) + EPS) * gain_ref[...]
        zh = z[rows_of(s), h * DN_HEAD_DIM:(h + 1) * DN_HEAD_DIM]
        o_ref[s, :, h * DN_HEAD_DIM:(h + 1) * DN_HEAD_DIM] = (o * _silu(zh)).astype(o_ref.dtype)


def _swa_attend(first_tab, q_slots, kv, sinks_ref, bias_ref, qgain_ref, kgain_ref, band_ref, o_ref,
                between):
    k = kv[:, :SWA_KV_WIDTH]
    between[0]()
    half = lax.broadcasted_iota(jnp.int32, (ROWS, LANES), 1) < SWA_HEAD_DIM

    def head_norm(p):
        p2 = p * p
        s_lo = jnp.sum(jnp.where(half, p2, 0.0), axis=-1, keepdims=True)
        s_hi = jnp.sum(jnp.where(half, 0.0, p2), axis=-1, keepdims=True)
        r_lo = lax.rsqrt(s_lo * (1.0 / SWA_HEAD_DIM) + EPS)
        r_hi = lax.rsqrt(s_hi * (1.0 / SWA_HEAD_DIM) + EPS)
        return p * jnp.where(half, r_lo, r_hi)

    qn = [head_norm(q_slots[j]) * qgain_ref[:, j * LANES:(j + 1) * LANES]
          for j in range(SWA_SLOTS)]
    kn = head_norm(k) * kgain_ref[...]

    for u in range(STREAMS):
        base = u * SWA_BAND
        band_ref[base:base + SWA_BLOCK, :] = band_ref[base + SWA_BLOCK:base + SWA_BAND, :]
        band_ref[base + SWA_BLOCK:base + SWA_BAND, :SWA_KV_WIDTH] = kn[u * TILE:(u + 1) * TILE]
        band_ref[base + SWA_BLOCK:base + SWA_BAND, SWA_KV_WIDTH:] = (
            kv[u * TILE:(u + 1) * TILE, SWA_KV_WIDTH:])
    k = band_ref[:, :SWA_KV_WIDTH]
    v = band_ref[:, SWA_KV_WIDTH:]

    lane = lax.broadcasted_iota(jnp.int32, (STREAMS * SWA_BAND, LANES), 1)
    low = lane < SWA_HEAD_DIM
    k_rot = pltpu.roll(k, SWA_HEAD_DIM, axis=1)
    v_rot = pltpu.roll(v, SWA_HEAD_DIM, axis=1)
    k_lo = (jnp.where(low, k, 0.0).astype(BF16), jnp.where(low, k_rot, 0.0).astype(BF16))
    k_hi = (jnp.where(low, 0.0, k_rot).astype(BF16), jnp.where(low, 0.0, k).astype(BF16))
    v_lo = (jnp.where(low, v, 0.0).astype(BF16), jnp.where(low, v_rot, 0.0).astype(BF16))
    v_hi = (jnp.where(low, 0.0, v_rot).astype(BF16), jnp.where(low, 0.0, v).astype(BF16))

    out_low = lax.broadcasted_iota(jnp.int32, (SWA_BLOCK, LANES), 1) < SWA_HEAD_DIM

    slots_per_kv = SWA_SLOTS // SWA_KV_HEADS
    units = [(u, j) for u in range(STREAMS) for j in range(SWA_SLOTS)]
    band_of = lambda arr, u: arr[u * SWA_BAND:(u + 1) * SWA_BAND]
    k_bd = {(u, kh): jnp.concatenate([band_of(k_lo[kh], u), band_of(k_hi[kh], u)], axis=0)
            for u in range(STREAMS) for kh in range(SWA_KV_HEADS)}
    v_bd = {(u, kh): jnp.concatenate([band_of(v_lo[kh], u), band_of(v_hi[kh], u)], axis=0)
            for u in range(STREAMS) for kh in range(SWA_KV_HEADS)}
    qn = [qn[j].astype(BF16) for j in range(SWA_SLOTS)]
    between[1]()
    logits = {(u, j): _mm(qn[j][u * TILE:(u + 1) * TILE], k_bd[u, j // slots_per_kv], _NT)
              for u, j in units}
    probs = {}
    inv_den = {}
    for u, j in units:
        for r in range(SWA_HEADS_PER_SLOT):
            hd = j * SWA_HEADS_PER_SLOT + r
            lg = logits[u, j][:, r * SWA_BAND:(r + 1) * SWA_BAND] + bias_ref[first_tab, hd]
            sink = sinks_ref[hd]
            m = jnp.maximum(jnp.max(lg, axis=-1, keepdims=True), sink)
            p = jnp.exp(lg - m)
            den = jnp.sum(p, axis=-1, keepdims=True) + jnp.exp(sink - m)
            probs[u, j, r] = p.astype(BF16)
            inv_den[u, j, r] = 1.0 / den
    between[2]()
    outs = {(u, j): _mm(jnp.concatenate([probs[u, j, r] for r in range(SWA_HEADS_PER_SLOT)],
                                        axis=1), v_bd[u, j // slots_per_kv])
            for u, j in units}
    between[3]()
    for u, j in units:
        o_ref[u, :, j * LANES:(j + 1) * LANES] = (
            outs[u, j] * jnp.where(out_low, inv_den[u, j, 0], inv_den[u, j, 1])).astype(o_ref.dtype)


def _mixer_kernel(sinks_ref, x_ref, gain_ref, w_head, w_ba, w_rest, bias_ref, qgain_ref, kgain_ref,
                  convw_ref, alog_ref, dtb_ref, dn_gain_ref, *refs, n_cast):
    cast_src = refs[:n_cast]
    yswa_ref, gates_ref, ydn_ref = refs[n_cast:n_cast + 3]
    cast_dst = refs[n_cast + 3:2 * n_cast + 3]
    band_ref, xbuf, qkv_scr, state = refs[2 * n_cast + 3:]
    for src, dst in zip(cast_src, cast_dst):
        dst[...] = src[...].astype(BF16)
    t = pl.program_id(1)

    @pl.when(t == 0)
    def _():
        band_ref[...] = jnp.zeros_like(band_ref)
        xbuf[:, :, TILE:, :] = jnp.zeros((QKV_BLOCKS, STREAMS, CONV_TAIL, DN_HEAD_DIM), F32)
        state[...] = jnp.zeros_like(state)

    x = x_ref[...].reshape(ROWS, D_MODEL)
    ms = jnp.mean(x * x, axis=-1, keepdims=True)
    h = (x * lax.rsqrt(ms + EPS) * gain_ref[...]).astype(BF16)
    proj = lambda w: lax.dot_general(h, w, _NT, preferred_element_type=F32)
    held = {}

    def dn_qkv():
        raw = proj(w_head[0:DN_QKV_WIDTH, :])
        xbuf[:, :, 0:CONV_TAIL, :] = xbuf[:, :, TILE:, :]
        for c in range(QKV_BLOCKS):
            for s in range(STREAMS):
                xbuf[c, s, CONV_TAIL:, :] = raw[s * TILE:(s + 1) * TILE,
                                                c * DN_HEAD_DIM:(c + 1) * DN_HEAD_DIM]

    def dn_preprocess():
        scale = DN_HEAD_DIM ** -0.5
        for c in range(QKV_BLOCKS):
            w = convw_ref[c]
            for s in range(STREAMS):
                acc = w[DN_CONV - 1:DN_CONV, :] * xbuf[c, s, CONV_TAIL:, :]
                for j in range(DN_CONV - 1):
                    off = CONV_TAIL - (DN_CONV - 1) + j
                    acc = acc + w[j:j + 1, :] * xbuf[c, s, off:off + TILE, :]
                act = _silu(acc)
                if c < 2 * DN_HEADS:
                    norm = lax.rsqrt(jnp.sum(act * act, axis=-1, keepdims=True) + EPS)
                    act = act * (norm * scale if c < DN_HEADS else norm)
                qkv_scr[c, s * TILE:(s + 1) * TILE, :] = act.astype(BF16)

    def gate_logits(pieces):
        for p in pieces:
            r0 = W_SWA + p * MXU_WIDTH
            gates_ref[p] = proj(w_rest[r0:r0 + MXU_WIDTH, :]).reshape(
                STREAMS, TILE, MXU_WIDTH).astype(gates_ref.dtype)

    def qkv_z_ba():
        dn_qkv()
        held["z"] = proj(w_head[DN_QKV_WIDTH:W_HEAD, :])
        held["ba"] = proj(w_ba[...])

    def gates_a_and_preprocess():
        gate_logits(range(0, GATES_EARLY))
        dn_preprocess()

    def delta_rule_with_gates():
        late = [functools.partial(gate_logits, [p]) for p in range(GATES_EARLY, GATE_PIECES)]
        _delta_rule(qkv_scr, held["ba"], held["z"], alog_ref, dtb_ref, dn_gain_ref, state,
                    ydn_ref, late)

    sw = [proj(w_rest[p * MXU_WIDTH:(p + 1) * MXU_WIDTH, :]) for p in range(SWA_PIECES)]
    slots_per_piece = MXU_WIDTH // LANES
    q_slots = [sw[j // slots_per_piece][:, (j % slots_per_piece) * LANES:
                                        (j % slots_per_piece + 1) * LANES] for j in range(SWA_SLOTS)]
    _swa_attend(jnp.where(t == 0, 0, 1), q_slots, sw[SWA_WIDTH // MXU_WIDTH], sinks_ref, bias_ref,
                qgain_ref, kgain_ref, band_ref, yswa_ref,
                between=(qkv_z_ba, gates_a_and_preprocess, delta_rule_with_gates, lambda: None))


def _mixer(x, gain, w_head, w_ba, w_rest, swa_params, dn_params, later_weights):
    sinks, bias_tabs, qgain, kgain = swa_params
    convw, alog_lane, dtb_lane, dn_gain = dn_params
    B, S, _ = x.shape
    tiles = S // TILE
    assert TILE == SWA_BLOCK and B % STREAMS == 0
    const = lambda shape: pl.BlockSpec(shape, lambda b, t, s: (0,) * len(shape),
                                       pipeline_mode=pl.Buffered(1))
    tok = lambda width: pl.BlockSpec((STREAMS, TILE, width), lambda b, t, s: (b, t, 0))
    cast_specs = _cast_specs(later_weights, tiles, axis=1)
    grid_spec = pltpu.PrefetchScalarGridSpec(
        num_scalar_prefetch=1,
        grid=(B // STREAMS, tiles),
        in_specs=[
            tok(D_MODEL),
            const((1, D_MODEL)),
            const((W_HEAD, D_MODEL)), const((LANES, D_MODEL)), const((W_SWA + W_GATES, D_MODEL)),
            const((2, SWA_HEADS, SWA_BLOCK, SWA_BAND)),
            const((1, SWA_WIDTH)),
            const((1, SWA_KV_WIDTH)),
            const((QKV_BLOCKS, DN_CONV, DN_HEAD_DIM)),
            const((1, LANES)), const((1, LANES)), const((1, DN_HEAD_DIM)),
        ] + cast_specs,
        out_specs=[
            tok(SWA_WIDTH),
            pl.BlockSpec((GATE_PIECES, STREAMS, TILE, MXU_WIDTH), lambda b, t, s: (0, b, t, 0)),
            tok(DN_WIDTH),
        ] + cast_specs,
        scratch_shapes=[pltpu.VMEM((STREAMS * SWA_BAND, 2 * SWA_KV_WIDTH), F32),
                        pltpu.VMEM((QKV_BLOCKS, STREAMS, TILE + CONV_TAIL, DN_HEAD_DIM), F32),
                        pltpu.VMEM((QKV_BLOCKS, ROWS, DN_HEAD_DIM), BF16),
                        pltpu.VMEM((STREAMS, DN_HEADS, DN_HEAD_DIM, DN_HEAD_DIM), F32)],
    )
    outs = pl.pallas_call(
        functools.partial(_mixer_kernel, n_cast=len(later_weights)),
        grid_spec=grid_spec,
        out_shape=[
            jax.ShapeDtypeStruct((B, S, SWA_WIDTH), BF16),
            jax.ShapeDtypeStruct((GATE_PIECES, B, S, MXU_WIDTH), BF16),
            jax.ShapeDtypeStruct((B, S, DN_WIDTH), BF16),
        ] + [jax.ShapeDtypeStruct(w.shape, BF16) for w in later_weights],
        compiler_params=pltpu.CompilerParams(
            dimension_semantics=("arbitrary", "arbitrary"), vmem_limit_bytes=VMEM_LIMIT),
        name="mixer",
    )(sinks, x, gain, w_head, w_ba, w_rest, bias_tabs, qgain, kgain, convw, alog_lane, dtb_lane,
      dn_gain, *later_weights)
    return outs[:3], outs[3:]


def _merge_ffn_kernel(x_ref, ydn_ref, yswa_ref, g_ref, wa_ref, wb_ref, wo_ref, gain_ref,
                      wg_ref, wu_ref, wd_ref, o_ref, act_ref):
    per_branch = D_MODEL // MXU_WIDTH
    ga, gb = [_sigmoid(jnp.concatenate(
        [g_ref[br * per_branch + p].reshape(ROWS, MXU_WIDTH).astype(F32)
         for p in range(per_branch)], axis=1)) for br in range(2)]
    ydn = ydn_ref[...].reshape(ROWS, DN_WIDTH)
    yswa = yswa_ref[...].reshape(ROWS, SWA_WIDTH)
    merged = ga * _mm(ydn, wa_ref[...]) + gb * _mm(yswa, wb_ref[...])
    x1 = x_ref[...].reshape(ROWS, D_MODEL) + _mm(merged, wo_ref[...])
    ms = jnp.mean(x1 * x1, axis=-1, keepdims=True)
    h2 = (x1 * lax.rsqrt(ms + EPS) * gain_ref[...]).astype(BF16)
    for c0 in range(0, D_FF, MXU_WIDTH):
        gate = jnp.dot(h2, wg_ref[:, c0:c0 + MXU_WIDTH], preferred_element_type=F32)
        up = jnp.dot(h2, wu_ref[:, c0:c0 + MXU_WIDTH], preferred_element_type=F32)
        act_ref[:, c0:c0 + MXU_WIDTH] = (_silu(gate) * up).astype(BF16)
    out = x1 + jnp.dot(act_ref[...], wd_ref[...], preferred_element_type=F32)
    o_ref[...] = out.reshape(STREAMS, TILE, D_MODEL)


def _merge_ffn(x, ydn, yswa, gates, wa, wb, wo, gain, wg, wu, wd):
    B, S, _ = x.shape
    tok = lambda width: pl.BlockSpec((STREAMS, TILE, width), lambda b, t: (b, t, 0))
    once = lambda a, b: pl.BlockSpec((a, b), lambda i, j: (0, 0), pipeline_mode=pl.Buffered(1))
    return pl.pallas_call(
        _merge_ffn_kernel,
        grid=(B // STREAMS, S // TILE),
        in_specs=[tok(D_MODEL), tok(DN_WIDTH), tok(SWA_WIDTH),
                  pl.BlockSpec((GATE_PIECES, STREAMS, TILE, MXU_WIDTH), lambda b, t: (0, b, t, 0)),
                  once(DN_WIDTH, D_MODEL), once(SWA_WIDTH, D_MODEL), once(D_MODEL, D_MODEL),
                  once(1, D_MODEL),
                  once(D_MODEL, D_FF), once(D_MODEL, D_FF), once(D_FF, D_MODEL)],
        out_specs=tok(D_MODEL),
        out_shape=jax.ShapeDtypeStruct((B, S, D_MODEL), F32),
        scratch_shapes=[pltpu.VMEM((ROWS, D_FF), BF16)],
        compiler_params=pltpu.CompilerParams(
            dimension_semantics=("arbitrary", "arbitrary"), vmem_limit_bytes=VMEM_LIMIT),
        name="merge_ffn",
    )(x, ydn, yswa, gates, wa, wb, wo, gain, wg, wu, wd)


def _t5_bucket_table():
    qi = jnp.arange(SWA_BLOCK)[:, None]
    kj = jnp.arange(SWA_BAND)[None, :]
    dist = SWA_BLOCK + qi - kj
    in_window = (dist >= 0) & (dist < WINDOW)
    n = jnp.maximum(dist, 0)
    max_exact = REL_BUCKETS // 2
    nf = jnp.maximum(n, 1).astype(F32)
    large = max_exact + (jnp.log(nf / max_exact) / math.log(REL_MAX_DIST / max_exact)
                         * (REL_BUCKETS - max_exact)).astype(jnp.int32)
    large = jnp.minimum(large, REL_BUCKETS - 1)
    return jnp.where(n < max_exact, n, large), in_window


def kernel(x, attn_norm, w_in, dn_conv, dn_a_log, dn_dt_bias, dn_out_norm, swa_q_norm, swa_k_norm, swa_sinks, rel_bias, w_branch_dn, w_branch_swa, w_out, ffn_norm, w_gate, w_up, w_down):
    depth = w_in.shape[0]
    bucket, in_window = _t5_bucket_table()
    rel = rel_bias.astype(F32)
    bias_tab = sum(jnp.where(bucket[None] == b, rel[b][:, None, None], 0.0)
                   for b in range(REL_BUCKETS))
    bias_tab = jnp.where(in_window[None], bias_tab, -jnp.inf)
    has_prev = jnp.arange(SWA_BAND)[None, None, :] >= SWA_BLOCK
    bias_tabs = jnp.stack([jnp.where(has_prev, bias_tab, -jnp.inf), bias_tab])

    for l in range(depth):
        w_head, w_ba, w_rest = _cast_w(jnp.swapaxes(w_in[l], 0, 1).astype(F32))
        later = [w[l].astype(F32) for w in (w_branch_dn, w_branch_swa, w_out, w_gate, w_up, w_down)]
        qgain = jnp.tile(swa_q_norm[l].astype(F32), SWA_HEADS)[None, :] * (SWA_HEAD_DIM ** -0.5)
        kgain = jnp.tile(swa_k_norm[l].astype(F32), SWA_KV_HEADS)[None, :]
        swa_params = (swa_sinks[l].astype(F32), bias_tabs, qgain, kgain)
        convw = dn_conv[l].astype(F32).reshape(DN_CONV, QKV_BLOCKS, DN_HEAD_DIM).transpose(1, 0, 2)
        pad_lo = jnp.zeros((DN_HEADS,), F32)
        pad_hi = jnp.zeros((LANES - 2 * DN_HEADS,), F32)
        alog_lane = jnp.concatenate([pad_lo, dn_a_log[l].astype(F32), pad_hi])[None, :]
        dtb_lane = jnp.concatenate([pad_lo, dn_dt_bias[l].astype(F32), pad_hi])[None, :]
        dn_params = (convw, alog_lane, dtb_lane, dn_out_norm[l][None, :].astype(F32))

        (y_swa, gates, y_dn), later_bf16 = _mixer(
            x, attn_norm[l][None, :], w_head, w_ba, w_rest, swa_params, dn_params, later)
        wa, wb, wo, wg, wu, wd = later_bf16
        x = _merge_ffn(x, y_dn, y_swa, gates, wa, wb, wo, ffn_norm[l][None, :], wg, wu, wd)
    return x
```

```python
import functools
import math

import jax
import jax.numpy as jnp
from jax import lax
from jax.experimental import pallas as pl
from jax.experimental.pallas import tpu as pltpu

D_MODEL = 1024
DN_HEADS = 4
DN_HEAD_DIM = 128
DN_WIDTH = DN_HEADS * DN_HEAD_DIM
DN_QKV_WIDTH = 3 * DN_WIDTH
DN_CONV = 4
DN_CHUNK = 64
SWA_HEADS = 8
SWA_KV_HEADS = 2
SWA_HEAD_DIM = 64
SWA_WIDTH = SWA_HEADS * SWA_HEAD_DIM
SWA_KV_WIDTH = SWA_KV_HEADS * SWA_HEAD_DIM
WINDOW = 128
SWA_BLOCK = 128
REL_BUCKETS = 32
REL_MAX_DIST = 128
D_FF = 2816
EPS = 1e-6

LANES = 128
SUBLANES = 8
BF16_ROWS = 2 * SUBLANES
MXU_WIDTH = 256
VMEM_LIMIT = 58 * 1024 * 1024

W_HEAD = DN_QKV_WIDTH + DN_WIDTH
W_BA = 2 * DN_HEADS
W_SWA = SWA_WIDTH + 2 * SWA_KV_WIDTH
W_GATES = 2 * D_MODEL
D_IN = W_HEAD + W_BA + W_SWA + W_GATES
SWA_PIECES = W_SWA // MXU_WIDTH
GATE_PIECES = W_GATES // MXU_WIDTH
QKV_BLOCKS = DN_QKV_WIDTH // DN_HEAD_DIM

TILE = 128
STREAMS = 4
ROWS = TILE * STREAMS
DN_NCH = TILE // DN_CHUNK
DN_LOG_CHUNK = int(math.log2(DN_CHUNK))
CONV_TAIL = SUBLANES
SWA_BAND = 2 * SWA_BLOCK
SWA_SLOTS = SWA_WIDTH // LANES
SWA_HEADS_PER_SLOT = LANES // SWA_HEAD_DIM
W_CAST_ROWS = 256
GATES_EARLY = 1

F32 = jnp.float32
BF16 = jnp.bfloat16


def _mm(a, b, dims=(((1,), (0,)), ((), ()))):
    return lax.dot_general(a.astype(BF16), b.astype(BF16), dims, preferred_element_type=F32)


_NT = (((1,), (1,)), ((), ()))
_TN = (((0,), (0,)), ((), ()))


def _sigmoid(x):
    return 0.5 * jnp.tanh(0.5 * x) + 0.5


def _silu(x):
    h = 0.5 * x
    return h * jnp.tanh(h) + h


def _cast_w_kernel(wt_ref, w_head, w_ba, w_rest):
    for r0 in range(0, W_HEAD, W_CAST_ROWS):
        w_head[r0:r0 + W_CAST_ROWS, :] = wt_ref[r0:r0 + W_CAST_ROWS, :].astype(BF16)
    w_ba[...] = jnp.zeros_like(w_ba)
    w_ba[0:W_BA, :] = wt_ref[W_HEAD:W_HEAD + W_BA, :].astype(BF16)
    base = W_HEAD + W_BA
    for r0 in range(0, W_SWA + W_GATES, W_CAST_ROWS):
        w_rest[r0:r0 + W_CAST_ROWS, :] = wt_ref[base + r0:base + r0 + W_CAST_ROWS, :].astype(BF16)


def _cast_w(w_t):
    shapes = (W_HEAD, LANES, W_SWA + W_GATES)
    return pl.pallas_call(
        _cast_w_kernel,
        grid=(1,),
        in_specs=[pl.BlockSpec((D_IN, D_MODEL), lambda i: (0, 0), pipeline_mode=pl.Buffered(1))],
        out_specs=[pl.BlockSpec((r, D_MODEL), lambda i: (0, 0)) for r in shapes],
        out_shape=[jax.ShapeDtypeStruct((r, D_MODEL), BF16) for r in shapes],
        compiler_params=pltpu.CompilerParams(
            dimension_semantics=("arbitrary",), vmem_limit_bytes=VMEM_LIMIT),
        name="cast_w",
    )(w_t)


def _cast_specs(weights, steps, axis):
    specs = []
    for w in weights:
        blocks = next(b for b in range(steps, 0, -1)
                      if steps % b == 0 and w.shape[0] % (b * BF16_ROWS) == 0)
        repeat = steps // blocks
        specs.append(pl.BlockSpec((w.shape[0] // blocks, w.shape[1]),
                                  lambda *idx, repeat=repeat: (idx[axis] // repeat, 0)))
    return specs


def _delta_rule(qkv_scr, ba, z, alog_ref, dtb_ref, gain_ref, state, o_ref, fillers):
    fillers = iter(fillers)
    fill = lambda: next(fillers, lambda: None)()
    streams = range(STREAMS)
    row = lax.broadcasted_iota(jnp.int32, (TILE, TILE), 0)
    col = lax.broadcasted_iota(jnp.int32, (TILE, TILE), 1)
    same_chunk = (row >> DN_LOG_CHUNK) == (col >> DN_LOG_CHUNK)
    lower_incl = same_chunk & (row >= col)
    strict_lower = same_chunk & (row > col)
    cumsum_mat = jnp.where(lower_incl, 1.0, 0.0).astype(BF16)
    eye = jnp.where(row == col, 1.0, 0.0)
    rows_of = lambda s: slice(s * TILE, (s + 1) * TILE)

    beta, gc, rev, gc_t = [], [], [], []
    for s in streams:
        ba_s = ba[rows_of(s)]
        beta.append(_sigmoid(ba_s))
        xs = ba_s + dtb_ref[...]
        softplus = jnp.maximum(xs, 0.0) + jnp.log1p(jnp.exp(-jnp.abs(xs)))
        g = -jnp.exp(alog_ref[...]) * softplus
        g_hi = g.astype(BF16)
        g_lo = (g - g_hi.astype(F32)).astype(BF16)
        gc2 = jnp.dot(cumsum_mat, jnp.concatenate([g_hi, g_lo], axis=1),
                      preferred_element_type=F32)
        gc.append(gc2[:, :LANES] + gc2[:, LANES:])
        g_last = jnp.concatenate(
            [jnp.broadcast_to(gc[s][(c + 1) * DN_CHUNK - 1:(c + 1) * DN_CHUNK, :], (DN_CHUNK, LANES))
             for c in range(DN_NCH)], axis=0)
        rev.append(g_last - gc[s])
        gc_t.append(gc[s].T)

    units = [(s, h) for s in streams for h in range(DN_HEADS)]
    gcol = lambda arr, h: arr[:, DN_HEADS + h:DN_HEADS + h + 1]
    q, k, kb, rhs = {}, {}, {}, {}
    for s, h in units:
        q[s, h] = qkv_scr[h, rows_of(s), :].astype(F32)
        k[s, h] = qkv_scr[DN_HEADS + h, rows_of(s), :].astype(F32)
        vh = qkv_scr[2 * DN_HEADS + h, rows_of(s), :].astype(F32)
        beta_c = beta[s][:, h:h + 1]
        kb[s, h] = k[s, h] * beta_c
        rhs[s, h] = jnp.concatenate([vh * beta_c, kb[s, h] * jnp.exp(gcol(gc[s], h))], axis=1)
    gram = {p: _mm(jnp.concatenate([kb[p], q[p]], axis=0), k[p], _NT) for p in units}
    fill()
    qk, n_pow, inv = {}, {}, {}
    for s, h in units:
        gc_r = gc_t[s][DN_HEADS + h:DN_HEADS + h + 1, :]
        decay = jnp.exp(jnp.where(lower_incl, gcol(gc[s], h) - gc_r, -jnp.inf))
        qk[s, h] = gram[s, h][TILE:] * decay
        n_pow[s, h] = -jnp.where(strict_lower, gram[s, h][:TILE] * decay, 0.0)
        inv[s, h] = eye + n_pow[s, h]
    for _ in range(DN_LOG_CHUNK - 1):
        n_pow = {p: _mm(n_pow[p], n_pow[p]) for p in units}
        inv = {p: inv[p] + _mm(inv[p], n_pow[p]) for p in units}
        fill()
    sol = {p: _mm(inv[p], rhs[p]) for p in units}
    fill()
    u = {p: sol[p][:, :DN_HEAD_DIM] for p in units}
    wmat = {p: sol[p][:, DN_HEAD_DIM:] for p in units}
    qe = {(s, h): q[s, h] * jnp.exp(gcol(gc[s], h)) for s, h in units}
    kdec = {(s, h): k[s, h] * jnp.exp(gcol(rev[s], h)) for s, h in units}

    st = {(s, h): state[s, h] for s, h in units}
    v_new = {p: [] for p in units}
    o_inter = {p: [] for p in units}
    for c in range(DN_NCH):
        r0, r1 = c * DN_CHUNK, (c + 1) * DN_CHUNK
        ws = {p: _mm(jnp.concatenate([wmat[p][r0:r1], qe[p][r0:r1]], axis=0), st[p]) for p in units}
        for p in units:
            v_new[p].append(u[p][r0:r1] - ws[p][:DN_CHUNK])
            o_inter[p].append(ws[p][DN_CHUNK:])
        upd = {p: _mm(kdec[p][r0:r1], v_new[p][c], _TN) for p in units}
        st = {(s, h): st[s, h] * jnp.exp(gc[s][r1 - 1:r1, DN_HEADS + h:DN_HEADS + h + 1]) + upd[s, h]
              for s, h in units}
        fill()
    o_intra = {p: _mm(qk[p], jnp.concatenate(v_new[p], axis=0)) for p in units}
    for f in fillers:
        f()
    for s, h in units:
        state[s, h] = st[s, h]
        o = jnp.concatenate(o_inter[s, h], axis=0) + o_intra[s, h]
        o = o * lax.rsqrt(jnp.mean(o * o, axis=-1, keepdims=True) + EPS) * gain_ref[...]
        zh = z[rows_of(s), h * DN_HEAD_DIM:(h + 1) * DN_HEAD_DIM]
        o_ref[s, :, h * DN_HEAD_DIM:(h + 1) * DN_HEAD_DIM] = (o * _silu(zh)).astype(o_ref.dtype)


def _swa_attend(first_tab, q_slots, kv, sinks_ref, bias_ref, qgain_ref, kgain_ref, band_ref, o_ref,
                between):
    k = kv[:, :SWA_KV_WIDTH]
    between[0]()
    half = lax.broadcasted_iota(jnp.int32, (ROWS, LANES), 1) < SWA_HEAD_DIM

    def head_norm(p):
        p2 = p * p
        s_lo = jnp.sum(jnp.where(half, p2, 0.0), axis=-1, keepdims=True)
        s_hi = jnp.sum(jnp.where(half, 0.0, p2), axis=-1, keepdims=True)
        r_lo = lax.rsqrt(s_lo * (1.0 / SWA_HEAD_DIM) + EPS)
        r_hi = lax.rsqrt(s_hi * (1.0 / SWA_HEAD_DIM) + EPS)
        return p * jnp.where(half, r_lo, r_hi)

    qn = [head_norm(q_slots[j]) * qgain_ref[:, j * LANES:(j + 1) * LANES]
          for j in range(SWA_SLOTS)]
    kn = head_norm(k) * kgain_ref[...]

    for u in range(STREAMS):
        base = u * SWA_BAND
        band_ref[base:base + SWA_BLOCK, :] = band_ref[base + SWA_BLOCK:base + SWA_BAND, :]
        band_ref[base + SWA_BLOCK:base + SWA_BAND, :SWA_KV_WIDTH] = kn[u * TILE:(u + 1) * TILE]
        band_ref[base + SWA_BLOCK:base + SWA_BAND, SWA_KV_WIDTH:] = (
            kv[u * TILE:(u + 1) * TILE, SWA_KV_WIDTH:])
    k = band_ref[:, :SWA_KV_WIDTH]
    v = band_ref[:, SWA_KV_WIDTH:]

    lane = lax.broadcasted_iota(jnp.int32, (STREAMS * SWA_BAND, LANES), 1)
    low = lane < SWA_HEAD_DIM
    k_rot = pltpu.roll(k, SWA_HEAD_DIM, axis=1)
    v_rot = pltpu.roll(v, SWA_HEAD_DIM, axis=1)
    k_lo = (jnp.where(low, k, 0.0).astype(BF16), jnp.where(low, k_rot, 0.0).astype(BF16))
    k_hi = (jnp.where(low, 0.0, k_rot).astype(BF16), jnp.where(low, 0.0, k).astype(BF16))
    v_lo = (jnp.where(low, v, 0.0).astype(BF16), jnp.where(low, v_rot, 0.0).astype(BF16))
    v_hi = (jnp.where(low, 0.0, v_rot).astype(BF16), jnp.where(low, 0.0, v).astype(BF16))

    out_low = lax.broadcasted_iota(jnp.int32, (SWA_BLOCK, LANES), 1) < SWA_HEAD_DIM

    slots_per_kv = SWA_SLOTS // SWA_KV_HEADS
    units = [(u, j) for u in range(STREAMS) for j in range(SWA_SLOTS)]
    band_of = lambda arr, u: arr[u * SWA_BAND:(u + 1) * SWA_BAND]
    k_bd = {(u, kh): jnp.concatenate([band_of(k_lo[kh], u), band_of(k_hi[kh], u)], axis=0)
            for u in range(STREAMS) for kh in range(SWA_KV_HEADS)}
    v_bd = {(u, kh): jnp.concatenate([band_of(v_lo[kh], u), band_of(v_hi[kh], u)], axis=0)
            for u in range(STREAMS) for kh in range(SWA_KV_HEADS)}
    qn = [qn[j].astype(BF16) for j in range(SWA_SLOTS)]
    between[1]()
    logits = {(u, j): _mm(qn[j][u * TILE:(u + 1) * TILE], k_bd[u, j // slots_per_kv], _NT)
              for u, j in units}
    probs = {}
    inv_den = {}
    for u, j in units:
        for r in range(SWA_HEADS_PER_SLOT):
            hd = j * SWA_HEADS_PER_SLOT + r
            lg = logits[u, j][:, r * SWA_BAND:(r + 1) * SWA_BAND] + bias_ref[first_tab, hd]
            sink = sinks_ref[hd]
            m = jnp.maximum(jnp.max(lg, axis=-1, keepdims=True), sink)
            p = jnp.exp(lg - m)
            den = jnp.sum(p, axis=-1, keepdims=True) + jnp.exp(sink - m)
            probs[u, j, r] = p.astype(BF16)
            inv_den[u, j, r] = 1.0 / den

    def values():
        outs = {(u, j): _mm(jnp.concatenate([probs[u, j, r] for r in range(SWA_HEADS_PER_SLOT)],
                                            axis=1), v_bd[u, j // slots_per_kv])
                for u, j in units}
        for u, j in units:
            o_ref[u, :, j * LANES:(j + 1) * LANES] = (
                outs[u, j] * jnp.where(out_low, inv_den[u, j, 0], inv_den[u, j, 1])
            ).astype(o_ref.dtype)

    between[2](values)


def _mixer_kernel(sinks_ref, x_ref, gain_ref, w_head, w_ba, w_rest, bias_ref, qgain_ref, kgain_ref,
                  convw_ref, alog_ref, dtb_ref, dn_gain_ref, *refs, n_cast):
    cast_src = refs[:n_cast]
    yswa_ref, gates_ref, ydn_ref = refs[n_cast:n_cast + 3]
    cast_dst = refs[n_cast + 3:2 * n_cast + 3]
    band_ref, xbuf, qkv_scr, state = refs[2 * n_cast + 3:]
    for src, dst in zip(cast_src, cast_dst):
        dst[...] = src[...].astype(BF16)
    t = pl.program_id(1)

    @pl.when(t == 0)
    def _():
        band_ref[...] = jnp.zeros_like(band_ref)
        xbuf[:, :, TILE:, :] = jnp.zeros((QKV_BLOCKS, STREAMS, CONV_TAIL, DN_HEAD_DIM), F32)
        state[...] = jnp.zeros_like(state)

    x = x_ref[...].reshape(ROWS, D_MODEL)
    ms = jnp.mean(x * x, axis=-1, keepdims=True)
    h = (x * lax.rsqrt(ms + EPS) * gain_ref[...]).astype(BF16)
    proj = lambda w: lax.dot_general(h, w, _NT, preferred_element_type=F32)
    held = {}

    def dn_qkv():
        raw = proj(w_head[0:DN_QKV_WIDTH, :])
        xbuf[:, :, 0:CONV_TAIL, :] = xbuf[:, :, TILE:, :]
        for c in range(QKV_BLOCKS):
            for s in range(STREAMS):
                xbuf[c, s, CONV_TAIL:, :] = raw[s * TILE:(s + 1) * TILE,
                                                c * DN_HEAD_DIM:(c + 1) * DN_HEAD_DIM]

    def dn_preprocess():
        scale = DN_HEAD_DIM ** -0.5
        for c in range(QKV_BLOCKS):
            w = convw_ref[c]
            for s in range(STREAMS):
                acc = w[DN_CONV - 1:DN_CONV, :] * xbuf[c, s, CONV_TAIL:, :]
                for j in range(DN_CONV - 1):
                    off = CONV_TAIL - (DN_CONV - 1) + j
                    acc = acc + w[j:j + 1, :] * xbuf[c, s, off:off + TILE, :]
                act = _silu(acc)
                if c < 2 * DN_HEADS:
                    norm = lax.rsqrt(jnp.sum(act * act, axis=-1, keepdims=True) + EPS)
                    act = act * (norm * scale if c < DN_HEADS else norm)
                qkv_scr[c, s * TILE:(s + 1) * TILE, :] = act.astype(BF16)

    def gate_logits(pieces):
        for p in pieces:
            r0 = W_SWA + p * MXU_WIDTH
            gates_ref[p] = proj(w_rest[r0:r0 + MXU_WIDTH, :]).reshape(
                STREAMS, TILE, MXU_WIDTH).astype(gates_ref.dtype)

    def qkv_z_ba():
        dn_qkv()
        held["z"] = proj(w_head[DN_QKV_WIDTH:W_HEAD, :])
        held["ba"] = proj(w_ba[...])

    def gates_a_and_preprocess():
        gate_logits(range(0, GATES_EARLY))
        dn_preprocess()

    def delta_rule_with_gates(swa_values):
        late = [functools.partial(gate_logits, [p]) for p in range(GATES_EARLY, GATE_PIECES)]
        _delta_rule(qkv_scr, held["ba"], held["z"], alog_ref, dtb_ref, dn_gain_ref, state,
                    ydn_ref, late)
        swa_values()

    sw = [proj(w_rest[p * MXU_WIDTH:(p + 1) * MXU_WIDTH, :]) for p in range(SWA_PIECES)]
    slots_per_piece = MXU_WIDTH // LANES
    q_slots = [sw[j // slots_per_piece][:, (j % slots_per_piece) * LANES:
                                        (j % slots_per_piece + 1) * LANES] for j in range(SWA_SLOTS)]
    _swa_attend(jnp.where(t == 0, 0, 1), q_slots, sw[SWA_WIDTH // MXU_WIDTH], sinks_ref, bias_ref,
                qgain_ref, kgain_ref, band_ref, yswa_ref,
                between=(qkv_z_ba, gates_a_and_preprocess, delta_rule_with_gates))


def _mixer(x, gain, w_head, w_ba, w_rest, swa_params, dn_params, later_weights):
    sinks, bias_tabs, qgain, kgain = swa_params
    convw, alog_lane, dtb_lane, dn_gain = dn_params
    B, S, _ = x.shape
    tiles = S // TILE
    assert TILE == SWA_BLOCK and B % STREAMS == 0
    const = lambda shape: pl.BlockSpec(shape, lambda b, t, s: (0,) * len(shape),
                                       pipeline_mode=pl.Buffered(1))
    tok = lambda width: pl.BlockSpec((STREAMS, TILE, width), lambda b, t, s: (b, t, 0))
    cast_specs = _cast_specs(later_weights, tiles, axis=1)
    grid_spec = pltpu.PrefetchScalarGridSpec(
        num_scalar_prefetch=1,
        grid=(B // STREAMS, tiles),
        in_specs=[
            tok(D_MODEL),
            const((1, D_MODEL)),
            const((W_HEAD, D_MODEL)), const((LANES, D_MODEL)), const((W_SWA + W_GATES, D_MODEL)),
            const((2, SWA_HEADS, SWA_BLOCK, SWA_BAND)),
            const((1, SWA_WIDTH)),
            const((1, SWA_KV_WIDTH)),
            const((QKV_BLOCKS, DN_CONV, DN_HEAD_DIM)),
            const((1, LANES)), const((1, LANES)), const((1, DN_HEAD_DIM)),
        ] + cast_specs,
        out_specs=[
            tok(SWA_WIDTH),
            pl.BlockSpec((GATE_PIECES, STREAMS, TILE, MXU_WIDTH), lambda b, t, s: (0, b, t, 0)),
            tok(DN_WIDTH),
        ] + cast_specs,
        scratch_shapes=[pltpu.VMEM((STREAMS * SWA_BAND, 2 * SWA_KV_WIDTH), F32),
                        pltpu.VMEM((QKV_BLOCKS, STREAMS, TILE + CONV_TAIL, DN_HEAD_DIM), F32),
                        pltpu.VMEM((QKV_BLOCKS, ROWS, DN_HEAD_DIM), BF16),
                        pltpu.VMEM((STREAMS, DN_HEADS, DN_HEAD_DIM, DN_HEAD_DIM), F32)],
    )
    outs = pl.pallas_call(
        functools.partial(_mixer_kernel, n_cast=len(later_weights)),
        grid_spec=grid_spec,
        out_shape=[
            jax.ShapeDtypeStruct((B, S, SWA_WIDTH), BF16),
            jax.ShapeDtypeStruct((GATE_PIECES, B, S, MXU_WIDTH), BF16),
            jax.ShapeDtypeStruct((B, S, DN_WIDTH), BF16),
        ] + [jax.ShapeDtypeStruct(w.shape, BF16) for w in later_weights],
        compiler_params=pltpu.CompilerParams(
            dimension_semantics=("arbitrary", "arbitrary"), vmem_limit_bytes=VMEM_LIMIT),
        name="mixer",
    )(sinks, x, gain, w_head, w_ba, w_rest, bias_tabs, qgain, kgain, convw, alog_lane, dtb_lane,
      dn_gain, *later_weights)
    return outs[:3], outs[3:]


def _merge_ffn_kernel(x_ref, ydn_ref, yswa_ref, g_ref, wa_ref, wb_ref, wo_ref, gain_ref,
                      wg_ref, wu_ref, wd_ref, o_ref, act_ref):
    per_branch = D_MODEL // MXU_WIDTH
    ga, gb = [_sigmoid(jnp.concatenate(
        [g_ref[br * per_branch + p].reshape(ROWS, MXU_WIDTH).astype(F32)
         for p in range(per_branch)], axis=1)) for br in range(2)]
    ydn = ydn_ref[...].reshape(ROWS, DN_WIDTH)
    yswa = yswa_ref[...].reshape(ROWS, SWA_WIDTH)
    merged = ga * _mm(ydn, wa_ref[...]) + gb * _mm(yswa, wb_ref[...])
    x1 = x_ref[...].reshape(ROWS, D_MODEL) + _mm(merged, wo_ref[...])
    ms = jnp.mean(x1 * x1, axis=-1, keepdims=True)
    h2 = (x1 * lax.rsqrt(ms + EPS) * gain_ref[...]).astype(BF16)
    for c0 in range(0, D_FF, MXU_WIDTH):
        gate = jnp.dot(h2, wg_ref[:, c0:c0 + MXU_WIDTH], preferred_element_type=F32)
        up = jnp.dot(h2, wu_ref[:, c0:c0 + MXU_WIDTH], preferred_element_type=F32)
        act_ref[:, c0:c0 + MXU_WIDTH] = (_silu(gate) * up).astype(BF16)
    out = x1 + jnp.dot(act_ref[...], wd_ref[...], preferred_element_type=F32)
    o_ref[...] = out.reshape(STREAMS, TILE, D_MODEL)


def _merge_ffn(x, ydn, yswa, gates, wa, wb, wo, gain, wg, wu, wd):
    B, S, _ = x.shape
    tok = lambda width: pl.BlockSpec((STREAMS, TILE, width), lambda b, t: (b, t, 0))
    once = lambda a, b: pl.BlockSpec((a, b), lambda i, j: (0, 0), pipeline_mode=pl.Buffered(1))
    return pl.pallas_call(
        _merge_ffn_kernel,
        grid=(B // STREAMS, S // TILE),
        in_specs=[tok(D_MODEL), tok(DN_WIDTH), tok(SWA_WIDTH),
                  pl.BlockSpec((GATE_PIECES, STREAMS, TILE, MXU_WIDTH), lambda b, t: (0, b, t, 0)),
                  once(DN_WIDTH, D_MODEL), once(SWA_WIDTH, D_MODEL), once(D_MODEL, D_MODEL),
                  once(1, D_MODEL),
                  once(D_MODEL, D_FF), once(D_MODEL, D_FF), once(D_FF, D_MODEL)],
        out_specs=tok(D_MODEL),
        out_shape=jax.ShapeDtypeStruct((B, S, D_MODEL), F32),
        scratch_shapes=[pltpu.VMEM((ROWS, D_FF), BF16)],
        compiler_params=pltpu.CompilerParams(
            dimension_semantics=("arbitrary", "arbitrary"), vmem_limit_bytes=VMEM_LIMIT),
        name="merge_ffn",
    )(x, ydn, yswa, gates, wa, wb, wo, gain, wg, wu, wd)


def _t5_bucket_table():
    qi = jnp.arange(SWA_BLOCK)[:, None]
    kj = jnp.arange(SWA_BAND)[None, :]
    dist = SWA_BLOCK + qi - kj
    in_window = (dist >= 0) & (dist < WINDOW)
    n = jnp.maximum(dist, 0)
    max_exact = REL_BUCKETS // 2
    nf = jnp.maximum(n, 1).astype(F32)
    large = max_exact + (jnp.log(nf / max_exact) / math.log(REL_MAX_DIST / max_exact)
                         * (REL_BUCKETS - max_exact)).astype(jnp.int32)
    large = jnp.minimum(large, REL_BUCKETS - 1)
    return jnp.where(n < max_exact, n, large), in_window


def kernel(x, attn_norm, w_in, dn_conv, dn_a_log, dn_dt_bias, dn_out_norm, swa_q_norm, swa_k_norm, swa_sinks, rel_bias, w_branch_dn, w_branch_swa, w_out, ffn_norm, w_gate, w_up, w_down):
    depth = w_in.shape[0]
    bucket, in_window = _t5_bucket_table()
    rel = rel_bias.astype(F32)
    bias_tab = sum(jnp.where(bucket[None] == b, rel[b][:, None, None], 0.0)
                   for b in range(REL_BUCKETS))
    bias_tab = jnp.where(in_window[None], bias_tab, -jnp.inf)
    has_prev = jnp.arange(SWA_BAND)[None, None, :] >= SWA_BLOCK
    bias_tabs = jnp.stack([jnp.where(has_prev, bias_tab, -jnp.inf), bias_tab])

    for l in range(depth):
        w_head, w_ba, w_rest = _cast_w(jnp.swapaxes(w_in[l], 0, 1).astype(F32))
        later = [w[l].astype(F32) for w in (w_branch_dn, w_branch_swa, w_out, w_gate, w_up, w_down)]
        qgain = jnp.tile(swa_q_norm[l].astype(F32), SWA_HEADS)[None, :] * (SWA_HEAD_DIM ** -0.5)
        kgain = jnp.tile(swa_k_norm[l].astype(F32), SWA_KV_HEADS)[None, :]
        swa_params = (swa_sinks[l].astype(F32), bias_tabs, qgain, kgain)
        convw = dn_conv[l].astype(F32).reshape(DN_CONV, QKV_BLOCKS, DN_HEAD_DIM).transpose(1, 0, 2)
        pad_lo = jnp.zeros((DN_HEADS,), F32)
        pad_hi = jnp.zeros((LANES - 2 * DN_HEADS,), F32)
        alog_lane = jnp.concatenate([pad_lo, dn_a_log[l].astype(F32), pad_hi])[None, :]
        dtb_lane = jnp.concatenate([pad_lo, dn_dt_bias[l].astype(F32), pad_hi])[None, :]
        dn_params = (convw, alog_lane, dtb_lane, dn_out_norm[l][None, :].astype(F32))

        (y_swa, gates, y_dn), later_bf16 = _mixer(
            x, attn_norm[l][None, :], w_head, w_ba, w_rest, swa_params, dn_params, later)
        wa, wb, wo, wg, wu, wd = later_bf16
        x = _merge_ffn(x, y_dn, y_swa, gates, wa, wb, wo, ffn_norm[l][None, :], wg, wu, wd)
    return x
```

```python
import functools
import math

import jax
import jax.numpy as jnp
from jax import lax
from jax.experimental import pallas as pl
from jax.experimental.pallas import tpu as pltpu

D_MODEL = 1024
DN_HEADS = 4
DN_HEAD_DIM = 128
DN_WIDTH = DN_HEADS * DN_HEAD_DIM
DN_QKV_WIDTH = 3 * DN_WIDTH
DN_CONV = 4
DN_CHUNK = 64
SWA_HEADS = 8
SWA_KV_HEADS = 2
SWA_HEAD_DIM = 64
SWA_WIDTH = SWA_HEADS * SWA_HEAD_DIM
SWA_KV_WIDTH = SWA_KV_HEADS * SWA_HEAD_DIM
WINDOW = 128
SWA_BLOCK = 128
REL_BUCKETS = 32
REL_MAX_DIST = 128
D_FF = 2816
EPS = 1e-6

LANES = 128
SUBLANES = 8
BF16_ROWS = 2 * SUBLANES
MXU_WIDTH = 256
VMEM_LIMIT = 58 * 1024 * 1024

W_HEAD = DN_QKV_WIDTH + DN_WIDTH
W_BA = 2 * DN_HEADS
W_SWA = SWA_WIDTH + 2 * SWA_KV_WIDTH
W_GATES = 2 * D_MODEL
D_IN = W_HEAD + W_BA + W_SWA + W_GATES
SWA_PIECES = W_SWA // MXU_WIDTH
GATE_PIECES = W_GATES // MXU_WIDTH
QKV_BLOCKS = DN_QKV_WIDTH // DN_HEAD_DIM

TILE = 128
STREAMS = 4
ROWS = TILE * STREAMS
DN_NCH = TILE // DN_CHUNK
DN_LOG_CHUNK = int(math.log2(DN_CHUNK))
CONV_TAIL = SUBLANES
SWA_BAND = 2 * SWA_BLOCK
SWA_SLOTS = SWA_WIDTH // LANES
SWA_HEADS_PER_SLOT = LANES // SWA_HEAD_DIM
W_CAST_ROWS = 256
GATES_EARLY = 1

F32 = jnp.float32
BF16 = jnp.bfloat16


def _mm(a, b, dims=(((1,), (0,)), ((), ()))):
    return lax.dot_general(a.astype(BF16), b.astype(BF16), dims, preferred_element_type=F32)


_NT = (((1,), (1,)), ((), ()))
_TN = (((0,), (0,)), ((), ()))


def _sigmoid(x):
    return 0.5 * jnp.tanh(0.5 * x) + 0.5


def _silu(x):
    h = 0.5 * x
    return h * jnp.tanh(h) + h


def _cast_w_kernel(wt_ref, w_head, w_ba, w_rest):
    for r0 in range(0, W_HEAD, W_CAST_ROWS):
        w_head[r0:r0 + W_CAST_ROWS, :] = wt_ref[r0:r0 + W_CAST_ROWS, :].astype(BF16)
    w_ba[...] = jnp.zeros_like(w_ba)
    w_ba[0:W_BA, :] = wt_ref[W_HEAD:W_HEAD + W_BA, :].astype(BF16)
    base = W_HEAD + W_BA
    for r0 in range(0, W_SWA + W_GATES, W_CAST_ROWS):
        w_rest[r0:r0 + W_CAST_ROWS, :] = wt_ref[base + r0:base + r0 + W_CAST_ROWS, :].astype(BF16)


def _cast_w(w_t):
    shapes = (W_HEAD, LANES, W_SWA + W_GATES)
    return pl.pallas_call(
        _cast_w_kernel,
        grid=(1,),
        in_specs=[pl.BlockSpec((D_IN, D_MODEL), lambda i: (0, 0), pipeline_mode=pl.Buffered(1))],
        out_specs=[pl.BlockSpec((r, D_MODEL), lambda i: (0, 0)) for r in shapes],
        out_shape=[jax.ShapeDtypeStruct((r, D_MODEL), BF16) for r in shapes],
        compiler_params=pltpu.CompilerParams(
            dimension_semantics=("arbitrary",), vmem_limit_bytes=VMEM_LIMIT),
        name="cast_w",
    )(w_t)


def _cast_specs(weights, steps, axis):
    specs = []
    for w in weights:
        blocks = next(b for b in range(steps, 0, -1)
                      if steps % b == 0 and w.shape[0] % (b * BF16_ROWS) == 0)
        repeat = steps // blocks
        specs.append(pl.BlockSpec((w.shape[0] // blocks, w.shape[1]),
                                  lambda *idx, repeat=repeat: (idx[axis] // repeat, 0)))
    return specs


def _delta_rule(qkv_scr, ba, z, alog_ref, dtb_ref, gain_ref, state, o_ref, fillers):
    fillers = iter(fillers)
    fill = lambda: next(fillers, lambda: None)()
    streams = range(STREAMS)
    row = lax.broadcasted_iota(jnp.int32, (TILE, TILE), 0)
    col = lax.broadcasted_iota(jnp.int32, (TILE, TILE), 1)
    same_chunk = (row >> DN_LOG_CHUNK) == (col >> DN_LOG_CHUNK)
    lower_incl = same_chunk & (row >= col)
    strict_lower = same_chunk & (row > col)
    cumsum_mat = jnp.where(lower_incl, 1.0, 0.0).astype(BF16)
    eye = jnp.where(row == col, 1.0, 0.0)
    rows_of = lambda s: slice(s * TILE, (s + 1) * TILE)

    beta, gc, rev, gc_t = [], [], [], []
    for s in streams:
        ba_s = ba[rows_of(s)]
        beta.append(_sigmoid(ba_s))
        xs = ba_s + dtb_ref[...]
        softplus = jnp.maximum(xs, 0.0) + jnp.log1p(jnp.exp(-jnp.abs(xs)))
        g = -jnp.exp(alog_ref[...]) * softplus
        g_hi = g.astype(BF16)
        g_lo = (g - g_hi.astype(F32)).astype(BF16)
        gc2 = jnp.dot(cumsum_mat, jnp.concatenate([g_hi, g_lo], axis=1),
                      preferred_element_type=F32)
        gc.append(gc2[:, :LANES] + gc2[:, LANES:])
        g_last = jnp.concatenate(
            [jnp.broadcast_to(gc[s][(c + 1) * DN_CHUNK - 1:(c + 1) * DN_CHUNK, :], (DN_CHUNK, LANES))
             for c in range(DN_NCH)], axis=0)
        rev.append(g_last - gc[s])
        gc_t.append(gc[s].T)

    units = [(s, h) for s in streams for h in range(DN_HEADS)]
    gcol = lambda arr, h: arr[:, DN_HEADS + h:DN_HEADS + h + 1]
    q, k, kb, rhs = {}, {}, {}, {}
    for s, h in units:
        q[s, h] = qkv_scr[h, rows_of(s), :].astype(F32)
        k[s, h] = qkv_scr[DN_HEADS + h, rows_of(s), :].astype(F32)
        vh = qkv_scr[2 * DN_HEADS + h, rows_of(s), :].astype(F32)
        beta_c = beta[s][:, h:h + 1]
        kb[s, h] = k[s, h] * beta_c
        rhs[s, h] = jnp.concatenate([vh * beta_c, kb[s, h] * jnp.exp(gcol(gc[s], h))], axis=1)
    gram = {p: _mm(jnp.concatenate([kb[p], q[p]], axis=0), k[p], _NT) for p in units}
    fill()
    qk, n_pow, inv = {}, {}, {}
    for s, h in units:
        gc_r = gc_t[s][DN_HEADS + h:DN_HEADS + h + 1, :]
        decay = jnp.exp(jnp.where(lower_incl, gcol(gc[s], h) - gc_r, -jnp.inf))
        qk[s, h] = gram[s, h][TILE:] * decay
        n_pow[s, h] = -jnp.where(strict_lower, gram[s, h][:TILE] * decay, 0.0)
        inv[s, h] = eye + n_pow[s, h]
    for _ in range(DN_LOG_CHUNK - 1):
        n_pow = {p: _mm(n_pow[p], n_pow[p]) for p in units}
        inv = {p: inv[p] + _mm(inv[p], n_pow[p]) for p in units}
        fill()
    sol = {p: _mm(inv[p], rhs[p]) for p in units}
    fill()
    u = {p: sol[p][:, :DN_HEAD_DIM] for p in units}
    wmat = {p: sol[p][:, DN_HEAD_DIM:] for p in units}
    qe = {(s, h): q[s, h] * jnp.exp(gcol(gc[s], h)) for s, h in units}
    kdec = {(s, h): k[s, h] * jnp.exp(gcol(rev[s], h)) for s, h in units}

    st = {(s, h): state[s, h] for s, h in units}
    v_new = {p: [] for p in units}
    o_inter = {p: [] for p in units}
    for c in range(DN_NCH):
        r0, r1 = c * DN_CHUNK, (c + 1) * DN_CHUNK
        ws = {p: _mm(jnp.concatenate([wmat[p][r0:r1], qe[p][r0:r1]], axis=0), st[p]) for p in units}
        for p in units:
            v_new[p].append(u[p][r0:r1] - ws[p][:DN_CHUNK])
            o_inter[p].append(ws[p][DN_CHUNK:])
        upd = {p: _mm(kdec[p][r0:r1], v_new[p][c], _TN) for p in units}
        st = {(s, h): st[s, h] * jnp.exp(gc[s][r1 - 1:r1, DN_HEADS + h:DN_HEADS + h + 1]) + upd[s, h]
              for s, h in units}
        fill()
    o_intra = {p: _mm(qk[p], jnp.concatenate(v_new[p], axis=0)) for p in units}
    for f in fillers:
        f()
    for s, h in units:
        state[s, h] = st[s, h]
        o = jnp.concatenate(o_inter[s, h], axis=0) + o_intra[s, h]
        o = o * lax.rsqrt(jnp.mean(o * o, axis=-1, keepdims=True) + EPS) * gain_ref[...]
        zh = z[rows_of(s), h * DN_HEAD_DIM:(h + 1) * DN_HEAD_DIM]
        o_ref[s, :, h * DN_HEAD_DIM:(h + 1) * DN_HEAD_DIM] = (o * _silu(zh)).astype(o_ref.dtype)


def _swa_attend(first_tab, q_slots, kv, sinks_ref, bias_ref, qgain_ref, kgain_ref, band_ref, o_ref,
                between):
    k = kv[:, :SWA_KV_WIDTH]
    between[0]()
    half = lax.broadcasted_iota(jnp.int32, (ROWS, LANES), 1) < SWA_HEAD_DIM

    def head_norm(p):
        p2 = p * p
        s_lo = jnp.sum(jnp.where(half, p2, 0.0), axis=-1, keepdims=True)
        s_hi = jnp.sum(jnp.where(half, 0.0, p2), axis=-1, keepdims=True)
        r_lo = lax.rsqrt(s_lo * (1.0 / SWA_HEAD_DIM) + EPS)
        r_hi = lax.rsqrt(s_hi * (1.0 / SWA_HEAD_DIM) + EPS)
        return p * jnp.where(half, r_lo, r_hi)

    qn = [head_norm(q_slots[j]) * qgain_ref[:, j * LANES:(j + 1) * LANES]
          for j in range(SWA_SLOTS)]
    kn = head_norm(k) * kgain_ref[...]

    for u in range(STREAMS):
        base = u * SWA_BAND
        band_ref[base:base + SWA_BLOCK, :] = band_ref[base + SWA_BLOCK:base + SWA_BAND, :]
        band_ref[base + SWA_BLOCK:base + SWA_BAND, :SWA_KV_WIDTH] = kn[u * TILE:(u + 1) * TILE]
        band_ref[base + SWA_BLOCK:base + SWA_BAND, SWA_KV_WIDTH:] = (
            kv[u * TILE:(u + 1) * TILE, SWA_KV_WIDTH:])
    k = band_ref[:, :SWA_KV_WIDTH]
    v = band_ref[:, SWA_KV_WIDTH:]

    lane = lax.broadcasted_iota(jnp.int32, (STREAMS * SWA_BAND, LANES), 1)
    low = lane < SWA_HEAD_DIM
    k_rot = pltpu.roll(k, SWA_HEAD_DIM, axis=1)
    v_rot = pltpu.roll(v, SWA_HEAD_DIM, axis=1)
    k_lo = (jnp.where(low, k, 0.0).astype(BF16), jnp.where(low, k_rot, 0.0).astype(BF16))
    k_hi = (jnp.where(low, 0.0, k_rot).astype(BF16), jnp.where(low, 0.0, k).astype(BF16))
    v_lo = (jnp.where(low, v, 0.0).astype(BF16), jnp.where(low, v_rot, 0.0).astype(BF16))
    v_hi = (jnp.where(low, 0.0, v_rot).astype(BF16), jnp.where(low, 0.0, v).astype(BF16))

    out_low = lax.broadcasted_iota(jnp.int32, (SWA_BLOCK, LANES), 1) < SWA_HEAD_DIM

    slots_per_kv = SWA_SLOTS // SWA_KV_HEADS
    units = [(u, j) for u in range(STREAMS) for j in range(SWA_SLOTS)]
    band_of = lambda arr, u: arr[u * SWA_BAND:(u + 1) * SWA_BAND]
    k_bd = {(u, kh): jnp.concatenate([band_of(k_lo[kh], u), band_of(k_hi[kh], u)], axis=0)
            for u in range(STREAMS) for kh in range(SWA_KV_HEADS)}
    v_bd = {(u, kh): jnp.concatenate([band_of(v_lo[kh], u), band_of(v_hi[kh], u)], axis=0)
            for u in range(STREAMS) for kh in range(SWA_KV_HEADS)}
    qn = [qn[j].astype(BF16) for j in range(SWA_SLOTS)]
    between[1]()
    logits = {(u, j): _mm(qn[j][u * TILE:(u + 1) * TILE], k_bd[u, j // slots_per_kv], _NT)
              for u, j in units}
    probs = {}
    inv_den = {}
    for u, j in units:
        for r in range(SWA_HEADS_PER_SLOT):
            hd = j * SWA_HEADS_PER_SLOT + r
            lg = logits[u, j][:, r * SWA_BAND:(r + 1) * SWA_BAND] + bias_ref[first_tab, hd]
            sink = sinks_ref[hd]
            m = jnp.maximum(jnp.max(lg, axis=-1, keepdims=True), sink)
            p = jnp.exp(lg - m)
            den = jnp.sum(p, axis=-1, keepdims=True) + jnp.exp(sink - m)
            probs[u, j, r] = p.astype(BF16)
            inv_den[u, j, r] = 1.0 / den

    def values():
        outs = {(u, j): _mm(jnp.concatenate([probs[u, j, r] for r in range(SWA_HEADS_PER_SLOT)],
                                            axis=1), v_bd[u, j // slots_per_kv])
                for u, j in units}
        for u, j in units:
            o_ref[u, :, j * LANES:(j + 1) * LANES] = (
                outs[u, j] * jnp.where(out_low, inv_den[u, j, 0], inv_den[u, j, 1])
            ).astype(o_ref.dtype)

    between[2](values)


def _mixer_kernel(sinks_ref, x_ref, x_next_ref, gain_ref, w_head, w_ba, w_rest, bias_ref, qgain_ref, kgain_ref,
                  convw_ref, alog_ref, dtb_ref, dn_gain_ref, *refs, n_cast):
    cast_src = refs[:n_cast]
    yswa_ref, gates_ref, ydn_ref = refs[n_cast:n_cast + 3]
    cast_dst = refs[n_cast + 3:2 * n_cast + 3]
    band_ref, xbuf, qkv_scr, state, h_ref = refs[2 * n_cast + 3:]
    for src, dst in zip(cast_src, cast_dst):
        dst[...] = src[...].astype(BF16)
    t = pl.program_id(1)

    def normed(xr):
        x = xr[...].reshape(ROWS, D_MODEL)
        ms = jnp.mean(x * x, axis=-1, keepdims=True)
        return (x * lax.rsqrt(ms + EPS) * gain_ref[...]).astype(BF16)

    @pl.when(t == 0)
    def _():
        band_ref[...] = jnp.zeros_like(band_ref)
        xbuf[:, :, TILE:, :] = jnp.zeros((QKV_BLOCKS, STREAMS, CONV_TAIL, DN_HEAD_DIM), F32)
        state[...] = jnp.zeros_like(state)
        h_ref[...] = normed(x_ref)

    proj = lambda w: lax.dot_general(h_ref[...], w, _NT, preferred_element_type=F32)
    held = {}

    def dn_qkv():
        raw = proj(w_head[0:DN_QKV_WIDTH, :])
        xbuf[:, :, 0:CONV_TAIL, :] = xbuf[:, :, TILE:, :]
        for c in range(QKV_BLOCKS):
            for s in range(STREAMS):
                xbuf[c, s, CONV_TAIL:, :] = raw[s * TILE:(s + 1) * TILE,
                                                c * DN_HEAD_DIM:(c + 1) * DN_HEAD_DIM]

    def dn_preprocess():
        scale = DN_HEAD_DIM ** -0.5
        for c in range(QKV_BLOCKS):
            w = convw_ref[c]
            for s in range(STREAMS):
                acc = w[DN_CONV - 1:DN_CONV, :] * xbuf[c, s, CONV_TAIL:, :]
                for j in range(DN_CONV - 1):
                    off = CONV_TAIL - (DN_CONV - 1) + j
                    acc = acc + w[j:j + 1, :] * xbuf[c, s, off:off + TILE, :]
                act = _silu(acc)
                if c < 2 * DN_HEADS:
                    norm = lax.rsqrt(jnp.sum(act * act, axis=-1, keepdims=True) + EPS)
                    act = act * (norm * scale if c < DN_HEADS else norm)
                qkv_scr[c, s * TILE:(s + 1) * TILE, :] = act.astype(BF16)

    def gate_logits(pieces):
        for p in pieces:
            r0 = W_SWA + p * MXU_WIDTH
            gates_ref[p] = proj(w_rest[r0:r0 + MXU_WIDTH, :]).reshape(
                STREAMS, TILE, MXU_WIDTH).astype(gates_ref.dtype)

    def qkv_z_ba():
        dn_qkv()
        held["z"] = proj(w_head[DN_QKV_WIDTH:W_HEAD, :])
        held["ba"] = proj(w_ba[...])

    def gates_a_and_preprocess():
        gate_logits(range(0, GATES_EARLY))
        dn_preprocess()

    def delta_rule_with_gates(swa_values):
        late = [functools.partial(gate_logits, [p]) for p in range(GATES_EARLY, GATE_PIECES)]
        _delta_rule(qkv_scr, held["ba"], held["z"], alog_ref, dtb_ref, dn_gain_ref, state,
                    ydn_ref, late)
        swa_values()

    sw = [proj(w_rest[p * MXU_WIDTH:(p + 1) * MXU_WIDTH, :]) for p in range(SWA_PIECES)]
    slots_per_piece = MXU_WIDTH // LANES
    q_slots = [sw[j // slots_per_piece][:, (j % slots_per_piece) * LANES:
                                        (j % slots_per_piece + 1) * LANES] for j in range(SWA_SLOTS)]
    _swa_attend(jnp.where(t == 0, 0, 1), q_slots, sw[SWA_WIDTH // MXU_WIDTH], sinks_ref, bias_ref,
                qgain_ref, kgain_ref, band_ref, yswa_ref,
                between=(qkv_z_ba, gates_a_and_preprocess, delta_rule_with_gates))
    h_ref[...] = normed(x_next_ref)


def _mixer(x, gain, w_head, w_ba, w_rest, swa_params, dn_params, later_weights):
    sinks, bias_tabs, qgain, kgain = swa_params
    convw, alog_lane, dtb_lane, dn_gain = dn_params
    B, S, _ = x.shape
    tiles = S // TILE
    assert TILE == SWA_BLOCK and B % STREAMS == 0
    const = lambda shape: pl.BlockSpec(shape, lambda b, t, s: (0,) * len(shape),
                                       pipeline_mode=pl.Buffered(1))
    tok = lambda width: pl.BlockSpec((STREAMS, TILE, width), lambda b, t, s: (b, t, 0))
    cast_specs = _cast_specs(later_weights, tiles, axis=1)
    grid_spec = pltpu.PrefetchScalarGridSpec(
        num_scalar_prefetch=1,
        grid=(B // STREAMS, tiles),
        in_specs=[
            tok(D_MODEL),
            pl.BlockSpec((STREAMS, TILE, D_MODEL),
                         lambda b, t, s: (b, jnp.minimum(t + 1, tiles - 1), 0)),
            const((1, D_MODEL)),
            const((W_HEAD, D_MODEL)), const((LANES, D_MODEL)), const((W_SWA + W_GATES, D_MODEL)),
            const((2, SWA_HEADS, SWA_BLOCK, SWA_BAND)),
            const((1, SWA_WIDTH)),
            const((1, SWA_KV_WIDTH)),
            const((QKV_BLOCKS, DN_CONV, DN_HEAD_DIM)),
            const((1, LANES)), const((1, LANES)), const((1, DN_HEAD_DIM)),
        ] + cast_specs,
        out_specs=[
            tok(SWA_WIDTH),
            pl.BlockSpec((GATE_PIECES, STREAMS, TILE, MXU_WIDTH), lambda b, t, s: (0, b, t, 0)),
            tok(DN_WIDTH),
        ] + cast_specs,
        scratch_shapes=[pltpu.VMEM((STREAMS * SWA_BAND, 2 * SWA_KV_WIDTH), F32),
                        pltpu.VMEM((QKV_BLOCKS, STREAMS, TILE + CONV_TAIL, DN_HEAD_DIM), F32),
                        pltpu.VMEM((QKV_BLOCKS, ROWS, DN_HEAD_DIM), BF16),
                        pltpu.VMEM((STREAMS, DN_HEADS, DN_HEAD_DIM, DN_HEAD_DIM), F32),
                        pltpu.VMEM((ROWS, D_MODEL), BF16)],
    )
    outs = pl.pallas_call(
        functools.partial(_mixer_kernel, n_cast=len(later_weights)),
        grid_spec=grid_spec,
        out_shape=[
            jax.ShapeDtypeStruct((B, S, SWA_WIDTH), BF16),
            jax.ShapeDtypeStruct((GATE_PIECES, B, S, MXU_WIDTH), BF16),
            jax.ShapeDtypeStruct((B, S, DN_WIDTH), BF16),
        ] + [jax.ShapeDtypeStruct(w.shape, BF16) for w in later_weights],
        compiler_params=pltpu.CompilerParams(
            dimension_semantics=("arbitrary", "arbitrary"), vmem_limit_bytes=VMEM_LIMIT),
        name="mixer",
    )(sinks, x, x, gain, w_head, w_ba, w_rest, bias_tabs, qgain, kgain, convw, alog_lane, dtb_lane,
      dn_gain, *later_weights)
    return outs[:3], outs[3:]


def _merge_ffn_kernel(x_ref, ydn_ref, yswa_ref, g_ref, wa_ref, wb_ref, wo_ref, gain_ref,
                      wg_ref, wu_ref, wd_ref, o_ref, act_ref):
    per_branch = D_MODEL // MXU_WIDTH
    ga, gb = [_sigmoid(jnp.concatenate(
        [g_ref[br * per_branch + p].reshape(ROWS, MXU_WIDTH).astype(F32)
         for p in range(per_branch)], axis=1)) for br in range(2)]
    ydn = ydn_ref[...].reshape(ROWS, DN_WIDTH)
    yswa = yswa_ref[...].reshape(ROWS, SWA_WIDTH)
    merged = ga * _mm(ydn, wa_ref[...]) + gb * _mm(yswa, wb_ref[...])
    x1 = x_ref[...].reshape(ROWS, D_MODEL) + _mm(merged, wo_ref[...])
    ms = jnp.mean(x1 * x1, axis=-1, keepdims=True)
    h2 = (x1 * lax.rsqrt(ms + EPS) * gain_ref[...]).astype(BF16)
    for c0 in range(0, D_FF, MXU_WIDTH):
        gate = jnp.dot(h2, wg_ref[:, c0:c0 + MXU_WIDTH], preferred_element_type=F32)
        up = jnp.dot(h2, wu_ref[:, c0:c0 + MXU_WIDTH], preferred_element_type=F32)
        act_ref[:, c0:c0 + MXU_WIDTH] = (_silu(gate) * up).astype(BF16)
    out = x1 + jnp.dot(act_ref[...], wd_ref[...], preferred_element_type=F32)
    o_ref[...] = out.reshape(STREAMS, TILE, D_MODEL)


def _merge_ffn(x, ydn, yswa, gates, wa, wb, wo, gain, wg, wu, wd):
    B, S, _ = x.shape
    tok = lambda width: pl.BlockSpec((STREAMS, TILE, width), lambda b, t: (b, t, 0))
    once = lambda a, b: pl.BlockSpec((a, b), lambda i, j: (0, 0), pipeline_mode=pl.Buffered(1))
    return pl.pallas_call(
        _merge_ffn_kernel,
        grid=(B // STREAMS, S // TILE),
        in_specs=[tok(D_MODEL), tok(DN_WIDTH), tok(SWA_WIDTH),
                  pl.BlockSpec((GATE_PIECES, STREAMS, TILE, MXU_WIDTH), lambda b, t: (0, b, t, 0)),
                  once(DN_WIDTH, D_MODEL), once(SWA_WIDTH, D_MODEL), once(D_MODEL, D_MODEL),
                  once(1, D_MODEL),
                  once(D_MODEL, D_FF), once(D_MODEL, D_FF), once(D_FF, D_MODEL)],
        out_specs=tok(D_MODEL),
        out_shape=jax.ShapeDtypeStruct((B, S, D_MODEL), F32),
        scratch_shapes=[pltpu.VMEM((ROWS, D_FF), BF16)],
        compiler_params=pltpu.CompilerParams(
            dimension_semantics=("arbitrary", "arbitrary"), vmem_limit_bytes=VMEM_LIMIT),
        name="merge_ffn",
    )(x, ydn, yswa, gates, wa, wb, wo, gain, wg, wu, wd)


def _t5_bucket_table():
    qi = jnp.arange(SWA_BLOCK)[:, None]
    kj = jnp.arange(SWA_BAND)[None, :]
    dist = SWA_BLOCK + qi - kj
    in_window = (dist >= 0) & (dist < WINDOW)
    n = jnp.maximum(dist, 0)
    max_exact = REL_BUCKETS // 2
    nf = jnp.maximum(n, 1).astype(F32)
    large = max_exact + (jnp.log(nf / max_exact) / math.log(REL_MAX_DIST / max_exact)
                         * (REL_BUCKETS - max_exact)).astype(jnp.int32)
    large = jnp.minimum(large, REL_BUCKETS - 1)
    return jnp.where(n < max_exact, n, large), in_window


def kernel(x, attn_norm, w_in, dn_conv, dn_a_log, dn_dt_bias, dn_out_norm, swa_q_norm, swa_k_norm, swa_sinks, rel_bias, w_branch_dn, w_branch_swa, w_out, ffn_norm, w_gate, w_up, w_down):
    depth = w_in.shape[0]
    bucket, in_window = _t5_bucket_table()
    rel = rel_bias.astype(F32)
    bias_tab = sum(jnp.where(bucket[None] == b, rel[b][:, None, None], 0.0)
                   for b in range(REL_BUCKETS))
    bias_tab = jnp.where(in_window[None], bias_tab, -jnp.inf)
    has_prev = jnp.arange(SWA_BAND)[None, None, :] >= SWA_BLOCK
    bias_tabs = jnp.stack([jnp.where(has_prev, bias_tab, -jnp.inf), bias_tab])

    for l in range(depth):
        w_head, w_ba, w_rest = _cast_w(jnp.swapaxes(w_in[l], 0, 1).astype(F32))
        later = [w[l].astype(F32) for w in (w_branch_dn, w_branch_swa, w_out, w_gate, w_up, w_down)]
        qgain = jnp.tile(swa_q_norm[l].astype(F32), SWA_HEADS)[None, :] * (SWA_HEAD_DIM ** -0.5)
        kgain = jnp.tile(swa_k_norm[l].astype(F32), SWA_KV_HEADS)[None, :]
        swa_params = (swa_sinks[l].astype(F32), bias_tabs, qgain, kgain)
        convw = dn_conv[l].astype(F32).reshape(DN_CONV, QKV_BLOCKS, DN_HEAD_DIM).transpose(1, 0, 2)
        pad_lo = jnp.zeros((DN_HEADS,), F32)
        pad_hi = jnp.zeros((LANES - 2 * DN_HEADS,), F32)
        alog_lane = jnp.concatenate([pad_lo, dn_a_log[l].astype(F32), pad_hi])[None, :]
        dtb_lane = jnp.concatenate([pad_lo, dn_dt_bias[l].astype(F32), pad_hi])[None, :]
        dn_params = (convw, alog_lane, dtb_lane, dn_out_norm[l][None, :].astype(F32))

        (y_swa, gates, y_dn), later_bf16 = _mixer(
            x, attn_norm[l][None, :], w_head, w_ba, w_rest, swa_params, dn_params, later)
        wa, wb, wo, wg, wu, wd = later_bf16
        x = _merge_ffn(x, y_dn, y_swa, gates, wa, wb, wo, ffn_norm[l][None, :], wg, wu, wd)
    return x
```

```python
import functools
import math

import jax
import jax.numpy as jnp
from jax import lax
from jax.experimental import pallas as pl
from jax.experimental.pallas import tpu as pltpu

D_MODEL = 1024
DN_HEADS = 4
DN_HEAD_DIM = 128
DN_WIDTH = DN_HEADS * DN_HEAD_DIM
DN_QKV_WIDTH = 3 * DN_WIDTH
DN_CONV = 4
DN_CHUNK = 64
SWA_HEADS = 8
SWA_KV_HEADS = 2
SWA_HEAD_DIM = 64
SWA_WIDTH = SWA_HEADS * SWA_HEAD_DIM
SWA_KV_WIDTH = SWA_KV_HEADS * SWA_HEAD_DIM
WINDOW = 128
SWA_BLOCK = 128
REL_BUCKETS = 32
REL_MAX_DIST = 128
D_FF = 2816
EPS = 1e-6

LANES = 128
SUBLANES = 8
BF16_ROWS = 2 * SUBLANES
MXU_WIDTH = 256
VMEM_LIMIT = 58 * 1024 * 1024

W_HEAD = DN_QKV_WIDTH + DN_WIDTH
W_BA = 2 * DN_HEADS
W_SWA = SWA_WIDTH + 2 * SWA_KV_WIDTH
W_GATES = 2 * D_MODEL
D_IN = W_HEAD + W_BA + W_SWA + W_GATES
SWA_PIECES = W_SWA // MXU_WIDTH
GATE_PIECES = W_GATES // MXU_WIDTH
QKV_BLOCKS = DN_QKV_WIDTH // DN_HEAD_DIM

TILE = 128
STREAMS = 4
ROWS = TILE * STREAMS
DN_NCH = TILE // DN_CHUNK
DN_LOG_CHUNK = int(math.log2(DN_CHUNK))
CONV_TAIL = SUBLANES
SWA_BAND = 2 * SWA_BLOCK
SWA_SLOTS = SWA_WIDTH // LANES
SWA_HEADS_PER_SLOT = LANES // SWA_HEAD_DIM
W_CAST_ROWS = 256
GATES_EARLY = 1

F32 = jnp.float32
BF16 = jnp.bfloat16


def _mm(a, b, dims=(((1,), (0,)), ((), ()))):
    return lax.dot_general(a.astype(BF16), b.astype(BF16), dims, preferred_element_type=F32)


_NT = (((1,), (1,)), ((), ()))
_TN = (((0,), (0,)), ((), ()))


def _sigmoid(x):
    return 0.5 * jnp.tanh(0.5 * x) + 0.5


def _silu(x):
    h = 0.5 * x
    return h * jnp.tanh(h) + h


def _cast_w_kernel(wt_ref, w_head, w_ba, w_rest):
    for r0 in range(0, W_HEAD, W_CAST_ROWS):
        w_head[r0:r0 + W_CAST_ROWS, :] = wt_ref[r0:r0 + W_CAST_ROWS, :].astype(BF16)
    w_ba[...] = jnp.zeros_like(w_ba)
    w_ba[0:W_BA, :] = wt_ref[W_HEAD:W_HEAD + W_BA, :].astype(BF16)
    base = W_HEAD + W_BA
    for r0 in range(0, W_SWA + W_GATES, W_CAST_ROWS):
        w_rest[r0:r0 + W_CAST_ROWS, :] = wt_ref[base + r0:base + r0 + W_CAST_ROWS, :].astype(BF16)


def _cast_w(w_t):
    shapes = (W_HEAD, LANES, W_SWA + W_GATES)
    return pl.pallas_call(
        _cast_w_kernel,
        grid=(1,),
        in_specs=[pl.BlockSpec((D_IN, D_MODEL), lambda i: (0, 0), pipeline_mode=pl.Buffered(1))],
        out_specs=[pl.BlockSpec((r, D_MODEL), lambda i: (0, 0)) for r in shapes],
        out_shape=[jax.ShapeDtypeStruct((r, D_MODEL), BF16) for r in shapes],
        compiler_params=pltpu.CompilerParams(
            dimension_semantics=("arbitrary",), vmem_limit_bytes=VMEM_LIMIT),
        name="cast_w",
    )(w_t)


def _cast_specs(weights, steps, axis):
    specs = []
    for w in weights:
        blocks = next(b for b in range(steps, 0, -1)
                      if steps % b == 0 and w.shape[0] % (b * BF16_ROWS) == 0)
        repeat = steps // blocks
        specs.append(pl.BlockSpec((w.shape[0] // blocks, w.shape[1]),
                                  lambda *idx, repeat=repeat: (idx[axis] // repeat, 0)))
    return specs


def _delta_rule(qkv_scr, ba, z, alog_ref, dtb_ref, gain_ref, state, o_ref, fillers):
    fillers = iter(fillers)
    fill = lambda: next(fillers, lambda: None)()
    streams = range(STREAMS)
    row = lax.broadcasted_iota(jnp.int32, (TILE, TILE), 0)
    col = lax.broadcasted_iota(jnp.int32, (TILE, TILE), 1)
    same_chunk = (row >> DN_LOG_CHUNK) == (col >> DN_LOG_CHUNK)
    lower_incl = same_chunk & (row >= col)
    strict_lower = same_chunk & (row > col)
    cumsum_mat = jnp.where(lower_incl, 1.0, 0.0).astype(BF16)
    eye = jnp.where(row == col, 1.0, 0.0)
    rows_of = lambda s: slice(s * TILE, (s + 1) * TILE)

    beta, gc, rev, gc_t = [], [], [], []
    for s in streams:
        ba_s = ba[rows_of(s)]
        beta.append(_sigmoid(ba_s))
        xs = ba_s + dtb_ref[...]
        softplus = jnp.maximum(xs, 0.0) + jnp.log1p(jnp.exp(-jnp.abs(xs)))
        g = -jnp.exp(alog_ref[...]) * softplus
        g_hi = g.astype(BF16)
        g_lo = (g - g_hi.astype(F32)).astype(BF16)
        gc2 = jnp.dot(cumsum_mat, jnp.concatenate([g_hi, g_lo], axis=1),
                      preferred_element_type=F32)
        gc.append(gc2[:, :LANES] + gc2[:, LANES:])
        g_last = jnp.concatenate(
            [jnp.broadcast_to(gc[s][(c + 1) * DN_CHUNK - 1:(c + 1) * DN_CHUNK, :], (DN_CHUNK, LANES))
             for c in range(DN_NCH)], axis=0)
        rev.append(g_last - gc[s])
        gc_t.append(gc[s].T)

    units = [(s, h) for s in streams for h in range(DN_HEADS)]
    gcol = lambda arr, h: arr[:, DN_HEADS + h:DN_HEADS + h + 1]
    q, k, kb, rhs = {}, {}, {}, {}
    for s, h in units:
        q[s, h] = qkv_scr[h, rows_of(s), :].astype(F32)
        k[s, h] = qkv_scr[DN_HEADS + h, rows_of(s), :].astype(F32)
        vh = qkv_scr[2 * DN_HEADS + h, rows_of(s), :].astype(F32)
        beta_c = beta[s][:, h:h + 1]
        kb[s, h] = k[s, h] * beta_c
        rhs[s, h] = jnp.concatenate([vh * beta_c, kb[s, h] * jnp.exp(gcol(gc[s], h))], axis=1)
    gram = {p: _mm(jnp.concatenate([kb[p], q[p]], axis=0), k[p], _NT) for p in units}
    fill()
    qk, n_pow, inv = {}, {}, {}
    for s, h in units:
        gc_r = gc_t[s][DN_HEADS + h:DN_HEADS + h + 1, :]
        decay = jnp.exp(jnp.where(lower_incl, gcol(gc[s], h) - gc_r, -jnp.inf))
        qk[s, h] = gram[s, h][TILE:] * decay
        n_pow[s, h] = -jnp.where(strict_lower, gram[s, h][:TILE] * decay, 0.0)
        inv[s, h] = eye + n_pow[s, h]
    for _ in range(DN_LOG_CHUNK - 1):
        n_pow = {p: _mm(n_pow[p], n_pow[p]) for p in units}
        inv = {p: inv[p] + _mm(inv[p], n_pow[p]) for p in units}
        fill()
    sol = {p: _mm(inv[p], rhs[p]) for p in units}
    fill()
    u = {p: sol[p][:, :DN_HEAD_DIM] for p in units}
    wmat = {p: sol[p][:, DN_HEAD_DIM:] for p in units}
    qe = {(s, h): q[s, h] * jnp.exp(gcol(gc[s], h)) for s, h in units}
    kdec = {(s, h): k[s, h] * jnp.exp(gcol(rev[s], h)) for s, h in units}

    st = {(s, h): state[s, h] for s, h in units}
    v_new = {p: [] for p in units}
    o_inter = {p: [] for p in units}
    for c in range(DN_NCH):
        r0, r1 = c * DN_CHUNK, (c + 1) * DN_CHUNK
        ws = {p: _mm(jnp.concatenate([wmat[p][r0:r1], qe[p][r0:r1]], axis=0), st[p]) for p in units}
        for p in units:
            v_new[p].append(u[p][r0:r1] - ws[p][:DN_CHUNK])
            o_inter[p].append(ws[p][DN_CHUNK:])
        upd = {p: _mm(kdec[p][r0:r1], v_new[p][c], _TN) for p in units}
        st = {(s, h): st[s, h] * jnp.exp(gc[s][r1 - 1:r1, DN_HEADS + h:DN_HEADS + h + 1]) + upd[s, h]
              for s, h in units}
        fill()
    o_intra = {p: _mm(qk[p], jnp.concatenate(v_new[p], axis=0)) for p in units}
    for f in fillers:
        f()
    for s, h in units:
        state[s, h] = st[s, h]
        o = jnp.concatenate(o_inter[s, h], axis=0) + o_intra[s, h]
        o = o * lax.rsqrt(jnp.mean(o * o, axis=-1, keepdims=True) + EPS) * gain_ref[...]
        zh = z[rows_of(s), h * DN_HEAD_DIM:(h + 1) * DN_HEAD_DIM]
        o_ref[s, :, h * DN_HEAD_DIM:(h + 1) * DN_HEAD_DIM] = (o * _silu(zh)).astype(o_ref.dtype)


def _swa_attend(first_tab, q_slots, kv, sinks_ref, bias_ref, qgain_ref, kgain_ref, band_ref, o_ref,
                between):
    k = kv[:, :SWA_KV_WIDTH]
    between[0]()
    half = lax.broadcasted_iota(jnp.int32, (ROWS, LANES), 1) < SWA_HEAD_DIM

    def head_norm(p):
        p2 = p * p
        s_lo = jnp.sum(jnp.where(half, p2, 0.0), axis=-1, keepdims=True)
        s_hi = jnp.sum(jnp.where(half, 0.0, p2), axis=-1, keepdims=True)
        r_lo = lax.rsqrt(s_lo * (1.0 / SWA_HEAD_DIM) + EPS)
        r_hi = lax.rsqrt(s_hi * (1.0 / SWA_HEAD_DIM) + EPS)
        return p * jnp.where(half, r_lo, r_hi)

    qn = [head_norm(q_slots[j]) * qgain_ref[:, j * LANES:(j + 1) * LANES]
          for j in range(SWA_SLOTS)]
    kn = head_norm(k) * kgain_ref[...]

    for u in range(STREAMS):
        base = u * SWA_BAND
        band_ref[base:base + SWA_BLOCK, :] = band_ref[base + SWA_BLOCK:base + SWA_BAND, :]
        band_ref[base + SWA_BLOCK:base + SWA_BAND, :SWA_KV_WIDTH] = kn[u * TILE:(u + 1) * TILE]
        band_ref[base + SWA_BLOCK:base + SWA_BAND, SWA_KV_WIDTH:] = (
            kv[u * TILE:(u + 1) * TILE, SWA_KV_WIDTH:])
    k = band_ref[:, :SWA_KV_WIDTH]
    v = band_ref[:, SWA_KV_WIDTH:]

    lane = lax.broadcasted_iota(jnp.int32, (STREAMS * SWA_BAND, LANES), 1)
    low = lane < SWA_HEAD_DIM
    k_rot = pltpu.roll(k, SWA_HEAD_DIM, axis=1)
    v_rot = pltpu.roll(v, SWA_HEAD_DIM, axis=1)
    k_lo = (jnp.where(low, k, 0.0).astype(BF16), jnp.where(low, k_rot, 0.0).astype(BF16))
    k_hi = (jnp.where(low, 0.0, k_rot).astype(BF16), jnp.where(low, 0.0, k).astype(BF16))
    v_lo = (jnp.where(low, v, 0.0).astype(BF16), jnp.where(low, v_rot, 0.0).astype(BF16))
    v_hi = (jnp.where(low, 0.0, v_rot).astype(BF16), jnp.where(low, 0.0, v).astype(BF16))

    out_low = lax.broadcasted_iota(jnp.int32, (SWA_BLOCK, LANES), 1) < SWA_HEAD_DIM

    slots_per_kv = SWA_SLOTS // SWA_KV_HEADS
    units = [(u, j) for u in range(STREAMS) for j in range(SWA_SLOTS)]
    band_of = lambda arr, u: arr[u * SWA_BAND:(u + 1) * SWA_BAND]
    k_bd = {(u, kh): jnp.concatenate([band_of(k_lo[kh], u), band_of(k_hi[kh], u)], axis=0)
            for u in range(STREAMS) for kh in range(SWA_KV_HEADS)}
    v_bd = {(u, kh): jnp.concatenate([band_of(v_lo[kh], u), band_of(v_hi[kh], u)], axis=0)
            for u in range(STREAMS) for kh in range(SWA_KV_HEADS)}
    qn = [qn[j].astype(BF16) for j in range(SWA_SLOTS)]
    between[1]()
    logits = {(u, j): _mm(qn[j][u * TILE:(u + 1) * TILE], k_bd[u, j // slots_per_kv], _NT)
              for u, j in units}
    probs = {}
    inv_den = {}
    for u, j in units:
        for r in range(SWA_HEADS_PER_SLOT):
            hd = j * SWA_HEADS_PER_SLOT + r
            lg = logits[u, j][:, r * SWA_BAND:(r + 1) * SWA_BAND] + bias_ref[first_tab, hd]
            sink = sinks_ref[hd]
            m = jnp.maximum(jnp.max(lg, axis=-1, keepdims=True), sink)
            p = jnp.exp(lg - m)
            den = jnp.sum(p, axis=-1, keepdims=True) + jnp.exp(sink - m)
            probs[u, j, r] = p.astype(BF16)
            inv_den[u, j, r] = 1.0 / den

    def values():
        outs = {(u, j): _mm(jnp.concatenate([probs[u, j, r] for r in range(SWA_HEADS_PER_SLOT)],
                                            axis=1), v_bd[u, j // slots_per_kv])
                for u, j in units}
        for u, j in units:
            o_ref[u, :, j * LANES:(j + 1) * LANES] = (
                outs[u, j] * jnp.where(out_low, inv_den[u, j, 0], inv_den[u, j, 1])
            ).astype(o_ref.dtype)

    between[2](values)


def _mixer_kernel(sinks_ref, x_ref, x_next_ref, gain_ref, w_head, w_ba, w_rest, bias_ref, qgain_ref, kgain_ref,
                  convw_ref, alog_ref, dtb_ref, dn_gain_ref, *refs, n_cast):
    cast_src = refs[:n_cast]
    yswa_ref, gates_ref, ydn_ref = refs[n_cast:n_cast + 3]
    cast_dst = refs[n_cast + 3:2 * n_cast + 3]
    band_ref, xbuf, qkv_scr, state, h_ref = refs[2 * n_cast + 3:]
    for src, dst in zip(cast_src, cast_dst):
        dst[...] = src[...].astype(BF16)
    t = pl.program_id(1)

    def normed(xr):
        x = xr[...].reshape(ROWS, D_MODEL)
        ms = jnp.mean(x * x, axis=-1, keepdims=True)
        return (x * lax.rsqrt(ms + EPS) * gain_ref[...]).astype(BF16)

    @pl.when(t == 0)
    def _():
        band_ref[...] = jnp.zeros_like(band_ref)
        xbuf[:, :, TILE:, :] = jnp.zeros((QKV_BLOCKS, STREAMS, CONV_TAIL, DN_HEAD_DIM), F32)
        state[...] = jnp.zeros_like(state)
        h_ref[...] = normed(x_ref)

    proj = lambda w: lax.dot_general(h_ref[...], w, _NT, preferred_element_type=F32)
    held = {}

    def dn_qkv():
        raw = proj(w_head[0:DN_QKV_WIDTH, :])
        xbuf[:, :, 0:CONV_TAIL, :] = xbuf[:, :, TILE:, :]
        for c in range(QKV_BLOCKS):
            for s in range(STREAMS):
                xbuf[c, s, CONV_TAIL:, :] = raw[s * TILE:(s + 1) * TILE,
                                                c * DN_HEAD_DIM:(c + 1) * DN_HEAD_DIM]

    def dn_preprocess():
        scale = DN_HEAD_DIM ** -0.5
        for c in range(QKV_BLOCKS):
            w = convw_ref[c]
            for s in range(STREAMS):
                acc = w[DN_CONV - 1:DN_CONV, :] * xbuf[c, s, CONV_TAIL:, :]
                for j in range(DN_CONV - 1):
                    off = CONV_TAIL - (DN_CONV - 1) + j
                    acc = acc + w[j:j + 1, :] * xbuf[c, s, off:off + TILE, :]
                act = _silu(acc)
                if c < 2 * DN_HEADS:
                    norm = lax.rsqrt(jnp.sum(act * act, axis=-1, keepdims=True) + EPS)
                    act = act * (norm * scale if c < DN_HEADS else norm)
                qkv_scr[c, s * TILE:(s + 1) * TILE, :] = act.astype(BF16)

    def gate_logits(pieces):
        for p in pieces:
            r0 = W_SWA + p * MXU_WIDTH
            gates_ref[p] = proj(w_rest[r0:r0 + MXU_WIDTH, :]).reshape(
                STREAMS, TILE, MXU_WIDTH).astype(gates_ref.dtype)

    def qkv_z_ba():
        dn_qkv()
        held["z"] = proj(w_head[DN_QKV_WIDTH:W_HEAD, :])
        held["ba"] = proj(w_ba[...])

    def gates_a_and_preprocess():
        gate_logits(range(0, GATES_EARLY))
        dn_preprocess()

    def delta_rule_with_gates(swa_values):
        late = [functools.partial(gate_logits, [p]) for p in range(GATES_EARLY, GATE_PIECES)]
        _delta_rule(qkv_scr, held["ba"], held["z"], alog_ref, dtb_ref, dn_gain_ref, state,
                    ydn_ref, late)
        swa_values()

    sw = [proj(w_rest[p * MXU_WIDTH:(p + 1) * MXU_WIDTH, :]) for p in range(SWA_PIECES)]
    slots_per_piece = MXU_WIDTH // LANES
    q_slots = [sw[j // slots_per_piece][:, (j % slots_per_piece) * LANES:
                                        (j % slots_per_piece + 1) * LANES] for j in range(SWA_SLOTS)]
    _swa_attend(jnp.where(t == 0, 0, 1), q_slots, sw[SWA_WIDTH // MXU_WIDTH], sinks_ref, bias_ref,
                qgain_ref, kgain_ref, band_ref, yswa_ref,
                between=(qkv_z_ba, gates_a_and_preprocess, delta_rule_with_gates))
    h_ref[...] = normed(x_next_ref)


def _mixer(x, gain, w_head, w_ba, w_rest, swa_params, dn_params, later_weights):
    sinks, bias_tabs, qgain, kgain = swa_params
    convw, alog_lane, dtb_lane, dn_gain = dn_params
    B, S, _ = x.shape
    tiles = S // TILE
    assert TILE == SWA_BLOCK and B % STREAMS == 0
    const = lambda shape: pl.BlockSpec(shape, lambda b, t, s: (0,) * len(shape),
                                       pipeline_mode=pl.Buffered(1))
    tok = lambda width: pl.BlockSpec((STREAMS, TILE, width), lambda b, t, s: (b, t, 0))
    cast_specs = _cast_specs(later_weights, tiles, axis=1)
    grid_spec = pltpu.PrefetchScalarGridSpec(
        num_scalar_prefetch=1,
        grid=(B // STREAMS, tiles),
        in_specs=[
            tok(D_MODEL),
            pl.BlockSpec((STREAMS, TILE, D_MODEL),
                         lambda b, t, s: (b, jnp.minimum(t + 1, tiles - 1), 0)),
            const((1, D_MODEL)),
            const((W_HEAD, D_MODEL)), const((LANES, D_MODEL)), const((W_SWA + W_GATES, D_MODEL)),
            const((2, SWA_HEADS, SWA_BLOCK, SWA_BAND)),
            const((1, SWA_WIDTH)),
            const((1, SWA_KV_WIDTH)),
            const((QKV_BLOCKS, DN_CONV, DN_HEAD_DIM)),
            const((1, LANES)), const((1, LANES)), const((1, DN_HEAD_DIM)),
        ] + cast_specs,
        out_specs=[
            tok(SWA_WIDTH),
            pl.BlockSpec((GATE_PIECES, STREAMS, TILE, MXU_WIDTH), lambda b, t, s: (0, b, t, 0)),
            tok(DN_WIDTH),
        ] + cast_specs,
        scratch_shapes=[pltpu.VMEM((STREAMS * SWA_BAND, 2 * SWA_KV_WIDTH), F32),
                        pltpu.VMEM((QKV_BLOCKS, STREAMS, TILE + CONV_TAIL, DN_HEAD_DIM), F32),
                        pltpu.VMEM((QKV_BLOCKS, ROWS, DN_HEAD_DIM), BF16),
                        pltpu.VMEM((STREAMS, DN_HEADS, DN_HEAD_DIM, DN_HEAD_DIM), F32),
                        pltpu.VMEM((ROWS, D_MODEL), BF16)],
    )
    outs = pl.pallas_call(
        functools.partial(_mixer_kernel, n_cast=len(later_weights)),
        grid_spec=grid_spec,
        out_shape=[
            jax.ShapeDtypeStruct((B, S, SWA_WIDTH), BF16),
            jax.ShapeDtypeStruct((GATE_PIECES, B, S, MXU_WIDTH), BF16),
            jax.ShapeDtypeStruct((B, S, DN_WIDTH), BF16),
        ] + [jax.ShapeDtypeStruct(w.shape, BF16) for w in later_weights],
        compiler_params=pltpu.CompilerParams(
            dimension_semantics=("arbitrary", "arbitrary"), vmem_limit_bytes=VMEM_LIMIT),
        name="mixer",
    )(sinks, x, x, gain, w_head, w_ba, w_rest, bias_tabs, qgain, kgain, convw, alog_lane, dtb_lane,
      dn_gain, *later_weights)
    return outs[:3], outs[3:]


def _merge_ffn_kernel(x_ref, ydn_ref, yswa_ref, g_ref, wa_ref, wb_ref, wo_ref, gain_ref,
                      wg_ref, wu_ref, wd_ref, o_ref, act_ref):
    per_branch = D_MODEL // MXU_WIDTH
    ga, gb = [_sigmoid(jnp.concatenate(
        [g_ref[br * per_branch + p].reshape(ROWS, MXU_WIDTH).astype(F32)
         for p in range(per_branch)], axis=1)) for br in range(2)]
    ydn = ydn_ref[...].reshape(ROWS, DN_WIDTH)
    yswa = yswa_ref[...].reshape(ROWS, SWA_WIDTH)
    x = x_ref[...].reshape(ROWS, D_MODEL)
    halves = [slice(r0, r0 + ROWS // 2) for r0 in range(0, ROWS, ROWS // 2)]
    x1, h2 = [], []
    for rows in halves:
        merged = ga[rows] * _mm(ydn[rows], wa_ref[...]) + gb[rows] * _mm(yswa[rows], wb_ref[...])
        x1.append(x[rows] + _mm(merged, wo_ref[...]))
        ms = jnp.mean(x1[-1] * x1[-1], axis=-1, keepdims=True)
        h2.append((x1[-1] * lax.rsqrt(ms + EPS) * gain_ref[...]).astype(BF16))
    for rows, h2_half in zip(halves, h2):
        for c0 in range(0, D_FF, MXU_WIDTH):
            gate = jnp.dot(h2_half, wg_ref[:, c0:c0 + MXU_WIDTH], preferred_element_type=F32)
            up = jnp.dot(h2_half, wu_ref[:, c0:c0 + MXU_WIDTH], preferred_element_type=F32)
            act_ref[rows, c0:c0 + MXU_WIDTH] = (_silu(gate) * up).astype(BF16)
    out = jnp.concatenate(x1, axis=0) + jnp.dot(act_ref[...], wd_ref[...],
                                                preferred_element_type=F32)
    o_ref[...] = out.reshape(STREAMS, TILE, D_MODEL)


def _merge_ffn(x, ydn, yswa, gates, wa, wb, wo, gain, wg, wu, wd):
    B, S, _ = x.shape
    tok = lambda width: pl.BlockSpec((STREAMS, TILE, width), lambda b, t: (b, t, 0))
    once = lambda a, b: pl.BlockSpec((a, b), lambda i, j: (0, 0), pipeline_mode=pl.Buffered(1))
    return pl.pallas_call(
        _merge_ffn_kernel,
        grid=(B // STREAMS, S // TILE),
        in_specs=[tok(D_MODEL), tok(DN_WIDTH), tok(SWA_WIDTH),
                  pl.BlockSpec((GATE_PIECES, STREAMS, TILE, MXU_WIDTH), lambda b, t: (0, b, t, 0)),
                  once(DN_WIDTH, D_MODEL), once(SWA_WIDTH, D_MODEL), once(D_MODEL, D_MODEL),
                  once(1, D_MODEL),
                  once(D_MODEL, D_FF), once(D_MODEL, D_FF), once(D_FF, D_MODEL)],
        out_specs=tok(D_MODEL),
        out_shape=jax.ShapeDtypeStruct((B, S, D_MODEL), F32),
        scratch_shapes=[pltpu.VMEM((ROWS, D_FF), BF16)],
        compiler_params=pltpu.CompilerParams(
            dimension_semantics=("arbitrary", "arbitrary"), vmem_limit_bytes=VMEM_LIMIT),
        name="merge_ffn",
    )(x, ydn, yswa, gates, wa, wb, wo, gain, wg, wu, wd)


def _t5_bucket_table():
    qi = jnp.arange(SWA_BLOCK)[:, None]
    kj = jnp.arange(SWA_BAND)[None, :]
    dist = SWA_BLOCK + qi - kj
    in_window = (dist >= 0) & (dist < WINDOW)
    n = jnp.maximum(dist, 0)
    max_exact = REL_BUCKETS // 2
    nf = jnp.maximum(n, 1).astype(F32)
    large = max_exact + (jnp.log(nf / max_exact) / math.log(REL_MAX_DIST / max_exact)
                         * (REL_BUCKETS - max_exact)).astype(jnp.int32)
    large = jnp.minimum(large, REL_BUCKETS - 1)
    return jnp.where(n < max_exact, n, large), in_window


def kernel(x, attn_norm, w_in, dn_conv, dn_a_log, dn_dt_bias, dn_out_norm, swa_q_norm, swa_k_norm, swa_sinks, rel_bias, w_branch_dn, w_branch_swa, w_out, ffn_norm, w_gate, w_up, w_down):
    depth = w_in.shape[0]
    bucket, in_window = _t5_bucket_table()
    rel = rel_bias.astype(F32)
    bias_tab = sum(jnp.where(bucket[None] == b, rel[b][:, None, None], 0.0)
                   for b in range(REL_BUCKETS))
    bias_tab = jnp.where(in_window[None], bias_tab, -jnp.inf)
    has_prev = jnp.arange(SWA_BAND)[None, None, :] >= SWA_BLOCK
    bias_tabs = jnp.stack([jnp.where(has_prev, bias_tab, -jnp.inf), bias_tab])

    for l in range(depth):
        w_head, w_ba, w_rest = _cast_w(jnp.swapaxes(w_in[l], 0, 1).astype(F32))
        later = [w[l].astype(F32) for w in (w_branch_dn, w_branch_swa, w_out, w_gate, w_up, w_down)]
        qgain = jnp.tile(swa_q_norm[l].astype(F32), SWA_HEADS)[None, :] * (SWA_HEAD_DIM ** -0.5)
        kgain = jnp.tile(swa_k_norm[l].astype(F32), SWA_KV_HEADS)[None, :]
        swa_params = (swa_sinks[l].astype(F32), bias_tabs, qgain, kgain)
        convw = dn_conv[l].astype(F32).reshape(DN_CONV, QKV_BLOCKS, DN_HEAD_DIM).transpose(1, 0, 2)
        pad_lo = jnp.zeros((DN_HEADS,), F32)
        pad_hi = jnp.zeros((LANES - 2 * DN_HEADS,), F32)
        alog_lane = jnp.concatenate([pad_lo, dn_a_log[l].astype(F32), pad_hi])[None, :]
        dtb_lane = jnp.concatenate([pad_lo, dn_dt_bias[l].astype(F32), pad_hi])[None, :]
        dn_params = (convw, alog_lane, dtb_lane, dn_out_norm[l][None, :].astype(F32))

        (y_swa, gates, y_dn), later_bf16 = _mixer(
            x, attn_norm[l][None, :], w_head, w_ba, w_rest, swa_params, dn_params, later)
        wa, wb, wo, wg, wu, wd = later_bf16
        x = _merge_ffn(x, y_dn, y_swa, gates, wa, wb, wo, ffn_norm[l][None, :], wg, wu, wd)
    return x
```

```python
import functools
import math

import jax
import jax.numpy as jnp
from jax import lax
from jax.experimental import pallas as pl
from jax.experimental.pallas import tpu as pltpu

D_MODEL = 1024
DN_HEADS = 4
DN_HEAD_DIM = 128
DN_WIDTH = DN_HEADS * DN_HEAD_DIM
DN_QKV_WIDTH = 3 * DN_WIDTH
DN_CONV = 4
DN_CHUNK = 64
SWA_HEADS = 8
SWA_KV_HEADS = 2
SWA_HEAD_DIM = 64
SWA_WIDTH = SWA_HEADS * SWA_HEAD_DIM
SWA_KV_WIDTH = SWA_KV_HEADS * SWA_HEAD_DIM
WINDOW = 128
SWA_BLOCK = 128
REL_BUCKETS = 32
REL_MAX_DIST = 128
D_FF = 2816
EPS = 1e-6

LANES = 128
SUBLANES = 8
BF16_ROWS = 2 * SUBLANES
MXU_WIDTH = 256
VMEM_LIMIT = 58 * 1024 * 1024

W_HEAD = DN_QKV_WIDTH + DN_WIDTH
W_BA = 2 * DN_HEADS
W_SWA = SWA_WIDTH + 2 * SWA_KV_WIDTH
W_GATES = 2 * D_MODEL
D_IN = W_HEAD + W_BA + W_SWA + W_GATES
SWA_PIECES = W_SWA // MXU_WIDTH
GATE_PIECES = W_GATES // MXU_WIDTH
QKV_BLOCKS = DN_QKV_WIDTH // DN_HEAD_DIM

TILE = 128
STREAMS = 4
ROWS = TILE * STREAMS
DN_NCH = TILE // DN_CHUNK
DN_LOG_CHUNK = int(math.log2(DN_CHUNK))
INV_LOG_BASE = 3
CONV_TAIL = SUBLANES
SWA_BAND = 2 * SWA_BLOCK
SWA_SLOTS = SWA_WIDTH // LANES
SWA_HEADS_PER_SLOT = LANES // SWA_HEAD_DIM
W_CAST_ROWS = 256
GATES_EARLY = 1

F32 = jnp.float32
BF16 = jnp.bfloat16


def _mm(a, b, dims=(((1,), (0,)), ((), ()))):
    return lax.dot_general(a.astype(BF16), b.astype(BF16), dims, preferred_element_type=F32)


_NT = (((1,), (1,)), ((), ()))
_TN = (((0,), (0,)), ((), ()))


def _sigmoid(x):
    return 0.5 * jnp.tanh(0.5 * x) + 0.5


def _silu(x):
    h = 0.5 * x
    return h * jnp.tanh(h) + h


def _cast_w_kernel(wt_ref, w_head, w_ba, w_rest):
    for r0 in range(0, W_HEAD, W_CAST_ROWS):
        w_head[r0:r0 + W_CAST_ROWS, :] = wt_ref[r0:r0 + W_CAST_ROWS, :].astype(BF16)
    w_ba[...] = jnp.zeros_like(w_ba)
    w_ba[0:W_BA, :] = wt_ref[W_HEAD:W_HEAD + W_BA, :].astype(BF16)
    base = W_HEAD + W_BA
    for r0 in range(0, W_SWA + W_GATES, W_CAST_ROWS):
        w_rest[r0:r0 + W_CAST_ROWS, :] = wt_ref[base + r0:base + r0 + W_CAST_ROWS, :].astype(BF16)


def _cast_w(w_t):
    shapes = (W_HEAD, LANES, W_SWA + W_GATES)
    return pl.pallas_call(
        _cast_w_kernel,
        grid=(1,),
        in_specs=[pl.BlockSpec((D_IN, D_MODEL), lambda i: (0, 0), pipeline_mode=pl.Buffered(1))],
        out_specs=[pl.BlockSpec((r, D_MODEL), lambda i: (0, 0)) for r in shapes],
        out_shape=[jax.ShapeDtypeStruct((r, D_MODEL), BF16) for r in shapes],
        compiler_params=pltpu.CompilerParams(
            dimension_semantics=("arbitrary",), vmem_limit_bytes=VMEM_LIMIT),
        name="cast_w",
    )(w_t)


def _cast_specs(weights, steps, axis):
    specs = []
    for w in weights:
        blocks = next(b for b in range(steps, 0, -1)
                      if steps % b == 0 and w.shape[0] % (b * BF16_ROWS) == 0)
        repeat = steps // blocks
        specs.append(pl.BlockSpec((w.shape[0] // blocks, w.shape[1]),
                                  lambda *idx, repeat=repeat: (idx[axis] // repeat, 0)))
    return specs


def _delta_rule(qkv_scr, ba, z, alog_ref, dtb_ref, gain_ref, state, o_ref, fillers):
    fillers = iter(fillers)
    fill = lambda: next(fillers, lambda: None)()
    streams = range(STREAMS)
    row = lax.broadcasted_iota(jnp.int32, (TILE, TILE), 0)
    col = lax.broadcasted_iota(jnp.int32, (TILE, TILE), 1)
    same_chunk = (row >> DN_LOG_CHUNK) == (col >> DN_LOG_CHUNK)
    lower_incl = same_chunk & (row >= col)
    strict_lower = same_chunk & (row > col)
    cumsum_mat = jnp.where(lower_incl, 1.0, 0.0).astype(BF16)
    eye = jnp.where(row == col, 1.0, 0.0)
    rows_of = lambda s: slice(s * TILE, (s + 1) * TILE)

    beta, gc, rev, gc_t = [], [], [], []
    for s in streams:
        ba_s = ba[rows_of(s)]
        beta.append(_sigmoid(ba_s))
        xs = ba_s + dtb_ref[...]
        softplus = jnp.maximum(xs, 0.0) + jnp.log1p(jnp.exp(-jnp.abs(xs)))
        g = -jnp.exp(alog_ref[...]) * softplus
        g_hi = g.astype(BF16)
        g_lo = (g - g_hi.astype(F32)).astype(BF16)
        gc2 = jnp.dot(cumsum_mat, jnp.concatenate([g_hi, g_lo], axis=1),
                      preferred_element_type=F32)
        gc.append(gc2[:, :LANES] + gc2[:, LANES:])
        g_last = jnp.concatenate(
            [jnp.broadcast_to(gc[s][(c + 1) * DN_CHUNK - 1:(c + 1) * DN_CHUNK, :], (DN_CHUNK, LANES))
             for c in range(DN_NCH)], axis=0)
        rev.append(g_last - gc[s])
        gc_t.append(gc[s].T)

    units = [(s, h) for s in streams for h in range(DN_HEADS)]
    gcol = lambda arr, h: arr[:, DN_HEADS + h:DN_HEADS + h + 1]
    q, k, kb, rhs = {}, {}, {}, {}
    for s, h in units:
        q[s, h] = qkv_scr[h, rows_of(s), :].astype(F32)
        k[s, h] = qkv_scr[DN_HEADS + h, rows_of(s), :].astype(F32)
        vh = qkv_scr[2 * DN_HEADS + h, rows_of(s), :].astype(F32)
        beta_c = beta[s][:, h:h + 1]
        kb[s, h] = k[s, h] * beta_c
        rhs[s, h] = jnp.concatenate([vh * beta_c, kb[s, h] * jnp.exp(gcol(gc[s], h))], axis=1)
    gram = {p: _mm(jnp.concatenate([kb[p], q[p]], axis=0), k[p], _NT) for p in units}
    fill()
    qk, a_mat, n_pow, inv = {}, {}, {}, {}
    same_block = lambda log_b: (row >> log_b) == (col >> log_b)
    for s, h in units:
        gc_r = gc_t[s][DN_HEADS + h:DN_HEADS + h + 1, :]
        decay = jnp.exp(jnp.where(lower_incl, gcol(gc[s], h) - gc_r, -jnp.inf))
        qk[s, h] = gram[s, h][TILE:] * decay
        a_mat[s, h] = jnp.where(strict_lower, gram[s, h][:TILE] * decay, 0.0)
        n_pow[s, h] = -jnp.where(same_block(INV_LOG_BASE), a_mat[s, h], 0.0)
        inv[s, h] = eye + n_pow[s, h]
    for _ in range(INV_LOG_BASE - 1):
        n_pow = {p: _mm(n_pow[p], n_pow[p]) for p in units}
        inv = {p: inv[p] + _mm(inv[p], n_pow[p]) for p in units}
        fill()
    for log_b in range(INV_LOG_BASE, DN_LOG_CHUNK):
        below = same_block(log_b + 1) & ~same_block(log_b)
        step = {p: _mm(jnp.where(below, a_mat[p], 0.0), inv[p]) for p in units}
        inv = {p: inv[p] - _mm(inv[p], step[p]) for p in units}
        fill()
    sol = {p: _mm(inv[p], rhs[p]) for p in units}
    fill()
    u = {p: sol[p][:, :DN_HEAD_DIM] for p in units}
    wmat = {p: sol[p][:, DN_HEAD_DIM:] for p in units}
    qe = {(s, h): q[s, h] * jnp.exp(gcol(gc[s], h)) for s, h in units}
    kdec = {(s, h): k[s, h] * jnp.exp(gcol(rev[s], h)) for s, h in units}

    st = {(s, h): state[s, h] for s, h in units}
    v_new = {p: [] for p in units}
    o_inter = {p: [] for p in units}
    for c in range(DN_NCH):
        r0, r1 = c * DN_CHUNK, (c + 1) * DN_CHUNK
        ws = {p: _mm(jnp.concatenate([wmat[p][r0:r1], qe[p][r0:r1]], axis=0), st[p]) for p in units}
        for p in units:
            v_new[p].append(u[p][r0:r1] - ws[p][:DN_CHUNK])
            o_inter[p].append(ws[p][DN_CHUNK:])
        upd = {p: _mm(kdec[p][r0:r1], v_new[p][c], _TN) for p in units}
        st = {(s, h): st[s, h] * jnp.exp(gc[s][r1 - 1:r1, DN_HEADS + h:DN_HEADS + h + 1]) + upd[s, h]
              for s, h in units}
        fill()
    o_intra = {p: _mm(qk[p], jnp.concatenate(v_new[p], axis=0)) for p in units}
    for f in fillers:
        f()
    for s, h in units:
        state[s, h] = st[s, h]
        o = jnp.concatenate(o_inter[s, h], axis=0) + o_intra[s, h]
        o = o * lax.rsqrt(jnp.mean(o * o, axis=-1, keepdims=True) + EPS) * gain_ref[...]
        zh = z[rows_of(s), h * DN_HEAD_DIM:(h + 1) * DN_HEAD_DIM]
        o_ref[s, :, h * DN_HEAD_DIM:(h + 1) * DN_HEAD_DIM] = (o * _silu(zh)).astype(o_ref.dtype)


def _swa_attend(first_tab, q_slots, kv, sinks_ref, bias_ref, qgain_ref, kgain_ref, band_ref, o_ref,
                between):
    k = kv[:, :SWA_KV_WIDTH]
    between[0]()
    half = lax.broadcasted_iota(jnp.int32, (ROWS, LANES), 1) < SWA_HEAD_DIM

    def head_norm(p):
        p2 = p * p
        s_lo = jnp.sum(jnp.where(half, p2, 0.0), axis=-1, keepdims=True)
        s_hi = jnp.sum(jnp.where(half, 0.0, p2), axis=-1, keepdims=True)
        r_lo = lax.rsqrt(s_lo * (1.0 / SWA_HEAD_DIM) + EPS)
        r_hi = lax.rsqrt(s_hi * (1.0 / SWA_HEAD_DIM) + EPS)
        return p * jnp.where(half, r_lo, r_hi)

    qn = [head_norm(q_slots[j]) * qgain_ref[:, j * LANES:(j + 1) * LANES]
          for j in range(SWA_SLOTS)]
    kn = head_norm(k) * kgain_ref[...]

    for u in range(STREAMS):
        base = u * SWA_BAND
        band_ref[base:base + SWA_BLOCK, :] = band_ref[base + SWA_BLOCK:base + SWA_BAND, :]
        band_ref[base + SWA_BLOCK:base + SWA_BAND, :SWA_KV_WIDTH] = kn[u * TILE:(u + 1) * TILE]
        band_ref[base + SWA_BLOCK:base + SWA_BAND, SWA_KV_WIDTH:] = (
            kv[u * TILE:(u + 1) * TILE, SWA_KV_WIDTH:])
    k = band_ref[:, :SWA_KV_WIDTH]
    v = band_ref[:, SWA_KV_WIDTH:]

    lane = lax.broadcasted_iota(jnp.int32, (STREAMS * SWA_BAND, LANES), 1)
    low = lane < SWA_HEAD_DIM
    k_rot = pltpu.roll(k, SWA_HEAD_DIM, axis=1)
    v_rot = pltpu.roll(v, SWA_HEAD_DIM, axis=1)
    k_lo = (jnp.where(low, k, 0.0).astype(BF16), jnp.where(low, k_rot, 0.0).astype(BF16))
    k_hi = (jnp.where(low, 0.0, k_rot).astype(BF16), jnp.where(low, 0.0, k).astype(BF16))
    v_lo = (jnp.where(low, v, 0.0).astype(BF16), jnp.where(low, v_rot, 0.0).astype(BF16))
    v_hi = (jnp.where(low, 0.0, v_rot).astype(BF16), jnp.where(low, 0.0, v).astype(BF16))

    out_low = lax.broadcasted_iota(jnp.int32, (SWA_BLOCK, LANES), 1) < SWA_HEAD_DIM

    slots_per_kv = SWA_SLOTS // SWA_KV_HEADS
    units = [(u, j) for u in range(STREAMS) for j in range(SWA_SLOTS)]
    band_of = lambda arr, u: arr[u * SWA_BAND:(u + 1) * SWA_BAND]
    k_bd = {(u, kh): jnp.concatenate([band_of(k_lo[kh], u), band_of(k_hi[kh], u)], axis=0)
            for u in range(STREAMS) for kh in range(SWA_KV_HEADS)}
    v_bd = {(u, kh): jnp.concatenate([band_of(v_lo[kh], u), band_of(v_hi[kh], u)], axis=0)
            for u in range(STREAMS) for kh in range(SWA_KV_HEADS)}
    qn = [qn[j].astype(BF16) for j in range(SWA_SLOTS)]
    between[1]()
    logits = {(u, j): _mm(qn[j][u * TILE:(u + 1) * TILE], k_bd[u, j // slots_per_kv], _NT)
              for u, j in units}
    probs = {}
    inv_den = {}
    for u, j in units:
        for r in range(SWA_HEADS_PER_SLOT):
            hd = j * SWA_HEADS_PER_SLOT + r
            lg = logits[u, j][:, r * SWA_BAND:(r + 1) * SWA_BAND] + bias_ref[first_tab, hd]
            sink = sinks_ref[hd]
            m = jnp.maximum(jnp.max(lg, axis=-1, keepdims=True), sink)
            p = jnp.exp(lg - m)
            den = jnp.sum(p, axis=-1, keepdims=True) + jnp.exp(sink - m)
            probs[u, j, r] = p.astype(BF16)
            inv_den[u, j, r] = 1.0 / den

    def values():
        outs = {(u, j): _mm(jnp.concatenate([probs[u, j, r] for r in range(SWA_HEADS_PER_SLOT)],
                                            axis=1), v_bd[u, j // slots_per_kv])
                for u, j in units}
        for u, j in units:
            o_ref[u, :, j * LANES:(j + 1) * LANES] = (
                outs[u, j] * jnp.where(out_low, inv_den[u, j, 0], inv_den[u, j, 1])
            ).astype(o_ref.dtype)

    between[2](values)


def _mixer_kernel(sinks_ref, x_ref, x_next_ref, gain_ref, w_head, w_ba, w_rest, bias_ref, qgain_ref, kgain_ref,
                  convw_ref, alog_ref, dtb_ref, dn_gain_ref, *refs, n_cast):
    cast_src = refs[:n_cast]
    yswa_ref, gates_ref, ydn_ref = refs[n_cast:n_cast + 3]
    cast_dst = refs[n_cast + 3:2 * n_cast + 3]
    band_ref, xbuf, qkv_scr, state, h_ref = refs[2 * n_cast + 3:]
    for src, dst in zip(cast_src, cast_dst):
        dst[...] = src[...].astype(BF16)
    t = pl.program_id(1)

    def normed(xr):
        x = xr[...].reshape(ROWS, D_MODEL)
        ms = jnp.mean(x * x, axis=-1, keepdims=True)
        return (x * lax.rsqrt(ms + EPS) * gain_ref[...]).astype(BF16)

    @pl.when(t == 0)
    def _():
        band_ref[...] = jnp.zeros_like(band_ref)
        xbuf[:, :, TILE:, :] = jnp.zeros((QKV_BLOCKS, STREAMS, CONV_TAIL, DN_HEAD_DIM), F32)
        state[...] = jnp.zeros_like(state)
        h_ref[...] = normed(x_ref)

    proj = lambda w: lax.dot_general(h_ref[...], w, _NT, preferred_element_type=F32)
    held = {}

    def dn_qkv():
        raw = proj(w_head[0:DN_QKV_WIDTH, :])
        xbuf[:, :, 0:CONV_TAIL, :] = xbuf[:, :, TILE:, :]
        for c in range(QKV_BLOCKS):
            for s in range(STREAMS):
                xbuf[c, s, CONV_TAIL:, :] = raw[s * TILE:(s + 1) * TILE,
                                                c * DN_HEAD_DIM:(c + 1) * DN_HEAD_DIM]

    def dn_preprocess():
        scale = DN_HEAD_DIM ** -0.5
        for c in range(QKV_BLOCKS):
            w = convw_ref[c]
            for s in range(STREAMS):
                acc = w[DN_CONV - 1:DN_CONV, :] * xbuf[c, s, CONV_TAIL:, :]
                for j in range(DN_CONV - 1):
                    off = CONV_TAIL - (DN_CONV - 1) + j
                    acc = acc + w[j:j + 1, :] * xbuf[c, s, off:off + TILE, :]
                act = _silu(acc)
                if c < 2 * DN_HEADS:
                    norm = lax.rsqrt(jnp.sum(act * act, axis=-1, keepdims=True) + EPS)
                    act = act * (norm * scale if c < DN_HEADS else norm)
                qkv_scr[c, s * TILE:(s + 1) * TILE, :] = act.astype(BF16)

    def gate_logits(pieces):
        for p in pieces:
            r0 = W_SWA + p * MXU_WIDTH
            gates_ref[p] = proj(w_rest[r0:r0 + MXU_WIDTH, :]).reshape(
                STREAMS, TILE, MXU_WIDTH).astype(gates_ref.dtype)

    def qkv_z_ba():
        dn_qkv()
        held["z"] = proj(w_head[DN_QKV_WIDTH:W_HEAD, :])
        held["ba"] = proj(w_ba[...])

    def gates_a_and_preprocess():
        gate_logits(range(0, GATES_EARLY))
        dn_preprocess()

    def delta_rule_with_gates(swa_values):
        late = [functools.partial(gate_logits, [p]) for p in range(GATES_EARLY, GATE_PIECES)]
        _delta_rule(qkv_scr, held["ba"], held["z"], alog_ref, dtb_ref, dn_gain_ref, state,
                    ydn_ref, late)
        swa_values()

    sw = [proj(w_rest[p * MXU_WIDTH:(p + 1) * MXU_WIDTH, :]) for p in range(SWA_PIECES)]
    slots_per_piece = MXU_WIDTH // LANES
    q_slots = [sw[j // slots_per_piece][:, (j % slots_per_piece) * LANES:
                                        (j % slots_per_piece + 1) * LANES] for j in range(SWA_SLOTS)]
    _swa_attend(jnp.where(t == 0, 0, 1), q_slots, sw[SWA_WIDTH // MXU_WIDTH], sinks_ref, bias_ref,
                qgain_ref, kgain_ref, band_ref, yswa_ref,
                between=(qkv_z_ba, gates_a_and_preprocess, delta_rule_with_gates))
    h_ref[...] = normed(x_next_ref)


def _mixer(x, gain, w_head, w_ba, w_rest, swa_params, dn_params, later_weights):
    sinks, bias_tabs, qgain, kgain = swa_params
    convw, alog_lane, dtb_lane, dn_gain = dn_params
    B, S, _ = x.shape
    tiles = S // TILE
    assert TILE == SWA_BLOCK and B % STREAMS == 0
    const = lambda shape: pl.BlockSpec(shape, lambda b, t, s: (0,) * len(shape),
                                       pipeline_mode=pl.Buffered(1))
    tok = lambda width: pl.BlockSpec((STREAMS, TILE, width), lambda b, t, s: (b, t, 0))
    cast_specs = _cast_specs(later_weights, tiles, axis=1)
    grid_spec = pltpu.PrefetchScalarGridSpec(
        num_scalar_prefetch=1,
        grid=(B // STREAMS, tiles),
        in_specs=[
            tok(D_MODEL),
            pl.BlockSpec((STREAMS, TILE, D_MODEL),
                         lambda b, t, s: (b, jnp.minimum(t + 1, tiles - 1), 0)),
            const((1, D_MODEL)),
            const((W_HEAD, D_MODEL)), const((LANES, D_MODEL)), const((W_SWA + W_GATES, D_MODEL)),
            const((2, SWA_HEADS, SWA_BLOCK, SWA_BAND)),
            const((1, SWA_WIDTH)),
            const((1, SWA_KV_WIDTH)),
            const((QKV_BLOCKS, DN_CONV, DN_HEAD_DIM)),
            const((1, LANES)), const((1, LANES)), const((1, DN_HEAD_DIM)),
        ] + cast_specs,
        out_specs=[
            tok(SWA_WIDTH),
            pl.BlockSpec((GATE_PIECES, STREAMS, TILE, MXU_WIDTH), lambda b, t, s: (0, b, t, 0)),
            tok(DN_WIDTH),
        ] + cast_specs,
        scratch_shapes=[pltpu.VMEM((STREAMS * SWA_BAND, 2 * SWA_KV_WIDTH), F32),
                        pltpu.VMEM((QKV_BLOCKS, STREAMS, TILE + CONV_TAIL, DN_HEAD_DIM), F32),
                        pltpu.VMEM((QKV_BLOCKS, ROWS, DN_HEAD_DIM), BF16),
                        pltpu.VMEM((STREAMS, DN_HEADS, DN_HEAD_DIM, DN_HEAD_DIM), F32),
                        pltpu.VMEM((ROWS, D_MODEL), BF16)],
    )
    outs = pl.pallas_call(
        functools.partial(_mixer_kernel, n_cast=len(later_weights)),
        grid_spec=grid_spec,
        out_shape=[
            jax.ShapeDtypeStruct((B, S, SWA_WIDTH), BF16),
            jax.ShapeDtypeStruct((GATE_PIECES, B, S, MXU_WIDTH), BF16),
            jax.ShapeDtypeStruct((B, S, DN_WIDTH), BF16),
        ] + [jax.ShapeDtypeStruct(w.shape, BF16) for w in later_weights],
        compiler_params=pltpu.CompilerParams(
            dimension_semantics=("arbitrary", "arbitrary"), vmem_limit_bytes=VMEM_LIMIT),
        name="mixer",
    )(sinks, x, x, gain, w_head, w_ba, w_rest, bias_tabs, qgain, kgain, convw, alog_lane, dtb_lane,
      dn_gain, *later_weights)
    return outs[:3], outs[3:]


def _merge_ffn_kernel(x_ref, ydn_ref, yswa_ref, g_ref, wa_ref, wb_ref, wo_ref, gain_ref,
                      wg_ref, wu_ref, wd_ref, o_ref, act_ref):
    per_branch = D_MODEL // MXU_WIDTH
    ga, gb = [_sigmoid(jnp.concatenate(
        [g_ref[br * per_branch + p].reshape(ROWS, MXU_WIDTH).astype(F32)
         for p in range(per_branch)], axis=1)) for br in range(2)]
    ydn = ydn_ref[...].reshape(ROWS, DN_WIDTH)
    yswa = yswa_ref[...].reshape(ROWS, SWA_WIDTH)
    merged = ga * _mm(ydn, wa_ref[...]) + gb * _mm(yswa, wb_ref[...])
    x1 = x_ref[...].reshape(ROWS, D_MODEL) + _mm(merged, wo_ref[...])
    ms = jnp.mean(x1 * x1, axis=-1, keepdims=True)
    h2 = (x1 * lax.rsqrt(ms + EPS) * gain_ref[...]).astype(BF16)
    for c0 in range(0, D_FF, MXU_WIDTH):
        gate = jnp.dot(h2, wg_ref[:, c0:c0 + MXU_WIDTH], preferred_element_type=F32)
        up = jnp.dot(h2, wu_ref[:, c0:c0 + MXU_WIDTH], preferred_element_type=F32)
        act_ref[:, c0:c0 + MXU_WIDTH] = (_silu(gate) * up).astype(BF16)
    out = x1 + jnp.dot(act_ref[...], wd_ref[...], preferred_element_type=F32)
    o_ref[...] = out.reshape(STREAMS, TILE, D_MODEL)


def _merge_ffn(x, ydn, yswa, gates, wa, wb, wo, gain, wg, wu, wd):
    B, S, _ = x.shape
    tok = lambda width: pl.BlockSpec((STREAMS, TILE, width), lambda b, t: (b, t, 0))
    once = lambda a, b: pl.BlockSpec((a, b), lambda i, j: (0, 0), pipeline_mode=pl.Buffered(1))
    return pl.pallas_call(
        _merge_ffn_kernel,
        grid=(B // STREAMS, S // TILE),
        in_specs=[tok(D_MODEL), tok(DN_WIDTH), tok(SWA_WIDTH),
                  pl.BlockSpec((GATE_PIECES, STREAMS, TILE, MXU_WIDTH), lambda b, t: (0, b, t, 0)),
                  once(DN_WIDTH, D_MODEL), once(SWA_WIDTH, D_MODEL), once(D_MODEL, D_MODEL),
                  once(1, D_MODEL),
                  once(D_MODEL, D_FF), once(D_MODEL, D_FF), once(D_FF, D_MODEL)],
        out_specs=tok(D_MODEL),
        out_shape=jax.ShapeDtypeStruct((B, S, D_MODEL), F32),
        scratch_shapes=[pltpu.VMEM((ROWS, D_FF), BF16)],
        compiler_params=pltpu.CompilerParams(
            dimension_semantics=("arbitrary", "arbitrary"), vmem_limit_bytes=VMEM_LIMIT),
        name="merge_ffn",
    )(x, ydn, yswa, gates, wa, wb, wo, gain, wg, wu, wd)


def _t5_bucket_table():
    qi = jnp.arange(SWA_BLOCK)[:, None]
    kj = jnp.arange(SWA_BAND)[None, :]
    dist = SWA_BLOCK + qi - kj
    in_window = (dist >= 0) & (dist < WINDOW)
    n = jnp.maximum(dist, 0)
    max_exact = REL_BUCKETS // 2
    nf = jnp.maximum(n, 1).astype(F32)
    large = max_exact + (jnp.log(nf / max_exact) / math.log(REL_MAX_DIST / max_exact)
                         * (REL_BUCKETS - max_exact)).astype(jnp.int32)
    large = jnp.minimum(large, REL_BUCKETS - 1)
    return jnp.where(n < max_exact, n, large), in_window


def kernel(x, attn_norm, w_in, dn_conv, dn_a_log, dn_dt_bias, dn_out_norm, swa_q_norm, swa_k_norm, swa_sinks, rel_bias, w_branch_dn, w_branch_swa, w_out, ffn_norm, w_gate, w_up, w_down):
    depth = w_in.shape[0]
    bucket, in_window = _t5_bucket_table()
    rel = rel_bias.astype(F32)
    bias_tab = sum(jnp.where(bucket[None] == b, rel[b][:, None, None], 0.0)
                   for b in range(REL_BUCKETS))
    bias_tab = jnp.where(in_window[None], bias_tab, -jnp.inf)
    has_prev = jnp.arange(SWA_BAND)[None, None, :] >= SWA_BLOCK
    bias_tabs = jnp.stack([jnp.where(has_prev, bias_tab, -jnp.inf), bias_tab])

    for l in range(depth):
        w_head, w_ba, w_rest = _cast_w(jnp.swapaxes(w_in[l], 0, 1).astype(F32))
        later = [w[l].astype(F32) for w in (w_branch_dn, w_branch_swa, w_out, w_gate, w_up, w_down)]
        qgain = jnp.tile(swa_q_norm[l].astype(F32), SWA_HEADS)[None, :] * (SWA_HEAD_DIM ** -0.5)
        kgain = jnp.tile(swa_k_norm[l].astype(F32), SWA_KV_HEADS)[None, :]
        swa_params = (swa_sinks[l].astype(F32), bias_tabs, qgain, kgain)
        convw = dn_conv[l].astype(F32).reshape(DN_CONV, QKV_BLOCKS, DN_HEAD_DIM).transpose(1, 0, 2)
        pad_lo = jnp.zeros((DN_HEADS,), F32)
        pad_hi = jnp.zeros((LANES - 2 * DN_HEADS,), F32)
        alog_lane = jnp.concatenate([pad_lo, dn_a_log[l].astype(F32), pad_hi])[None, :]
        dtb_lane = jnp.concatenate([pad_lo, dn_dt_bias[l].astype(F32), pad_hi])[None, :]
        dn_params = (convw, alog_lane, dtb_lane, dn_out_norm[l][None, :].astype(F32))

        (y_swa, gates, y_dn), later_bf16 = _mixer(
            x, attn_norm[l][None, :], w_head, w_ba, w_rest, swa_params, dn_params, later)
        wa, wb, wo, wg, wu, wd = later_bf16
        x = _merge_ffn(x, y_dn, y_swa, gates, wa, wb, wo, ffn_norm[l][None, :], wg, wu, wd)
    return x
```
